```python
import jax, jax.numpy as jnp
from jax import lax
import numpy as np

D_MODEL = 1024
BATCH = 8
SEQ = 4096
DEPTH = 4

N_EVEN = (DEPTH + 1) // 2
N_ODD = DEPTH // 2
D_PLE = 256
D_FF = 2816
EPS = 1e-6
N_NORMS = 5
NEG_BIG = -1e30
F_MIN = 1e-6
LB_MAX = 0.999

A_HEADS = 8
A_HEAD_DIM = 64
A_WIDTH = A_HEADS * A_HEAD_DIM
A_CHUNK = 128
B_HEADS = 4
B_KEY_DIM = 128
B_VAL_DIM = 128
B_KW = B_HEADS * B_KEY_DIM
B_VW = B_HEADS * B_VAL_DIM
B_CHUNK = 64
EVEN_IN = 2 * A_WIDTH + 2 * B_KW + 2 * B_VW
EVEN_OUT = A_WIDTH + B_VW
C_HEADS = 8
C_NOPE = 128
C_ROPE = 64
C_V = 128
C_QK = C_NOPE + C_ROPE
Q_LORA = 384
KV_LORA = 256
ODD_IN = Q_LORA + KV_LORA + C_ROPE
ATTN_BLOCK = 128
ROPE_THETA = 10000.0
ATTN_SCALE = C_QK ** -0.5

kernel_name = "hybrid_gmlp_hgrn2_mla_macaron_trunk"


def rms_norm(x, g):
    x32 = x.astype(jnp.float32)
    y = x32 * lax.rsqrt(jnp.mean(x32 * x32, axis=-1, keepdims=True) + EPS)
    return (y * g.astype(jnp.float32)).astype(x.dtype)


def swiglu(h, w_gate, w_up, w_down):
    return (jax.nn.silu(h @ w_gate) * (h @ w_up)) @ w_down


def gmlp_spatial(u, v, v_norm, w_s, b_s):
    bsz, seq = u.shape[:2]
    n_chunks = seq // A_CHUNK
    v = rms_norm(v, v_norm)
    vc = v.reshape(bsz, n_chunks, A_CHUNK, A_HEADS, A_HEAD_DIM)
    causal = jnp.tril(jnp.ones((A_CHUNK, A_CHUNK), dtype=bool))
    w = jnp.where(causal[None], w_s, 0.0).astype(vc.dtype)
    mixed = jnp.einsum('hts,bcshd->bcthd', w, vc) + b_s.T[None, None, :, :, None].astype(vc.dtype)
    return u * mixed.reshape(bsz, seq, A_HEADS, A_HEAD_DIM)


def hgrn2_chunkwise(q, log_f, k, v):
    bsz, seq = q.shape[:2]
    n = seq // B_CHUNK

    def to_chunks(t):
        t = t.astype(jnp.float32).reshape(bsz, n, B_CHUNK, B_HEADS, t.shape[-1])
        return jnp.transpose(t, (1, 0, 3, 2, 4))

    qc, kc, vc = to_chunks(q), to_chunks(k), to_chunks(v)
    bc = jnp.cumsum(to_chunks(log_f), axis=3)
    causal = jnp.tril(jnp.ones((B_CHUNK, B_CHUNK), dtype=bool))[:, :, None]

    def step(state, xs):
        q_, k_, v_, b_ = xs
        diff = b_[:, :, :, None, :] - b_[:, :, None, :, :]
        decay = jnp.exp(jnp.where(causal, diff, NEG_BIG))
        scores = jnp.einsum('bhtd,bhtsd->bhts', q_, decay * k_[:, :, None, :, :])
        o = jnp.einsum('bhts,bhse->bhte', scores, v_)
        o = o + jnp.einsum('bhtd,bhde->bhte', q_ * jnp.exp(b_), state)
        b_last = b_[:, :, -1:, :]
        state = jnp.exp(b_last)[:, :, 0, :, None] * state + jnp.einsum(
            'bhsd,bhse->bhde', k_ * jnp.exp(b_last - b_), v_)
        return state, o

    state0 = jnp.zeros((bsz, B_HEADS, B_KEY_DIM, B_VAL_DIM), jnp.float32)
    _, oc = lax.scan(step, state0, (qc, kc, vc, bc))
    return jnp.transpose(oc, (1, 0, 3, 2, 4)).reshape(bsz, seq, B_HEADS, B_VAL_DIM)


def even_mixer(h, w_in, v_norm, w_s, b_s, lb, out_norm, w_out):
    bsz, seq, _ = h.shape
    z = h @ w_in
    a_u, a_v, b_q, b_f, b_i, b_g = jnp.split(
        z, [A_WIDTH, 2 * A_WIDTH, 2 * A_WIDTH + B_KW, 2 * A_WIDTH + 2 * B_KW,
            2 * A_WIDTH + 2 * B_KW + B_VW], axis=-1)
    a_u = jax.nn.gelu(a_u).reshape(bsz, seq, A_HEADS, A_HEAD_DIM)
    a_v = jax.nn.gelu(a_v).reshape(bsz, seq, A_HEADS, A_HEAD_DIM)
    a_out = gmlp_spatial(a_u, a_v, v_norm, w_s, b_s).reshape(bsz, seq, A_WIDTH)
    zf = b_f.astype(jnp.float32)
    f = lb + (1.0 - lb) * jax.nn.sigmoid(zf)
    log_f = jnp.log(jnp.maximum(f, F_MIN))
    k = 1.0 - f
    o = hgrn2_chunkwise(b_q.reshape(bsz, seq, B_HEADS, B_KEY_DIM),
                        log_f.reshape(bsz, seq, B_HEADS, B_KEY_DIM),
                        k.reshape(bsz, seq, B_HEADS, B_KEY_DIM),
                        b_i.reshape(bsz, seq, B_HEADS, B_VAL_DIM))
    o = rms_norm(o, out_norm).astype(h.dtype) * jax.nn.silu(b_g.reshape(bsz, seq, B_HEADS, B_VAL_DIM))
    mixed = jnp.concatenate([a_out, o.reshape(bsz, seq, B_VW)], axis=-1)
    return mixed @ w_out


def rotate_half(x, cos, sin):
    x1, x2 = jnp.split(x, 2, axis=-1)
    return jnp.concatenate([x1 * cos - x2 * sin, x2 * cos + x1 * sin], axis=-1).astype(x.dtype)


def mla_mixer(h, positions, w_in, q_a_norm, kv_a_norm, w_q_b, w_kv_b, q_norm, k_norm, w_out):
    bsz, seq, _ = h.shape
    z = h @ w_in
    c_q, c_kv, k_rope = jnp.split(z, [Q_LORA, Q_LORA + KV_LORA], axis=-1)
    q = (rms_norm(c_q, q_a_norm) @ w_q_b).reshape(bsz, seq, C_HEADS, C_QK)
    kv = (rms_norm(c_kv, kv_a_norm) @ w_kv_b).reshape(bsz, seq, C_HEADS, C_NOPE + C_V)
    k_nope, v = jnp.split(kv, [C_NOPE], axis=-1)
    k_rope = jnp.broadcast_to(k_rope[:, :, None, :], (bsz, seq, C_HEADS, C_ROPE))
    k = jnp.concatenate([k_nope, k_rope], axis=-1)
    q = rms_norm(q, q_norm)
    k = rms_norm(k, k_norm)
    inv_freq = ROPE_THETA ** (-jnp.arange(0, C_ROPE, 2, dtype=jnp.float32) / C_ROPE)
    ang = positions.astype(jnp.float32)[..., None] * inv_freq
    cos = jnp.cos(ang)[:, :, None, :]
    sin = jnp.sin(ang)[:, :, None, :]
    q = jnp.concatenate([q[..., :C_NOPE], rotate_half(q[..., C_NOPE:], cos, sin)], axis=-1)
    k = jnp.concatenate([k[..., :C_NOPE], rotate_half(k[..., C_NOPE:], cos, sin)], axis=-1)
    q = jnp.transpose(q, (0, 2, 1, 3))
    k = jnp.transpose(k, (0, 2, 1, 3))
    v = jnp.transpose(v, (0, 2, 1, 3))
    n_blocks = seq // ATTN_BLOCK
    q_blocks = jnp.transpose(q.reshape(bsz, C_HEADS, n_blocks, ATTN_BLOCK, C_QK), (2, 0, 1, 3, 4))
    key_pos = jnp.arange(seq)

    def attend(args):
        qb, start = args
        s = jnp.einsum('bhqd,bhkd->bhqk', qb, k).astype(jnp.float32) * ATTN_SCALE
        q_pos = start + jnp.arange(ATTN_BLOCK)
        s = jnp.where(key_pos[None, :] <= q_pos[:, None], s, NEG_BIG)
        pr = jax.nn.softmax(s, axis=-1).astype(v.dtype)
        return jnp.einsum('bhqk,bhkd->bhqd', pr, v)

    out = lax.map(attend, (q_blocks, jnp.arange(n_blocks) * ATTN_BLOCK))
    out = jnp.transpose(out, (1, 0, 3, 2, 4)).reshape(bsz, seq, C_HEADS * C_V)
    return out @ w_out


def per_layer_embedding(h, p_i, w_gate, w_proj, post_norm):
    return rms_norm(jax.nn.sigmoid(h @ w_gate) * (p_i @ w_proj), post_norm)


def setup_inputs(seed: int = 0) -> dict:
    key = jax.random.key(seed)
    ks = jax.random.split(key, 32)
    f32 = jnp.float32

    def nrm(k, shape, fan_in):
        return jax.random.normal(k, shape, f32) * (fan_in ** -0.5)

    def gain(k, shape):
        return 1.0 + 0.02 * jax.random.normal(k, shape, f32)

    x = jax.random.normal(ks[0], (BATCH, SEQ, D_MODEL), f32)
    p = jax.random.normal(ks[1], (DEPTH, BATCH, SEQ, D_PLE), f32)
    offset = jax.random.randint(ks[2], (BATCH, 1), 0, 1024, dtype=jnp.int32)
    positions = offset + jnp.arange(SEQ, dtype=jnp.int32)[None, :]
    return {
        'x': x,
        'p': p,
        'positions': positions,
        'norm_gains': gain(ks[3], (DEPTH, N_NORMS, D_MODEL)),
        'ffn_w_gate': nrm(ks[4], (DEPTH, 2, D_MODEL, D_FF), D_MODEL),
        'ffn_w_up': nrm(ks[5], (DEPTH, 2, D_MODEL, D_FF), D_MODEL),
        'ffn_w_down': nrm(ks[6], (DEPTH, 2, D_FF, D_MODEL), D_FF),
        'ple_w_gate': nrm(ks[7], (DEPTH, D_MODEL, D_MODEL), D_MODEL),
        'ple_w_proj': nrm(ks[8], (DEPTH, D_PLE, D_MODEL), D_PLE),
        'even_w_in': nrm(ks[9], (N_EVEN, D_MODEL, EVEN_IN), D_MODEL),
        'gmlp_v_norm': gain(ks[10], (N_EVEN, A_HEADS, A_HEAD_DIM)),
        'gmlp_w_s': nrm(ks[11], (N_EVEN, A_HEADS, A_CHUNK, A_CHUNK), A_CHUNK),
        'gmlp_b_s': 1.0 + 0.01 * jax.random.normal(ks[12], (N_EVEN, A_HEADS, A_CHUNK), f32),
        'hgrn_lb_raw': 0.5 * jax.random.normal(ks[13], (N_EVEN, B_KW), f32),
        'hgrn_out_norm': gain(ks[14], (N_EVEN, B_VAL_DIM)),
        'even_w_out': nrm(ks[15], (N_EVEN, EVEN_OUT, D_MODEL), EVEN_OUT),
        'mla_w_in': nrm(ks[16], (N_ODD, D_MODEL, ODD_IN), D_MODEL),
        'mla_q_a_norm': gain(ks[17], (N_ODD, Q_LORA)),
        'mla_kv_a_norm': gain(ks[18], (N_ODD, KV_LORA)),
        'mla_w_q_b': nrm(ks[19], (N_ODD, Q_LORA, C_HEADS * C_QK), Q_LORA),
        'mla_w_kv_b': nrm(ks[20], (N_ODD, KV_LORA, C_HEADS * (C_NOPE + C_V)), KV_LORA),
        'mla_q_norm': gain(ks[21], (N_ODD, C_QK)),
        'mla_k_norm': gain(ks[22], (N_ODD, C_QK)),
        'mla_w_out': nrm(ks[23], (N_ODD, C_HEADS * C_V, D_MODEL), C_HEADS * C_V),
    }


def reference(x, p, positions, norm_gains, ffn_w_gate, ffn_w_up, ffn_w_down,
              ple_w_gate, ple_w_proj, even_w_in, gmlp_v_norm, gmlp_w_s, gmlp_b_s,
              hgrn_lb_raw, hgrn_out_norm, even_w_out, mla_w_in, mla_q_a_norm,
              mla_kv_a_norm, mla_w_q_b, mla_w_kv_b, mla_q_norm, mla_k_norm, mla_w_out):
    lb_sm = jax.nn.softmax(hgrn_lb_raw.astype(jnp.float32), axis=0)
    lower_bounds = jnp.clip(jnp.cumsum(lb_sm, axis=0) - lb_sm[0], 0.0, LB_MAX)
    for i in range(DEPTH):
        g = norm_gains[i]
        x = x + 0.5 * swiglu(rms_norm(x, g[0]), ffn_w_gate[i, 0], ffn_w_up[i, 0], ffn_w_down[i, 0])
        h = rms_norm(x, g[1])
        j = i // 2
        if i % 2 == 0:
            m = even_mixer(h, even_w_in[j], gmlp_v_norm[j], gmlp_w_s[j], gmlp_b_s[j],
                           lower_bounds[j], hgrn_out_norm[j], even_w_out[j])
        else:
            m = mla_mixer(h, positions, mla_w_in[j], mla_q_a_norm[j], mla_kv_a_norm[j],
                          mla_w_q_b[j], mla_w_kv_b[j], mla_q_norm[j], mla_k_norm[j], mla_w_out[j])
        x = x + m
        x = x + 0.5 * swiglu(rms_norm(x, g[2]), ffn_w_gate[i, 1], ffn_w_up[i, 1], ffn_w_down[i, 1])
        x = x + per_layer_embedding(rms_norm(x, g[3]), p[i], ple_w_gate[i], ple_w_proj[i], g[4])
    return x
```

```python
import functools
import math

import numpy as np
import jax
import jax.numpy as jnp
from jax import lax
from jax.experimental import pallas as pl
from jax.experimental.pallas import tpu as pltpu

F32 = jnp.float32
BF16 = jnp.bfloat16

D_MODEL = 1024
D_FF = 2816
D_PLE = 256
EPS = 1e-6
NEG_BIG = -1e30
F_MIN = 1e-6
LB_MAX = 0.999

A_HEADS = 8
A_HEAD_DIM = 64
A_WIDTH = A_HEADS * A_HEAD_DIM
A_CHUNK = 128
B_HEADS = 4
B_DIM = 128
B_WIDTH = B_HEADS * B_DIM
EVEN_PARTS = 6

C_HEADS = 8
C_NOPE = 128
C_ROPE = 64
C_V = 128
C_QK = C_NOPE + C_ROPE
Q_LORA = 384
KV_LORA = 256
ODD_IN_PAD = Q_LORA + KV_LORA + 128
Q_HEAD_PAD = 256
ROPE_THETA = 10000.0
ATTN_SCALE = C_QK ** -0.5

LANES = 128
SUBLANES = 8
MXU_DIM = 256
VMEM_LIMIT_BYTES = 60000 * 1024

TOKEN_TILE = 512
FF_CHUNK = 2 * MXU_DIM
HGRN_TILE = 256
HGRN_LEVELS = (128, 64, 32, 16, 8)
ATTN_TILE = 512


def _ff_chunks():
    chunks, off = [], 0
    while off < D_FF:
        size = min(FF_CHUNK, D_FF - off)
        chunks.append((off, size))
        off += size
    return tuple(chunks)


def _sigmoid(x):
    return 1.0 / (1.0 + jnp.exp(-x))


def _gelu_tanh(x):
    c = math.sqrt(2.0 / math.pi)
    return x * (0.5 * (1.0 + jnp.tanh(c * (x + 0.044715 * (x * x * x)))))


def _rms(x, g):
    ms = jnp.mean(x * x, axis=-1, keepdims=True)
    return x * lax.rsqrt(ms + EPS) * g


def _dot(a, b):
    return jnp.dot(a, b, preferred_element_type=F32)


def _dot_nt(a, b):
    return lax.dot_general(a, b, (((1,), (1,)), ((), ())), preferred_element_type=F32)


def _dot_tn(a, b):
    return lax.dot_general(a, b, (((0,), (0,)), ((), ())), preferred_element_type=F32)


def _split3(x):
    hi = x.astype(BF16)
    r1 = x - hi.astype(F32)
    mid = r1.astype(BF16)
    lo = (r1 - mid.astype(F32)).astype(BF16)
    return hi, mid, lo


def _swiglu_into(acc_ref, hn, wg_ref, wu_ref, wd_ref):
    for ci, (off, size) in enumerate(_ff_chunks()):
        gate = _dot(hn, wg_ref[:, off:off + size])
        up = _dot(hn, wu_ref[:, off:off + size])
        act = (gate * _sigmoid(gate) * up).astype(BF16)
        part = _dot(act, wd_ref[off:off + size, :])
        if ci == 0:
            acc_ref[...] = part
        else:
            acc_ref[...] += part


def _s1_even_kernel(x_ref, g0_ref, wg_ref, wu_ref, wd_ref, g1_ref, win_ref,
                    x1_ref, u_ref, v_ref, q_ref, f_ref, i_ref, gg_ref, acc_ref):
    x = x_ref[...]
    hn = _rms(x, g0_ref[...]).astype(BF16)
    _swiglu_into(acc_ref, hn, wg_ref, wu_ref, wd_ref)
    x1 = x + 0.5 * acc_ref[...]
    x1_ref[...] = x1
    h = _rms(x1, g1_ref[...]).astype(BF16)
    outs = (u_ref, v_ref, q_ref, f_ref, i_ref, gg_ref)
    for k, o_ref in enumerate(outs):
        z = _dot(h, win_ref[:, k * A_WIDTH:(k + 1) * A_WIDTH])
        o_ref[...] = z.astype(o_ref.dtype)


def _const_spec(shape):
    nd = len(shape)
    return pl.BlockSpec(shape, lambda *_: (0,) * nd, pipeline_mode=pl.Buffered(1))


def _stage1_even(x2d, g0, wg, wu, wd, g1, win):
    n = x2d.shape[0]
    tm = TOKEN_TILE
    row = lambda i: (i, 0)
    part_spec = pl.BlockSpec((tm, A_WIDTH), row)
    out_shape = [jax.ShapeDtypeStruct((n, D_MODEL), F32)]
    out_shape += [jax.ShapeDtypeStruct((n, A_WIDTH), F32 if k == 3 else BF16) for k in range(EVEN_PARTS)]
    return pl.pallas_call(
        _s1_even_kernel,
        grid=(n // tm,),
        in_specs=[pl.BlockSpec((tm, D_MODEL), row), _const_spec((1, D_MODEL)),
                  _const_spec((D_MODEL, D_FF)), _const_spec((D_MODEL, D_FF)), _const_spec((D_FF, D_MODEL)),
                  _const_spec((1, D_MODEL)), _const_spec((D_MODEL, EVEN_PARTS * A_WIDTH))],
        out_specs=[pl.BlockSpec((tm, D_MODEL), row)] + [part_spec] * EVEN_PARTS,
        out_shape=out_shape,
        scratch_shapes=[pltpu.VMEM((tm, D_MODEL), F32)],
        compiler_params=pltpu.CompilerParams(dimension_semantics=("parallel",),
                                             vmem_limit_bytes=VMEM_LIMIT_BYTES),
        name="stage1_even",
    )(x2d, g0, wg, wu, wd, g1, win)


def _rope_pad(x, cos_t, sin_t):
    return x * cos_t + (pltpu.roll(x, 32, axis=1) - pltpu.roll(x, 96, axis=1)) * sin_t


def _s1_odd_kernel(x_ref, pos_ref, invf_ref, g0_ref, wg_ref, wu_ref, wd_ref, g1_ref, win_ref,
                   qan_ref, kvan_ref, wqb_ref, wkvb_ref, qn_ref, kn_ref,
                   x1_ref, q_ref, k_ref, v_ref, acc_ref):
    x = x_ref[0]
    hn = _rms(x, g0_ref[...]).astype(BF16)
    _swiglu_into(acc_ref, hn, wg_ref, wu_ref, wd_ref)
    x1 = x + 0.5 * acc_ref[...]
    x1_ref[0] = x1
    h = _rms(x1, g1_ref[...]).astype(BF16)
    z = _dot(h, win_ref[...])
    c_q = z[:, :Q_LORA]
    c_kv = z[:, Q_LORA:Q_LORA + KV_LORA]
    k_rope = z[:, Q_LORA + KV_LORA:]
    qf = _dot(_rms(c_q, qan_ref[...]).astype(BF16), wqb_ref[...])
    kvf = _dot(_rms(c_kv, kvan_ref[...]).astype(BF16), wkvb_ref[...])

    lane = lax.broadcasted_iota(jnp.int32, (1, LANES), 1)
    valid = lane < C_ROPE
    ang = pos_ref[0] * invf_ref[...]
    cos_t = jnp.where(valid, jnp.cos(ang), 0.0)
    sin_t = jnp.where(valid, jnp.sin(ang), 0.0)

    qn = qn_ref[...]
    kn = kn_ref[...]
    qn_nope, qn_rope = qn[:, :C_NOPE], qn[:, C_NOPE:]
    kn_nope, kn_rope = kn[:, :C_NOPE], kn[:, C_NOPE:]

    kr_ss = jnp.sum(k_rope * k_rope, axis=-1, keepdims=True)
    kr_rot = _rope_pad(k_rope * kn_rope, cos_t, sin_t)
    for hd in range(C_HEADS):
        q_nope = qf[:, hd * Q_HEAD_PAD:hd * Q_HEAD_PAD + C_NOPE]
        q_rope = qf[:, hd * Q_HEAD_PAD + C_NOPE:(hd + 1) * Q_HEAD_PAD]
        ss = jnp.sum(q_nope * q_nope, axis=-1, keepdims=True) + jnp.sum(q_rope * q_rope, axis=-1, keepdims=True)
        r = lax.rsqrt(ss * (1.0 / C_QK) + EPS) * ATTN_SCALE
        q_ref[0, hd, :, 0:C_NOPE] = (q_nope * r * qn_nope).astype(BF16)
        q_rot = _rope_pad(q_rope * r * qn_rope, cos_t, sin_t)
        q_ref[0, hd, :, C_NOPE:C_QK] = q_rot[:, :C_ROPE].astype(BF16)

        k_nope = kvf[:, hd * C_NOPE:(hd + 1) * C_NOPE]
        ssk = jnp.sum(k_nope * k_nope, axis=-1, keepdims=True) + kr_ss
        rk = lax.rsqrt(ssk * (1.0 / C_QK) + EPS)
        k_ref[0, hd, :, 0:C_NOPE] = (k_nope * rk * kn_nope).astype(BF16)
        k_ref[0, hd, :, C_NOPE:C_QK] = (kr_rot * rk)[:, :C_ROPE].astype(BF16)
    v_ref[0] = kvf[:, C_HEADS * C_NOPE:].astype(BF16)


def _stage1_odd(x3d, pos3d, invf, g0, wg, wu, wd, g1, win, qan, kvan, wqb, wkvb, qn, kn):
    bsz, seq, _ = x3d.shape
    tm = TOKEN_TILE
    tile = lambda b, i: (b, i, 0)
    head_tile = lambda b, i: (b, 0, i, 0)
    return pl.pallas_call(
        _s1_odd_kernel,
        grid=(bsz, seq // tm),
        in_specs=[pl.BlockSpec((1, tm, D_MODEL), tile), pl.BlockSpec((1, tm, 1), tile),
                  _const_spec((1, LANES)), _const_spec((1, D_MODEL)),
                  _const_spec((D_MODEL, D_FF)), _const_spec((D_MODEL, D_FF)), _const_spec((D_FF, D_MODEL)),
                  _const_spec((1, D_MODEL)), _const_spec((D_MODEL, ODD_IN_PAD)),
                  _const_spec((1, Q_LORA)), _const_spec((1, KV_LORA)),
                  _const_spec((Q_LORA, C_HEADS * Q_HEAD_PAD)), _const_spec((KV_LORA, C_HEADS * (C_NOPE + C_V))),
                  _const_spec((1, Q_HEAD_PAD)), _const_spec((1, Q_HEAD_PAD))],
        out_specs=[pl.BlockSpec((1, tm, D_MODEL), tile),
                   pl.BlockSpec((1, C_HEADS, tm, C_QK), head_tile),
                   pl.BlockSpec((1, C_HEADS, tm, C_QK), head_tile),
                   pl.BlockSpec((1, tm, C_HEADS * C_V), tile)],
        out_shape=[jax.ShapeDtypeStruct((bsz, seq, D_MODEL), F32),
                   jax.ShapeDtypeStruct((bsz, C_HEADS, seq, C_QK), BF16),
                   jax.ShapeDtypeStruct((bsz, C_HEADS, seq, C_QK), BF16),
                   jax.ShapeDtypeStruct((bsz, seq, C_HEADS * C_V), BF16)],
        scratch_shapes=[pltpu.VMEM((tm, D_MODEL), F32)],
        compiler_params=pltpu.CompilerParams(dimension_semantics=("parallel", "parallel"),
                                             vmem_limit_bytes=VMEM_LIMIT_BYTES),
        name="stage1_odd",
    )(x3d, pos3d, invf, g0, wg, wu, wd, g1, win, qan, kvan, wqb, wkvb, qn, kn)


def _s2_kernel(x_ref, m_ref, p_ref, wo_ref, g2_ref, wg_ref, wu_ref, wd_ref,
               g3_ref, wpg_ref, wpp_ref, g4_ref, o_ref, acc_ref):
    x = x_ref[...] + _dot(m_ref[...], wo_ref[...])
    hn = _rms(x, g2_ref[...]).astype(BF16)
    _swiglu_into(acc_ref, hn, wg_ref, wu_ref, wd_ref)
    x = x + 0.5 * acc_ref[...]
    h3 = _rms(x, g3_ref[...]).astype(BF16)
    gate = _sigmoid(_dot(h3, wpg_ref[...]))
    proj = _dot(p_ref[...].astype(BF16), wpp_ref[...])
    o_ref[...] = x + _rms(gate * proj, g4_ref[...])


def _stage2(x2d, mixed, p2d, wo, g2, wg, wu, wd, g3, wpg, wpp, g4):
    n = x2d.shape[0]
    tm = TOKEN_TILE
    row = lambda i: (i, 0)
    return pl.pallas_call(
        _s2_kernel,
        grid=(n // tm,),
        in_specs=[pl.BlockSpec((tm, D_MODEL), row), pl.BlockSpec((tm, D_MODEL), row),
                  pl.BlockSpec((tm, D_PLE), row), _const_spec((D_MODEL, D_MODEL)),
                  _const_spec((1, D_MODEL)),
                  _const_spec((D_MODEL, D_FF)), _const_spec((D_MODEL, D_FF)), _const_spec((D_FF, D_MODEL)),
                  _const_spec((1, D_MODEL)), _const_spec((D_MODEL, D_MODEL)), _const_spec((D_PLE, D_MODEL)),
                  _const_spec((1, D_MODEL))],
        out_specs=pl.BlockSpec((tm, D_MODEL), row),
        out_shape=jax.ShapeDtypeStruct((n, D_MODEL), F32),
        scratch_shapes=[pltpu.VMEM((tm, D_MODEL), F32)],
        compiler_params=pltpu.CompilerParams(dimension_semantics=("parallel",),
                                             vmem_limit_bytes=VMEM_LIMIT_BYTES),
        name="stage2",
    )(x2d, mixed, p2d, wo, g2, wg, wu, wd, g3, wpg, wpp, g4)


def _level_map(ts):
    t = np.arange(ts)[:, None]
    s = np.arange(ts)[None, :]
    lvl = np.full((ts, ts), len(HGRN_LEVELS) + 1, np.int32)
    lvl[(t // SUBLANES == s // SUBLANES) & (s <= t)] = len(HGRN_LEVELS)
    for li, h in enumerate(HGRN_LEVELS):
        own = (t // (2 * h) == s // (2 * h)) & (t % (2 * h) >= h) & (s % (2 * h) < h)
        lvl[own] = li
    return lvl


def _even_core_kernel(u_ref, v_ref, q_ref, f_ref, i_ref, g_ref,
                      vnorm_ref, ind_ref, wcat_ref, bfull_ref, lb_ref, onorm_ref, ltri_ref, lvl_ref,
                      o_ref, state_ref):
    ts = HGRN_TILE
    n_lvl = len(HGRN_LEVELS)

    u = _gelu_tanh(u_ref[0].astype(F32))
    v = _gelu_tanh(v_ref[0].astype(F32))
    v2_hi, v2_mid, _ = _split3(v * v)
    ss = _dot(v2_hi, ind_ref[...]) + _dot(v2_mid, ind_ref[...])
    vn = (v * lax.rsqrt(ss * (1.0 / A_HEAD_DIM) + EPS) * vnorm_ref[...]).astype(BF16)
    lane_head = lax.broadcasted_iota(jnp.int32, (1, A_WIDTH), 1) // A_HEAD_DIM
    for c in range(ts // A_CHUNK):
        rows = slice(c * A_CHUNK, (c + 1) * A_CHUNK)
        vc = vn[rows, :]
        zero = jnp.zeros_like(vc)
        vbd = jnp.concatenate([jnp.where(lane_head == hd, vc, zero) for hd in range(A_HEADS)], axis=0)
        mixed = _dot(wcat_ref[...], vbd) + bfull_ref[...]
        o_ref[0, rows, 0:A_WIDTH] = (u[rows, :] * mixed).astype(BF16)

    @pl.when(pl.program_id(1) == 0)
    def _():
        state_ref[...] = jnp.zeros_like(state_ref)

    lb = lb_ref[...]
    fg = lb + (1.0 - lb) * _sigmoid(f_ref[0])
    lf = jnp.log(jnp.maximum(fg, F_MIN))
    kk = 1.0 - fg
    qq = q_ref[0].astype(F32)
    lf_hi, lf_mid, lf_lo = _split3(lf)
    ltri = ltri_ref[...]
    b = _dot(ltri, lf_hi) + _dot(ltri, lf_mid) + _dot(ltri, lf_lo)

    q_lvls, k_lvls = [], []
    for h in HGRN_LEVELS:
        q_parts, k_parts = [], []
        zero = jnp.zeros((h, B_WIDTH), F32)
        for j in range(ts // (2 * h)):
            r0 = j * 2 * h
            bm = b[r0 + h - 1:r0 + h, :]
            lo_rows = slice(r0, r0 + h)
            up_rows = slice(r0 + h, r0 + 2 * h)
            k_parts += [kk[lo_rows, :] * jnp.exp(bm - b[lo_rows, :]), zero]
            q_parts += [zero, qq[up_rows, :] * jnp.exp(b[up_rows, :] - bm)]
        q_lvls.append(jnp.concatenate(q_parts, axis=0).astype(BF16))
        k_lvls.append(jnp.concatenate(k_parts, axis=0).astype(BF16))
    b3 = b.reshape(ts // SUBLANES, SUBLANES, B_WIDTH)
    bref = jnp.broadcast_to(b3[:, 3:4, :], b3.shape).reshape(ts, B_WIDTH)
    q_lvls.append((qq * jnp.exp(b - bref)).astype(BF16))
    k_lvls.append((kk * jnp.exp(bref - b)).astype(BF16))

    b_last = b[ts - 1:ts, :]
    q_in = (qq * jnp.exp(b)).astype(BF16)
    k_out = (kk * jnp.exp(b_last - b)).astype(BF16)
    carry = jnp.exp(b_last)
    vv = i_ref[0]
    gate = g_ref[0].astype(F32)
    gate = gate * _sigmoid(gate)
    onorm = onorm_ref[...]

    lvl = lvl_ref[...]
    masks = [lvl == li for li in range(n_lvl + 1)]
    for hd in range(B_HEADS):
        cols = slice(hd * B_DIM, (hd + 1) * B_DIM)
        scores = jnp.where(masks[n_lvl], _dot_nt(q_lvls[n_lvl][:, cols], k_lvls[n_lvl][:, cols]), 0.0)
        for li in range(n_lvl):
            scores = jnp.where(masks[li], _dot_nt(q_lvls[li][:, cols], k_lvls[li][:, cols]), scores)
        v_h = vv[:, cols]
        st = state_ref[hd]
        o = _dot(scores.astype(BF16), v_h) + _dot_nt(q_in[:, cols], st.astype(BF16))
        state_ref[hd] = st * carry[:, cols] + _dot_tn(v_h, k_out[:, cols])
        on = _rms(o, onorm[:, cols]) * gate[:, cols]
        o_ref[0, :, A_WIDTH + hd * B_DIM:A_WIDTH + (hd + 1) * B_DIM] = on.astype(BF16)


def _even_core(parts, vnorm, ind, wcat, bfull, lb, onorm, ltri, lvl):
    bsz, seq, _ = parts[0].shape
    ts = HGRN_TILE
    tile = lambda b, i: (b, i, 0)
    part_spec = pl.BlockSpec((1, ts, A_WIDTH), tile)
    return pl.pallas_call(
        _even_core_kernel,
        grid=(bsz, seq // ts),
        in_specs=[part_spec] * EVEN_PARTS + [
            _const_spec((1, A_WIDTH)), _const_spec((A_WIDTH, A_WIDTH)),
            _const_spec((A_CHUNK, A_HEADS * A_CHUNK)), _const_spec((A_CHUNK, A_WIDTH)),
            _const_spec((1, B_WIDTH)), _const_spec((1, B_WIDTH)),
            _const_spec((ts, ts)), _const_spec((ts, ts))],
        out_specs=pl.BlockSpec((1, ts, D_MODEL), tile),
        out_shape=jax.ShapeDtypeStruct((bsz, seq, D_MODEL), BF16),
        scratch_shapes=[pltpu.VMEM((B_HEADS, B_DIM, B_DIM), F32)],
        compiler_params=pltpu.CompilerParams(dimension_semantics=("parallel", "arbitrary"),
                                             vmem_limit_bytes=VMEM_LIMIT_BYTES),
        name="even_core",
    )(*parts, vnorm, ind, wcat, bfull, lb, onorm, ltri, lvl)


def _attn_kernel(q_ref, k_ref, v_ref, o_ref, m_ref, l_ref, acc_ref):
    tq = ATTN_TILE
    qi = pl.program_id(2)
    q = q_ref[0, 0]

    m_ref[...] = jnp.full_like(m_ref, NEG_BIG)
    l_ref[...] = jnp.zeros_like(l_ref)
    acc_ref[...] = jnp.zeros_like(acc_ref)

    def block(j, masked):
        start = pl.multiple_of(j * tq, tq)
        kb = k_ref[0, 0, pl.ds(start, tq), :]
        vb = v_ref[0, pl.ds(start, tq), :]
        s = _dot_nt(q, kb)
        if masked:
            row = lax.broadcasted_iota(jnp.int32, (tq, tq), 0)
            col = lax.broadcasted_iota(jnp.int32, (tq, tq), 1)
            s = jnp.where(col <= row, s, NEG_BIG)
        m_prev = m_ref[...]
        m_new = jnp.maximum(m_prev, jnp.max(s, axis=-1, keepdims=True))
        alpha = jnp.exp(m_prev - m_new)
        p = jnp.exp(s - m_new)
        l_ref[...] = alpha * l_ref[...] + jnp.sum(p, axis=-1, keepdims=True)
        acc_ref[...] = alpha * acc_ref[...] + _dot(p.astype(BF16), vb)
        m_ref[...] = m_new

    def body(j, carry):
        block(j, False)
        return carry

    lax.fori_loop(0, qi, body, 0)
    block(qi, True)
    o_ref[0] = (acc_ref[...] / l_ref[...]).astype(BF16)


def _attention(q, k, v):
    bsz, heads, seq, _ = q.shape
    tq = ATTN_TILE
    return pl.pallas_call(
        _attn_kernel,
        grid=(bsz, heads, seq // tq),
        in_specs=[pl.BlockSpec((1, 1, tq, C_QK), lambda b, h, i: (b, h, i, 0)),
                  pl.BlockSpec((1, 1, seq, C_QK), lambda b, h, i: (b, h, 0, 0)),
                  pl.BlockSpec((1, seq, C_V), lambda b, h, i: (b, 0, h))],
        out_specs=pl.BlockSpec((1, tq, C_V), lambda b, h, i: (b, i, h)),
        out_shape=jax.ShapeDtypeStruct((bsz, seq, heads * C_V), BF16),
        scratch_shapes=[pltpu.VMEM((tq, 1), F32), pltpu.VMEM((tq, 1), F32), pltpu.VMEM((tq, C_V), F32)],
        compiler_params=pltpu.CompilerParams(dimension_semantics=("parallel", "parallel", "arbitrary"),
                                             vmem_limit_bytes=VMEM_LIMIT_BYTES),
        name="mla_attention",
    )(q, k, v)


def _row(v):
    return v.reshape(1, -1).astype(F32)


def _pad_cols(w, total):
    return jnp.pad(w, ((0, 0), (0, total - w.shape[1])))


def _even_params(w_in, v_norm, w_s, b_s, lb, out_norm):
    causal = np.tril(np.ones((A_CHUNK, A_CHUNK), bool))
    w = jnp.where(causal[None], w_s, 0.0)
    wcat = jnp.transpose(w, (1, 0, 2)).reshape(A_CHUNK, A_HEADS * A_CHUNK)
    bfull = jnp.repeat(b_s.T, A_HEAD_DIM, axis=1)
    head_of = np.arange(A_WIDTH) // A_HEAD_DIM
    ind = (head_of[:, None] == head_of[None, :]).astype(np.float32)
    ltri = np.tril(np.ones((HGRN_TILE, HGRN_TILE), np.float32))
    return dict(win=w_in.astype(BF16), vnorm=_row(v_norm), ind=jnp.asarray(ind, BF16),
                wcat=wcat.astype(BF16), bfull=bfull.astype(F32), lb=_row(lb),
                onorm=_row(jnp.tile(out_norm, B_HEADS)), ltri=jnp.asarray(ltri, BF16),
                lvl=jnp.asarray(_level_map(HGRN_TILE)))


def _odd_params(w_in, q_a_norm, kv_a_norm, w_q_b, w_kv_b, q_norm, k_norm):
    win = _pad_cols(w_in, ODD_IN_PAD)
    wq = w_q_b.reshape(Q_LORA, C_HEADS, C_QK)
    wq = jnp.pad(wq, ((0, 0), (0, 0), (0, Q_HEAD_PAD - C_QK))).reshape(Q_LORA, C_HEADS * Q_HEAD_PAD)
    wkv = w_kv_b.reshape(KV_LORA, C_HEADS, C_NOPE + C_V)
    wkv = jnp.concatenate([wkv[:, :, :C_NOPE].reshape(KV_LORA, C_HEADS * C_NOPE),
                           wkv[:, :, C_NOPE:].reshape(KV_LORA, C_HEADS * C_V)], axis=1)
    inv_freq = ROPE_THETA ** (-jnp.arange(0, C_ROPE, 2, dtype=F32) / C_ROPE)
    invf = jnp.tile(inv_freq, LANES // (C_ROPE // 2)).reshape(1, LANES)
    return dict(win=win.astype(BF16), qan=_row(q_a_norm), kvan=_row(kv_a_norm),
                wqb=wq.astype(BF16), wkvb=wkv.astype(BF16),
                qn=_pad_cols(_row(q_norm), Q_HEAD_PAD), kn=_pad_cols(_row(k_norm), Q_HEAD_PAD), invf=invf)


def kernel(x, p, positions, norm_gains, ffn_w_gate, ffn_w_up, ffn_w_down, ple_w_gate, ple_w_proj,
           even_w_in, gmlp_v_norm, gmlp_w_s, gmlp_b_s, hgrn_lb_raw, hgrn_out_norm, even_w_out,
           mla_w_in, mla_q_a_norm, mla_kv_a_norm, mla_w_q_b, mla_w_kv_b, mla_q_norm, mla_k_norm, mla_w_out):
    bsz, seq, d = x.shape
    depth = norm_gains.shape[0]
    n = bsz * seq
    assert d == D_MODEL and seq % ATTN_TILE == 0 and seq % TOKEN_TILE == 0 and seq % HGRN_TILE == 0

    lb_sm = jax.nn.softmax(hgrn_lb_raw.astype(F32), axis=0)
    lower_bounds = jnp.clip(jnp.cumsum(lb_sm, axis=0) - lb_sm[0], 0.0, LB_MAX)
    pos3d = positions.astype(F32).reshape(bsz, seq, 1)

    xc = x.reshape(n, d)
    for i in range(depth):
        g = norm_gains[i]
        j = i // 2
        wg1, wu1, wd1 = (w[i, 0].astype(BF16) for w in (ffn_w_gate, ffn_w_up, ffn_w_down))
        wg2, wu2, wd2 = (w[i, 1].astype(BF16) for w in (ffn_w_gate, ffn_w_up, ffn_w_down))
        if i % 2 == 0:
            ep = _even_params(even_w_in[j], gmlp_v_norm[j], gmlp_w_s[j], gmlp_b_s[j],
                              lower_bounds[j], hgrn_out_norm[j])
            outs = _stage1_even(xc, _row(g[0]), wg1, wu1, wd1, _row(g[1]), ep["win"])
            xc = outs[0]
            parts = [o.reshape(bsz, seq, A_WIDTH) for o in outs[1:]]
            mixed = _even_core(parts, ep["vnorm"], ep["ind"], ep["wcat"], ep["bfull"], ep["lb"],
                               ep["onorm"], ep["ltri"], ep["lvl"])
            w_out = even_w_out[j]
        else:
            op = _odd_params(mla_w_in[j], mla_q_a_norm[j], mla_kv_a_norm[j], mla_w_q_b[j], mla_w_kv_b[j],
                             mla_q_norm[j], mla_k_norm[j])
            x1, q, k, v = _stage1_odd(xc.reshape(bsz, seq, d), pos3d, op["invf"], _row(g[0]), wg1, wu1, wd1,
                                      _row(g[1]), op["win"], op["qan"], op["kvan"], op["wqb"], op["wkvb"],
                                      op["qn"], op["kn"])
            xc = x1.reshape(n, d)
            mixed = _attention(q, k, v)
            w_out = mla_w_out[j]
        xc = _stage2(xc, mixed.reshape(n, d), p[i].reshape(n, D_PLE), w_out.astype(BF16), _row(g[2]),
                     wg2, wu2, wd2, _row(g[3]), ple_w_gate[i].astype(BF16), ple_w_proj[i].astype(BF16),
                     _row(g[4]))
    return xc.reshape(bsz, seq, d)
```

```python
import functools
import math

import numpy as np
import jax
import jax.numpy as jnp
from jax import lax
from jax.experimental import pallas as pl
from jax.experimental.pallas import tpu as pltpu

F32 = jnp.float32
BF16 = jnp.bfloat16

D_MODEL = 1024
D_FF = 2816
D_PLE = 256
EPS = 1e-6
NEG_BIG = -1e30
F_MIN = 1e-6
LB_MAX = 0.999

A_HEADS = 8
A_HEAD_DIM = 64
A_WIDTH = A_HEADS * A_HEAD_DIM
A_CHUNK = 128
B_HEADS = 4
B_DIM = 128
B_WIDTH = B_HEADS * B_DIM
EVEN_PARTS = 6

C_HEADS = 8
C_NOPE = 128
C_ROPE = 64
C_V = 128
C_QK = C_NOPE + C_ROPE
Q_LORA = 384
KV_LORA = 256
ODD_IN_PAD = Q_LORA + KV_LORA + 128
Q_HEAD_PAD = 256
ROPE_THETA = 10000.0
ATTN_SCALE = C_QK ** -0.5

LANES = 128
SUBLANES = 8
MXU_DIM = 256
VMEM_LIMIT_BYTES = 60000 * 1024

TOKEN_TILE = 512
FF_CHUNK = 2 * MXU_DIM
HGRN_TILE = 256
HGRN_LEVELS = (128, 64, 32, 16, 8)
ATTN_TILE = 512
ATTN_HEADS_PER_STEP = 4


def _ff_chunks():
    chunks, off = [], 0
    while off < D_FF:
        size = min(FF_CHUNK, D_FF - off)
        chunks.append((off, size))
        off += size
    return tuple(chunks)


def _sigmoid(x):
    return 1.0 / (1.0 + jnp.exp(-x))


def _gelu_tanh(x):
    c = math.sqrt(2.0 / math.pi)
    return x * (0.5 * (1.0 + jnp.tanh(c * (x + 0.044715 * (x * x * x)))))


def _rms(x, g):
    ms = jnp.mean(x * x, axis=-1, keepdims=True)
    return x * lax.rsqrt(ms + EPS) * g


def _dot(a, b):
    return jnp.dot(a, b, preferred_element_type=F32)


def _dot_nt(a, b):
    return lax.dot_general(a, b, (((1,), (1,)), ((), ())), preferred_element_type=F32)


def _dot_tn(a, b):
    return lax.dot_general(a, b, (((0,), (0,)), ((), ())), preferred_element_type=F32)


def _split3(x):
    hi = x.astype(BF16)
    r1 = x - hi.astype(F32)
    mid = r1.astype(BF16)
    lo = (r1 - mid.astype(F32)).astype(BF16)
    return hi, mid, lo


def _swiglu_into(acc_ref, hn, wg_ref, wu_ref, wd_ref):
    for ci, (off, size) in enumerate(_ff_chunks()):
        gate = _dot(hn, wg_ref[:, off:off + size])
        up = _dot(hn, wu_ref[:, off:off + size])
        act = (gate * _sigmoid(gate) * up).astype(BF16)
        part = _dot(act, wd_ref[off:off + size, :])
        if ci == 0:
            acc_ref[...] = part
        else:
            acc_ref[...] += part


def _s1_even_kernel(x_ref, g0_ref, wg_ref, wu_ref, wd_ref, g1_ref, win_ref,
                    x1_ref, u_ref, v_ref, q_ref, f_ref, i_ref, gg_ref, acc_ref):
    x = x_ref[...]
    hn = _rms(x, g0_ref[...]).astype(BF16)
    _swiglu_into(acc_ref, hn, wg_ref, wu_ref, wd_ref)
    x1 = x + 0.5 * acc_ref[...]
    x1_ref[...] = x1
    h = _rms(x1, g1_ref[...]).astype(BF16)
    outs = (u_ref, v_ref, q_ref, f_ref, i_ref, gg_ref)
    for k, o_ref in enumerate(outs):
        z = _dot(h, win_ref[:, k * A_WIDTH:(k + 1) * A_WIDTH])
        o_ref[...] = z.astype(o_ref.dtype)


def _const_spec(shape):
    nd = len(shape)
    return pl.BlockSpec(shape, lambda *_: (0,) * nd, pipeline_mode=pl.Buffered(1))


def _stage1_even(x2d, g0, wg, wu, wd, g1, win):
    n = x2d.shape[0]
    tm = TOKEN_TILE
    row = lambda i: (i, 0)
    part_spec = pl.BlockSpec((tm, A_WIDTH), row)
    out_shape = [jax.ShapeDtypeStruct((n, D_MODEL), F32)]
    out_shape += [jax.ShapeDtypeStruct((n, A_WIDTH), F32 if k == 3 else BF16) for k in range(EVEN_PARTS)]
    return pl.pallas_call(
        _s1_even_kernel,
        grid=(n // tm,),
        in_specs=[pl.BlockSpec((tm, D_MODEL), row), _const_spec((1, D_MODEL)),
                  _const_spec((D_MODEL, D_FF)), _const_spec((D_MODEL, D_FF)), _const_spec((D_FF, D_MODEL)),
                  _const_spec((1, D_MODEL)), _const_spec((D_MODEL, EVEN_PARTS * A_WIDTH))],
        out_specs=[pl.BlockSpec((tm, D_MODEL), row)] + [part_spec] * EVEN_PARTS,
        out_shape=out_shape,
        scratch_shapes=[pltpu.VMEM((tm, D_MODEL), F32)],
        compiler_params=pltpu.CompilerParams(dimension_semantics=("parallel",),
                                             vmem_limit_bytes=VMEM_LIMIT_BYTES),
        name="stage1_even",
    )(x2d, g0, wg, wu, wd, g1, win)


def _rope_pad(x, cos_t, sin_t):
    return x * cos_t + (pltpu.roll(x, 32, axis=1) - pltpu.roll(x, 96, axis=1)) * sin_t


def _s1_odd_kernel(x_ref, pos_ref, invf_ref, g0_ref, wg_ref, wu_ref, wd_ref, g1_ref, win_ref,
                   qan_ref, kvan_ref, wqb_ref, wkvb_ref, qn_ref, kn_ref,
                   x1_ref, q_ref, k_ref, v_ref, acc_ref):
    x = x_ref[0]
    hn = _rms(x, g0_ref[...]).astype(BF16)
    _swiglu_into(acc_ref, hn, wg_ref, wu_ref, wd_ref)
    x1 = x + 0.5 * acc_ref[...]
    x1_ref[0] = x1
    h = _rms(x1, g1_ref[...]).astype(BF16)
    z = _dot(h, win_ref[...])
    c_q = z[:, :Q_LORA]
    c_kv = z[:, Q_LORA:Q_LORA + KV_LORA]
    k_rope = z[:, Q_LORA + KV_LORA:]
    qf = _dot(_rms(c_q, qan_ref[...]).astype(BF16), wqb_ref[...])
    kvf = _dot(_rms(c_kv, kvan_ref[...]).astype(BF16), wkvb_ref[...])

    lane = lax.broadcasted_iota(jnp.int32, (1, LANES), 1)
    valid = lane < C_ROPE
    ang = pos_ref[0] * invf_ref[...]
    cos_t = jnp.where(valid, jnp.cos(ang), 0.0)
    sin_t = jnp.where(valid, jnp.sin(ang), 0.0)

    qn = qn_ref[...]
    kn = kn_ref[...]
    qn_nope, qn_rope = qn[:, :C_NOPE], qn[:, C_NOPE:]
    kn_nope, kn_rope = kn[:, :C_NOPE], kn[:, C_NOPE:]

    kr_ss = jnp.sum(k_rope * k_rope, axis=-1, keepdims=True)
    kr_rot = _rope_pad(k_rope * kn_rope, cos_t, sin_t)
    for hd in range(C_HEADS):
        q_nope = qf[:, hd * Q_HEAD_PAD:hd * Q_HEAD_PAD + C_NOPE]
        q_rope = qf[:, hd * Q_HEAD_PAD + C_NOPE:(hd + 1) * Q_HEAD_PAD]
        ss = jnp.sum(q_nope * q_nope, axis=-1, keepdims=True) + jnp.sum(q_rope * q_rope, axis=-1, keepdims=True)
        r = lax.rsqrt(ss * (1.0 / C_QK) + EPS) * ATTN_SCALE
        q_ref[0, hd, :, 0:C_NOPE] = (q_nope * r * qn_nope).astype(BF16)
        q_rot = _rope_pad(q_rope * r * qn_rope, cos_t, sin_t)
        q_ref[0, hd, :, C_NOPE:C_QK] = q_rot[:, :C_ROPE].astype(BF16)

        k_nope = kvf[:, hd * C_NOPE:(hd + 1) * C_NOPE]
        ssk = jnp.sum(k_nope * k_nope, axis=-1, keepdims=True) + kr_ss
        rk = lax.rsqrt(ssk * (1.0 / C_QK) + EPS)
        k_ref[0, hd, :, 0:C_NOPE] = (k_nope * rk * kn_nope).astype(BF16)
        k_ref[0, hd, :, C_NOPE:C_QK] = (kr_rot * rk)[:, :C_ROPE].astype(BF16)
    v_ref[0] = kvf[:, C_HEADS * C_NOPE:].astype(BF16)


def _stage1_odd(x3d, pos3d, invf, g0, wg, wu, wd, g1, win, qan, kvan, wqb, wkvb, qn, kn):
    bsz, seq, _ = x3d.shape
    tm = TOKEN_TILE
    tile = lambda b, i: (b, i, 0)
    head_tile = lambda b, i: (b, 0, i, 0)
    return pl.pallas_call(
        _s1_odd_kernel,
        grid=(bsz, seq // tm),
        in_specs=[pl.BlockSpec((1, tm, D_MODEL), tile), pl.BlockSpec((1, tm, 1), tile),
                  _const_spec((1, LANES)), _const_spec((1, D_MODEL)),
                  _const_spec((D_MODEL, D_FF)), _const_spec((D_MODEL, D_FF)), _const_spec((D_FF, D_MODEL)),
                  _const_spec((1, D_MODEL)), _const_spec((D_MODEL, ODD_IN_PAD)),
                  _const_spec((1, Q_LORA)), _const_spec((1, KV_LORA)),
                  _const_spec((Q_LORA, C_HEADS * Q_HEAD_PAD)), _const_spec((KV_LORA, C_HEADS * (C_NOPE + C_V))),
                  _const_spec((1, Q_HEAD_PAD)), _const_spec((1, Q_HEAD_PAD))],
        out_specs=[pl.BlockSpec((1, tm, D_MODEL), tile),
                   pl.BlockSpec((1, C_HEADS, tm, C_QK), head_tile),
                   pl.BlockSpec((1, C_HEADS, tm, C_QK), head_tile),
                   pl.BlockSpec((1, tm, C_HEADS * C_V), tile)],
        out_shape=[jax.ShapeDtypeStruct((bsz, seq, D_MODEL), F32),
                   jax.ShapeDtypeStruct((bsz, C_HEADS, seq, C_QK), BF16),
                   jax.ShapeDtypeStruct((bsz, C_HEADS, seq, C_QK), BF16),
                   jax.ShapeDtypeStruct((bsz, seq, C_HEADS * C_V), BF16)],
        scratch_shapes=[pltpu.VMEM((tm, D_MODEL), F32)],
        compiler_params=pltpu.CompilerParams(dimension_semantics=("parallel", "parallel"),
                                             vmem_limit_bytes=VMEM_LIMIT_BYTES),
        name="stage1_odd",
    )(x3d, pos3d, invf, g0, wg, wu, wd, g1, win, qan, kvan, wqb, wkvb, qn, kn)


def _s2_kernel(x_ref, m_ref, p_ref, wo_ref, g2_ref, wg_ref, wu_ref, wd_ref,
               g3_ref, wpg_ref, wpp_ref, g4_ref, o_ref, acc_ref):
    x = x_ref[...] + _dot(m_ref[...], wo_ref[...])
    hn = _rms(x, g2_ref[...]).astype(BF16)
    _swiglu_into(acc_ref, hn, wg_ref, wu_ref, wd_ref)
    x = x + 0.5 * acc_ref[...]
    h3 = _rms(x, g3_ref[...]).astype(BF16)
    gate = _sigmoid(_dot(h3, wpg_ref[...]))
    proj = _dot(p_ref[...].astype(BF16), wpp_ref[...])
    o_ref[...] = x + _rms(gate * proj, g4_ref[...])


def _stage2(x2d, mixed, p2d, wo, g2, wg, wu, wd, g3, wpg, wpp, g4):
    n = x2d.shape[0]
    tm = TOKEN_TILE
    row = lambda i: (i, 0)
    return pl.pallas_call(
        _s2_kernel,
        grid=(n // tm,),
        in_specs=[pl.BlockSpec((tm, D_MODEL), row), pl.BlockSpec((tm, D_MODEL), row),
                  pl.BlockSpec((tm, D_PLE), row), _const_spec((D_MODEL, D_MODEL)),
                  _const_spec((1, D_MODEL)),
                  _const_spec((D_MODEL, D_FF)), _const_spec((D_MODEL, D_FF)), _const_spec((D_FF, D_MODEL)),
                  _const_spec((1, D_MODEL)), _const_spec((D_MODEL, D_MODEL)), _const_spec((D_PLE, D_MODEL)),
                  _const_spec((1, D_MODEL))],
        out_specs=pl.BlockSpec((tm, D_MODEL), row),
        out_shape=jax.ShapeDtypeStruct((n, D_MODEL), F32),
        scratch_shapes=[pltpu.VMEM((tm, D_MODEL), F32)],
        compiler_params=pltpu.CompilerParams(dimension_semantics=("parallel",),
                                             vmem_limit_bytes=VMEM_LIMIT_BYTES),
        name="stage2",
    )(x2d, mixed, p2d, wo, g2, wg, wu, wd, g3, wpg, wpp, g4)


def _level_map(ts):
    t = np.arange(ts)[:, None]
    s = np.arange(ts)[None, :]
    lvl = np.full((ts, ts), len(HGRN_LEVELS) + 1, np.int32)
    lvl[(t // SUBLANES == s // SUBLANES) & (s <= t)] = len(HGRN_LEVELS)
    for li, h in enumerate(HGRN_LEVELS):
        own = (t // (2 * h) == s // (2 * h)) & (t % (2 * h) >= h) & (s % (2 * h) < h)
        lvl[own] = li
    return lvl


def _even_core_kernel(u_ref, v_ref, q_ref, f_ref, i_ref, g_ref,
                      vnorm_ref, ind_ref, wcat_ref, bfull_ref, lb_ref, onorm_ref, ltri_ref, lvl_ref,
                      o_ref, state_ref):
    ts = HGRN_TILE
    n_lvl = len(HGRN_LEVELS)

    u = _gelu_tanh(u_ref[0].astype(F32))
    v = _gelu_tanh(v_ref[0].astype(F32))
    v2_hi, v2_mid, _ = _split3(v * v)
    ss = _dot(v2_hi, ind_ref[...]) + _dot(v2_mid, ind_ref[...])
    vn = (v * lax.rsqrt(ss * (1.0 / A_HEAD_DIM) + EPS) * vnorm_ref[...]).astype(BF16)
    lane_head = lax.broadcasted_iota(jnp.int32, (1, A_WIDTH), 1) // A_HEAD_DIM
    for c in range(ts // A_CHUNK):
        rows = slice(c * A_CHUNK, (c + 1) * A_CHUNK)
        vc = vn[rows, :]
        zero = jnp.zeros_like(vc)
        vbd = jnp.concatenate([jnp.where(lane_head == hd, vc, zero) for hd in range(A_HEADS)], axis=0)
        mixed = _dot(wcat_ref[...], vbd) + bfull_ref[...]
        o_ref[0, rows, 0:A_WIDTH] = (u[rows, :] * mixed).astype(BF16)

    @pl.when(pl.program_id(1) == 0)
    def _():
        state_ref[...] = jnp.zeros_like(state_ref)

    lb = lb_ref[...]
    fg = lb + (1.0 - lb) * _sigmoid(f_ref[0])
    lf = jnp.log(jnp.maximum(fg, F_MIN))
    kk = 1.0 - fg
    qq = q_ref[0].astype(F32)
    lf_hi, lf_mid, lf_lo = _split3(lf)
    ltri = ltri_ref[...]
    b = _dot(ltri, lf_hi) + _dot(ltri, lf_mid) + _dot(ltri, lf_lo)

    q_lvls, k_lvls = [], []
    for h in HGRN_LEVELS:
        q_parts, k_parts = [], []
        zero = jnp.zeros((h, B_WIDTH), F32)
        for j in range(ts // (2 * h)):
            r0 = j * 2 * h
            bm = b[r0 + h - 1:r0 + h, :]
            lo_rows = slice(r0, r0 + h)
            up_rows = slice(r0 + h, r0 + 2 * h)
            k_parts += [kk[lo_rows, :] * jnp.exp(bm - b[lo_rows, :]), zero]
            q_parts += [zero, qq[up_rows, :] * jnp.exp(b[up_rows, :] - bm)]
        q_lvls.append(jnp.concatenate(q_parts, axis=0).astype(BF16))
        k_lvls.append(jnp.concatenate(k_parts, axis=0).astype(BF16))
    b3 = b.reshape(ts // SUBLANES, SUBLANES, B_WIDTH)
    bref = jnp.broadcast_to(b3[:, 3:4, :], b3.shape).reshape(ts, B_WIDTH)
    q_lvls.append((qq * jnp.exp(b - bref)).astype(BF16))
    k_lvls.append((kk * jnp.exp(bref - b)).astype(BF16))

    b_last = b[ts - 1:ts, :]
    q_in = (qq * jnp.exp(b)).astype(BF16)
    k_out = (kk * jnp.exp(b_last - b)).astype(BF16)
    carry = jnp.exp(b_last)
    vv = i_ref[0]
    gate = g_ref[0].astype(F32)
    gate = gate * _sigmoid(gate)
    onorm = onorm_ref[...]

    lvl = lvl_ref[...]
    masks = [lvl == li for li in range(n_lvl + 1)]
    for hd in range(B_HEADS):
        cols = slice(hd * B_DIM, (hd + 1) * B_DIM)
        scores = jnp.where(masks[n_lvl], _dot_nt(q_lvls[n_lvl][:, cols], k_lvls[n_lvl][:, cols]), 0.0)
        for li in range(n_lvl):
            scores = jnp.where(masks[li], _dot_nt(q_lvls[li][:, cols], k_lvls[li][:, cols]), scores)
        v_h = vv[:, cols]
        st = state_ref[hd]
        o = _dot(scores.astype(BF16), v_h) + _dot_nt(q_in[:, cols], st.astype(BF16))
        state_ref[hd] = st * carry[:, cols] + _dot_tn(v_h, k_out[:, cols])
        on = _rms(o, onorm[:, cols]) * gate[:, cols]
        o_ref[0, :, A_WIDTH + hd * B_DIM:A_WIDTH + (hd + 1) * B_DIM] = on.astype(BF16)


def _even_core(parts, vnorm, ind, wcat, bfull, lb, onorm, ltri, lvl):
    bsz, seq, _ = parts[0].shape
    ts = HGRN_TILE
    tile = lambda b, i: (b, i, 0)
    part_spec = pl.BlockSpec((1, ts, A_WIDTH), tile)
    return pl.pallas_call(
        _even_core_kernel,
        grid=(bsz, seq // ts),
        in_specs=[part_spec] * EVEN_PARTS + [
            _const_spec((1, A_WIDTH)), _const_spec((A_WIDTH, A_WIDTH)),
            _const_spec((A_CHUNK, A_HEADS * A_CHUNK)), _const_spec((A_CHUNK, A_WIDTH)),
            _const_spec((1, B_WIDTH)), _const_spec((1, B_WIDTH)),
            _const_spec((ts, ts)), _const_spec((ts, ts))],
        out_specs=pl.BlockSpec((1, ts, D_MODEL), tile),
        out_shape=jax.ShapeDtypeStruct((bsz, seq, D_MODEL), BF16),
        scratch_shapes=[pltpu.VMEM((B_HEADS, B_DIM, B_DIM), F32)],
        compiler_params=pltpu.CompilerParams(dimension_semantics=("parallel", "arbitrary"),
                                             vmem_limit_bytes=VMEM_LIMIT_BYTES),
        name="even_core",
    )(*parts, vnorm, ind, wcat, bfull, lb, onorm, ltri, lvl)


def _attn_kernel(q_ref, k_ref, v_ref, o_ref, m_ref, l_ref, acc_ref):
    tq = ATTN_TILE
    n_chunks = tq // LANES
    qi = pl.program_id(2)

    m_ref[...] = jnp.full_like(m_ref, NEG_BIG)
    l_ref[...] = jnp.zeros_like(l_ref)
    acc_ref[...] = jnp.zeros_like(acc_ref)

    def block(j, masked):
        start = pl.multiple_of(j * tq, tq)
        if masked:
            row = lax.broadcasted_iota(jnp.int32, (tq, tq), 0)
            col = lax.broadcasted_iota(jnp.int32, (tq, tq), 1)
            keep = col <= row
        for hd in range(ATTN_HEADS_PER_STEP):
            kb = k_ref[0, hd, pl.ds(start, tq), :]
            vb = v_ref[0, pl.ds(start, tq), hd * C_V:(hd + 1) * C_V]
            s = _dot_nt(q_ref[0, hd], kb)
            if masked:
                s = jnp.where(keep, s, NEG_BIG)
            cols = [s[:, c * LANES:(c + 1) * LANES] for c in range(n_chunks)]
            cmax = functools.reduce(jnp.maximum, cols)
            m_prev = m_ref[hd]
            m_new = jnp.maximum(m_prev, jnp.max(cmax, axis=-1, keepdims=True))
            alpha = jnp.exp(m_prev - m_new)
            ps = [jnp.exp(c - m_new) for c in cols]
            psum = functools.reduce(jnp.add, ps)
            l_ref[hd] = alpha * l_ref[hd] + jnp.sum(psum, axis=-1, keepdims=True)
            p = jnp.concatenate([c.astype(BF16) for c in ps], axis=1)
            acc_ref[hd] = alpha * acc_ref[hd] + _dot(p, vb)
            m_ref[hd] = m_new

    def body(j, carry):
        block(j, False)
        return carry

    lax.fori_loop(0, qi, body, 0)
    block(qi, True)
    for hd in range(ATTN_HEADS_PER_STEP):
        o_ref[0, :, hd * C_V:(hd + 1) * C_V] = (acc_ref[hd] / l_ref[hd]).astype(BF16)


def _attention(q, k, v):
    bsz, heads, seq, _ = q.shape
    tq = ATTN_TILE
    g = ATTN_HEADS_PER_STEP
    return pl.pallas_call(
        _attn_kernel,
        grid=(bsz, heads // g, seq // tq),
        in_specs=[pl.BlockSpec((1, g, tq, C_QK), lambda b, h, i: (b, h, i, 0)),
                  pl.BlockSpec((1, g, seq, C_QK), lambda b, h, i: (b, h, 0, 0)),
                  pl.BlockSpec((1, seq, g * C_V), lambda b, h, i: (b, 0, h))],
        out_specs=pl.BlockSpec((1, tq, g * C_V), lambda b, h, i: (b, i, h)),
        out_shape=jax.ShapeDtypeStruct((bsz, seq, heads * C_V), BF16),
        scratch_shapes=[pltpu.VMEM((g, tq, LANES), F32), pltpu.VMEM((g, tq, LANES), F32),
                        pltpu.VMEM((g, tq, C_V), F32)],
        compiler_params=pltpu.CompilerParams(dimension_semantics=("parallel", "parallel", "arbitrary"),
                                             vmem_limit_bytes=VMEM_LIMIT_BYTES),
        name="mla_attention",
    )(q, k, v)


def _row(v):
    return v.reshape(1, -1).astype(F32)


def _pad_cols(w, total):
    return jnp.pad(w, ((0, 0), (0, total - w.shape[1])))


def _even_params(w_in, v_norm, w_s, b_s, lb, out_norm):
    causal = np.tril(np.ones((A_CHUNK, A_CHUNK), bool))
    w = jnp.where(causal[None], w_s, 0.0)
    wcat = jnp.transpose(w, (1, 0, 2)).reshape(A_CHUNK, A_HEADS * A_CHUNK)
    bfull = jnp.repeat(b_s.T, A_HEAD_DIM, axis=1)
    head_of = np.arange(A_WIDTH) // A_HEAD_DIM
    ind = (head_of[:, None] == head_of[None, :]).astype(np.float32)
    ltri = np.tril(np.ones((HGRN_TILE, HGRN_TILE), np.float32))
    return dict(win=w_in.astype(BF16), vnorm=_row(v_norm), ind=jnp.asarray(ind, BF16),
                wcat=wcat.astype(BF16), bfull=bfull.astype(F32), lb=_row(lb),
                onorm=_row(jnp.tile(out_norm, B_HEADS)), ltri=jnp.asarray(ltri, BF16),
                lvl=jnp.asarray(_level_map(HGRN_TILE)))


def _odd_params(w_in, q_a_norm, kv_a_norm, w_q_b, w_kv_b, q_norm, k_norm):
    win = _pad_cols(w_in, ODD_IN_PAD)
    wq = w_q_b.reshape(Q_LORA, C_HEADS, C_QK)
    wq = jnp.pad(wq, ((0, 0), (0, 0), (0, Q_HEAD_PAD - C_QK))).reshape(Q_LORA, C_HEADS * Q_HEAD_PAD)
    wkv = w_kv_b.reshape(KV_LORA, C_HEADS, C_NOPE + C_V)
    wkv = jnp.concatenate([wkv[:, :, :C_NOPE].reshape(KV_LORA, C_HEADS * C_NOPE),
                           wkv[:, :, C_NOPE:].reshape(KV_LORA, C_HEADS * C_V)], axis=1)
    inv_freq = ROPE_THETA ** (-jnp.arange(0, C_ROPE, 2, dtype=F32) / C_ROPE)
    invf = jnp.tile(inv_freq, LANES // (C_ROPE // 2)).reshape(1, LANES)
    return dict(win=win.astype(BF16), qan=_row(q_a_norm), kvan=_row(kv_a_norm),
                wqb=wq.astype(BF16), wkvb=wkv.astype(BF16),
                qn=_pad_cols(_row(q_norm), Q_HEAD_PAD), kn=_pad_cols(_row(k_norm), Q_HEAD_PAD), invf=invf)


def kernel(x, p, positions, norm_gains, ffn_w_gate, ffn_w_up, ffn_w_down, ple_w_gate, ple_w_proj,
           even_w_in, gmlp_v_norm, gmlp_w_s, gmlp_b_s, hgrn_lb_raw, hgrn_out_norm, even_w_out,
           mla_w_in, mla_q_a_norm, mla_kv_a_norm, mla_w_q_b, mla_w_kv_b, mla_q_norm, mla_k_norm, mla_w_out):
    bsz, seq, d = x.shape
    depth = norm_gains.shape[0]
    n = bsz * seq
    assert d == D_MODEL and seq % ATTN_TILE == 0 and seq % TOKEN_TILE == 0 and seq % HGRN_TILE == 0

    lb_sm = jax.nn.softmax(hgrn_lb_raw.astype(F32), axis=0)
    lower_bounds = jnp.clip(jnp.cumsum(lb_sm, axis=0) - lb_sm[0], 0.0, LB_MAX)
    pos3d = positions.astype(F32).reshape(bsz, seq, 1)

    xc = x.reshape(n, d)
    for i in range(depth):
        g = norm_gains[i]
        j = i // 2
        wg1, wu1, wd1 = (w[i, 0].astype(BF16) for w in (ffn_w_gate, ffn_w_up, ffn_w_down))
        wg2, wu2, wd2 = (w[i, 1].astype(BF16) for w in (ffn_w_gate, ffn_w_up, ffn_w_down))
        if i % 2 == 0:
            ep = _even_params(even_w_in[j], gmlp_v_norm[j], gmlp_w_s[j], gmlp_b_s[j],
                              lower_bounds[j], hgrn_out_norm[j])
            outs = _stage1_even(xc, _row(g[0]), wg1, wu1, wd1, _row(g[1]), ep["win"])
            xc = outs[0]
            parts = [o.reshape(bsz, seq, A_WIDTH) for o in outs[1:]]
            mixed = _even_core(parts, ep["vnorm"], ep["ind"], ep["wcat"], ep["bfull"], ep["lb"],
                               ep["onorm"], ep["ltri"], ep["lvl"])
            w_out = even_w_out[j]
        else:
            op = _odd_params(mla_w_in[j], mla_q_a_norm[j], mla_kv_a_norm[j], mla_w_q_b[j], mla_w_kv_b[j],
                             mla_q_norm[j], mla_k_norm[j])
            x1, q, k, v = _stage1_odd(xc.reshape(bsz, seq, d), pos3d, op["invf"], _row(g[0]), wg1, wu1, wd1,
                                      _row(g[1]), op["win"], op["qan"], op["kvan"], op["wqb"], op["wkvb"],
                                      op["qn"], op["kn"])
            xc = x1.reshape(n, d)
            mixed = _attention(q, k, v)
            w_out = mla_w_out[j]
        xc = _stage2(xc, mixed.reshape(n, d), p[i].reshape(n, D_PLE), w_out.astype(BF16), _row(g[2]),
                     wg2, wu2, wd2, _row(g[3]), ple_w_gate[i].astype(BF16), ple_w_proj[i].astype(BF16),
                     _row(g[4]))
    return xc.reshape(bsz, seq, d)
```

```python
import functools
import math

import numpy as np
import jax
import jax.numpy as jnp
from jax import lax
from jax.experimental import pallas as pl
from jax.experimental.pallas import tpu as pltpu

F32 = jnp.float32
BF16 = jnp.bfloat16

D_MODEL = 1024
D_FF = 2816
D_PLE = 256
EPS = 1e-6
NEG_BIG = -1e30
F_MIN = 1e-6
LB_MAX = 0.999

A_HEADS = 8
A_HEAD_DIM = 64
A_WIDTH = A_HEADS * A_HEAD_DIM
A_CHUNK = 128
B_HEADS = 4
B_DIM = 128
B_WIDTH = B_HEADS * B_DIM
EVEN_PARTS = 6

C_HEADS = 8
C_NOPE = 128
C_ROPE = 64
C_V = 128
C_QK = C_NOPE + C_ROPE
Q_LORA = 384
KV_LORA = 256
ODD_IN_PAD = Q_LORA + KV_LORA + 128
Q_HEAD_PAD = 256
ROPE_THETA = 10000.0
ATTN_SCALE = C_QK ** -0.5
LOG2E = math.log2(math.e)

LANES = 128
SUBLANES = 8
MXU_DIM = 256
VMEM_LIMIT_BYTES = 60000 * 1024

TOKEN_TILE = 512
FF_CHUNK = 2 * MXU_DIM
HGRN_TILE = 256
HGRN_LEVELS = (128, 64, 32, 16, 8)
ATTN_TILE = 512
ATTN_HEADS_PER_STEP = 4
ATTN_KEY_SPLIT = 2


def _ff_chunks():
    chunks, off = [], 0
    while off < D_FF:
        size = min(FF_CHUNK, D_FF - off)
        chunks.append((off, size))
        off += size
    return tuple(chunks)


def _sigmoid(x):
    return 1.0 / (1.0 + jnp.exp(-x))


def _gelu_tanh(x):
    c = math.sqrt(2.0 / math.pi)
    return x * (0.5 * (1.0 + jnp.tanh(c * (x + 0.044715 * (x * x * x)))))


def _rms(x, g):
    ms = jnp.mean(x * x, axis=-1, keepdims=True)
    return x * lax.rsqrt(ms + EPS) * g


def _dot(a, b):
    return jnp.dot(a, b, preferred_element_type=F32)


def _dot_nt(a, b):
    return lax.dot_general(a, b, (((1,), (1,)), ((), ())), preferred_element_type=F32)


def _dot_tn(a, b):
    return lax.dot_general(a, b, (((0,), (0,)), ((), ())), preferred_element_type=F32)


def _split3(x):
    hi = x.astype(BF16)
    r1 = x - hi.astype(F32)
    mid = r1.astype(BF16)
    lo = (r1 - mid.astype(F32)).astype(BF16)
    return hi, mid, lo


def _swiglu_into(acc_ref, hn, wg_ref, wu_ref, wd_ref):
    for ci, (off, size) in enumerate(_ff_chunks()):
        gate = _dot(hn, wg_ref[:, off:off + size])
        up = _dot(hn, wu_ref[:, off:off + size])
        act = (gate * _sigmoid(gate) * up).astype(BF16)
        part = _dot(act, wd_ref[off:off + size, :])
        if ci == 0:
            acc_ref[...] = part
        else:
            acc_ref[...] += part


def _s1_even_kernel(x_ref, g0_ref, wg_ref, wu_ref, wd_ref, g1_ref, win_ref,
                    x1_ref, u_ref, v_ref, q_ref, f_ref, i_ref, gg_ref, acc_ref):
    x = x_ref[...]
    hn = _rms(x, g0_ref[...]).astype(BF16)
    _swiglu_into(acc_ref, hn, wg_ref, wu_ref, wd_ref)
    x1 = x + 0.5 * acc_ref[...]
    x1_ref[...] = x1
    h = _rms(x1, g1_ref[...]).astype(BF16)
    outs = (u_ref, v_ref, q_ref, f_ref, i_ref, gg_ref)
    for k, o_ref in enumerate(outs):
        z = _dot(h, win_ref[:, k * A_WIDTH:(k + 1) * A_WIDTH])
        o_ref[...] = z.astype(o_ref.dtype)


def _const_spec(shape):
    nd = len(shape)
    return pl.BlockSpec(shape, lambda *_: (0,) * nd, pipeline_mode=pl.Buffered(1))


def _stage1_even(x2d, g0, wg, wu, wd, g1, win):
    n = x2d.shape[0]
    tm = TOKEN_TILE
    row = lambda i: (i, 0)
    part_spec = pl.BlockSpec((tm, A_WIDTH), row)
    out_shape = [jax.ShapeDtypeStruct((n, D_MODEL), F32)]
    out_shape += [jax.ShapeDtypeStruct((n, A_WIDTH), F32 if k == 3 else BF16) for k in range(EVEN_PARTS)]
    return pl.pallas_call(
        _s1_even_kernel,
        grid=(n // tm,),
        in_specs=[pl.BlockSpec((tm, D_MODEL), row), _const_spec((1, D_MODEL)),
                  _const_spec((D_MODEL, D_FF)), _const_spec((D_MODEL, D_FF)), _const_spec((D_FF, D_MODEL)),
                  _const_spec((1, D_MODEL)), _const_spec((D_MODEL, EVEN_PARTS * A_WIDTH))],
        out_specs=[pl.BlockSpec((tm, D_MODEL), row)] + [part_spec] * EVEN_PARTS,
        out_shape=out_shape,
        scratch_shapes=[pltpu.VMEM((tm, D_MODEL), F32)],
        compiler_params=pltpu.CompilerParams(dimension_semantics=("parallel",),
                                             vmem_limit_bytes=VMEM_LIMIT_BYTES),
        name="stage1_even",
    )(x2d, g0, wg, wu, wd, g1, win)


def _rope_pad(x, cos_t, sin_t):
    return x * cos_t + (pltpu.roll(x, 32, axis=1) - pltpu.roll(x, 96, axis=1)) * sin_t


def _s1_odd_kernel(x_ref, posc_ref, posr_ref, invfc_ref, invfr_ref, g0_ref, wg_ref, wu_ref, wd_ref, g1_ref,
                   win_ref, qan_ref, kvan_ref, wqbt_ref, wkvk_ref, wkvvt_ref, gq_ref, kn_ref, indk_ref, splat_ref,
                   x1_ref, qt_ref, k_ref, vt_ref, acc_ref):
    x = x_ref[0]
    hn = _rms(x, g0_ref[...]).astype(BF16)
    _swiglu_into(acc_ref, hn, wg_ref, wu_ref, wd_ref)
    x1 = x + 0.5 * acc_ref[...]
    x1_ref[0] = x1
    h = _rms(x1, g1_ref[...]).astype(BF16)
    z = _dot(h, win_ref[...])
    cqn = _rms(z[:, :Q_LORA], qan_ref[...]).astype(BF16)
    ckvn = _rms(z[:, Q_LORA:Q_LORA + KV_LORA], kvan_ref[...]).astype(BF16)
    k_rope = z[:, Q_LORA + KV_LORA:]
    q_t = _dot_nt(wqbt_ref[...], cqn)
    k_nope = _dot(ckvn, wkvk_ref[...])
    vt_ref[0] = _dot_nt(wkvvt_ref[...], ckvn).astype(BF16)

    ang_t = invfc_ref[...] * posr_ref[0]
    cos_f, sin_f = jnp.cos(ang_t), jnp.sin(ang_t)
    gq = gq_ref[...]
    half = C_ROPE // 2
    for hd in range(C_HEADS):
        blk = q_t[hd * C_QK:(hd + 1) * C_QK, :]
        ss = jnp.sum(blk * blk, axis=0, keepdims=True)
        r = lax.rsqrt(ss * (1.0 / C_QK) + EPS) * (ATTN_SCALE * LOG2E)
        sc = blk * r * gq
        a1 = sc[C_NOPE:C_NOPE + half, :]
        a2 = sc[C_NOPE + half:, :]
        qt_ref[0, hd, 0:C_NOPE, :] = sc[:C_NOPE, :].astype(BF16)
        qt_ref[0, hd, C_NOPE:C_NOPE + half, :] = (a1 * cos_f - a2 * sin_f).astype(BF16)
        qt_ref[0, hd, C_NOPE + half:C_QK, :] = (a2 * cos_f + a1 * sin_f).astype(BF16)

    lane = lax.broadcasted_iota(jnp.int32, (1, LANES), 1)
    valid = lane < C_ROPE
    ang = posc_ref[0] * invfr_ref[...]
    cos_t = jnp.where(valid, jnp.cos(ang), 0.0)
    sin_t = jnp.where(valid, jnp.sin(ang), 0.0)
    kn = kn_ref[...]
    kn_nope, kn_rope = kn[:, :C_NOPE], kn[:, C_NOPE:]
    kr_ss = jnp.sum(k_rope * k_rope, axis=-1, keepdims=True)
    kr_rot = _rope_pad(k_rope * kn_rope, cos_t, sin_t)
    k2_hi, k2_lo, _ = _split3(k_nope * k_nope)
    ssk = _dot(k2_hi, indk_ref[...]) + _dot(k2_lo, indk_ref[...])
    rk = jnp.where(lane < C_HEADS, lax.rsqrt((ssk + kr_ss) * (1.0 / C_QK) + EPS), 0.0)
    rk_hi, rk_lo, _ = _split3(rk)
    rk_b = _dot(rk_hi, splat_ref[...]) + _dot(rk_lo, splat_ref[...])
    for hd in range(C_HEADS):
        cols = slice(hd * C_NOPE, (hd + 1) * C_NOPE)
        k_ref[0, hd, :, 0:C_NOPE] = (k_nope[:, cols] * rk_b[:, cols] * kn_nope).astype(BF16)
        k_ref[0, hd, :, C_NOPE:C_QK] = (kr_rot * rk_b[:, cols])[:, :C_ROPE].astype(BF16)


def _stage1_odd(x3d, posc, posr, op, g0, wg, wu, wd, g1):
    bsz, seq, _ = x3d.shape
    tm = TOKEN_TILE
    tile = lambda b, i: (b, i, 0)
    return pl.pallas_call(
        _s1_odd_kernel,
        grid=(bsz, seq // tm),
        in_specs=[pl.BlockSpec((1, tm, D_MODEL), tile), pl.BlockSpec((1, tm, 1), tile),
                  pl.BlockSpec((1, 1, tm), lambda b, i: (b, 0, i)),
                  _const_spec((C_ROPE // 2, tm)), _const_spec((1, LANES)), _const_spec((1, D_MODEL)),
                  _const_spec((D_MODEL, D_FF)), _const_spec((D_MODEL, D_FF)), _const_spec((D_FF, D_MODEL)),
                  _const_spec((1, D_MODEL)), _const_spec((D_MODEL, ODD_IN_PAD)),
                  _const_spec((1, Q_LORA)), _const_spec((1, KV_LORA)),
                  _const_spec((C_HEADS * C_QK, Q_LORA)), _const_spec((KV_LORA, C_HEADS * C_NOPE)),
                  _const_spec((C_HEADS * C_V, KV_LORA)), _const_spec((C_QK, tm)), _const_spec((1, Q_HEAD_PAD)),
                  _const_spec((C_HEADS * C_NOPE, LANES)), _const_spec((LANES, C_HEADS * C_NOPE))],
        out_specs=[pl.BlockSpec((1, tm, D_MODEL), tile),
                   pl.BlockSpec((1, C_HEADS, C_QK, tm), lambda b, i: (b, 0, 0, i)),
                   pl.BlockSpec((1, C_HEADS, tm, C_QK), lambda b, i: (b, 0, i, 0)),
                   pl.BlockSpec((1, C_HEADS * C_V, tm), lambda b, i: (b, 0, i))],
        out_shape=[jax.ShapeDtypeStruct((bsz, seq, D_MODEL), F32),
                   jax.ShapeDtypeStruct((bsz, C_HEADS, C_QK, seq), BF16),
                   jax.ShapeDtypeStruct((bsz, C_HEADS, seq, C_QK), BF16),
                   jax.ShapeDtypeStruct((bsz, C_HEADS * C_V, seq), BF16)],
        scratch_shapes=[pltpu.VMEM((tm, D_MODEL), F32)],
        compiler_params=pltpu.CompilerParams(dimension_semantics=("parallel", "parallel"),
                                             vmem_limit_bytes=VMEM_LIMIT_BYTES),
        name="stage1_odd",
    )(x3d, posc, posr, op["invfc"], op["invfr"], g0, wg, wu, wd, g1, op["win"], op["qan"], op["kvan"],
      op["wqbt"], op["wkvk"], op["wkvvt"], op["gq"], op["kn"], op["indk"], op["splat"])


def _s2_kernel(x_ref, m_ref, p_ref, wo_ref, g2_ref, wg_ref, wu_ref, wd_ref,
               g3_ref, wpg_ref, wpp_ref, g4_ref, o_ref, acc_ref):
    x = x_ref[...] + _dot(m_ref[...], wo_ref[...])
    hn = _rms(x, g2_ref[...]).astype(BF16)
    _swiglu_into(acc_ref, hn, wg_ref, wu_ref, wd_ref)
    x = x + 0.5 * acc_ref[...]
    h3 = _rms(x, g3_ref[...]).astype(BF16)
    gate = _sigmoid(_dot(h3, wpg_ref[...]))
    proj = _dot(p_ref[...].astype(BF16), wpp_ref[...])
    o_ref[...] = x + _rms(gate * proj, g4_ref[...])


def _stage2(x2d, mixed, p2d, wo, g2, wg, wu, wd, g3, wpg, wpp, g4):
    n = x2d.shape[0]
    tm = TOKEN_TILE
    row = lambda i: (i, 0)
    return pl.pallas_call(
        _s2_kernel,
        grid=(n // tm,),
        in_specs=[pl.BlockSpec((tm, D_MODEL), row), pl.BlockSpec((tm, D_MODEL), row),
                  pl.BlockSpec((tm, D_PLE), row), _const_spec((D_MODEL, D_MODEL)),
                  _const_spec((1, D_MODEL)),
                  _const_spec((D_MODEL, D_FF)), _const_spec((D_MODEL, D_FF)), _const_spec((D_FF, D_MODEL)),
                  _const_spec((1, D_MODEL)), _const_spec((D_MODEL, D_MODEL)), _const_spec((D_PLE, D_MODEL)),
                  _const_spec((1, D_MODEL))],
        out_specs=pl.BlockSpec((tm, D_MODEL), row),
        out_shape=jax.ShapeDtypeStruct((n, D_MODEL), F32),
        scratch_shapes=[pltpu.VMEM((tm, D_MODEL), F32)],
        compiler_params=pltpu.CompilerParams(dimension_semantics=("parallel",),
                                             vmem_limit_bytes=VMEM_LIMIT_BYTES),
        name="stage2",
    )(x2d, mixed, p2d, wo, g2, wg, wu, wd, g3, wpg, wpp, g4)


def _level_map(ts):
    t = np.arange(ts)[:, None]
    s = np.arange(ts)[None, :]
    lvl = np.full((ts, ts), len(HGRN_LEVELS) + 1, np.int32)
    lvl[(t // SUBLANES == s // SUBLANES) & (s <= t)] = len(HGRN_LEVELS)
    for li, h in enumerate(HGRN_LEVELS):
        own = (t // (2 * h) == s // (2 * h)) & (t % (2 * h) >= h) & (s % (2 * h) < h)
        lvl[own] = li
    return lvl


def _even_core_kernel(u_ref, v_ref, q_ref, f_ref, i_ref, g_ref,
                      vnorm_ref, ind_ref, wcat_ref, bfull_ref, lb_ref, onorm_ref, ltri_ref, lvl_ref,
                      o_ref, state_ref):
    ts = HGRN_TILE
    n_lvl = len(HGRN_LEVELS)

    u = _gelu_tanh(u_ref[0].astype(F32))
    v = _gelu_tanh(v_ref[0].astype(F32))
    v2_hi, v2_mid, _ = _split3(v * v)
    ss = _dot(v2_hi, ind_ref[...]) + _dot(v2_mid, ind_ref[...])
    vn = (v * lax.rsqrt(ss * (1.0 / A_HEAD_DIM) + EPS) * vnorm_ref[...]).astype(BF16)
    lane_head = lax.broadcasted_iota(jnp.int32, (1, A_WIDTH), 1) // A_HEAD_DIM
    for c in range(ts // A_CHUNK):
        rows = slice(c * A_CHUNK, (c + 1) * A_CHUNK)
        vc = vn[rows, :]
        zero = jnp.zeros_like(vc)
        vbd = jnp.concatenate([jnp.where(lane_head == hd, vc, zero) for hd in range(A_HEADS)], axis=0)
        mixed = _dot(wcat_ref[...], vbd) + bfull_ref[...]
        o_ref[0, rows, 0:A_WIDTH] = (u[rows, :] * mixed).astype(BF16)

    @pl.when(pl.program_id(1) == 0)
    def _():
        state_ref[...] = jnp.zeros_like(state_ref)

    lb = lb_ref[...]
    fg = lb + (1.0 - lb) * _sigmoid(f_ref[0])
    lf = jnp.log(jnp.maximum(fg, F_MIN))
    kk = 1.0 - fg
    qq = q_ref[0].astype(F32)
    lf_hi, lf_mid, lf_lo = _split3(lf)
    ltri = ltri_ref[...]
    b = _dot(ltri, lf_hi) + _dot(ltri, lf_mid) + _dot(ltri, lf_lo)

    q_lvls, k_lvls = [], []
    for h in HGRN_LEVELS:
        q_parts, k_parts = [], []
        zero = jnp.zeros((h, B_WIDTH), F32)
        for j in range(ts // (2 * h)):
            r0 = j * 2 * h
            bm = b[r0 + h - 1:r0 + h, :]
            lo_rows = slice(r0, r0 + h)
            up_rows = slice(r0 + h, r0 + 2 * h)
            k_parts += [kk[lo_rows, :] * jnp.exp(bm - b[lo_rows, :]), zero]
            q_parts += [zero, qq[up_rows, :] * jnp.exp(b[up_rows, :] - bm)]
        q_lvls.append(jnp.concatenate(q_parts, axis=0).astype(BF16))
        k_lvls.append(jnp.concatenate(k_parts, axis=0).astype(BF16))
    b3 = b.reshape(ts // SUBLANES, SUBLANES, B_WIDTH)
    bref = jnp.broadcast_to(b3[:, 3:4, :], b3.shape).reshape(ts, B_WIDTH)
    q_lvls.append((qq * jnp.exp(b - bref)).astype(BF16))
    k_lvls.append((kk * jnp.exp(bref - b)).astype(BF16))

    b_last = b[ts - 1:ts, :]
    q_in = (qq * jnp.exp(b)).astype(BF16)
    k_out = (kk * jnp.exp(b_last - b)).astype(BF16)
    carry = jnp.exp(b_last)
    vv = i_ref[0]
    gate = g_ref[0].astype(F32)
    gate = gate * _sigmoid(gate)
    onorm = onorm_ref[...]

    lvl = lvl_ref[...]
    masks = [lvl == li for li in range(n_lvl + 1)]
    for hd in range(B_HEADS):
        cols = slice(hd * B_DIM, (hd + 1) * B_DIM)
        scores = jnp.where(masks[n_lvl], _dot_nt(q_lvls[n_lvl][:, cols], k_lvls[n_lvl][:, cols]), 0.0)
        for li in range(n_lvl):
            scores = jnp.where(masks[li], _dot_nt(q_lvls[li][:, cols], k_lvls[li][:, cols]), scores)
        v_h = vv[:, cols]
        st = state_ref[hd]
        o = _dot(scores.astype(BF16), v_h) + _dot_nt(q_in[:, cols], st.astype(BF16))
        state_ref[hd] = st * carry[:, cols] + _dot_tn(v_h, k_out[:, cols])
        on = _rms(o, onorm[:, cols]) * gate[:, cols]
        o_ref[0, :, A_WIDTH + hd * B_DIM:A_WIDTH + (hd + 1) * B_DIM] = on.astype(BF16)


def _even_core(parts, vnorm, ind, wcat, bfull, lb, onorm, ltri, lvl):
    bsz, seq, _ = parts[0].shape
    ts = HGRN_TILE
    tile = lambda b, i: (b, i, 0)
    part_spec = pl.BlockSpec((1, ts, A_WIDTH), tile)
    return pl.pallas_call(
        _even_core_kernel,
        grid=(bsz, seq // ts),
        in_specs=[part_spec] * EVEN_PARTS + [
            _const_spec((1, A_WIDTH)), _const_spec((A_WIDTH, A_WIDTH)),
            _const_spec((A_CHUNK, A_HEADS * A_CHUNK)), _const_spec((A_CHUNK, A_WIDTH)),
            _const_spec((1, B_WIDTH)), _const_spec((1, B_WIDTH)),
            _const_spec((ts, ts)), _const_spec((ts, ts))],
        out_specs=pl.BlockSpec((1, ts, D_MODEL), tile),
        out_shape=jax.ShapeDtypeStruct((bsz, seq, D_MODEL), BF16),
        scratch_shapes=[pltpu.VMEM((B_HEADS, B_DIM, B_DIM), F32)],
        compiler_params=pltpu.CompilerParams(dimension_semantics=("parallel", "arbitrary"),
                                             vmem_limit_bytes=VMEM_LIMIT_BYTES),
        name="even_core",
    )(*parts, vnorm, ind, wcat, bfull, lb, onorm, ltri, lvl)


def _attn_kernel(qt_ref, k_ref, vt_ref, o_ref, m_ref, l_ref, acc_ref):
    tq = ATTN_TILE
    tks = tq // ATTN_KEY_SPLIT
    qi = pl.program_id(2)

    m_ref[...] = jnp.full_like(m_ref, NEG_BIG)
    l_ref[...] = jnp.zeros_like(l_ref)
    acc_ref[...] = jnp.zeros_like(acc_ref)

    def block(j, masked):
        start = pl.multiple_of(j * tq, tq)
        if masked:
            key = lax.broadcasted_iota(jnp.int32, (tq, tq), 0)
            qry = lax.broadcasted_iota(jnp.int32, (tq, tq), 1)
            keep = key <= qry
        heads = range(ATTN_HEADS_PER_STEP)
        subs = [pl.ds(pl.multiple_of(start + c * tks, tks), tks) for c in range(ATTN_KEY_SPLIT)]
        s3 = {}
        for hd in heads:
            qt = qt_ref[0, hd]
            for c, sub in enumerate(subs):
                s = _dot(k_ref[0, hd, sub, :], qt)
                if masked:
                    s = jnp.where(keep[c * tks:(c + 1) * tks, :], s, NEG_BIG)
                s3[hd, c] = s.reshape(tks // SUBLANES, SUBLANES, tq)
        m_new, alpha = {}, {}
        for hd in heads:
            m_prev = m_ref[hd]
            cmax = functools.reduce(jnp.maximum, [jnp.max(s3[hd, c], axis=0) for c in range(ATTN_KEY_SPLIT)])
            cmax = jnp.max(cmax, axis=0, keepdims=True)
            m_new[hd] = jnp.maximum(m_prev, jnp.broadcast_to(cmax, (SUBLANES, tq)))
            alpha[hd] = jnp.exp2(m_prev - m_new[hd])
            m_ref[hd] = m_new[hd]
        for hd in heads:
            acc3 = acc_ref[hd].reshape(C_V // SUBLANES, SUBLANES, tq) * alpha[hd][None]
            acc = acc3.reshape(C_V, tq)
            csum = jnp.zeros((SUBLANES, tq), F32)
            for c, sub in enumerate(subs):
                p3 = jnp.exp2(s3[hd, c] - m_new[hd][None])
                csum = csum + jnp.sum(p3, axis=0)
                vtb = vt_ref[0, hd * C_V:(hd + 1) * C_V, sub]
                acc = acc + _dot(vtb, p3.reshape(tks, tq).astype(BF16))
            csum = jnp.sum(csum, axis=0, keepdims=True)
            l_ref[hd] = alpha[hd] * l_ref[hd] + jnp.broadcast_to(csum, (SUBLANES, tq))
            acc_ref[hd] = acc

    def body(j, carry):
        block(j, False)
        return carry

    lax.fori_loop(0, qi, body, 0)
    block(qi, True)
    for hd in range(ATTN_HEADS_PER_STEP):
        inv_l = 1.0 / l_ref[hd]
        out_t = acc_ref[hd].reshape(C_V // SUBLANES, SUBLANES, tq) * inv_l[None]
        o_ref[0, :, hd * C_V:(hd + 1) * C_V] = out_t.reshape(C_V, tq).T.astype(BF16)


def _attention(qt, k, vt):
    bsz, heads, seq, _ = k.shape
    tq = ATTN_TILE
    g = ATTN_HEADS_PER_STEP
    return pl.pallas_call(
        _attn_kernel,
        grid=(bsz, heads // g, seq // tq),
        in_specs=[pl.BlockSpec((1, g, C_QK, tq), lambda b, h, i: (b, h, 0, i)),
                  pl.BlockSpec((1, g, seq, C_QK), lambda b, h, i: (b, h, 0, 0)),
                  pl.BlockSpec((1, g * C_V, seq), lambda b, h, i: (b, h, 0))],
        out_specs=pl.BlockSpec((1, tq, g * C_V), lambda b, h, i: (b, i, h)),
        out_shape=jax.ShapeDtypeStruct((bsz, seq, heads * C_V), BF16),
        scratch_shapes=[pltpu.VMEM((g, SUBLANES, tq), F32), pltpu.VMEM((g, SUBLANES, tq), F32),
                        pltpu.VMEM((g, C_V, tq), F32)],
        compiler_params=pltpu.CompilerParams(dimension_semantics=("parallel", "parallel", "arbitrary"),
                                             vmem_limit_bytes=VMEM_LIMIT_BYTES),
        name="mla_attention",
    )(qt, k, vt)


def _row(v):
    return v.reshape(1, -1).astype(F32)


def _pad_cols(w, total):
    return jnp.pad(w, ((0, 0), (0, total - w.shape[1])))


def _even_params(w_in, v_norm, w_s, b_s, lb, out_norm):
    causal = np.tril(np.ones((A_CHUNK, A_CHUNK), bool))
    w = jnp.where(causal[None], w_s, 0.0)
    wcat = jnp.transpose(w, (1, 0, 2)).reshape(A_CHUNK, A_HEADS * A_CHUNK)
    bfull = jnp.repeat(b_s.T, A_HEAD_DIM, axis=1)
    head_of = np.arange(A_WIDTH) // A_HEAD_DIM
    ind = (head_of[:, None] == head_of[None, :]).astype(np.float32)
    ltri = np.tril(np.ones((HGRN_TILE, HGRN_TILE), np.float32))
    return dict(win=w_in.astype(BF16), vnorm=_row(v_norm), ind=jnp.asarray(ind, BF16),
                wcat=wcat.astype(BF16), bfull=bfull.astype(F32), lb=_row(lb),
                onorm=_row(jnp.tile(out_norm, B_HEADS)), ltri=jnp.asarray(ltri, BF16),
                lvl=jnp.asarray(_level_map(HGRN_TILE)))


def _odd_params(w_in, q_a_norm, kv_a_norm, w_q_b, w_kv_b, q_norm, k_norm):
    tm = TOKEN_TILE
    win = _pad_cols(w_in, ODD_IN_PAD)
    wkv = w_kv_b.reshape(KV_LORA, C_HEADS, C_NOPE + C_V)
    wkvk = wkv[:, :, :C_NOPE].reshape(KV_LORA, C_HEADS * C_NOPE)
    wkvvt = wkv[:, :, C_NOPE:].reshape(KV_LORA, C_HEADS * C_V).T
    inv_freq = ROPE_THETA ** (-jnp.arange(0, C_ROPE, 2, dtype=F32) / C_ROPE)
    invfr = jnp.tile(inv_freq, LANES // (C_ROPE // 2)).reshape(1, LANES)
    invfc = jnp.broadcast_to(inv_freq[:, None], (C_ROPE // 2, tm))
    head_of = np.arange(C_HEADS * C_NOPE) // C_NOPE
    indk = (head_of[:, None] == np.arange(LANES)[None, :]).astype(np.float32)
    return dict(win=win.astype(BF16), qan=_row(q_a_norm), kvan=_row(kv_a_norm),
                wqbt=w_q_b.T.astype(BF16), wkvk=wkvk.astype(BF16), wkvvt=wkvvt.astype(BF16),
                gq=jnp.broadcast_to(q_norm.astype(F32)[:, None], (C_QK, tm)),
                kn=_pad_cols(_row(k_norm), Q_HEAD_PAD), invfr=invfr, invfc=invfc,
                indk=jnp.asarray(indk, BF16), splat=jnp.asarray(indk.T, BF16))


def kernel(x, p, positions, norm_gains, ffn_w_gate, ffn_w_up, ffn_w_down, ple_w_gate, ple_w_proj,
           even_w_in, gmlp_v_norm, gmlp_w_s, gmlp_b_s, hgrn_lb_raw, hgrn_out_norm, even_w_out,
           mla_w_in, mla_q_a_norm, mla_kv_a_norm, mla_w_q_b, mla_w_kv_b, mla_q_norm, mla_k_norm, mla_w_out):
    bsz, seq, d = x.shape
    depth = norm_gains.shape[0]
    n = bsz * seq
    assert d == D_MODEL and seq % ATTN_TILE == 0 and seq % TOKEN_TILE == 0 and seq % HGRN_TILE == 0

    lb_sm = jax.nn.softmax(hgrn_lb_raw.astype(F32), axis=0)
    lower_bounds = jnp.clip(jnp.cumsum(lb_sm, axis=0) - lb_sm[0], 0.0, LB_MAX)
    posc = positions.astype(F32).reshape(bsz, seq, 1)
    posr = positions.astype(F32).reshape(bsz, 1, seq)

    xc = x.reshape(n, d)
    for i in range(depth):
        g = norm_gains[i]
        j = i // 2
        wg1, wu1, wd1 = (w[i, 0].astype(BF16) for w in (ffn_w_gate, ffn_w_up, ffn_w_down))
        wg2, wu2, wd2 = (w[i, 1].astype(BF16) for w in (ffn_w_gate, ffn_w_up, ffn_w_down))
        if i % 2 == 0:
            ep = _even_params(even_w_in[j], gmlp_v_norm[j], gmlp_w_s[j], gmlp_b_s[j],
                              lower_bounds[j], hgrn_out_norm[j])
            outs = _stage1_even(xc, _row(g[0]), wg1, wu1, wd1, _row(g[1]), ep["win"])
            xc = outs[0]
            parts = [o.reshape(bsz, seq, A_WIDTH) for o in outs[1:]]
            mixed = _even_core(parts, ep["vnorm"], ep["ind"], ep["wcat"], ep["bfull"], ep["lb"],
                               ep["onorm"], ep["ltri"], ep["lvl"])
            w_out = even_w_out[j]
        else:
            op = _odd_params(mla_w_in[j], mla_q_a_norm[j], mla_kv_a_norm[j], mla_w_q_b[j], mla_w_kv_b[j],
                             mla_q_norm[j], mla_k_norm[j])
            x1, qt, k, vt = _stage1_odd(xc.reshape(bsz, seq, d), posc, posr, op, _row(g[0]), wg1, wu1, wd1,
                                        _row(g[1]))
            xc = x1.reshape(n, d)
            mixed = _attention(qt, k, vt)
            w_out = mla_w_out[j]
        xc = _stage2(xc, mixed.reshape(n, d), p[i].reshape(n, D_PLE), w_out.astype(BF16), _row(g[2]),
                     wg2, wu2, wd2, _row(g[3]), ple_w_gate[i].astype(BF16), ple_w_proj[i].astype(BF16),
                     _row(g[4]))
    return xc.reshape(bsz, seq, d)
```

```python
import functools
import math

import numpy as np
import jax
import jax.numpy as jnp
from jax import lax
from jax.experimental import pallas as pl
from jax.experimental.pallas import tpu as pltpu

F32 = jnp.float32
BF16 = jnp.bfloat16

D_MODEL = 1024
D_FF = 2816
D_PLE = 256
EPS = 1e-6
NEG_BIG = -1e30
F_MIN = 1e-6
LB_MAX = 0.999

A_HEADS = 8
A_HEAD_DIM = 64
A_WIDTH = A_HEADS * A_HEAD_DIM
A_CHUNK = 128
B_HEADS = 4
B_DIM = 128
B_WIDTH = B_HEADS * B_DIM
EVEN_PARTS = 6

C_HEADS = 8
C_NOPE = 128
C_ROPE = 64
C_V = 128
C_QK = C_NOPE + C_ROPE
Q_LORA = 384
KV_LORA = 256
ODD_IN_PAD = Q_LORA + KV_LORA + 128
Q_HEAD_PAD = 256
ROPE_THETA = 10000.0
ATTN_SCALE = C_QK ** -0.5
LOG2E = math.log2(math.e)

LANES = 128
SUBLANES = 8
MXU_DIM = 256
VMEM_LIMIT_BYTES = 60000 * 1024

TOKEN_TILE = 512
FF_CHUNK = 2 * MXU_DIM
HGRN_TILE = 256
HGRN_LEVELS = (128, 64, 32, 16, 8)
ATTN_TILE = 512
ATTN_HEADS_PER_STEP = 4
ATTN_KEY_SPLIT = 2


def _ff_chunks():
    chunks, off = [], 0
    while off < D_FF:
        size = min(FF_CHUNK, D_FF - off)
        chunks.append((off, size))
        off += size
    return tuple(chunks)


def _sigmoid(x):
    return 0.5 + 0.5 * jnp.tanh(0.5 * x)


def _silu(x):
    t = 0.5 * x
    return t + t * jnp.tanh(t)


def _gelu_tanh(x):
    c = math.sqrt(2.0 / math.pi)
    return x * (0.5 * (1.0 + jnp.tanh(c * (x + 0.044715 * (x * x * x)))))


def _rms(x, g):
    ms = jnp.mean(x * x, axis=-1, keepdims=True)
    return x * lax.rsqrt(ms + EPS) * g


def _dot(a, b):
    return jnp.dot(a, b, preferred_element_type=F32)


def _dot_nt(a, b):
    return lax.dot_general(a, b, (((1,), (1,)), ((), ())), preferred_element_type=F32)


def _dot_tn(a, b):
    return lax.dot_general(a, b, (((0,), (0,)), ((), ())), preferred_element_type=F32)


def _split2(x):
    hi = x.astype(BF16)
    lo = (x - hi.astype(F32)).astype(BF16)
    return hi, lo


def _swiglu_into(acc_ref, hn, wg_ref, wu_ref, wd_ref):
    for ci, (off, size) in enumerate(_ff_chunks()):
        gate = _dot(hn, wg_ref[:, off:off + size])
        up = _dot(hn, wu_ref[:, off:off + size])
        act = (_silu(gate) * up).astype(BF16)
        part = _dot(act, wd_ref[off:off + size, :])
        if ci == 0:
            acc_ref[...] = part
        else:
            acc_ref[...] += part


def _s1_even_kernel(x_ref, g0_ref, wg_ref, wu_ref, wd_ref, g1_ref, win_ref,
                    x1_ref, u_ref, v_ref, q_ref, f_ref, i_ref, gg_ref, acc_ref):
    x = x_ref[...]
    hn = _rms(x, g0_ref[...]).astype(BF16)
    _swiglu_into(acc_ref, hn, wg_ref, wu_ref, wd_ref)
    x1 = x + 0.5 * acc_ref[...]
    x1_ref[...] = x1
    h = _rms(x1, g1_ref[...]).astype(BF16)
    outs = (u_ref, v_ref, q_ref, f_ref, i_ref, gg_ref)
    for k, o_ref in enumerate(outs):
        z = _dot(h, win_ref[:, k * A_WIDTH:(k + 1) * A_WIDTH])
        o_ref[...] = z.astype(o_ref.dtype)


def _const_spec(shape, prefix=()):
    nd = len(shape)
    return pl.BlockSpec((None,) * len(prefix) + tuple(shape), lambda *_: tuple(prefix) + (0,) * nd,
                        pipeline_mode=pl.Buffered(1))


def _ffn_specs(layer):
    return [_const_spec((D_MODEL, D_FF), layer), _const_spec((D_MODEL, D_FF), layer),
            _const_spec((D_FF, D_MODEL), layer)]


def _stage1_even(x2d, g0, ffn, ffn_layer, g1, win, j):
    n = x2d.shape[0]
    tm = TOKEN_TILE
    row = lambda i: (i, 0)
    part_spec = pl.BlockSpec((tm, A_WIDTH), row)
    out_shape = [jax.ShapeDtypeStruct((n, D_MODEL), F32)]
    out_shape += [jax.ShapeDtypeStruct((n, A_WIDTH), F32 if k == 3 else BF16) for k in range(EVEN_PARTS)]
    return pl.pallas_call(
        _s1_even_kernel,
        grid=(n // tm,),
        in_specs=[pl.BlockSpec((tm, D_MODEL), row), _const_spec((1, D_MODEL))] + _ffn_specs(ffn_layer) + [
                  _const_spec((1, D_MODEL)), _const_spec((D_MODEL, EVEN_PARTS * A_WIDTH), (j,))],
        out_specs=[pl.BlockSpec((tm, D_MODEL), row)] + [part_spec] * EVEN_PARTS,
        out_shape=out_shape,
        scratch_shapes=[pltpu.VMEM((tm, D_MODEL), F32)],
        compiler_params=pltpu.CompilerParams(dimension_semantics=("parallel",),
                                             vmem_limit_bytes=VMEM_LIMIT_BYTES),
        name="stage1_even",
    )(x2d, g0, *ffn, g1, win)


def _rope_pad(x, cos_t, sin_t):
    return x * cos_t + (pltpu.roll(x, 32, axis=1) - pltpu.roll(x, 96, axis=1)) * sin_t


def _s1_odd_kernel(x_ref, posc_ref, posr_ref, invfc_ref, invfr_ref, g0_ref, wg_ref, wu_ref, wd_ref, g1_ref,
                   win_ref, qan_ref, kvan_ref, wqbt_ref, wkvk_ref, wkvvt_ref, gq_ref, kn_ref, indk_ref, splat_ref,
                   x1_ref, qt_ref, k_ref, vt_ref, acc_ref):
    x = x_ref[0]
    hn = _rms(x, g0_ref[...]).astype(BF16)
    _swiglu_into(acc_ref, hn, wg_ref, wu_ref, wd_ref)
    x1 = x + 0.5 * acc_ref[...]
    x1_ref[0] = x1
    h = _rms(x1, g1_ref[...]).astype(BF16)
    z = _dot(h, win_ref[...])
    cqn = _rms(z[:, :Q_LORA], qan_ref[...]).astype(BF16)
    ckvn = _rms(z[:, Q_LORA:Q_LORA + KV_LORA], kvan_ref[...]).astype(BF16)
    k_rope = z[:, Q_LORA + KV_LORA:]
    q_t = _dot_nt(wqbt_ref[...], cqn)
    k_nope = _dot(ckvn, wkvk_ref[...])
    vt_ref[0] = _dot_nt(wkvvt_ref[...], ckvn).astype(BF16)

    ang_t = invfc_ref[...] * posr_ref[0]
    cos_f, sin_f = jnp.cos(ang_t), jnp.sin(ang_t)
    gq = gq_ref[...]
    half = C_ROPE // 2
    for hd in range(C_HEADS):
        blk = q_t[hd * C_QK:(hd + 1) * C_QK, :]
        ss = jnp.sum(blk * blk, axis=0, keepdims=True)
        r = lax.rsqrt(ss * (1.0 / C_QK) + EPS) * (ATTN_SCALE * LOG2E)
        sc = blk * r * gq
        a1 = sc[C_NOPE:C_NOPE + half, :]
        a2 = sc[C_NOPE + half:, :]
        qt_ref[0, hd, 0:C_NOPE, :] = sc[:C_NOPE, :].astype(BF16)
        qt_ref[0, hd, C_NOPE:C_NOPE + half, :] = (a1 * cos_f - a2 * sin_f).astype(BF16)
        qt_ref[0, hd, C_NOPE + half:C_QK, :] = (a2 * cos_f + a1 * sin_f).astype(BF16)

    lane = lax.broadcasted_iota(jnp.int32, (1, LANES), 1)
    valid = lane < C_ROPE
    ang = posc_ref[0] * invfr_ref[...]
    cos_t = jnp.where(valid, jnp.cos(ang), 0.0)
    sin_t = jnp.where(valid, jnp.sin(ang), 0.0)
    kn = kn_ref[...]
    kn_nope, kn_rope = kn[:, :C_NOPE], kn[:, C_NOPE:]
    kr_ss = jnp.sum(k_rope * k_rope, axis=-1, keepdims=True)
    kr_rot = _rope_pad(k_rope * kn_rope, cos_t, sin_t)
    k2_hi, k2_lo = _split2(k_nope * k_nope)
    ssk = _dot(k2_hi, indk_ref[...]) + _dot(k2_lo, indk_ref[...])
    rk = jnp.where(lane < C_HEADS, lax.rsqrt((ssk + kr_ss) * (1.0 / C_QK) + EPS), 0.0)
    rk_hi, rk_lo = _split2(rk)
    rk_b = _dot(rk_hi, splat_ref[...]) + _dot(rk_lo, splat_ref[...])
    for hd in range(C_HEADS):
        cols = slice(hd * C_NOPE, (hd + 1) * C_NOPE)
        k_ref[0, hd, :, 0:C_NOPE] = (k_nope[:, cols] * rk_b[:, cols] * kn_nope).astype(BF16)
        k_ref[0, hd, :, C_NOPE:C_QK] = (kr_rot * rk_b[:, cols])[:, :C_ROPE].astype(BF16)


def _stage1_odd(x3d, posc, posr, op, g0, ffn, ffn_layer, g1):
    bsz, seq, _ = x3d.shape
    tm = TOKEN_TILE
    tile = lambda b, i: (b, i, 0)
    return pl.pallas_call(
        _s1_odd_kernel,
        grid=(bsz, seq // tm),
        in_specs=[pl.BlockSpec((1, tm, D_MODEL), tile), pl.BlockSpec((1, tm, 1), tile),
                  pl.BlockSpec((1, 1, tm), lambda b, i: (b, 0, i)),
                  _const_spec((C_ROPE // 2, tm)), _const_spec((1, LANES)), _const_spec((1, D_MODEL))]
                 + _ffn_specs(ffn_layer) + [
                  _const_spec((1, D_MODEL)), _const_spec((D_MODEL, ODD_IN_PAD)),
                  _const_spec((1, Q_LORA)), _const_spec((1, KV_LORA)),
                  _const_spec((C_HEADS * C_QK, Q_LORA)), _const_spec((KV_LORA, C_HEADS * C_NOPE)),
                  _const_spec((C_HEADS * C_V, KV_LORA)), _const_spec((C_QK, tm)), _const_spec((1, Q_HEAD_PAD)),
                  _const_spec((C_HEADS * C_NOPE, LANES)), _const_spec((LANES, C_HEADS * C_NOPE))],
        out_specs=[pl.BlockSpec((1, tm, D_MODEL), tile),
                   pl.BlockSpec((1, C_HEADS, C_QK, tm), lambda b, i: (b, 0, 0, i)),
                   pl.BlockSpec((1, C_HEADS, tm, C_QK), lambda b, i: (b, 0, i, 0)),
                   pl.BlockSpec((1, C_HEADS * C_V, tm), lambda b, i: (b, 0, i))],
        out_shape=[jax.ShapeDtypeStruct((bsz, seq, D_MODEL), F32),
                   jax.ShapeDtypeStruct((bsz, C_HEADS, C_QK, seq), BF16),
                   jax.ShapeDtypeStruct((bsz, C_HEADS, seq, C_QK), BF16),
                   jax.ShapeDtypeStruct((bsz, C_HEADS * C_V, seq), BF16)],
        scratch_shapes=[pltpu.VMEM((tm, D_MODEL), F32)],
        compiler_params=pltpu.CompilerParams(dimension_semantics=("parallel", "parallel"),
                                             vmem_limit_bytes=VMEM_LIMIT_BYTES),
        name="stage1_odd",
    )(x3d, posc, posr, op["invfc"], op["invfr"], g0, *ffn, g1, op["win"], op["qan"], op["kvan"],
      op["wqbt"], op["wkvk"], op["wkvvt"], op["gq"], op["kn"], op["indk"], op["splat"])


def _s2_kernel(x_ref, m_ref, p_ref, wo_ref, g2_ref, wg_ref, wu_ref, wd_ref,
               g3_ref, wpg_ref, wpp_ref, g4_ref, o_ref, acc_ref):
    x = x_ref[...] + _dot(m_ref[...], wo_ref[...])
    hn = _rms(x, g2_ref[...]).astype(BF16)
    _swiglu_into(acc_ref, hn, wg_ref, wu_ref, wd_ref)
    x = x + 0.5 * acc_ref[...]
    h3 = _rms(x, g3_ref[...]).astype(BF16)
    gate = _sigmoid(_dot(h3, wpg_ref[...]))
    proj = _dot(p_ref[...].astype(BF16), wpp_ref[...])
    o_ref[...] = x + _rms(gate * proj, g4_ref[...])


def _stage2(x2d, mixed, p3d, layer, wo, j, g2, ffn, ffn_layer, g3, wpg, wpp, g4):
    n = x2d.shape[0]
    tm = TOKEN_TILE
    row = lambda i: (i, 0)
    return pl.pallas_call(
        _s2_kernel,
        grid=(n // tm,),
        in_specs=[pl.BlockSpec((tm, D_MODEL), row), pl.BlockSpec((tm, D_MODEL), row),
                  pl.BlockSpec((None, tm, D_PLE), lambda i: (layer, i, 0)),
                  _const_spec((D_MODEL, D_MODEL), (j,)), _const_spec((1, D_MODEL))] + _ffn_specs(ffn_layer) + [
                  _const_spec((1, D_MODEL)), _const_spec((D_MODEL, D_MODEL), (layer,)),
                  _const_spec((D_PLE, D_MODEL), (layer,)), _const_spec((1, D_MODEL))],
        out_specs=pl.BlockSpec((tm, D_MODEL), row),
        out_shape=jax.ShapeDtypeStruct((n, D_MODEL), F32),
        scratch_shapes=[pltpu.VMEM((tm, D_MODEL), F32)],
        compiler_params=pltpu.CompilerParams(dimension_semantics=("parallel",),
                                             vmem_limit_bytes=VMEM_LIMIT_BYTES),
        name="stage2",
    )(x2d, mixed, p3d, wo, g2, *ffn, g3, wpg, wpp, g4)


def _level_map(ts):
    t = np.arange(ts)[:, None]
    s = np.arange(ts)[None, :]
    lvl = np.full((ts, ts), len(HGRN_LEVELS) + 1, np.int32)
    lvl[(t // SUBLANES == s // SUBLANES) & (s <= t)] = len(HGRN_LEVELS)
    for li, h in enumerate(HGRN_LEVELS):
        own = (t // (2 * h) == s // (2 * h)) & (t % (2 * h) >= h) & (s % (2 * h) < h)
        lvl[own] = li
    return lvl


def _even_core_kernel(u_ref, v_ref, q_ref, f_ref, i_ref, g_ref,
                      vnorm_ref, ind_ref, wcat_ref, bfull_ref, lb_ref, onorm_ref, ltri_ref, lvl_ref,
                      o_ref, state_ref):
    ts = HGRN_TILE
    n_lvl = len(HGRN_LEVELS)

    u = _gelu_tanh(u_ref[0].astype(F32))
    v = _gelu_tanh(v_ref[0].astype(F32))
    v2_hi, v2_lo = _split2(v * v)
    ss = _dot(v2_hi, ind_ref[...]) + _dot(v2_lo, ind_ref[...])
    vn = (v * lax.rsqrt(ss * (1.0 / A_HEAD_DIM) + EPS) * vnorm_ref[...]).astype(BF16)
    lane_head = lax.broadcasted_iota(jnp.int32, (1, A_WIDTH), 1) // A_HEAD_DIM
    for c in range(ts // A_CHUNK):
        rows = slice(c * A_CHUNK, (c + 1) * A_CHUNK)
        vc = vn[rows, :]
        zero = jnp.zeros_like(vc)
        vbd = jnp.concatenate([jnp.where(lane_head == hd, vc, zero) for hd in range(A_HEADS)], axis=0)
        mixed = _dot(wcat_ref[...], vbd) + bfull_ref[...]
        o_ref[0, rows, 0:A_WIDTH] = (u[rows, :] * mixed).astype(BF16)

    @pl.when(pl.program_id(1) == 0)
    def _():
        state_ref[...] = jnp.zeros_like(state_ref)

    lb = lb_ref[...]
    fg = lb + (1.0 - lb) * _sigmoid(f_ref[0])
    lf = jnp.log(jnp.maximum(fg, F_MIN))
    kk = 1.0 - fg
    qq = q_ref[0].astype(F32)
    lf_hi, lf_lo = _split2(lf)
    ltri = ltri_ref[...]
    b = _dot(ltri, lf_hi) + _dot(ltri, lf_lo)

    q_lvls, k_lvls = [], []
    for h in HGRN_LEVELS:
        q_parts, k_parts = [], []
        zero = jnp.zeros((h, B_WIDTH), F32)
        for j in range(ts // (2 * h)):
            r0 = j * 2 * h
            bm = b[r0 + h - 1:r0 + h, :]
            lo_rows = slice(r0, r0 + h)
            up_rows = slice(r0 + h, r0 + 2 * h)
            k_parts += [kk[lo_rows, :] * jnp.exp(bm - b[lo_rows, :]), zero]
            q_parts += [zero, qq[up_rows, :] * jnp.exp(b[up_rows, :] - bm)]
        q_lvls.append(jnp.concatenate(q_parts, axis=0).astype(BF16))
        k_lvls.append(jnp.concatenate(k_parts, axis=0).astype(BF16))
    b3 = b.reshape(ts // SUBLANES, SUBLANES, B_WIDTH)
    bref = jnp.broadcast_to(b3[:, 3:4, :], b3.shape).reshape(ts, B_WIDTH)
    q_lvls.append((qq * jnp.exp(b - bref)).astype(BF16))
    k_lvls.append((kk * jnp.exp(bref - b)).astype(BF16))

    b_last = b[ts - 1:ts, :]
    q_in = (qq * jnp.exp(b)).astype(BF16)
    k_out = (kk * jnp.exp(b_last - b)).astype(BF16)
    carry = jnp.exp(b_last)
    vv = i_ref[0]
    gate = _silu(g_ref[0].astype(F32))
    onorm = onorm_ref[...]

    lvl = lvl_ref[...]
    masks = [lvl == li for li in range(n_lvl + 1)]
    for hd in range(B_HEADS):
        cols = slice(hd * B_DIM, (hd + 1) * B_DIM)
        scores = jnp.where(masks[n_lvl], _dot_nt(q_lvls[n_lvl][:, cols], k_lvls[n_lvl][:, cols]), 0.0)
        for li in range(n_lvl):
            scores = jnp.where(masks[li], _dot_nt(q_lvls[li][:, cols], k_lvls[li][:, cols]), scores)
        v_h = vv[:, cols]
        st = state_ref[hd]
        o = _dot(scores.astype(BF16), v_h) + _dot_nt(q_in[:, cols], st.astype(BF16))
        state_ref[hd] = st * carry[:, cols] + _dot_tn(v_h, k_out[:, cols])
        on = _rms(o, onorm[:, cols]) * gate[:, cols]
        o_ref[0, :, A_WIDTH + hd * B_DIM:A_WIDTH + (hd + 1) * B_DIM] = on.astype(BF16)


def _even_core(parts, vnorm, ind, wcat, bfull, lb, onorm, ltri, lvl):
    bsz, seq, _ = parts[0].shape
    ts = HGRN_TILE
    tile = lambda b, i: (b, i, 0)
    part_spec = pl.BlockSpec((1, ts, A_WIDTH), tile)
    return pl.pallas_call(
        _even_core_kernel,
        grid=(bsz, seq // ts),
        in_specs=[part_spec] * EVEN_PARTS + [
            _const_spec((1, A_WIDTH)), _const_spec((A_WIDTH, A_WIDTH)),
            _const_spec((A_CHUNK, A_HEADS * A_CHUNK)), _const_spec((A_CHUNK, A_WIDTH)),
            _const_spec((1, B_WIDTH)), _const_spec((1, B_WIDTH)),
            _const_spec((ts, ts)), _const_spec((ts, ts))],
        out_specs=pl.BlockSpec((1, ts, D_MODEL), tile),
        out_shape=jax.ShapeDtypeStruct((bsz, seq, D_MODEL), BF16),
        scratch_shapes=[pltpu.VMEM((B_HEADS, B_DIM, B_DIM), F32)],
        compiler_params=pltpu.CompilerParams(dimension_semantics=("parallel", "arbitrary"),
                                             vmem_limit_bytes=VMEM_LIMIT_BYTES),
        name="even_core",
    )(*parts, vnorm, ind, wcat, bfull, lb, onorm, ltri, lvl)


def _attn_kernel(qt_ref, k_ref, vt_ref, o_ref, m_ref, l_ref, acc_ref):
    tq = ATTN_TILE
    tks = tq // ATTN_KEY_SPLIT
    qi = pl.program_id(2)

    m_ref[...] = jnp.full_like(m_ref, NEG_BIG)
    l_ref[...] = jnp.zeros_like(l_ref)
    acc_ref[...] = jnp.zeros_like(acc_ref)

    def block(j, masked):
        start = pl.multiple_of(j * tq, tq)
        if masked:
            key = lax.broadcasted_iota(jnp.int32, (tq, tq), 0)
            qry = lax.broadcasted_iota(jnp.int32, (tq, tq), 1)
            keep = key <= qry
        heads = range(ATTN_HEADS_PER_STEP)
        subs = [pl.ds(pl.multiple_of(start + c * tks, tks), tks) for c in range(ATTN_KEY_SPLIT)]
        s3 = {}
        for hd in heads:
            qt = qt_ref[0, hd]
            for c, sub in enumerate(subs):
                s = _dot(k_ref[0, hd, sub, :], qt)
                if masked:
                    s = jnp.where(keep[c * tks:(c + 1) * tks, :], s, NEG_BIG)
                s3[hd, c] = s.reshape(tks // SUBLANES, SUBLANES, tq)
        m_new, alpha = {}, {}
        for hd in heads:
            m_prev = m_ref[hd]
            cmax = functools.reduce(jnp.maximum, [jnp.max(s3[hd, c], axis=0) for c in range(ATTN_KEY_SPLIT)])
            cmax = jnp.max(cmax, axis=0, keepdims=True)
            m_new[hd] = jnp.maximum(m_prev, jnp.broadcast_to(cmax, (SUBLANES, tq)))
            alpha[hd] = jnp.exp2(m_prev - m_new[hd])
            m_ref[hd] = m_new[hd]
        ones = jnp.ones((2 * SUBLANES, tks), BF16)
        for hd in heads:
            acc3 = acc_ref[hd].reshape(C_V // SUBLANES, SUBLANES, tq) * alpha[hd][None]
            acc = acc3.reshape(C_V, tq)
            csum = jnp.zeros((SUBLANES, tq), F32)
            for c, sub in enumerate(subs):
                p = jnp.exp2(s3[hd, c] - m_new[hd][None]).reshape(tks, tq).astype(BF16)
                lhs = jnp.concatenate([vt_ref[0, hd * C_V:(hd + 1) * C_V, sub], ones], axis=0)
                pv = _dot(lhs, p)
                acc = acc + pv[:C_V, :]
                csum = csum + pv[C_V:C_V + SUBLANES, :]
            l_ref[hd] = alpha[hd] * l_ref[hd] + csum
            acc_ref[hd] = acc

    def body(j, carry):
        block(j, False)
        return carry

    lax.fori_loop(0, qi, body, 0)
    block(qi, True)
    for hd in range(ATTN_HEADS_PER_STEP):
        inv_l = 1.0 / l_ref[hd]
        out_t = acc_ref[hd].reshape(C_V // SUBLANES, SUBLANES, tq) * inv_l[None]
        o_ref[0, :, hd * C_V:(hd + 1) * C_V] = out_t.reshape(C_V, tq).T.astype(BF16)


def _attention(qt, k, vt):
    bsz, heads, seq, _ = k.shape
    tq = ATTN_TILE
    g = ATTN_HEADS_PER_STEP
    return pl.pallas_call(
        _attn_kernel,
        grid=(bsz, heads // g, seq // tq),
        in_specs=[pl.BlockSpec((1, g, C_QK, tq), lambda b, h, i: (b, h, 0, i)),
                  pl.BlockSpec((1, g, seq, C_QK), lambda b, h, i: (b, h, 0, 0)),
                  pl.BlockSpec((1, g * C_V, seq), lambda b, h, i: (b, h, 0))],
        out_specs=pl.BlockSpec((1, tq, g * C_V), lambda b, h, i: (b, i, h)),
        out_shape=jax.ShapeDtypeStruct((bsz, seq, heads * C_V), BF16),
        scratch_shapes=[pltpu.VMEM((g, SUBLANES, tq), F32), pltpu.VMEM((g, SUBLANES, tq), F32),
                        pltpu.VMEM((g, C_V, tq), F32)],
        compiler_params=pltpu.CompilerParams(dimension_semantics=("parallel", "parallel", "arbitrary"),
                                             vmem_limit_bytes=VMEM_LIMIT_BYTES),
        name="mla_attention",
    )(qt, k, vt)


def _row(v):
    return v.reshape(1, -1).astype(F32)


def _pad_cols(w, total):
    return jnp.pad(w, ((0, 0), (0, total - w.shape[1])))


def _even_params(v_norm, w_s, b_s, lb, out_norm):
    causal = np.tril(np.ones((A_CHUNK, A_CHUNK), bool))
    w = jnp.where(causal[None], w_s, 0.0)
    wcat = jnp.transpose(w, (1, 0, 2)).reshape(A_CHUNK, A_HEADS * A_CHUNK)
    bfull = jnp.repeat(b_s.T, A_HEAD_DIM, axis=1)
    head_of = np.arange(A_WIDTH) // A_HEAD_DIM
    ind = (head_of[:, None] == head_of[None, :]).astype(np.float32)
    ltri = np.tril(np.ones((HGRN_TILE, HGRN_TILE), np.float32))
    return dict(vnorm=_row(v_norm), ind=jnp.asarray(ind, BF16),
                wcat=wcat.astype(BF16), bfull=bfull.astype(F32), lb=_row(lb),
                onorm=_row(jnp.tile(out_norm, B_HEADS)), ltri=jnp.asarray(ltri, BF16),
                lvl=jnp.asarray(_level_map(HGRN_TILE)))


def _odd_params(w_in, q_a_norm, kv_a_norm, w_q_b, w_kv_b, q_norm, k_norm):
    tm = TOKEN_TILE
    win = _pad_cols(w_in, ODD_IN_PAD)
    wkv = w_kv_b.reshape(KV_LORA, C_HEADS, C_NOPE + C_V)
    wkvk = wkv[:, :, :C_NOPE].reshape(KV_LORA, C_HEADS * C_NOPE)
    wkvvt = wkv[:, :, C_NOPE:].reshape(KV_LORA, C_HEADS * C_V).T
    inv_freq = ROPE_THETA ** (-jnp.arange(0, C_ROPE, 2, dtype=F32) / C_ROPE)
    invfr = jnp.tile(inv_freq, LANES // (C_ROPE // 2)).reshape(1, LANES)
    invfc = jnp.broadcast_to(inv_freq[:, None], (C_ROPE // 2, tm))
    head_of = np.arange(C_HEADS * C_NOPE) // C_NOPE
    indk = (head_of[:, None] == np.arange(LANES)[None, :]).astype(np.float32)
    return dict(win=win.astype(BF16), qan=_row(q_a_norm), kvan=_row(kv_a_norm),
                wqbt=w_q_b.T.astype(BF16), wkvk=wkvk.astype(BF16), wkvvt=wkvvt.astype(BF16),
                gq=jnp.broadcast_to(q_norm.astype(F32)[:, None], (C_QK, tm)),
                kn=_pad_cols(_row(k_norm), Q_HEAD_PAD), invfr=invfr, invfc=invfc,
                indk=jnp.asarray(indk, BF16), splat=jnp.asarray(indk.T, BF16))


def kernel(x, p, positions, norm_gains, ffn_w_gate, ffn_w_up, ffn_w_down, ple_w_gate, ple_w_proj,
           even_w_in, gmlp_v_norm, gmlp_w_s, gmlp_b_s, hgrn_lb_raw, hgrn_out_norm, even_w_out,
           mla_w_in, mla_q_a_norm, mla_kv_a_norm, mla_w_q_b, mla_w_kv_b, mla_q_norm, mla_k_norm, mla_w_out):
    bsz, seq, d = x.shape
    depth = norm_gains.shape[0]
    n = bsz * seq
    assert d == D_MODEL and seq % ATTN_TILE == 0 and seq % TOKEN_TILE == 0 and seq % HGRN_TILE == 0

    lb_sm = jax.nn.softmax(hgrn_lb_raw.astype(F32), axis=0)
    lower_bounds = jnp.clip(jnp.cumsum(lb_sm, axis=0) - lb_sm[0], 0.0, LB_MAX)
    posc = positions.astype(F32).reshape(bsz, seq, 1)
    posr = positions.astype(F32).reshape(bsz, 1, seq)

    ffn = tuple(w.astype(BF16) for w in (ffn_w_gate, ffn_w_up, ffn_w_down))
    ple_wg, ple_wp = ple_w_gate.astype(BF16), ple_w_proj.astype(BF16)
    even_win, even_wout, mla_wout = (w.astype(BF16) for w in (even_w_in, even_w_out, mla_w_out))
    p3d = p.reshape(depth, n, D_PLE)

    xc = x.reshape(n, d)
    for i in range(depth):
        g = norm_gains[i]
        j = i // 2
        if i % 2 == 0:
            ep = _even_params(gmlp_v_norm[j], gmlp_w_s[j], gmlp_b_s[j], lower_bounds[j], hgrn_out_norm[j])
            outs = _stage1_even(xc, _row(g[0]), ffn, (i, 0), _row(g[1]), even_win, j)
            xc = outs[0]
            parts = [o.reshape(bsz, seq, A_WIDTH) for o in outs[1:]]
            mixed = _even_core(parts, ep["vnorm"], ep["ind"], ep["wcat"], ep["bfull"], ep["lb"],
                               ep["onorm"], ep["ltri"], ep["lvl"])
            w_out = even_wout
        else:
            op = _odd_params(mla_w_in[j], mla_q_a_norm[j], mla_kv_a_norm[j], mla_w_q_b[j], mla_w_kv_b[j],
                             mla_q_norm[j], mla_k_norm[j])
            x1, qt, k, vt = _stage1_odd(xc.reshape(bsz, seq, d), posc, posr, op, _row(g[0]), ffn, (i, 0),
                                        _row(g[1]))
            xc = x1.reshape(n, d)
            mixed = _attention(qt, k, vt)
            w_out = mla_wout
        xc = _stage2(xc, mixed.reshape(n, d), p3d, i, w_out, j, _row(g[2]), ffn, (i, 1), _row(g[3]),
                     ple_wg, ple_wp, _row(g[4]))
    return xc.reshape(bsz, seq, d)
```

```python
import functools
import math

import numpy as np
import jax
import jax.numpy as jnp
from jax import lax
from jax.experimental import pallas as pl
from jax.experimental.pallas import tpu as pltpu

F32 = jnp.float32
BF16 = jnp.bfloat16

D_MODEL = 1024
D_FF = 2816
D_PLE = 256
EPS = 1e-6
NEG_BIG = -1e30
F_MIN = 1e-6
LB_MAX = 0.999

A_HEADS = 8
A_HEAD_DIM = 64
A_WIDTH = A_HEADS * A_HEAD_DIM
A_CHUNK = 128
B_HEADS = 4
B_DIM = 128
B_WIDTH = B_HEADS * B_DIM
EVEN_PARTS = 6

C_HEADS = 8
C_NOPE = 128
C_ROPE = 64
C_V = 128
C_QK = C_NOPE + C_ROPE
Q_LORA = 384
KV_LORA = 256
ODD_IN_PAD = Q_LORA + KV_LORA + 128
Q_HEAD_PAD = 256
ROPE_THETA = 10000.0
ATTN_SCALE = C_QK ** -0.5
LOG2E = math.log2(math.e)

LANES = 128
SUBLANES = 8
MXU_DIM = 256
VMEM_LIMIT_BYTES = 60000 * 1024

TOKEN_TILE = 512
FF_CHUNK = 2 * MXU_DIM
HGRN_TILE = 256
HGRN_LEVELS = (128, 64, 32, 16, 8)
ATTN_TILE = 512
ATTN_HEADS_PER_STEP = 4
ATTN_KEY_SPLIT = 2


def _ff_chunks():
    chunks, off = [], 0
    while off < D_FF:
        size = min(FF_CHUNK, D_FF - off)
        chunks.append((off, size))
        off += size
    return tuple(chunks)


def _sigmoid(x):
    return 0.5 + 0.5 * jnp.tanh(0.5 * x)


def _silu(x):
    t = 0.5 * x
    return t + t * jnp.tanh(t)


def _gelu_tanh(x):
    c = math.sqrt(2.0 / math.pi)
    return x * (0.5 * (1.0 + jnp.tanh(c * (x + 0.044715 * (x * x * x)))))


def _rms(x, g):
    ms = jnp.mean(x * x, axis=-1, keepdims=True)
    return x * lax.rsqrt(ms + EPS) * g


def _dot(a, b):
    return jnp.dot(a, b, preferred_element_type=F32)


def _dot_nt(a, b):
    return lax.dot_general(a, b, (((1,), (1,)), ((), ())), preferred_element_type=F32)


def _dot_tn(a, b):
    return lax.dot_general(a, b, (((0,), (0,)), ((), ())), preferred_element_type=F32)


def _split2(x):
    hi = x.astype(BF16)
    lo = (x - hi.astype(F32)).astype(BF16)
    return hi, lo


def _swiglu_into(acc_ref, hn, wg_ref, wu_ref, wd_ref):
    for ci, (off, size) in enumerate(_ff_chunks()):
        gate = _dot(hn, wg_ref[:, off:off + size])
        up = _dot(hn, wu_ref[:, off:off + size])
        act = (_silu(gate) * up).astype(BF16)
        part = _dot(act, wd_ref[off:off + size, :])
        if ci == 0:
            acc_ref[...] = part
        else:
            acc_ref[...] += part


def _s1_even_kernel(x_ref, g0_ref, wg_ref, wu_ref, wd_ref, g1_ref, win_ref,
                    x1_ref, u_ref, v_ref, q_ref, f_ref, i_ref, gg_ref, acc_ref):
    x = x_ref[...]
    hn = _rms(x, g0_ref[...]).astype(BF16)
    _swiglu_into(acc_ref, hn, wg_ref, wu_ref, wd_ref)
    x1 = x + 0.5 * acc_ref[...]
    x1_ref[...] = x1
    h = _rms(x1, g1_ref[...]).astype(BF16)
    outs = (u_ref, v_ref, q_ref, f_ref, i_ref, gg_ref)
    for k, o_ref in enumerate(outs):
        z = _dot(h, win_ref[:, k * A_WIDTH:(k + 1) * A_WIDTH])
        o_ref[...] = z.astype(o_ref.dtype)


def _const_spec(shape, prefix=()):
    nd = len(shape)
    return pl.BlockSpec((None,) * len(prefix) + tuple(shape), lambda *_: tuple(prefix) + (0,) * nd,
                        pipeline_mode=pl.Buffered(1))


def _ffn_specs(layer):
    return [_const_spec((D_MODEL, D_FF), layer), _const_spec((D_MODEL, D_FF), layer),
            _const_spec((D_FF, D_MODEL), layer)]


def _stage1_even(x2d, g0, ffn, ffn_layer, g1, win, j):
    n = x2d.shape[0]
    tm = TOKEN_TILE
    row = lambda i: (i, 0)
    part_spec = pl.BlockSpec((tm, A_WIDTH), row)
    out_shape = [jax.ShapeDtypeStruct((n, D_MODEL), F32)]
    out_shape += [jax.ShapeDtypeStruct((n, A_WIDTH), F32 if k == 3 else BF16) for k in range(EVEN_PARTS)]
    return pl.pallas_call(
        _s1_even_kernel,
        grid=(n // tm,),
        in_specs=[pl.BlockSpec((tm, D_MODEL), row), _const_spec((1, D_MODEL))] + _ffn_specs(ffn_layer) + [
                  _const_spec((1, D_MODEL)), _const_spec((D_MODEL, EVEN_PARTS * A_WIDTH), (j,))],
        out_specs=[pl.BlockSpec((tm, D_MODEL), row)] + [part_spec] * EVEN_PARTS,
        out_shape=out_shape,
        scratch_shapes=[pltpu.VMEM((tm, D_MODEL), F32)],
        compiler_params=pltpu.CompilerParams(dimension_semantics=("parallel",),
                                             vmem_limit_bytes=VMEM_LIMIT_BYTES),
        name="stage1_even",
    )(x2d, g0, *ffn, g1, win)


def _rope_pad(x, cos_t, sin_t):
    return x * cos_t + (pltpu.roll(x, 32, axis=1) - pltpu.roll(x, 96, axis=1)) * sin_t


def _s1_odd_kernel(x_ref, posc_ref, posr_ref, invfc_ref, invfr_ref, g0_ref, wg_ref, wu_ref, wd_ref, g1_ref,
                   win_ref, qan_ref, kvan_ref, wqbt_ref, wkvk_ref, wkvvt_ref, gq_ref, kn_ref, indk_ref, splat_ref,
                   x1_ref, qt_ref, k_ref, vt_ref, acc_ref):
    x = x_ref[0]
    hn = _rms(x, g0_ref[...]).astype(BF16)
    _swiglu_into(acc_ref, hn, wg_ref, wu_ref, wd_ref)
    x1 = x + 0.5 * acc_ref[...]
    x1_ref[0] = x1
    h = _rms(x1, g1_ref[...]).astype(BF16)
    z = _dot(h, win_ref[...])
    cqn = _rms(z[:, :Q_LORA], qan_ref[...]).astype(BF16)
    ckvn = _rms(z[:, Q_LORA:Q_LORA + KV_LORA], kvan_ref[...]).astype(BF16)
    k_rope = z[:, Q_LORA + KV_LORA:]
    q_t = _dot_nt(wqbt_ref[...], cqn)
    k_nope = _dot(ckvn, wkvk_ref[...])
    vt_ref[0] = _dot_nt(wkvvt_ref[...], ckvn).astype(BF16)

    ang_t = invfc_ref[...] * posr_ref[0]
    cos_f, sin_f = jnp.cos(ang_t), jnp.sin(ang_t)
    gq = gq_ref[...]
    half = C_ROPE // 2
    for hd in range(C_HEADS):
        blk = q_t[hd * C_QK:(hd + 1) * C_QK, :]
        ss = jnp.sum(blk * blk, axis=0, keepdims=True)
        r = lax.rsqrt(ss * (1.0 / C_QK) + EPS) * (ATTN_SCALE * LOG2E)
        sc = blk * r * gq
        a1 = sc[C_NOPE:C_NOPE + half, :]
        a2 = sc[C_NOPE + half:, :]
        qt_ref[0, hd, 0:C_NOPE, :] = sc[:C_NOPE, :].astype(BF16)
        qt_ref[0, hd, C_NOPE:C_NOPE + half, :] = (a1 * cos_f - a2 * sin_f).astype(BF16)
        qt_ref[0, hd, C_NOPE + half:C_QK, :] = (a2 * cos_f + a1 * sin_f).astype(BF16)

    lane = lax.broadcasted_iota(jnp.int32, (1, LANES), 1)
    valid = lane < C_ROPE
    ang = posc_ref[0] * invfr_ref[...]
    cos_t = jnp.where(valid, jnp.cos(ang), 0.0)
    sin_t = jnp.where(valid, jnp.sin(ang), 0.0)
    kn = kn_ref[...]
    kn_nope, kn_rope = kn[:, :C_NOPE], kn[:, C_NOPE:]
    kr_ss = jnp.sum(k_rope * k_rope, axis=-1, keepdims=True)
    kr_rot = _rope_pad(k_rope * kn_rope, cos_t, sin_t)
    k2_hi, k2_lo = _split2(k_nope * k_nope)
    ssk = _dot(k2_hi, indk_ref[...]) + _dot(k2_lo, indk_ref[...])
    rk = jnp.where(lane < C_HEADS, lax.rsqrt((ssk + kr_ss) * (1.0 / C_QK) + EPS), 0.0)
    rk_hi, rk_lo = _split2(rk)
    rk_b = _dot(rk_hi, splat_ref[...]) + _dot(rk_lo, splat_ref[...])
    for hd in range(C_HEADS):
        cols = slice(hd * C_NOPE, (hd + 1) * C_NOPE)
        k_ref[0, hd, :, 0:C_NOPE] = (k_nope[:, cols] * rk_b[:, cols] * kn_nope).astype(BF16)
        k_ref[0, hd, :, C_NOPE:C_QK] = (kr_rot * rk_b[:, cols])[:, :C_ROPE].astype(BF16)


def _stage1_odd(x3d, posc, posr, op, g0, ffn, ffn_layer, g1):
    bsz, seq, _ = x3d.shape
    tm = TOKEN_TILE
    tile = lambda b, i: (b, i, 0)
    return pl.pallas_call(
        _s1_odd_kernel,
        grid=(bsz, seq // tm),
        in_specs=[pl.BlockSpec((1, tm, D_MODEL), tile), pl.BlockSpec((1, tm, 1), tile),
                  pl.BlockSpec((1, 1, tm), lambda b, i: (b, 0, i)),
                  _const_spec((C_ROPE // 2, tm)), _const_spec((1, LANES)), _const_spec((1, D_MODEL))]
                 + _ffn_specs(ffn_layer) + [
                  _const_spec((1, D_MODEL)), _const_spec((D_MODEL, ODD_IN_PAD)),
                  _const_spec((1, Q_LORA)), _const_spec((1, KV_LORA)),
                  _const_spec((C_HEADS * C_QK, Q_LORA)), _const_spec((KV_LORA, C_HEADS * C_NOPE)),
                  _const_spec((C_HEADS * C_V, KV_LORA)), _const_spec((C_QK, tm)), _const_spec((1, Q_HEAD_PAD)),
                  _const_spec((C_HEADS * C_NOPE, LANES)), _const_spec((LANES, C_HEADS * C_NOPE))],
        out_specs=[pl.BlockSpec((1, tm, D_MODEL), tile),
                   pl.BlockSpec((1, C_HEADS, C_QK, tm), lambda b, i: (b, 0, 0, i)),
                   pl.BlockSpec((1, C_HEADS, tm, C_QK), lambda b, i: (b, 0, i, 0)),
                   pl.BlockSpec((1, C_HEADS * C_V, tm), lambda b, i: (b, 0, i))],
        out_shape=[jax.ShapeDtypeStruct((bsz, seq, D_MODEL), F32),
                   jax.ShapeDtypeStruct((bsz, C_HEADS, C_QK, seq), BF16),
                   jax.ShapeDtypeStruct((bsz, C_HEADS, seq, C_QK), BF16),
                   jax.ShapeDtypeStruct((bsz, C_HEADS * C_V, seq), BF16)],
        scratch_shapes=[pltpu.VMEM((tm, D_MODEL), F32)],
        compiler_params=pltpu.CompilerParams(dimension_semantics=("parallel", "parallel"),
                                             vmem_limit_bytes=VMEM_LIMIT_BYTES),
        name="stage1_odd",
    )(x3d, posc, posr, op["invfc"], op["invfr"], g0, *ffn, g1, op["win"], op["qan"], op["kvan"],
      op["wqbt"], op["wkvk"], op["wkvvt"], op["gq"], op["kn"], op["indk"], op["splat"])


def _s2_kernel(x_ref, m_ref, p_ref, wo_ref, g2_ref, wg_ref, wu_ref, wd_ref,
               g3_ref, wpg_ref, wpp_ref, g4_ref, o_ref, acc_ref):
    x = x_ref[...] + _dot(m_ref[...], wo_ref[...])
    hn = _rms(x, g2_ref[...]).astype(BF16)
    _swiglu_into(acc_ref, hn, wg_ref, wu_ref, wd_ref)
    x = x + 0.5 * acc_ref[...]
    h3 = _rms(x, g3_ref[...]).astype(BF16)
    gate = _sigmoid(_dot(h3, wpg_ref[...]))
    proj = _dot(p_ref[...].astype(BF16), wpp_ref[...])
    o_ref[...] = x + _rms(gate * proj, g4_ref[...])


def _stage2(x2d, mixed, p3d, layer, wo, j, g2, ffn, ffn_layer, g3, wpg, wpp, g4):
    n = x2d.shape[0]
    tm = TOKEN_TILE
    row = lambda i: (i, 0)
    return pl.pallas_call(
        _s2_kernel,
        grid=(n // tm,),
        in_specs=[pl.BlockSpec((tm, D_MODEL), row), pl.BlockSpec((tm, D_MODEL), row),
                  pl.BlockSpec((None, tm, D_PLE), lambda i: (layer, i, 0)),
                  _const_spec((D_MODEL, D_MODEL), (j,)), _const_spec((1, D_MODEL))] + _ffn_specs(ffn_layer) + [
                  _const_spec((1, D_MODEL)), _const_spec((D_MODEL, D_MODEL), (layer,)),
                  _const_spec((D_PLE, D_MODEL), (layer,)), _const_spec((1, D_MODEL))],
        out_specs=pl.BlockSpec((tm, D_MODEL), row),
        out_shape=jax.ShapeDtypeStruct((n, D_MODEL), F32),
        scratch_shapes=[pltpu.VMEM((tm, D_MODEL), F32)],
        compiler_params=pltpu.CompilerParams(dimension_semantics=("parallel",),
                                             vmem_limit_bytes=VMEM_LIMIT_BYTES),
        name="stage2",
    )(x2d, mixed, p3d, wo, g2, *ffn, g3, wpg, wpp, g4)


def _level_map(ts):
    t = np.arange(ts)[:, None]
    s = np.arange(ts)[None, :]
    lvl = np.full((ts, ts), len(HGRN_LEVELS) + 1, np.int32)
    lvl[(t // SUBLANES == s // SUBLANES) & (s <= t)] = len(HGRN_LEVELS)
    for li, h in enumerate(HGRN_LEVELS):
        own = (t // (2 * h) == s // (2 * h)) & (t % (2 * h) >= h) & (s % (2 * h) < h)
        lvl[own] = li
    return lvl


def _even_core_kernel(u_ref, v_ref, q_ref, f_ref, i_ref, g_ref,
                      vnorm_ref, ind_ref, wcat_ref, bfull_ref, lb_ref, onorm_ref, ltri_ref, lvl_ref,
                      o_ref, state_ref):
    ts = HGRN_TILE
    n_lvl = len(HGRN_LEVELS)

    u = _gelu_tanh(u_ref[0].astype(F32))
    v = _gelu_tanh(v_ref[0].astype(F32))
    v2_hi, v2_lo = _split2(v * v)
    ss = _dot(v2_hi, ind_ref[...]) + _dot(v2_lo, ind_ref[...])
    vn = (v * lax.rsqrt(ss * (1.0 / A_HEAD_DIM) + EPS) * vnorm_ref[...]).astype(BF16)
    lane_head = lax.broadcasted_iota(jnp.int32, (1, A_WIDTH), 1) // A_HEAD_DIM
    for c in range(ts // A_CHUNK):
        rows = slice(c * A_CHUNK, (c + 1) * A_CHUNK)
        vc = vn[rows, :]
        zero = jnp.zeros_like(vc)
        vbd = jnp.concatenate([jnp.where(lane_head == hd, vc, zero) for hd in range(A_HEADS)], axis=0)
        mixed = _dot(wcat_ref[...], vbd) + bfull_ref[...]
        o_ref[0, rows, 0:A_WIDTH] = (u[rows, :] * mixed).astype(BF16)

    @pl.when(pl.program_id(1) == 0)
    def _():
        state_ref[...] = jnp.zeros_like(state_ref)

    lb = lb_ref[...]
    fg = lb + (1.0 - lb) * _sigmoid(f_ref[0])
    lf = jnp.log(jnp.maximum(fg, F_MIN))
    kk = 1.0 - fg
    qq = q_ref[0].astype(F32)
    lf_hi, lf_lo = _split2(lf)
    ltri = ltri_ref[...]
    b = _dot(ltri, lf_hi) + _dot(ltri, lf_lo)

    q_lvls, k_lvls = [], []
    for h in HGRN_LEVELS:
        q_parts, k_parts = [], []
        zero = jnp.zeros((h, B_WIDTH), F32)
        for j in range(ts // (2 * h)):
            r0 = j * 2 * h
            bm = b[r0 + h - 1:r0 + h, :]
            lo_rows = slice(r0, r0 + h)
            up_rows = slice(r0 + h, r0 + 2 * h)
            k_parts += [kk[lo_rows, :] * jnp.exp(bm - b[lo_rows, :]), zero]
            q_parts += [zero, qq[up_rows, :] * jnp.exp(b[up_rows, :] - bm)]
        q_lvls.append(jnp.concatenate(q_parts, axis=0).astype(BF16))
        k_lvls.append(jnp.concatenate(k_parts, axis=0).astype(BF16))
    b3 = b.reshape(ts // SUBLANES, SUBLANES, B_WIDTH)
    bref = jnp.broadcast_to(b3[:, 3:4, :], b3.shape).reshape(ts, B_WIDTH)
    q_lvls.append((qq * jnp.exp(b - bref)).astype(BF16))
    k_lvls.append((kk * jnp.exp(bref - b)).astype(BF16))

    b_last = b[ts - 1:ts, :]
    q_in = (qq * jnp.exp(b)).astype(BF16)
    k_out = (kk * jnp.exp(b_last - b)).astype(BF16)
    carry = jnp.exp(b_last)
    vv = i_ref[0]
    gate = _silu(g_ref[0].astype(F32))
    onorm = onorm_ref[...]

    lvl = lvl_ref[...]
    masks = [lvl == li for li in range(n_lvl + 1)]
    for hd in range(B_HEADS):
        cols = slice(hd * B_DIM, (hd + 1) * B_DIM)
        scores = jnp.where(masks[n_lvl], _dot_nt(q_lvls[n_lvl][:, cols], k_lvls[n_lvl][:, cols]), 0.0)
        for li in range(n_lvl):
            scores = jnp.where(masks[li], _dot_nt(q_lvls[li][:, cols], k_lvls[li][:, cols]), scores)
        v_h = vv[:, cols]
        st = state_ref[hd]
        o = _dot(scores.astype(BF16), v_h) + _dot_nt(q_in[:, cols], st.astype(BF16))
        state_ref[hd] = st * carry[:, cols] + _dot_tn(v_h, k_out[:, cols])
        on = _rms(o, onorm[:, cols]) * gate[:, cols]
        o_ref[0, :, A_WIDTH + hd * B_DIM:A_WIDTH + (hd + 1) * B_DIM] = on.astype(BF16)


def _even_core(parts, vnorm, ind, wcat, bfull, lb, onorm, ltri, lvl):
    bsz, seq, _ = parts[0].shape
    ts = HGRN_TILE
    tile = lambda b, i: (b, i, 0)
    part_spec = pl.BlockSpec((1, ts, A_WIDTH), tile)
    return pl.pallas_call(
        _even_core_kernel,
        grid=(bsz, seq // ts),
        in_specs=[part_spec] * EVEN_PARTS + [
            _const_spec((1, A_WIDTH)), _const_spec((A_WIDTH, A_WIDTH)),
            _const_spec((A_CHUNK, A_HEADS * A_CHUNK)), _const_spec((A_CHUNK, A_WIDTH)),
            _const_spec((1, B_WIDTH)), _const_spec((1, B_WIDTH)),
            _const_spec((ts, ts)), _const_spec((ts, ts))],
        out_specs=pl.BlockSpec((1, ts, D_MODEL), tile),
        out_shape=jax.ShapeDtypeStruct((bsz, seq, D_MODEL), BF16),
        scratch_shapes=[pltpu.VMEM((B_HEADS, B_DIM, B_DIM), F32)],
        compiler_params=pltpu.CompilerParams(dimension_semantics=("parallel", "arbitrary"),
                                             vmem_limit_bytes=VMEM_LIMIT_BYTES),
        name="even_core",
    )(*parts, vnorm, ind, wcat, bfull, lb, onorm, ltri, lvl)


def _attn_kernel(qt_ref, k_ref, vt_ref, o_ref, m_ref, l_ref, acc_ref, sa_ref, sb_ref):
    tq = ATTN_TILE
    tks = tq // ATTN_KEY_SPLIT
    qi = pl.program_id(2)
    heads = range(ATTN_HEADS_PER_STEP)

    m_ref[...] = jnp.full_like(m_ref, NEG_BIG)
    l_ref[...] = jnp.zeros_like(l_ref)
    acc_ref[...] = jnp.zeros_like(acc_ref)

    def key_subs(j):
        start = pl.multiple_of(j * tq, tq)
        return [pl.ds(pl.multiple_of(start + c * tks, tks), tks) for c in range(ATTN_KEY_SPLIT)]

    def scores(j, s_ref):
        for hd in heads:
            qt = qt_ref[0, hd]
            for c, sub in enumerate(key_subs(j)):
                s_ref[hd, c * tks:(c + 1) * tks, :] = _dot(k_ref[0, hd, sub, :], qt)

    def consume(j, s_ref, masked):
        if masked:
            key = lax.broadcasted_iota(jnp.int32, (tks, tq), 0)
            qry = lax.broadcasted_iota(jnp.int32, (tks, tq), 1)
        s3 = {}
        for hd in heads:
            for c in range(ATTN_KEY_SPLIT):
                s = s_ref[hd, c * tks:(c + 1) * tks, :]
                if masked:
                    s = jnp.where(key + c * tks <= qry, s, NEG_BIG)
                s3[hd, c] = s.reshape(tks // SUBLANES, SUBLANES, tq)
        m_new, alpha = {}, {}
        for hd in heads:
            m_prev = m_ref[hd]
            cmax = functools.reduce(jnp.maximum, [jnp.max(s3[hd, c], axis=0) for c in range(ATTN_KEY_SPLIT)])
            cmax = jnp.max(cmax, axis=0, keepdims=True)
            m_new[hd] = jnp.maximum(m_prev, jnp.broadcast_to(cmax, (SUBLANES, tq)))
            alpha[hd] = jnp.exp2(m_prev - m_new[hd])
            m_ref[hd] = m_new[hd]
        ones = jnp.ones((2 * SUBLANES, tks), BF16)
        for hd in heads:
            acc3 = acc_ref[hd].reshape(C_V // SUBLANES, SUBLANES, tq) * alpha[hd][None]
            acc = acc3.reshape(C_V, tq)
            csum = jnp.zeros((SUBLANES, tq), F32)
            for c, sub in enumerate(key_subs(j)):
                p = jnp.exp2(s3[hd, c] - m_new[hd][None]).reshape(tks, tq).astype(BF16)
                lhs = jnp.concatenate([vt_ref[0, hd * C_V:(hd + 1) * C_V, sub], ones], axis=0)
                pv = _dot(lhs, p)
                acc = acc + pv[:C_V, :]
                csum = csum + pv[C_V:C_V + SUBLANES, :]
            l_ref[hd] = alpha[hd] * l_ref[hd] + csum
            acc_ref[hd] = acc

    scores(0, sa_ref)
    pairs = qi // 2

    def body(t, carry):
        j = 2 * t
        scores(j + 1, sb_ref)
        consume(j, sa_ref, False)
        scores(j + 2, sa_ref)
        consume(j + 1, sb_ref, False)
        return carry

    lax.fori_loop(0, pairs, body, 0)

    @pl.when(qi % 2 == 0)
    def _():
        consume(qi, sa_ref, True)

    @pl.when(qi % 2 == 1)
    def _():
        scores(qi, sb_ref)
        consume(qi - 1, sa_ref, False)
        consume(qi, sb_ref, True)
    for hd in range(ATTN_HEADS_PER_STEP):
        inv_l = 1.0 / l_ref[hd]
        out_t = acc_ref[hd].reshape(C_V // SUBLANES, SUBLANES, tq) * inv_l[None]
        o_ref[0, :, hd * C_V:(hd + 1) * C_V] = out_t.reshape(C_V, tq).T.astype(BF16)


def _attention(qt, k, vt):
    bsz, heads, seq, _ = k.shape
    tq = ATTN_TILE
    g = ATTN_HEADS_PER_STEP
    return pl.pallas_call(
        _attn_kernel,
        grid=(bsz, heads // g, seq // tq),
        in_specs=[pl.BlockSpec((1, g, C_QK, tq), lambda b, h, i: (b, h, 0, i)),
                  pl.BlockSpec((1, g, seq, C_QK), lambda b, h, i: (b, h, 0, 0)),
                  pl.BlockSpec((1, g * C_V, seq), lambda b, h, i: (b, h, 0))],
        out_specs=pl.BlockSpec((1, tq, g * C_V), lambda b, h, i: (b, i, h)),
        out_shape=jax.ShapeDtypeStruct((bsz, seq, heads * C_V), BF16),
        scratch_shapes=[pltpu.VMEM((g, SUBLANES, tq), F32), pltpu.VMEM((g, SUBLANES, tq), F32),
                        pltpu.VMEM((g, C_V, tq), F32), pltpu.VMEM((g, tq, tq), F32),
                        pltpu.VMEM((g, tq, tq), F32)],
        compiler_params=pltpu.CompilerParams(dimension_semantics=("parallel", "parallel", "arbitrary"),
                                             vmem_limit_bytes=VMEM_LIMIT_BYTES),
        name="mla_attention",
    )(qt, k, vt)


def _row(v):
    return v.reshape(1, -1).astype(F32)


def _pad_cols(w, total):
    return jnp.pad(w, ((0, 0), (0, total - w.shape[1])))


def _even_params(v_norm, w_s, b_s, lb, out_norm):
    causal = np.tril(np.ones((A_CHUNK, A_CHUNK), bool))
    w = jnp.where(causal[None], w_s, 0.0)
    wcat = jnp.transpose(w, (1, 0, 2)).reshape(A_CHUNK, A_HEADS * A_CHUNK)
    bfull = jnp.repeat(b_s.T, A_HEAD_DIM, axis=1)
    head_of = np.arange(A_WIDTH) // A_HEAD_DIM
    ind = (head_of[:, None] == head_of[None, :]).astype(np.float32)
    ltri = np.tril(np.ones((HGRN_TILE, HGRN_TILE), np.float32))
    return dict(vnorm=_row(v_norm), ind=jnp.asarray(ind, BF16),
                wcat=wcat.astype(BF16), bfull=bfull.astype(F32), lb=_row(lb),
                onorm=_row(jnp.tile(out_norm, B_HEADS)), ltri=jnp.asarray(ltri, BF16),
                lvl=jnp.asarray(_level_map(HGRN_TILE)))


def _odd_params(w_in, q_a_norm, kv_a_norm, w_q_b, w_kv_b, q_norm, k_norm):
    tm = TOKEN_TILE
    win = _pad_cols(w_in, ODD_IN_PAD)
    wkv = w_kv_b.reshape(KV_LORA, C_HEADS, C_NOPE + C_V)
    wkvk = wkv[:, :, :C_NOPE].reshape(KV_LORA, C_HEADS * C_NOPE)
    wkvvt = wkv[:, :, C_NOPE:].reshape(KV_LORA, C_HEADS * C_V).T
    inv_freq = ROPE_THETA ** (-jnp.arange(0, C_ROPE, 2, dtype=F32) / C_ROPE)
    invfr = jnp.tile(inv_freq, LANES // (C_ROPE // 2)).reshape(1, LANES)
    invfc = jnp.broadcast_to(inv_freq[:, None], (C_ROPE // 2, tm))
    head_of = np.arange(C_HEADS * C_NOPE) // C_NOPE
    indk = (head_of[:, None] == np.arange(LANES)[None, :]).astype(np.float32)
    return dict(win=win.astype(BF16), qan=_row(q_a_norm), kvan=_row(kv_a_norm),
                wqbt=w_q_b.T.astype(BF16), wkvk=wkvk.astype(BF16), wkvvt=wkvvt.astype(BF16),
                gq=jnp.broadcast_to(q_norm.astype(F32)[:, None], (C_QK, tm)),
                kn=_pad_cols(_row(k_norm), Q_HEAD_PAD), invfr=invfr, invfc=invfc,
                indk=jnp.asarray(indk, BF16), splat=jnp.asarray(indk.T, BF16))


def kernel(x, p, positions, norm_gains, ffn_w_gate, ffn_w_up, ffn_w_down, ple_w_gate, ple_w_proj,
           even_w_in, gmlp_v_norm, gmlp_w_s, gmlp_b_s, hgrn_lb_raw, hgrn_out_norm, even_w_out,
           mla_w_in, mla_q_a_norm, mla_kv_a_norm, mla_w_q_b, mla_w_kv_b, mla_q_norm, mla_k_norm, mla_w_out):
    bsz, seq, d = x.shape
    depth = norm_gains.shape[0]
    n = bsz * seq
    assert d == D_MODEL and seq % ATTN_TILE == 0 and seq % TOKEN_TILE == 0 and seq % HGRN_TILE == 0

    lb_sm = jax.nn.softmax(hgrn_lb_raw.astype(F32), axis=0)
    lower_bounds = jnp.clip(jnp.cumsum(lb_sm, axis=0) - lb_sm[0], 0.0, LB_MAX)
    posc = positions.astype(F32).reshape(bsz, seq, 1)
    posr = positions.astype(F32).reshape(bsz, 1, seq)

    ffn = tuple(w.astype(BF16) for w in (ffn_w_gate, ffn_w_up, ffn_w_down))
    ple_wg, ple_wp = ple_w_gate.astype(BF16), ple_w_proj.astype(BF16)
    even_win, even_wout, mla_wout = (w.astype(BF16) for w in (even_w_in, even_w_out, mla_w_out))
    p3d = p.reshape(depth, n, D_PLE)

    xc = x.reshape(n, d)
    for i in range(depth):
        g = norm_gains[i]
        j = i // 2
        if i % 2 == 0:
            ep = _even_params(gmlp_v_norm[j], gmlp_w_s[j], gmlp_b_s[j], lower_bounds[j], hgrn_out_norm[j])
            outs = _stage1_even(xc, _row(g[0]), ffn, (i, 0), _row(g[1]), even_win, j)
            xc = outs[0]
            parts = [o.reshape(bsz, seq, A_WIDTH) for o in outs[1:]]
            mixed = _even_core(parts, ep["vnorm"], ep["ind"], ep["wcat"], ep["bfull"], ep["lb"],
                               ep["onorm"], ep["ltri"], ep["lvl"])
            w_out = even_wout
        else:
            op = _odd_params(mla_w_in[j], mla_q_a_norm[j], mla_kv_a_norm[j], mla_w_q_b[j], mla_w_kv_b[j],
                             mla_q_norm[j], mla_k_norm[j])
            x1, qt, k, vt = _stage1_odd(xc.reshape(bsz, seq, d), posc, posr, op, _row(g[0]), ffn, (i, 0),
                                        _row(g[1]))
            xc = x1.reshape(n, d)
            mixed = _attention(qt, k, vt)
            w_out = mla_wout
        xc = _stage2(xc, mixed.reshape(n, d), p3d, i, w_out, j, _row(g[2]), ffn, (i, 1), _row(g[3]),
                     ple_wg, ple_wp, _row(g[4]))
    return xc.reshape(bsz, seq, d)
```

```python
import functools
import math

import numpy as np
import jax
import jax.numpy as jnp
from jax import lax
from jax.experimental import pallas as pl
from jax.experimental.pallas import tpu as pltpu

F32 = jnp.float32
BF16 = jnp.bfloat16

D_MODEL = 1024
D_FF = 2816
D_PLE = 256
EPS = 1e-6
NEG_BIG = -1e30
F_MIN = 1e-6
LB_MAX = 0.999

A_HEADS = 8
A_HEAD_DIM = 64
A_WIDTH = A_HEADS * A_HEAD_DIM
A_CHUNK = 128
B_HEADS = 4
B_DIM = 128
B_WIDTH = B_HEADS * B_DIM
EVEN_PARTS = 6

C_HEADS = 8
C_NOPE = 128
C_ROPE = 64
C_V = 128
C_QK = C_NOPE + C_ROPE
Q_LORA = 384
KV_LORA = 256
ODD_IN_PAD = Q_LORA + KV_LORA + 128
Q_HEAD_PAD = 256
ROPE_THETA = 10000.0
ATTN_SCALE = C_QK ** -0.5
LOG2E = math.log2(math.e)

LANES = 128
SUBLANES = 8
MXU_DIM = 256
VMEM_LIMIT_BYTES = 60000 * 1024

TOKEN_TILE = 512
FF_CHUNK = 2 * MXU_DIM
HGRN_TILE = 256
HGRN_SUB = HGRN_TILE // 2
HGRN_LEVELS = (64, 32, 16, 8)
ATTN_TILE = 512
ATTN_HEADS_PER_STEP = 4
ATTN_KEY_SPLIT = 2


def _ff_chunks():
    chunks, off = [], 0
    while off < D_FF:
        size = min(FF_CHUNK, D_FF - off)
        chunks.append((off, size))
        off += size
    return tuple(chunks)


def _sigmoid(x):
    return 0.5 + 0.5 * jnp.tanh(0.5 * x)


def _silu(x):
    t = 0.5 * x
    return t + t * jnp.tanh(t)


def _gelu_tanh(x):
    c = math.sqrt(2.0 / math.pi)
    hx = 0.5 * x
    return hx + hx * jnp.tanh(x * (c + (c * 0.044715) * (x * x)))


def _rms(x, g):
    ms = jnp.mean(x * x, axis=-1, keepdims=True)
    return x * lax.rsqrt(ms + EPS) * g


def _dot(a, b):
    return jnp.dot(a, b, preferred_element_type=F32)


def _dot_nt(a, b):
    return lax.dot_general(a, b, (((1,), (1,)), ((), ())), preferred_element_type=F32)


def _dot_tn(a, b):
    return lax.dot_general(a, b, (((0,), (0,)), ((), ())), preferred_element_type=F32)


def _split2(x):
    hi = x.astype(BF16)
    lo = (x - hi.astype(F32)).astype(BF16)
    return hi, lo


def _swiglu_into(acc_ref, hn, wg_ref, wu_ref, wd_ref):
    for ci, (off, size) in enumerate(_ff_chunks()):
        gate = _dot(hn, wg_ref[:, off:off + size])
        up = _dot(hn, wu_ref[:, off:off + size])
        act = (_silu(gate) * up).astype(BF16)
        part = _dot(act, wd_ref[off:off + size, :])
        if ci == 0:
            acc_ref[...] = part
        else:
            acc_ref[...] += part


def _s1_even_kernel(x_ref, g0_ref, wg_ref, wu_ref, wd_ref, g1_ref, win_ref,
                    x1_ref, u_ref, v_ref, q_ref, f_ref, i_ref, gg_ref, acc_ref):
    x = x_ref[...]
    hn = _rms(x, g0_ref[...]).astype(BF16)
    _swiglu_into(acc_ref, hn, wg_ref, wu_ref, wd_ref)
    x1 = x + 0.5 * acc_ref[...]
    x1_ref[...] = x1
    h = _rms(x1, g1_ref[...]).astype(BF16)
    outs = (u_ref, v_ref, q_ref, f_ref, i_ref, gg_ref)
    for k, o_ref in enumerate(outs):
        z = _dot(h, win_ref[:, k * A_WIDTH:(k + 1) * A_WIDTH])
        o_ref[...] = z.astype(o_ref.dtype)


def _const_spec(shape, prefix=()):
    nd = len(shape)
    return pl.BlockSpec((None,) * len(prefix) + tuple(shape), lambda *_: tuple(prefix) + (0,) * nd,
                        pipeline_mode=pl.Buffered(1))


def _ffn_specs(layer):
    return [_const_spec((D_MODEL, D_FF), layer), _const_spec((D_MODEL, D_FF), layer),
            _const_spec((D_FF, D_MODEL), layer)]


def _stage1_even(x2d, g0, ffn, ffn_layer, g1, win, j):
    n = x2d.shape[0]
    tm = TOKEN_TILE
    row = lambda i: (i, 0)
    part_spec = pl.BlockSpec((tm, A_WIDTH), row)
    out_shape = [jax.ShapeDtypeStruct((n, D_MODEL), F32)]
    out_shape += [jax.ShapeDtypeStruct((n, A_WIDTH), F32 if k == 3 else BF16) for k in range(EVEN_PARTS)]
    return pl.pallas_call(
        _s1_even_kernel,
        grid=(n // tm,),
        in_specs=[pl.BlockSpec((tm, D_MODEL), row), _const_spec((1, D_MODEL))] + _ffn_specs(ffn_layer) + [
                  _const_spec((1, D_MODEL)), _const_spec((D_MODEL, EVEN_PARTS * A_WIDTH), (j,))],
        out_specs=[pl.BlockSpec((tm, D_MODEL), row)] + [part_spec] * EVEN_PARTS,
        out_shape=out_shape,
        scratch_shapes=[pltpu.VMEM((tm, D_MODEL), F32)],
        compiler_params=pltpu.CompilerParams(dimension_semantics=("parallel",),
                                             vmem_limit_bytes=VMEM_LIMIT_BYTES),
        name="stage1_even",
    )(x2d, g0, *ffn, g1, win)


def _rope_pad(x, cos_t, sin_t):
    return x * cos_t + (pltpu.roll(x, 32, axis=1) - pltpu.roll(x, 96, axis=1)) * sin_t


def _s1_odd_kernel(x_ref, posc_ref, posr_ref, invfc_ref, invfr_ref, g0_ref, wg_ref, wu_ref, wd_ref, g1_ref,
                   win_ref, qan_ref, kvan_ref, wqbt_ref, wkvk_ref, wkvvt_ref, gq_ref, kn_ref, indk_ref, splat_ref,
                   x1_ref, qt_ref, k_ref, vt_ref, acc_ref):
    x = x_ref[0]
    hn = _rms(x, g0_ref[...]).astype(BF16)
    _swiglu_into(acc_ref, hn, wg_ref, wu_ref, wd_ref)
    x1 = x + 0.5 * acc_ref[...]
    x1_ref[0] = x1
    h = _rms(x1, g1_ref[...]).astype(BF16)
    z = _dot(h, win_ref[...])
    cqn = _rms(z[:, :Q_LORA], qan_ref[...]).astype(BF16)
    ckvn = _rms(z[:, Q_LORA:Q_LORA + KV_LORA], kvan_ref[...]).astype(BF16)
    k_rope = z[:, Q_LORA + KV_LORA:]
    q_t = _dot_nt(wqbt_ref[...], cqn)
    k_nope = _dot(ckvn, wkvk_ref[...])
    vt_ref[0] = _dot_nt(wkvvt_ref[...], ckvn).astype(BF16)

    ang_t = invfc_ref[...] * posr_ref[0]
    cos_f, sin_f = jnp.cos(ang_t), jnp.sin(ang_t)
    gq = gq_ref[...]
    half = C_ROPE // 2
    for hd in range(C_HEADS):
        blk = q_t[hd * C_QK:(hd + 1) * C_QK, :]
        ss = jnp.sum(blk * blk, axis=0, keepdims=True)
        r = lax.rsqrt(ss * (1.0 / C_QK) + EPS) * (ATTN_SCALE * LOG2E)
        sc = blk * r * gq
        a1 = sc[C_NOPE:C_NOPE + half, :]
        a2 = sc[C_NOPE + half:, :]
        qt_ref[0, hd, 0:C_NOPE, :] = sc[:C_NOPE, :].astype(BF16)
        qt_ref[0, hd, C_NOPE:C_NOPE + half, :] = (a1 * cos_f - a2 * sin_f).astype(BF16)
        qt_ref[0, hd, C_NOPE + half:C_QK, :] = (a2 * cos_f + a1 * sin_f).astype(BF16)

    lane = lax.broadcasted_iota(jnp.int32, (1, LANES), 1)
    valid = lane < C_ROPE
    ang = posc_ref[0] * invfr_ref[...]
    cos_t = jnp.where(valid, jnp.cos(ang), 0.0)
    sin_t = jnp.where(valid, jnp.sin(ang), 0.0)
    kn = kn_ref[...]
    kn_nope, kn_rope = kn[:, :C_NOPE], kn[:, C_NOPE:]
    kr_ss = jnp.sum(k_rope * k_rope, axis=-1, keepdims=True)
    kr_rot = _rope_pad(k_rope * kn_rope, cos_t, sin_t)
    k2_hi, k2_lo = _split2(k_nope * k_nope)
    ssk = _dot(k2_hi, indk_ref[...]) + _dot(k2_lo, indk_ref[...])
    rk = jnp.where(lane < C_HEADS, lax.rsqrt((ssk + kr_ss) * (1.0 / C_QK) + EPS), 0.0)
    rk_hi, rk_lo = _split2(rk)
    rk_b = _dot(rk_hi, splat_ref[...]) + _dot(rk_lo, splat_ref[...])
    for hd in range(C_HEADS):
        cols = slice(hd * C_NOPE, (hd + 1) * C_NOPE)
        k_ref[0, hd, :, 0:C_NOPE] = (k_nope[:, cols] * rk_b[:, cols] * kn_nope).astype(BF16)
        k_ref[0, hd, :, C_NOPE:C_QK] = (kr_rot * rk_b[:, cols])[:, :C_ROPE].astype(BF16)


def _stage1_odd(x3d, posc, posr, op, g0, ffn, ffn_layer, g1):
    bsz, seq, _ = x3d.shape
    tm = TOKEN_TILE
    tile = lambda b, i: (b, i, 0)
    return pl.pallas_call(
        _s1_odd_kernel,
        grid=(bsz, seq // tm),
        in_specs=[pl.BlockSpec((1, tm, D_MODEL), tile), pl.BlockSpec((1, tm, 1), tile),
                  pl.BlockSpec((1, 1, tm), lambda b, i: (b, 0, i)),
                  _const_spec((C_ROPE // 2, tm)), _const_spec((1, LANES)), _const_spec((1, D_MODEL))]
                 + _ffn_specs(ffn_layer) + [
                  _const_spec((1, D_MODEL)), _const_spec((D_MODEL, ODD_IN_PAD)),
                  _const_spec((1, Q_LORA)), _const_spec((1, KV_LORA)),
                  _const_spec((C_HEADS * C_QK, Q_LORA)), _const_spec((KV_LORA, C_HEADS * C_NOPE)),
                  _const_spec((C_HEADS * C_V, KV_LORA)), _const_spec((C_QK, tm)), _const_spec((1, Q_HEAD_PAD)),
                  _const_spec((C_HEADS * C_NOPE, LANES)), _const_spec((LANES, C_HEADS * C_NOPE))],
        out_specs=[pl.BlockSpec((1, tm, D_MODEL), tile),
                   pl.BlockSpec((1, C_HEADS, C_QK, tm), lambda b, i: (b, 0, 0, i)),
                   pl.BlockSpec((1, C_HEADS, tm, C_QK), lambda b, i: (b, 0, i, 0)),
                   pl.BlockSpec((1, C_HEADS * C_V, tm), lambda b, i: (b, 0, i))],
        out_shape=[jax.ShapeDtypeStruct((bsz, seq, D_MODEL), F32),
                   jax.ShapeDtypeStruct((bsz, C_HEADS, C_QK, seq), BF16),
                   jax.ShapeDtypeStruct((bsz, C_HEADS, seq, C_QK), BF16),
                   jax.ShapeDtypeStruct((bsz, C_HEADS * C_V, seq), BF16)],
        scratch_shapes=[pltpu.VMEM((tm, D_MODEL), F32)],
        compiler_params=pltpu.CompilerParams(dimension_semantics=("parallel", "parallel"),
                                             vmem_limit_bytes=VMEM_LIMIT_BYTES),
        name="stage1_odd",
    )(x3d, posc, posr, op["invfc"], op["invfr"], g0, *ffn, g1, op["win"], op["qan"], op["kvan"],
      op["wqbt"], op["wkvk"], op["wkvvt"], op["gq"], op["kn"], op["indk"], op["splat"])


def _s2_kernel(x_ref, m_ref, p_ref, wo_ref, g2_ref, wg_ref, wu_ref, wd_ref,
               g3_ref, wpg_ref, wpp_ref, g4_ref, o_ref, acc_ref):
    x = x_ref[...] + _dot(m_ref[...], wo_ref[...])
    hn = _rms(x, g2_ref[...]).astype(BF16)
    _swiglu_into(acc_ref, hn, wg_ref, wu_ref, wd_ref)
    x = x + 0.5 * acc_ref[...]
    h3 = _rms(x, g3_ref[...]).astype(BF16)
    gate = _sigmoid(_dot(h3, wpg_ref[...]))
    proj = _dot(p_ref[...].astype(BF16), wpp_ref[...])
    o_ref[...] = x + _rms(gate * proj, g4_ref[...])


def _stage2(x2d, mixed, p3d, layer, wo, j, g2, ffn, ffn_layer, g3, wpg, wpp, g4):
    n = x2d.shape[0]
    tm = TOKEN_TILE
    row = lambda i: (i, 0)
    return pl.pallas_call(
        _s2_kernel,
        grid=(n // tm,),
        in_specs=[pl.BlockSpec((tm, D_MODEL), row), pl.BlockSpec((tm, D_MODEL), row),
                  pl.BlockSpec((None, tm, D_PLE), lambda i: (layer, i, 0)),
                  _const_spec((D_MODEL, D_MODEL), (j,)), _const_spec((1, D_MODEL))] + _ffn_specs(ffn_layer) + [
                  _const_spec((1, D_MODEL)), _const_spec((D_MODEL, D_MODEL), (layer,)),
                  _const_spec((D_PLE, D_MODEL), (layer,)), _const_spec((1, D_MODEL))],
        out_specs=pl.BlockSpec((tm, D_MODEL), row),
        out_shape=jax.ShapeDtypeStruct((n, D_MODEL), F32),
        scratch_shapes=[pltpu.VMEM((tm, D_MODEL), F32)],
        compiler_params=pltpu.CompilerParams(dimension_semantics=("parallel",),
                                             vmem_limit_bytes=VMEM_LIMIT_BYTES),
        name="stage2",
    )(x2d, mixed, p3d, wo, g2, *ffn, g3, wpg, wpp, g4)


def _level_map(ts):
    t = np.arange(ts)[:, None]
    s = np.arange(ts)[None, :]
    lvl = np.full((ts, ts), len(HGRN_LEVELS) + 1, np.int32)
    lvl[(t // SUBLANES == s // SUBLANES) & (s <= t)] = len(HGRN_LEVELS)
    for li, h in enumerate(HGRN_LEVELS):
        own = (t // (2 * h) == s // (2 * h)) & (t % (2 * h) >= h) & (s % (2 * h) < h)
        lvl[own] = li
    return lvl


def _even_core_kernel(u_ref, v_ref, q_ref, f_ref, i_ref, g_ref,
                      vnorm_ref, ind_ref, wcat_ref, bfull_ref, lb_ref, onorm_ref, ltri_ref, lvl_ref,
                      o_ref, state_ref):
    ts = HGRN_TILE
    sub = HGRN_SUB
    n_lvl = len(HGRN_LEVELS)

    u = _gelu_tanh(u_ref[0].astype(F32))
    v = _gelu_tanh(v_ref[0].astype(F32))
    ss = _dot((v * v).astype(BF16), ind_ref[...])
    vn = (v * lax.rsqrt(ss * (1.0 / A_HEAD_DIM) + EPS) * vnorm_ref[...]).astype(BF16)
    lane_head = lax.broadcasted_iota(jnp.int32, (1, A_WIDTH), 1) // A_HEAD_DIM
    for c in range(ts // A_CHUNK):
        rows = slice(c * A_CHUNK, (c + 1) * A_CHUNK)
        vc = vn[rows, :]
        zero = jnp.zeros_like(vc)
        vbd = jnp.concatenate([jnp.where(lane_head == hd, vc, zero) for hd in range(A_HEADS)], axis=0)
        mixed = _dot(wcat_ref[...], vbd) + bfull_ref[...]
        o_ref[0, rows, 0:A_WIDTH] = (u[rows, :] * mixed).astype(BF16)

    @pl.when(pl.program_id(1) == 0)
    def _():
        state_ref[...] = jnp.zeros_like(state_ref)

    lb = lb_ref[...]
    fg = lb + (1.0 - lb) * _sigmoid(f_ref[0])
    lf = jnp.log2(jnp.maximum(fg, F_MIN))
    kk = 1.0 - fg
    qq = q_ref[0].astype(F32)
    lf_hi, lf_lo = _split2(lf)
    ltri = ltri_ref[...]
    b = _dot(ltri, lf_hi) + _dot(ltri, lf_lo)

    q_lvls, k_lvls = [], []
    for h in HGRN_LEVELS:
        q_parts, k_parts = [], []
        zero = jnp.zeros((h, B_WIDTH), F32)
        for j in range(ts // (2 * h)):
            r0 = j * 2 * h
            bm = b[r0 + h - 1:r0 + h, :]
            lo_rows = slice(r0, r0 + h)
            up_rows = slice(r0 + h, r0 + 2 * h)
            k_parts += [kk[lo_rows, :] * jnp.exp2(bm - b[lo_rows, :]), zero]
            q_parts += [zero, qq[up_rows, :] * jnp.exp2(b[up_rows, :] - bm)]
        q_lvls.append(jnp.concatenate(q_parts, axis=0).astype(BF16))
        k_lvls.append(jnp.concatenate(k_parts, axis=0).astype(BF16))
    b3 = b.reshape(ts // SUBLANES, SUBLANES, B_WIDTH)
    bref = jnp.broadcast_to(b3[:, 3:4, :], b3.shape).reshape(ts, B_WIDTH)
    q_lvls.append((qq * jnp.exp2(b - bref)).astype(BF16))
    k_lvls.append((kk * jnp.exp2(bref - b)).astype(BF16))

    bm = b[sub - 1:sub, :]
    q_top = (qq[sub:, :] * jnp.exp2(b[sub:, :] - bm)).astype(BF16)
    k_top = (kk[:sub, :] * jnp.exp2(bm - b[:sub, :])).astype(BF16)

    b_last = b[ts - 1:ts, :]
    q_in = (qq * jnp.exp2(b)).astype(BF16)
    k_out = (kk * jnp.exp2(b_last - b)).astype(BF16)
    carry = jnp.exp2(b_last)
    vv = i_ref[0]
    gate = _silu(g_ref[0].astype(F32))
    onorm = onorm_ref[...]

    lvl = lvl_ref[...]
    masks = [lvl == li for li in range(n_lvl + 1)]
    head_cols = [slice(hd * B_DIM, (hd + 1) * B_DIM) for hd in range(B_HEADS)]
    states = [state_ref[hd] for hd in range(B_HEADS)]
    inter = [_dot_nt(q_in[:, cols], st.astype(BF16)) for cols, st in zip(head_cols, states)]
    for hd, cols in enumerate(head_cols):
        state_ref[hd] = states[hd] * carry[:, cols] + _dot_tn(vv[:, cols], k_out[:, cols])
    score_blocks = []
    for cols in head_cols:
        diag = []
        for a in range(ts // sub):
            rows = slice(a * sub, (a + 1) * sub)
            blk = jnp.where(masks[n_lvl], _dot_nt(q_lvls[n_lvl][rows, cols], k_lvls[n_lvl][rows, cols]), 0.0)
            for li in range(n_lvl):
                blk = jnp.where(masks[li], _dot_nt(q_lvls[li][rows, cols], k_lvls[li][rows, cols]), blk)
            diag.append(blk.astype(BF16))
        top = _dot_nt(q_top[:, cols], k_top[:, cols]).astype(BF16)
        score_blocks.append((diag[0], jnp.concatenate([top, diag[1]], axis=1)))
    outs = []
    for hd, cols in enumerate(head_cols):
        v_h = vv[:, cols]
        first, second = score_blocks[hd]
        o_intra = jnp.concatenate([_dot(first, v_h[:sub, :]), _dot(second, v_h)], axis=0)
        outs.append(o_intra + inter[hd])
    for hd, cols in enumerate(head_cols):
        on = _rms(outs[hd], onorm[:, cols]) * gate[:, cols]
        o_ref[0, :, A_WIDTH + hd * B_DIM:A_WIDTH + (hd + 1) * B_DIM] = on.astype(BF16)


def _even_core(parts, vnorm, ind, wcat, bfull, lb, onorm, ltri, lvl):
    bsz, seq, _ = parts[0].shape
    ts = HGRN_TILE
    tile = lambda b, i: (b, i, 0)
    part_spec = pl.BlockSpec((1, ts, A_WIDTH), tile)
    return pl.pallas_call(
        _even_core_kernel,
        grid=(bsz, seq // ts),
        in_specs=[part_spec] * EVEN_PARTS + [
            _const_spec((1, A_WIDTH)), _const_spec((A_WIDTH, A_WIDTH)),
            _const_spec((A_CHUNK, A_HEADS * A_CHUNK)), _const_spec((A_CHUNK, A_WIDTH)),
            _const_spec((1, B_WIDTH)), _const_spec((1, B_WIDTH)),
            _const_spec((ts, ts)), _const_spec((HGRN_SUB, HGRN_SUB))],
        out_specs=pl.BlockSpec((1, ts, D_MODEL), tile),
        out_shape=jax.ShapeDtypeStruct((bsz, seq, D_MODEL), BF16),
        scratch_shapes=[pltpu.VMEM((B_HEADS, B_DIM, B_DIM), F32)],
        compiler_params=pltpu.CompilerParams(dimension_semantics=("parallel", "arbitrary"),
                                             vmem_limit_bytes=VMEM_LIMIT_BYTES),
        name="even_core",
    )(*parts, vnorm, ind, wcat, bfull, lb, onorm, ltri, lvl)


def _attn_kernel(qt_ref, k_ref, vt_ref, o_ref, m_ref, l_ref, acc_ref, sa_ref, sb_ref):
    tq = ATTN_TILE
    tks = tq // ATTN_KEY_SPLIT
    qi = pl.program_id(2)
    heads = range(ATTN_HEADS_PER_STEP)

    m_ref[...] = jnp.full_like(m_ref, NEG_BIG)
    l_ref[...] = jnp.zeros_like(l_ref)
    acc_ref[...] = jnp.zeros_like(acc_ref)

    def key_subs(j):
        start = pl.multiple_of(j * tq, tq)
        return [pl.ds(pl.multiple_of(start + c * tks, tks), tks) for c in range(ATTN_KEY_SPLIT)]

    def scores(j, s_ref):
        for hd in heads:
            qt = qt_ref[0, hd]
            for c, sub in enumerate(key_subs(j)):
                s_ref[hd, c * tks:(c + 1) * tks, :] = _dot(k_ref[0, hd, sub, :], qt)

    def consume(j, s_ref, masked):
        if masked:
            key = lax.broadcasted_iota(jnp.int32, (tks, tq), 0)
            qry = lax.broadcasted_iota(jnp.int32, (tks, tq), 1)
        s3 = {}
        for hd in heads:
            for c in range(ATTN_KEY_SPLIT):
                s = s_ref[hd, c * tks:(c + 1) * tks, :]
                if masked:
                    s = jnp.where(key + c * tks <= qry, s, NEG_BIG)
                s3[hd, c] = s.reshape(tks // SUBLANES, SUBLANES, tq)
        m_new, alpha = {}, {}
        for hd in heads:
            m_prev = m_ref[hd]
            cmax = functools.reduce(jnp.maximum, [jnp.max(s3[hd, c], axis=0) for c in range(ATTN_KEY_SPLIT)])
            cmax = jnp.max(cmax, axis=0, keepdims=True)
            m_new[hd] = jnp.maximum(m_prev, jnp.broadcast_to(cmax, (SUBLANES, tq)))
            alpha[hd] = jnp.exp2(m_prev - m_new[hd])
            m_ref[hd] = m_new[hd]
        ones = jnp.ones((2 * SUBLANES, tks), BF16)
        for hd in heads:
            acc3 = acc_ref[hd].reshape(C_V // SUBLANES, SUBLANES, tq) * alpha[hd][None]
            acc = acc3.reshape(C_V, tq)
            csum = jnp.zeros((SUBLANES, tq), F32)
            for c, sub in enumerate(key_subs(j)):
                p = jnp.exp2(s3[hd, c] - m_new[hd][None]).reshape(tks, tq).astype(BF16)
                lhs = jnp.concatenate([vt_ref[0, hd * C_V:(hd + 1) * C_V, sub], ones], axis=0)
                pv = _dot(lhs, p)
                acc = acc + pv[:C_V, :]
                csum = csum + pv[C_V:C_V + SUBLANES, :]
            l_ref[hd] = alpha[hd] * l_ref[hd] + csum
            acc_ref[hd] = acc

    scores(0, sa_ref)
    pairs = qi // 2

    def body(t, carry):
        j = 2 * t
        scores(j + 1, sb_ref)
        consume(j, sa_ref, False)
        scores(j + 2, sa_ref)
        consume(j + 1, sb_ref, False)
        return carry

    lax.fori_loop(0, pairs, body, 0)

    @pl.when(qi % 2 == 0)
    def _():
        consume(qi, sa_ref, True)

    @pl.when(qi % 2 == 1)
    def _():
        scores(qi, sb_ref)
        consume(qi - 1, sa_ref, False)
        consume(qi, sb_ref, True)
    for hd in range(ATTN_HEADS_PER_STEP):
        inv_l = 1.0 / l_ref[hd]
        out_t = acc_ref[hd].reshape(C_V // SUBLANES, SUBLANES, tq) * inv_l[None]
        o_ref[0, :, hd * C_V:(hd + 1) * C_V] = out_t.reshape(C_V, tq).T.astype(BF16)


def _attention(qt, k, vt):
    bsz, heads, seq, _ = k.shape
    tq = ATTN_TILE
    g = ATTN_HEADS_PER_STEP
    return pl.pallas_call(
        _attn_kernel,
        grid=(bsz, heads // g, seq // tq),
        in_specs=[pl.BlockSpec((1, g, C_QK, tq), lambda b, h, i: (b, h, 0, i)),
                  pl.BlockSpec((1, g, seq, C_QK), lambda b, h, i: (b, h, 0, 0)),
                  pl.BlockSpec((1, g * C_V, seq), lambda b, h, i: (b, h, 0))],
        out_specs=pl.BlockSpec((1, tq, g * C_V), lambda b, h, i: (b, i, h)),
        out_shape=jax.ShapeDtypeStruct((bsz, seq, heads * C_V), BF16),
        scratch_shapes=[pltpu.VMEM((g, SUBLANES, tq), F32), pltpu.VMEM((g, SUBLANES, tq), F32),
                        pltpu.VMEM((g, C_V, tq), F32), pltpu.VMEM((g, tq, tq), F32),
                        pltpu.VMEM((g, tq, tq), F32)],
        compiler_params=pltpu.CompilerParams(dimension_semantics=("parallel", "parallel", "arbitrary"),
                                             vmem_limit_bytes=VMEM_LIMIT_BYTES),
        name="mla_attention",
    )(qt, k, vt)


def _row(v):
    return v.reshape(1, -1).astype(F32)


def _pad_cols(w, total):
    return jnp.pad(w, ((0, 0), (0, total - w.shape[1])))


def _even_params(v_norm, w_s, b_s, lb, out_norm):
    causal = np.tril(np.ones((A_CHUNK, A_CHUNK), bool))
    w = jnp.where(causal[None], w_s, 0.0)
    wcat = jnp.transpose(w, (1, 0, 2)).reshape(A_CHUNK, A_HEADS * A_CHUNK)
    bfull = jnp.repeat(b_s.T, A_HEAD_DIM, axis=1)
    head_of = np.arange(A_WIDTH) // A_HEAD_DIM
    ind = (head_of[:, None] == head_of[None, :]).astype(np.float32)
    ltri = np.tril(np.ones((HGRN_TILE, HGRN_TILE), np.float32))
    return dict(vnorm=_row(v_norm), ind=jnp.asarray(ind, BF16),
                wcat=wcat.astype(BF16), bfull=bfull.astype(F32), lb=_row(lb),
                onorm=_row(jnp.tile(out_norm, B_HEADS)), ltri=jnp.asarray(ltri, BF16),
                lvl=jnp.asarray(_level_map(HGRN_SUB)))


def _odd_params(w_in, q_a_norm, kv_a_norm, w_q_b, w_kv_b, q_norm, k_norm):
    tm = TOKEN_TILE
    win = _pad_cols(w_in, ODD_IN_PAD)
    wkv = w_kv_b.reshape(KV_LORA, C_HEADS, C_NOPE + C_V)
    wkvk = wkv[:, :, :C_NOPE].reshape(KV_LORA, C_HEADS * C_NOPE)
    wkvvt = wkv[:, :, C_NOPE:].reshape(KV_LORA, C_HEADS * C_V).T
    inv_freq = ROPE_THETA ** (-jnp.arange(0, C_ROPE, 2, dtype=F32) / C_ROPE)
    invfr = jnp.tile(inv_freq, LANES // (C_ROPE // 2)).reshape(1, LANES)
    invfc = jnp.broadcast_to(inv_freq[:, None], (C_ROPE // 2, tm))
    head_of = np.arange(C_HEADS * C_NOPE) // C_NOPE
    indk = (head_of[:, None] == np.arange(LANES)[None, :]).astype(np.float32)
    return dict(win=win.astype(BF16), qan=_row(q_a_norm), kvan=_row(kv_a_norm),
                wqbt=w_q_b.T.astype(BF16), wkvk=wkvk.astype(BF16), wkvvt=wkvvt.astype(BF16),
                gq=jnp.broadcast_to(q_norm.astype(F32)[:, None], (C_QK, tm)),
                kn=_pad_cols(_row(k_norm), Q_HEAD_PAD), invfr=invfr, invfc=invfc,
                indk=jnp.asarray(indk, BF16), splat=jnp.asarray(indk.T, BF16))


def kernel(x, p, positions, norm_gains, ffn_w_gate, ffn_w_up, ffn_w_down, ple_w_gate, ple_w_proj,
           even_w_in, gmlp_v_norm, gmlp_w_s, gmlp_b_s, hgrn_lb_raw, hgrn_out_norm, even_w_out,
           mla_w_in, mla_q_a_norm, mla_kv_a_norm, mla_w_q_b, mla_w_kv_b, mla_q_norm, mla_k_norm, mla_w_out):
    bsz, seq, d = x.shape
    depth = norm_gains.shape[0]
    n = bsz * seq
    assert d == D_MODEL and seq % ATTN_TILE == 0 and seq % TOKEN_TILE == 0 and seq % HGRN_TILE == 0

    lb_sm = jax.nn.softmax(hgrn_lb_raw.astype(F32), axis=0)
    lower_bounds = jnp.clip(jnp.cumsum(lb_sm, axis=0) - lb_sm[0], 0.0, LB_MAX)
    posc = positions.astype(F32).reshape(bsz, seq, 1)
    posr = positions.astype(F32).reshape(bsz, 1, seq)

    ffn = tuple(w.astype(BF16) for w in (ffn_w_gate, ffn_w_up, ffn_w_down))
    ple_wg, ple_wp = ple_w_gate.astype(BF16), ple_w_proj.astype(BF16)
    even_win, even_wout, mla_wout = (w.astype(BF16) for w in (even_w_in, even_w_out, mla_w_out))
    p3d = p.reshape(depth, n, D_PLE)

    xc = x.reshape(n, d)
    for i in range(depth):
        g = norm_gains[i]
        j = i // 2
        if i % 2 == 0:
            ep = _even_params(gmlp_v_norm[j], gmlp_w_s[j], gmlp_b_s[j], lower_bounds[j], hgrn_out_norm[j])
            outs = _stage1_even(xc, _row(g[0]), ffn, (i, 0), _row(g[1]), even_win, j)
            xc = outs[0]
            parts = [o.reshape(bsz, seq, A_WIDTH) for o in outs[1:]]
            mixed = _even_core(parts, ep["vnorm"], ep["ind"], ep["wcat"], ep["bfull"], ep["lb"],
                               ep["onorm"], ep["ltri"], ep["lvl"])
            w_out = even_wout
        else:
            op = _odd_params(mla_w_in[j], mla_q_a_norm[j], mla_kv_a_norm[j], mla_w_q_b[j], mla_w_kv_b[j],
                             mla_q_norm[j], mla_k_norm[j])
            x1, qt, k, vt = _stage1_odd(xc.reshape(bsz, seq, d), posc, posr, op, _row(g[0]), ffn, (i, 0),
                                        _row(g[1]))
            xc = x1.reshape(n, d)
            mixed = _attention(qt, k, vt)
            w_out = mla_wout
        xc = _stage2(xc, mixed.reshape(n, d), p3d, i, w_out, j, _row(g[2]), ffn, (i, 1), _row(g[3]),
                     ple_wg, ple_wp, _row(g[4]))
    return xc.reshape(bsz, seq, d)
```

```python
import functools
import math

import numpy as np
import jax
import jax.numpy as jnp
from jax import lax
from jax.experimental import pallas as pl
from jax.experimental.pallas import tpu as pltpu

F32 = jnp.float32
BF16 = jnp.bfloat16

D_MODEL = 1024
D_FF = 2816
D_PLE = 256
EPS = 1e-6
NEG_BIG = -1e30
F_MIN = 1e-6
LB_MAX = 0.999

A_HEADS = 8
A_HEAD_DIM = 64
A_WIDTH = A_HEADS * A_HEAD_DIM
A_CHUNK = 128
B_HEADS = 4
B_DIM = 128
B_WIDTH = B_HEADS * B_DIM
EVEN_PARTS = 6

C_HEADS = 8
C_NOPE = 128
C_ROPE = 64
C_V = 128
C_QK = C_NOPE + C_ROPE
Q_LORA = 384
KV_LORA = 256
ODD_IN_PAD = Q_LORA + KV_LORA + 128
ROPE_THETA = 10000.0
ATTN_SCALE = C_QK ** -0.5
LOG2E = math.log2(math.e)

LANES = 128
SUBLANES = 8
MXU_DIM = 256
VMEM_LIMIT_BYTES = 60000 * 1024

TOKEN_TILE = 512
FF_CHUNK = 2 * MXU_DIM
HGRN_TILE = 256
HGRN_BATCH = 2
HGRN_SUB = HGRN_TILE // 2
HGRN_LEVELS = (64, 32, 16, 8)
ATTN_TILE = 512
ATTN_HEADS_PER_STEP = 4
ATTN_KEY_SPLIT = 2


def _ff_chunks():
    chunks, off = [], 0
    while off < D_FF:
        size = min(FF_CHUNK, D_FF - off)
        chunks.append((off, size))
        off += size
    return tuple(chunks)


def _sigmoid(x):
    return 0.5 + 0.5 * jnp.tanh(0.5 * x)


def _silu(x):
    t = 0.5 * x
    return t + t * jnp.tanh(t)


def _gelu_tanh(x):
    c = math.sqrt(2.0 / math.pi)
    hx = 0.5 * x
    return hx + hx * jnp.tanh(x * (c + (c * 0.044715) * (x * x)))


def _rms(x, g):
    ms = jnp.mean(x * x, axis=-1, keepdims=True)
    return x * lax.rsqrt(ms + EPS) * g


def _dot(a, b):
    return jnp.dot(a, b, preferred_element_type=F32)


def _dot_nt(a, b):
    return lax.dot_general(a, b, (((1,), (1,)), ((), ())), preferred_element_type=F32)


def _dot_tn(a, b):
    return lax.dot_general(a, b, (((0,), (0,)), ((), ())), preferred_element_type=F32)


def _split2(x):
    hi = x.astype(BF16)
    lo = (x - hi.astype(F32)).astype(BF16)
    return hi, lo


def _swiglu_into(acc_ref, hn, wg_ref, wu_ref, wd_ref):
    for ci, (off, size) in enumerate(_ff_chunks()):
        gate = _dot(hn, wg_ref[:, off:off + size])
        up = _dot(hn, wu_ref[:, off:off + size])
        act = (_silu(gate) * up).astype(BF16)
        part = _dot(act, wd_ref[off:off + size, :])
        if ci == 0:
            acc_ref[...] = part
        else:
            acc_ref[...] += part


def _s1_even_kernel(x_ref, g0_ref, wg_ref, wu_ref, wd_ref, g1_ref, win_ref,
                    x1_ref, u_ref, v_ref, q_ref, f_ref, i_ref, gg_ref, acc_ref):
    x = x_ref[...]
    hn = _rms(x, g0_ref[...]).astype(BF16)
    _swiglu_into(acc_ref, hn, wg_ref, wu_ref, wd_ref)
    x1 = x + 0.5 * acc_ref[...]
    x1_ref[...] = x1
    h = _rms(x1, g1_ref[...]).astype(BF16)
    outs = (u_ref, v_ref, q_ref, f_ref, i_ref, gg_ref)
    for k, o_ref in enumerate(outs):
        z = _dot(h, win_ref[:, k * A_WIDTH:(k + 1) * A_WIDTH])
        o_ref[...] = z.astype(o_ref.dtype)


def _const_spec(shape, prefix=()):
    nd = len(shape)
    return pl.BlockSpec((None,) * len(prefix) + tuple(shape), lambda *_: tuple(prefix) + (0,) * nd,
                        pipeline_mode=pl.Buffered(1))


def _ffn_specs(layer):
    return [_const_spec((D_MODEL, D_FF), layer), _const_spec((D_MODEL, D_FF), layer),
            _const_spec((D_FF, D_MODEL), layer)]


def _stage1_even(x2d, g0, ffn, ffn_layer, g1, win, j):
    n = x2d.shape[0]
    tm = TOKEN_TILE
    row = lambda i: (i, 0)
    part_spec = pl.BlockSpec((tm, A_WIDTH), row)
    out_shape = [jax.ShapeDtypeStruct((n, D_MODEL), F32)]
    out_shape += [jax.ShapeDtypeStruct((n, A_WIDTH), F32 if k == 3 else BF16) for k in range(EVEN_PARTS)]
    return pl.pallas_call(
        _s1_even_kernel,
        grid=(n // tm,),
        in_specs=[pl.BlockSpec((tm, D_MODEL), row), _const_spec((1, D_MODEL))] + _ffn_specs(ffn_layer) + [
                  _const_spec((1, D_MODEL)), _const_spec((D_MODEL, EVEN_PARTS * A_WIDTH), (j,))],
        out_specs=[pl.BlockSpec((tm, D_MODEL), row)] + [part_spec] * EVEN_PARTS,
        out_shape=out_shape,
        scratch_shapes=[pltpu.VMEM((tm, D_MODEL), F32)],
        compiler_params=pltpu.CompilerParams(dimension_semantics=("parallel",),
                                             vmem_limit_bytes=VMEM_LIMIT_BYTES),
        name="stage1_even",
    )(x2d, g0, *ffn, g1, win)


def _s1_odd_kernel(x_ref, posr_ref, invfc_ref, g0_ref, wg_ref, wu_ref, wd_ref, g1_ref,
                   win_ref, wropet_ref, qan_ref, kvan_ref, wqbt_ref, wkvk_ref, wkvvt_ref, gq_ref, gkr_ref, kn_ref,
                   indk_ref, splat_ref,
                   x1_ref, qt_ref, k_ref, vt_ref, acc_ref):
    x = x_ref[0]
    hn = _rms(x, g0_ref[...]).astype(BF16)
    _swiglu_into(acc_ref, hn, wg_ref, wu_ref, wd_ref)
    x1 = x + 0.5 * acc_ref[...]
    x1_ref[0] = x1
    h = _rms(x1, g1_ref[...]).astype(BF16)
    z = _dot(h, win_ref[...])
    cqn = _rms(z[:, :Q_LORA], qan_ref[...]).astype(BF16)
    ckvn = _rms(z[:, Q_LORA:Q_LORA + KV_LORA], kvan_ref[...]).astype(BF16)
    k_rope = z[:, Q_LORA + KV_LORA:]
    q_t = _dot_nt(wqbt_ref[...], cqn)
    k_nope = _dot(ckvn, wkvk_ref[...])
    vt_ref[0] = _dot_nt(wkvvt_ref[...], ckvn).astype(BF16)

    ang_t = invfc_ref[...] * posr_ref[0]
    cos_f, sin_f = jnp.cos(ang_t), jnp.sin(ang_t)
    gq = gq_ref[...]
    half = C_ROPE // 2
    for hd in range(C_HEADS):
        blk = q_t[hd * C_QK:(hd + 1) * C_QK, :]
        ss = jnp.sum(blk * blk, axis=0, keepdims=True)
        r = lax.rsqrt(ss * (1.0 / C_QK) + EPS) * (ATTN_SCALE * LOG2E)
        sc = blk * r * gq
        a1 = sc[C_NOPE:C_NOPE + half, :]
        a2 = sc[C_NOPE + half:, :]
        qt_ref[0, hd, 0:C_NOPE, :] = sc[:C_NOPE, :].astype(BF16)
        qt_ref[0, hd, C_NOPE:C_NOPE + half, :] = (a1 * cos_f - a2 * sin_f).astype(BF16)
        qt_ref[0, hd, C_NOPE + half:C_QK, :] = (a2 * cos_f + a1 * sin_f).astype(BF16)

    kr_t = _dot_nt(wropet_ref[...], h) * gkr_ref[...]
    b1, b2 = kr_t[:half, :], kr_t[half:, :]
    rot_t = jnp.concatenate([b1 * cos_f - b2 * sin_f, b2 * cos_f + b1 * sin_f,
                             jnp.zeros((LANES - C_ROPE, kr_t.shape[1]), F32)], axis=0)
    kr_rot = rot_t.T

    lane = lax.broadcasted_iota(jnp.int32, (1, LANES), 1)
    kn_nope = kn_ref[...]
    k2_hi, k2_lo = _split2(jnp.concatenate([k_nope * k_nope, k_rope * k_rope], axis=1))
    ssk = _dot(k2_hi, indk_ref[...]) + _dot(k2_lo, indk_ref[...])
    rk = jnp.where(lane < C_HEADS, lax.rsqrt(ssk * (1.0 / C_QK) + EPS), 0.0)
    rk_hi, rk_lo = _split2(rk)
    rk_b = _dot(rk_hi, splat_ref[...]) + _dot(rk_lo, splat_ref[...])
    for hd in range(C_HEADS):
        cols = slice(hd * C_NOPE, (hd + 1) * C_NOPE)
        k_ref[0, hd, :, 0:C_NOPE] = (k_nope[:, cols] * rk_b[:, cols] * kn_nope).astype(BF16)
        k_ref[0, hd, :, C_NOPE:C_QK] = (kr_rot * rk_b[:, cols])[:, :C_ROPE].astype(BF16)


def _stage1_odd(x3d, posr, op, g0, ffn, ffn_layer, g1):
    bsz, seq, _ = x3d.shape
    tm = TOKEN_TILE
    tile = lambda b, i: (b, i, 0)
    return pl.pallas_call(
        _s1_odd_kernel,
        grid=(bsz, seq // tm),
        in_specs=[pl.BlockSpec((1, tm, D_MODEL), tile), pl.BlockSpec((1, 1, tm), lambda b, i: (b, 0, i)),
                  _const_spec((C_ROPE // 2, tm)), _const_spec((1, D_MODEL))]
                 + _ffn_specs(ffn_layer) + [
                  _const_spec((1, D_MODEL)), _const_spec((D_MODEL, ODD_IN_PAD)), _const_spec((C_ROPE, D_MODEL)),
                  _const_spec((1, Q_LORA)), _const_spec((1, KV_LORA)),
                  _const_spec((C_HEADS * C_QK, Q_LORA)), _const_spec((KV_LORA, C_HEADS * C_NOPE)),
                  _const_spec((C_HEADS * C_V, KV_LORA)), _const_spec((C_QK, tm)), _const_spec((C_ROPE, tm)),
                  _const_spec((1, C_NOPE)), _const_spec((C_HEADS * C_NOPE + LANES, LANES)),
                  _const_spec((LANES, C_HEADS * C_NOPE))],
        out_specs=[pl.BlockSpec((1, tm, D_MODEL), tile),
                   pl.BlockSpec((1, C_HEADS, C_QK, tm), lambda b, i: (b, 0, 0, i)),
                   pl.BlockSpec((1, C_HEADS, tm, C_QK), lambda b, i: (b, 0, i, 0)),
                   pl.BlockSpec((1, C_HEADS * C_V, tm), lambda b, i: (b, 0, i))],
        out_shape=[jax.ShapeDtypeStruct((bsz, seq, D_MODEL), F32),
                   jax.ShapeDtypeStruct((bsz, C_HEADS, C_QK, seq), BF16),
                   jax.ShapeDtypeStruct((bsz, C_HEADS, seq, C_QK), BF16),
                   jax.ShapeDtypeStruct((bsz, C_HEADS * C_V, seq), BF16)],
        scratch_shapes=[pltpu.VMEM((tm, D_MODEL), F32)],
        compiler_params=pltpu.CompilerParams(dimension_semantics=("parallel", "parallel"),
                                             vmem_limit_bytes=VMEM_LIMIT_BYTES),
        name="stage1_odd",
    )(x3d, posr, op["invfc"], g0, *ffn, g1, op["win"], op["wropet"], op["qan"], op["kvan"],
      op["wqbt"], op["wkvk"], op["wkvvt"], op["gq"], op["gkr"], op["kn"], op["indk"], op["splat"])


def _s2_kernel(x_ref, m_ref, p_ref, wo_ref, g2_ref, wg_ref, wu_ref, wd_ref,
               g3_ref, wpg_ref, wpp_ref, g4_ref, o_ref, acc_ref):
    x = x_ref[...] + _dot(m_ref[...], wo_ref[...])
    hn = _rms(x, g2_ref[...]).astype(BF16)
    _swiglu_into(acc_ref, hn, wg_ref, wu_ref, wd_ref)
    x = x + 0.5 * acc_ref[...]
    h3 = _rms(x, g3_ref[...]).astype(BF16)
    gate = _sigmoid(_dot(h3, wpg_ref[...]))
    proj = _dot(p_ref[...].astype(BF16), wpp_ref[...])
    o_ref[...] = x + _rms(gate * proj, g4_ref[...])


def _stage2(x2d, mixed, p3d, layer, wo, j, g2, ffn, ffn_layer, g3, wpg, wpp, g4):
    n = x2d.shape[0]
    tm = TOKEN_TILE
    row = lambda i: (i, 0)
    return pl.pallas_call(
        _s2_kernel,
        grid=(n // tm,),
        in_specs=[pl.BlockSpec((tm, D_MODEL), row), pl.BlockSpec((tm, D_MODEL), row),
                  pl.BlockSpec((None, tm, D_PLE), lambda i: (layer, i, 0)),
                  _const_spec((D_MODEL, D_MODEL), (j,)), _const_spec((1, D_MODEL))] + _ffn_specs(ffn_layer) + [
                  _const_spec((1, D_MODEL)), _const_spec((D_MODEL, D_MODEL), (layer,)),
                  _const_spec((D_PLE, D_MODEL), (layer,)), _const_spec((1, D_MODEL))],
        out_specs=pl.BlockSpec((tm, D_MODEL), row),
        out_shape=jax.ShapeDtypeStruct((n, D_MODEL), F32),
        scratch_shapes=[pltpu.VMEM((tm, D_MODEL), F32)],
        compiler_params=pltpu.CompilerParams(dimension_semantics=("parallel",),
                                             vmem_limit_bytes=VMEM_LIMIT_BYTES),
        name="stage2",
    )(x2d, mixed, p3d, wo, g2, *ffn, g3, wpg, wpp, g4)


def _level_map(ts):
    t = np.arange(ts)[:, None]
    s = np.arange(ts)[None, :]
    lvl = np.full((ts, ts), len(HGRN_LEVELS) + 1, np.int32)
    lvl[(t // SUBLANES == s // SUBLANES) & (s <= t)] = len(HGRN_LEVELS)
    for li, h in enumerate(HGRN_LEVELS):
        own = (t // (2 * h) == s // (2 * h)) & (t % (2 * h) >= h) & (s % (2 * h) < h)
        lvl[own] = li
    return lvl


def _even_core_kernel(u_ref, v_ref, q_ref, f_ref, i_ref, g_ref,
                      vnorm_ref, ind_ref, wcat_ref, bfull_ref, lb_ref, onorm_ref, ltri_ref, lvl_ref,
                      o_ref, state_ref):
    nb = HGRN_BATCH
    ts = HGRN_TILE
    sub = HGRN_SUB
    n_lvl = len(HGRN_LEVELS)
    rows_all = nb * ts

    def stacked(ref):
        return ref[...].reshape(rows_all, ref.shape[-1])

    u = _gelu_tanh(stacked(u_ref).astype(F32))
    v = _gelu_tanh(stacked(v_ref).astype(F32))
    ss = _dot((v * v).astype(BF16), ind_ref[...])
    vn = (v * lax.rsqrt(ss * (1.0 / A_HEAD_DIM) + EPS) * vnorm_ref[...]).astype(BF16)
    lane_head = lax.broadcasted_iota(jnp.int32, (1, A_WIDTH), 1) // A_HEAD_DIM
    for c in range(rows_all // A_CHUNK):
        rows = slice(c * A_CHUNK, (c + 1) * A_CHUNK)
        vc = vn[rows, :]
        zero = jnp.zeros_like(vc)
        vbd = jnp.concatenate([jnp.where(lane_head == hd, vc, zero) for hd in range(A_HEADS)], axis=0)
        mixed = _dot(wcat_ref[...], vbd) + bfull_ref[...]
        bi, local = divmod(c * A_CHUNK, ts)
        o_ref[bi, local:local + A_CHUNK, 0:A_WIDTH] = (u[rows, :] * mixed).astype(BF16)

    @pl.when(pl.program_id(1) == 0)
    def _():
        state_ref[...] = jnp.zeros_like(state_ref)

    lb = lb_ref[...]
    fg = lb + (1.0 - lb) * _sigmoid(stacked(f_ref))
    lf = jnp.log2(jnp.maximum(fg, F_MIN))
    kk = 1.0 - fg
    qq = stacked(q_ref).astype(F32)
    lf_hi, lf_lo = _split2(lf)
    ltri = ltri_ref[...]
    seqs = [slice(bi * ts, (bi + 1) * ts) for bi in range(nb)]
    b = jnp.concatenate([_dot(ltri, lf_hi[r, :]) + _dot(ltri, lf_lo[r, :]) for r in seqs], axis=0)

    q_lvls, k_lvls = [], []
    for h in HGRN_LEVELS:
        q_parts, k_parts = [], []
        zero = jnp.zeros((h, B_WIDTH), F32)
        for j in range(rows_all // (2 * h)):
            r0 = j * 2 * h
            bm = b[r0 + h - 1:r0 + h, :]
            lo_rows = slice(r0, r0 + h)
            up_rows = slice(r0 + h, r0 + 2 * h)
            k_parts += [kk[lo_rows, :] * jnp.exp2(bm - b[lo_rows, :]), zero]
            q_parts += [zero, qq[up_rows, :] * jnp.exp2(b[up_rows, :] - bm)]
        q_lvls.append(jnp.concatenate(q_parts, axis=0).astype(BF16))
        k_lvls.append(jnp.concatenate(k_parts, axis=0).astype(BF16))
    b3 = b.reshape(rows_all // SUBLANES, SUBLANES, B_WIDTH)
    bref = jnp.broadcast_to(b3[:, 3:4, :], b3.shape).reshape(rows_all, B_WIDTH)
    q_lvls.append((qq * jnp.exp2(b - bref)).astype(BF16))
    k_lvls.append((kk * jnp.exp2(bref - b)).astype(BF16))

    q_top, k_top, k_out, carry = [], [], [], []
    for bi in range(nb):
        r0 = bi * ts
        bm = b[r0 + sub - 1:r0 + sub, :]
        q_top.append((qq[r0 + sub:r0 + ts, :] * jnp.exp2(b[r0 + sub:r0 + ts, :] - bm)).astype(BF16))
        k_top.append((kk[r0:r0 + sub, :] * jnp.exp2(bm - b[r0:r0 + sub, :])).astype(BF16))
        b_last = b[r0 + ts - 1:r0 + ts, :]
        k_out.append((kk[seqs[bi], :] * jnp.exp2(b_last - b[seqs[bi], :])).astype(BF16))
        carry.append(jnp.exp2(b_last))
    q_in = (qq * jnp.exp2(b)).astype(BF16)
    vv = stacked(i_ref)
    gate = _silu(stacked(g_ref).astype(F32))
    onorm = onorm_ref[...]

    lvl = lvl_ref[...]
    masks = [lvl == li for li in range(n_lvl + 1)]
    units = [(bi, hd, slice(hd * B_DIM, (hd + 1) * B_DIM)) for hd in range(B_HEADS) for bi in range(nb)]
    states = {(bi, hd): state_ref[bi, hd] for bi, hd, _ in units}
    inter = {(bi, hd): _dot_nt(q_in[seqs[bi], cols], states[bi, hd].astype(BF16)) for bi, hd, cols in units}
    for bi, hd, cols in units:
        state_ref[bi, hd] = (states[bi, hd] * carry[bi][:, cols]
                             + _dot_tn(vv[seqs[bi], cols], k_out[bi][:, cols]))
    score_blocks = {}
    for bi, hd, cols in units:
        diag = []
        for a in range(ts // sub):
            rows = slice(bi * ts + a * sub, bi * ts + (a + 1) * sub)
            blk = jnp.where(masks[n_lvl], _dot_nt(q_lvls[n_lvl][rows, cols], k_lvls[n_lvl][rows, cols]), 0.0)
            for li in range(n_lvl):
                blk = jnp.where(masks[li], _dot_nt(q_lvls[li][rows, cols], k_lvls[li][rows, cols]), blk)
            diag.append(blk.astype(BF16))
        top = _dot_nt(q_top[bi][:, cols], k_top[bi][:, cols]).astype(BF16)
        score_blocks[bi, hd] = (diag[0], jnp.concatenate([top, diag[1]], axis=1))
    outs = {}
    for bi, hd, cols in units:
        v_h = vv[seqs[bi], cols]
        first, second = score_blocks[bi, hd]
        o_intra = jnp.concatenate([_dot(first, v_h[:sub, :]), _dot(second, v_h)], axis=0)
        outs[bi, hd] = o_intra + inter[bi, hd]
    for bi, hd, cols in units:
        on = _rms(outs[bi, hd], onorm[:, cols]) * gate[seqs[bi], cols]
        o_ref[bi, :, A_WIDTH + hd * B_DIM:A_WIDTH + (hd + 1) * B_DIM] = on.astype(BF16)


def _even_core(parts, vnorm, ind, wcat, bfull, lb, onorm, ltri, lvl):
    bsz, seq, _ = parts[0].shape
    ts = HGRN_TILE
    nb = HGRN_BATCH
    tile = lambda b, i: (b, i, 0)
    part_spec = pl.BlockSpec((nb, ts, A_WIDTH), tile)
    return pl.pallas_call(
        _even_core_kernel,
        grid=(bsz // nb, seq // ts),
        in_specs=[part_spec] * EVEN_PARTS + [
            _const_spec((1, A_WIDTH)), _const_spec((A_WIDTH, A_WIDTH)),
            _const_spec((A_CHUNK, A_HEADS * A_CHUNK)), _const_spec((A_CHUNK, A_WIDTH)),
            _const_spec((1, B_WIDTH)), _const_spec((1, B_WIDTH)),
            _const_spec((ts, ts)), _const_spec((HGRN_SUB, HGRN_SUB))],
        out_specs=pl.BlockSpec((nb, ts, D_MODEL), tile),
        out_shape=jax.ShapeDtypeStruct((bsz, seq, D_MODEL), BF16),
        scratch_shapes=[pltpu.VMEM((nb, B_HEADS, B_DIM, B_DIM), F32)],
        compiler_params=pltpu.CompilerParams(dimension_semantics=("parallel", "arbitrary"),
                                             vmem_limit_bytes=VMEM_LIMIT_BYTES),
        name="even_core",
    )(*parts, vnorm, ind, wcat, bfull, lb, onorm, ltri, lvl)


def _attn_kernel(qt_ref, k_ref, vt_ref, o_ref, m_ref, l_ref, acc_ref, sa_ref, sb_ref):
    tq = ATTN_TILE
    tks = tq // ATTN_KEY_SPLIT
    qi = pl.program_id(2)
    heads = range(ATTN_HEADS_PER_STEP)

    m_ref[...] = jnp.full_like(m_ref, NEG_BIG)
    l_ref[...] = jnp.zeros_like(l_ref)
    acc_ref[...] = jnp.zeros_like(acc_ref)

    def key_subs(j):
        start = pl.multiple_of(j * tq, tq)
        return [pl.ds(pl.multiple_of(start + c * tks, tks), tks) for c in range(ATTN_KEY_SPLIT)]

    def scores(j, s_ref):
        for hd in heads:
            qt = qt_ref[0, hd]
            for c, sub in enumerate(key_subs(j)):
                s_ref[hd, c * tks:(c + 1) * tks, :] = _dot(k_ref[0, hd, sub, :], qt)

    def consume(j, s_ref, masked):
        if masked:
            key = lax.broadcasted_iota(jnp.int32, (tks, tq), 0)
            qry = lax.broadcasted_iota(jnp.int32, (tks, tq), 1)
        s3 = {}
        for hd in heads:
            for c in range(ATTN_KEY_SPLIT):
                s = s_ref[hd, c * tks:(c + 1) * tks, :]
                if masked:
                    s = jnp.where(key + c * tks <= qry, s, NEG_BIG)
                s3[hd, c] = s.reshape(tks // SUBLANES, SUBLANES, tq)
        m_new, alpha = {}, {}
        for hd in heads:
            m_prev = m_ref[hd]
            cmax = functools.reduce(jnp.maximum, [jnp.max(s3[hd, c], axis=0) for c in range(ATTN_KEY_SPLIT)])
            cmax = jnp.max(cmax, axis=0, keepdims=True)
            m_new[hd] = jnp.maximum(m_prev, jnp.broadcast_to(cmax, (SUBLANES, tq)))
            alpha[hd] = jnp.exp2(m_prev - m_new[hd])
            m_ref[hd] = m_new[hd]
        ones = jnp.ones((2 * SUBLANES, tks), BF16)
        for hd in heads:
            acc3 = acc_ref[hd].reshape(C_V // SUBLANES, SUBLANES, tq) * alpha[hd][None]
            acc = acc3.reshape(C_V, tq)
            csum = jnp.zeros((SUBLANES, tq), F32)
            for c, sub in enumerate(key_subs(j)):
                p = jnp.exp2(s3[hd, c] - m_new[hd][None]).reshape(tks, tq).astype(BF16)
                lhs = jnp.concatenate([vt_ref[0, hd * C_V:(hd + 1) * C_V, sub], ones], axis=0)
                pv = _dot(lhs, p)
                acc = acc + pv[:C_V, :]
                csum = csum + pv[C_V:C_V + SUBLANES, :]
            l_ref[hd] = alpha[hd] * l_ref[hd] + csum
            acc_ref[hd] = acc

    scores(0, sa_ref)
    pairs = qi // 2

    def body(t, carry):
        j = 2 * t
        scores(j + 1, sb_ref)
        consume(j, sa_ref, False)
        scores(j + 2, sa_ref)
        consume(j + 1, sb_ref, False)
        return carry

    lax.fori_loop(0, pairs, body, 0)

    @pl.when(qi % 2 == 0)
    def _():
        consume(qi, sa_ref, True)

    @pl.when(qi % 2 == 1)
    def _():
        scores(qi, sb_ref)
        consume(qi - 1, sa_ref, False)
        consume(qi, sb_ref, True)
    for hd in range(ATTN_HEADS_PER_STEP):
        inv_l = 1.0 / l_ref[hd]
        out_t = acc_ref[hd].reshape(C_V // SUBLANES, SUBLANES, tq) * inv_l[None]
        o_ref[0, :, hd * C_V:(hd + 1) * C_V] = out_t.reshape(C_V, tq).T.astype(BF16)


def _attention(qt, k, vt):
    bsz, heads, seq, _ = k.shape
    tq = ATTN_TILE
    g = ATTN_HEADS_PER_STEP
    return pl.pallas_call(
        _attn_kernel,
        grid=(bsz, heads // g, seq // tq),
        in_specs=[pl.BlockSpec((1, g, C_QK, tq), lambda b, h, i: (b, h, 0, i)),
                  pl.BlockSpec((1, g, seq, C_QK), lambda b, h, i: (b, h, 0, 0)),
                  pl.BlockSpec((1, g * C_V, seq), lambda b, h, i: (b, h, 0))],
        out_specs=pl.BlockSpec((1, tq, g * C_V), lambda b, h, i: (b, i, h)),
        out_shape=jax.ShapeDtypeStruct((bsz, seq, heads * C_V), BF16),
        scratch_shapes=[pltpu.VMEM((g, SUBLANES, tq), F32), pltpu.VMEM((g, SUBLANES, tq), F32),
                        pltpu.VMEM((g, C_V, tq), F32), pltpu.VMEM((g, tq, tq), F32),
                        pltpu.VMEM((g, tq, tq), F32)],
        compiler_params=pltpu.CompilerParams(dimension_semantics=("parallel", "parallel", "arbitrary"),
                                             vmem_limit_bytes=VMEM_LIMIT_BYTES),
        name="mla_attention",
    )(qt, k, vt)


def _row(v):
    return v.reshape(1, -1).astype(F32)


def _pad_cols(w, total):
    return jnp.pad(w, ((0, 0), (0, total - w.shape[1])))


def _even_params(v_norm, w_s, b_s, lb, out_norm):
    causal = np.tril(np.ones((A_CHUNK, A_CHUNK), bool))
    w = jnp.where(causal[None], w_s, 0.0)
    wcat = jnp.transpose(w, (1, 0, 2)).reshape(A_CHUNK, A_HEADS * A_CHUNK)
    bfull = jnp.repeat(b_s.T, A_HEAD_DIM, axis=1)
    head_of = np.arange(A_WIDTH) // A_HEAD_DIM
    ind = (head_of[:, None] == head_of[None, :]).astype(np.float32)
    ltri = np.tril(np.ones((HGRN_TILE, HGRN_TILE), np.float32))
    return dict(vnorm=_row(v_norm), ind=jnp.asarray(ind, BF16),
                wcat=wcat.astype(BF16), bfull=bfull.astype(F32), lb=_row(lb),
                onorm=_row(jnp.tile(out_norm, B_HEADS)), ltri=jnp.asarray(ltri, BF16),
                lvl=jnp.asarray(_level_map(HGRN_SUB)))


def _odd_params(w_in, q_a_norm, kv_a_norm, w_q_b, w_kv_b, q_norm, k_norm):
    tm = TOKEN_TILE
    win = _pad_cols(w_in, ODD_IN_PAD)
    wkv = w_kv_b.reshape(KV_LORA, C_HEADS, C_NOPE + C_V)
    wkvk = wkv[:, :, :C_NOPE].reshape(KV_LORA, C_HEADS * C_NOPE)
    wkvvt = wkv[:, :, C_NOPE:].reshape(KV_LORA, C_HEADS * C_V).T
    inv_freq = ROPE_THETA ** (-jnp.arange(0, C_ROPE, 2, dtype=F32) / C_ROPE)
    invfc = jnp.broadcast_to(inv_freq[:, None], (C_ROPE // 2, tm))
    head_of = np.arange(C_HEADS * C_NOPE) // C_NOPE
    splat = (head_of[:, None] == np.arange(LANES)[None, :]).astype(np.float32)
    rope_rows = np.broadcast_to((np.arange(LANES) < C_HEADS).astype(np.float32), (LANES, LANES))
    indk = np.concatenate([splat, rope_rows], axis=0)
    return dict(win=win.astype(BF16), wropet=w_in[:, Q_LORA + KV_LORA:].T.astype(BF16),
                qan=_row(q_a_norm), kvan=_row(kv_a_norm),
                wqbt=w_q_b.T.astype(BF16), wkvk=wkvk.astype(BF16), wkvvt=wkvvt.astype(BF16),
                gq=jnp.broadcast_to(q_norm.astype(F32)[:, None], (C_QK, tm)),
                gkr=jnp.broadcast_to(k_norm.astype(F32)[C_NOPE:, None], (C_ROPE, tm)),
                kn=_row(k_norm[:C_NOPE]), invfc=invfc,
                indk=jnp.asarray(indk, BF16), splat=jnp.asarray(splat.T, BF16))


def kernel(x, p, positions, norm_gains, ffn_w_gate, ffn_w_up, ffn_w_down, ple_w_gate, ple_w_proj,
           even_w_in, gmlp_v_norm, gmlp_w_s, gmlp_b_s, hgrn_lb_raw, hgrn_out_norm, even_w_out,
           mla_w_in, mla_q_a_norm, mla_kv_a_norm, mla_w_q_b, mla_w_kv_b, mla_q_norm, mla_k_norm, mla_w_out):
    bsz, seq, d = x.shape
    depth = norm_gains.shape[0]
    n = bsz * seq
    assert d == D_MODEL and seq % ATTN_TILE == 0 and seq % TOKEN_TILE == 0 and seq % HGRN_TILE == 0
    assert bsz % HGRN_BATCH == 0

    lb_sm = jax.nn.softmax(hgrn_lb_raw.astype(F32), axis=0)
    lower_bounds = jnp.clip(jnp.cumsum(lb_sm, axis=0) - lb_sm[0], 0.0, LB_MAX)
    posr = positions.astype(F32).reshape(bsz, 1, seq)

    ffn = tuple(w.astype(BF16) for w in (ffn_w_gate, ffn_w_up, ffn_w_down))
    ple_wg, ple_wp = ple_w_gate.astype(BF16), ple_w_proj.astype(BF16)
    even_win, even_wout, mla_wout = (w.astype(BF16) for w in (even_w_in, even_w_out, mla_w_out))
    p3d = p.reshape(depth, n, D_PLE)

    xc = x.reshape(n, d)
    for i in range(depth):
        g = norm_gains[i]
        j = i // 2
        if i % 2 == 0:
            ep = _even_params(gmlp_v_norm[j], gmlp_w_s[j], gmlp_b_s[j], lower_bounds[j], hgrn_out_norm[j])
            outs = _stage1_even(xc, _row(g[0]), ffn, (i, 0), _row(g[1]), even_win, j)
            xc = outs[0]
            parts = [o.reshape(bsz, seq, A_WIDTH) for o in outs[1:]]
            mixed = _even_core(parts, ep["vnorm"], ep["ind"], ep["wcat"], ep["bfull"], ep["lb"],
                               ep["onorm"], ep["ltri"], ep["lvl"])
            w_out = even_wout
        else:
            op = _odd_params(mla_w_in[j], mla_q_a_norm[j], mla_kv_a_norm[j], mla_w_q_b[j], mla_w_kv_b[j],
                             mla_q_norm[j], mla_k_norm[j])
            x1, qt, k, vt = _stage1_odd(xc.reshape(bsz, seq, d), posr, op, _row(g[0]), ffn, (i, 0),
                                        _row(g[1]))
            xc = x1.reshape(n, d)
            mixed = _attention(qt, k, vt)
            w_out = mla_wout
        xc = _stage2(xc, mixed.reshape(n, d), p3d, i, w_out, j, _row(g[2]), ffn, (i, 1), _row(g[3]),
                     ple_wg, ple_wp, _row(g[4]))
    return xc.reshape(bsz, seq, d)
```

```python
import functools
import math

import numpy as np
import jax
import jax.numpy as jnp
from jax import lax
from jax.experimental import pallas as pl
from jax.experimental.pallas import tpu as pltpu

F32 = jnp.float32
BF16 = jnp.bfloat16

D_MODEL = 1024
D_FF = 2816
D_PLE = 256
EPS = 1e-6
NEG_BIG = -1e30
F_MIN = 1e-6
LB_MAX = 0.999

A_HEADS = 8
A_HEAD_DIM = 64
A_WIDTH = A_HEADS * A_HEAD_DIM
A_CHUNK = 128
B_HEADS = 4
B_DIM = 128
B_WIDTH = B_HEADS * B_DIM
EVEN_PARTS = 6

C_HEADS = 8
C_NOPE = 128
C_ROPE = 64
C_V = 128
C_QK = C_NOPE + C_ROPE
Q_LORA = 384
KV_LORA = 256
ODD_IN_PAD = Q_LORA + KV_LORA + 128
ROPE_THETA = 10000.0
ATTN_SCALE = C_QK ** -0.5
LOG2E = math.log2(math.e)

LANES = 128
SUBLANES = 8
MXU_DIM = 256
VMEM_LIMIT_BYTES = 60000 * 1024

TOKEN_TILE = 512
FF_CHUNK = 2 * MXU_DIM
HGRN_TILE = 256
HGRN_BATCH = 2
HGRN_SUB = HGRN_TILE // 2
HGRN_LEVELS = (64, 32, 16, 8)
ATTN_TILE = 512
ATTN_HEADS_PER_STEP = 4
ATTN_KEY_SPLIT = 2


def _ff_chunks():
    chunks, off = [], 0
    while off < D_FF:
        size = min(FF_CHUNK, D_FF - off)
        chunks.append((off, size))
        off += size
    return tuple(chunks)


def _sigmoid(x):
    return 0.5 + 0.5 * jnp.tanh(0.5 * x)


def _silu(x):
    t = 0.5 * x
    return t + t * jnp.tanh(t)


def _gelu_tanh(x):
    c = math.sqrt(2.0 / math.pi)
    hx = 0.5 * x
    return hx + hx * jnp.tanh(x * (c + (c * 0.044715) * (x * x)))


def _rms(x, g):
    ms = jnp.mean(x * x, axis=-1, keepdims=True)
    return x * lax.rsqrt(ms + EPS) * g


def _dot(a, b):
    return jnp.dot(a, b, preferred_element_type=F32)


def _dot_nt(a, b):
    return lax.dot_general(a, b, (((1,), (1,)), ((), ())), preferred_element_type=F32)


def _dot_tn(a, b):
    return lax.dot_general(a, b, (((0,), (0,)), ((), ())), preferred_element_type=F32)


def _split2(x):
    hi = x.astype(BF16)
    lo = (x - hi.astype(F32)).astype(BF16)
    return hi, lo


def _swiglu_into(acc_ref, hn, wg_ref, wu_ref, wd_ref):
    for ci, (off, size) in enumerate(_ff_chunks()):
        gate = _dot(hn, wg_ref[:, off:off + size])
        up = _dot(hn, wu_ref[:, off:off + size])
        act = (_silu(gate) * up).astype(BF16)
        part = _dot(act, wd_ref[off:off + size, :])
        if ci == 0:
            acc_ref[...] = part
        else:
            acc_ref[...] += part


def _s1_even_kernel(x_ref, g0_ref, wg_ref, wu_ref, wd_ref, g1_ref, win_ref,
                    x1_ref, u_ref, v_ref, q_ref, f_ref, i_ref, gg_ref, acc_ref):
    x = x_ref[...]
    hn = _rms(x, g0_ref[...]).astype(BF16)
    _swiglu_into(acc_ref, hn, wg_ref, wu_ref, wd_ref)
    x1 = x + 0.5 * acc_ref[...]
    x1_ref[...] = x1
    h = _rms(x1, g1_ref[...]).astype(BF16)
    outs = (u_ref, v_ref, q_ref, f_ref, i_ref, gg_ref)
    for k, o_ref in enumerate(outs):
        z = _dot(h, win_ref[:, k * A_WIDTH:(k + 1) * A_WIDTH])
        o_ref[...] = z.astype(o_ref.dtype)


def _const_spec(shape, prefix=()):
    nd = len(shape)
    return pl.BlockSpec((None,) * len(prefix) + tuple(shape), lambda *_: tuple(prefix) + (0,) * nd,
                        pipeline_mode=pl.Buffered(1))


def _ffn_specs(layer):
    return [_const_spec((D_MODEL, D_FF), layer), _const_spec((D_MODEL, D_FF), layer),
            _const_spec((D_FF, D_MODEL), layer)]


def _stage1_even(x2d, g0, ffn, ffn_layer, g1, win, j):
    n = x2d.shape[0]
    tm = TOKEN_TILE
    row = lambda i: (i, 0)
    part_spec = pl.BlockSpec((tm, A_WIDTH), row)
    out_shape = [jax.ShapeDtypeStruct((n, D_MODEL), F32)]
    out_shape += [jax.ShapeDtypeStruct((n, A_WIDTH), F32 if k == 3 else BF16) for k in range(EVEN_PARTS)]
    return pl.pallas_call(
        _s1_even_kernel,
        grid=(n // tm,),
        in_specs=[pl.BlockSpec((tm, D_MODEL), row), _const_spec((1, D_MODEL))] + _ffn_specs(ffn_layer) + [
                  _const_spec((1, D_MODEL)), _const_spec((D_MODEL, EVEN_PARTS * A_WIDTH), (j,))],
        out_specs=[pl.BlockSpec((tm, D_MODEL), row)] + [part_spec] * EVEN_PARTS,
        out_shape=out_shape,
        scratch_shapes=[pltpu.VMEM((tm, D_MODEL), F32)],
        compiler_params=pltpu.CompilerParams(dimension_semantics=("parallel",),
                                             vmem_limit_bytes=VMEM_LIMIT_BYTES),
        name="stage1_even",
    )(x2d, g0, *ffn, g1, win)


def _s1_odd_kernel(x_ref, posr_ref, invfc_ref, g0_ref, wg_ref, wu_ref, wd_ref, g1_ref,
                   win_ref, wropet_ref, qan_ref, kvan_ref, wqbt_ref, wkvk_ref, wkvvt_ref, gq_ref, gkr_ref, kn_ref,
                   indk_ref, splat_ref,
                   x1_ref, qt_ref, k_ref, vt_ref, acc_ref):
    x = x_ref[0]
    hn = _rms(x, g0_ref[...]).astype(BF16)
    _swiglu_into(acc_ref, hn, wg_ref, wu_ref, wd_ref)
    x1 = x + 0.5 * acc_ref[...]
    x1_ref[0] = x1
    h = _rms(x1, g1_ref[...]).astype(BF16)
    z = _dot(h, win_ref[...])
    cqn = _rms(z[:, :Q_LORA], qan_ref[...]).astype(BF16)
    ckvn = _rms(z[:, Q_LORA:Q_LORA + KV_LORA], kvan_ref[...]).astype(BF16)
    k_rope = z[:, Q_LORA + KV_LORA:]
    q_t = _dot_nt(wqbt_ref[...], cqn)
    k_nope = _dot(ckvn, wkvk_ref[...])
    vt_ref[0] = _dot_nt(wkvvt_ref[...], ckvn).astype(BF16)

    ang_t = invfc_ref[...] * posr_ref[0]
    cos_f, sin_f = jnp.cos(ang_t), jnp.sin(ang_t)
    gq = gq_ref[...]
    half = C_ROPE // 2
    for hd in range(C_HEADS):
        blk = q_t[hd * C_QK:(hd + 1) * C_QK, :]
        ss = jnp.sum(blk * blk, axis=0, keepdims=True)
        r = lax.rsqrt(ss * (1.0 / C_QK) + EPS) * (ATTN_SCALE * LOG2E)
        sc = blk * r * gq
        a1 = sc[C_NOPE:C_NOPE + half, :]
        a2 = sc[C_NOPE + half:, :]
        qt_ref[0, hd, 0:C_NOPE, :] = sc[:C_NOPE, :].astype(BF16)
        qt_ref[0, hd, C_NOPE:C_NOPE + half, :] = (a1 * cos_f - a2 * sin_f).astype(BF16)
        qt_ref[0, hd, C_NOPE + half:C_QK, :] = (a2 * cos_f + a1 * sin_f).astype(BF16)

    kr_t = _dot_nt(wropet_ref[...], h) * gkr_ref[...]
    b1, b2 = kr_t[:half, :], kr_t[half:, :]
    rot_t = jnp.concatenate([b1 * cos_f - b2 * sin_f, b2 * cos_f + b1 * sin_f,
                             jnp.zeros((LANES - C_ROPE, kr_t.shape[1]), F32)], axis=0)
    kr_rot = rot_t.T

    lane = lax.broadcasted_iota(jnp.int32, (1, LANES), 1)
    kn_nope = kn_ref[...]
    k2_hi, k2_lo = _split2(jnp.concatenate([k_nope * k_nope, k_rope * k_rope], axis=1))
    ssk = _dot(k2_hi, indk_ref[...]) + _dot(k2_lo, indk_ref[...])
    rk = jnp.where(lane < C_HEADS, lax.rsqrt(ssk * (1.0 / C_QK) + EPS), 0.0)
    rk_hi, rk_lo = _split2(rk)
    rk_b = _dot(rk_hi, splat_ref[...]) + _dot(rk_lo, splat_ref[...])
    for hd in range(C_HEADS):
        cols = slice(hd * C_NOPE, (hd + 1) * C_NOPE)
        k_ref[0, hd, :, 0:C_NOPE] = (k_nope[:, cols] * rk_b[:, cols] * kn_nope).astype(BF16)
        k_ref[0, hd, :, C_NOPE:C_QK] = (kr_rot * rk_b[:, cols])[:, :C_ROPE].astype(BF16)


def _stage1_odd(x3d, posr, op, g0, ffn, ffn_layer, g1):
    bsz, seq, _ = x3d.shape
    tm = TOKEN_TILE
    tile = lambda b, i: (b, i, 0)
    return pl.pallas_call(
        _s1_odd_kernel,
        grid=(bsz, seq // tm),
        in_specs=[pl.BlockSpec((1, tm, D_MODEL), tile), pl.BlockSpec((1, 1, tm), lambda b, i: (b, 0, i)),
                  _const_spec((C_ROPE // 2, tm)), _const_spec((1, D_MODEL))]
                 + _ffn_specs(ffn_layer) + [
                  _const_spec((1, D_MODEL)), _const_spec((D_MODEL, ODD_IN_PAD)), _const_spec((C_ROPE, D_MODEL)),
                  _const_spec((1, Q_LORA)), _const_spec((1, KV_LORA)),
                  _const_spec((C_HEADS * C_QK, Q_LORA)), _const_spec((KV_LORA, C_HEADS * C_NOPE)),
                  _const_spec((C_HEADS * C_V, KV_LORA)), _const_spec((C_QK, tm)), _const_spec((C_ROPE, tm)),
                  _const_spec((1, C_NOPE)), _const_spec((C_HEADS * C_NOPE + LANES, LANES)),
                  _const_spec((LANES, C_HEADS * C_NOPE))],
        out_specs=[pl.BlockSpec((1, tm, D_MODEL), tile),
                   pl.BlockSpec((1, C_HEADS, C_QK, tm), lambda b, i: (b, 0, 0, i)),
                   pl.BlockSpec((1, C_HEADS, tm, C_QK), lambda b, i: (b, 0, i, 0)),
                   pl.BlockSpec((1, C_HEADS * C_V, tm), lambda b, i: (b, 0, i))],
        out_shape=[jax.ShapeDtypeStruct((bsz, seq, D_MODEL), F32),
                   jax.ShapeDtypeStruct((bsz, C_HEADS, C_QK, seq), BF16),
                   jax.ShapeDtypeStruct((bsz, C_HEADS, seq, C_QK), BF16),
                   jax.ShapeDtypeStruct((bsz, C_HEADS * C_V, seq), BF16)],
        scratch_shapes=[pltpu.VMEM((tm, D_MODEL), F32)],
        compiler_params=pltpu.CompilerParams(dimension_semantics=("parallel", "parallel"),
                                             vmem_limit_bytes=VMEM_LIMIT_BYTES),
        name="stage1_odd",
    )(x3d, posr, op["invfc"], g0, *ffn, g1, op["win"], op["wropet"], op["qan"], op["kvan"],
      op["wqbt"], op["wkvk"], op["wkvvt"], op["gq"], op["gkr"], op["kn"], op["indk"], op["splat"])


def _s2_kernel(x_ref, m_ref, p_ref, wo_ref, g2_ref, wg_ref, wu_ref, wd_ref,
               g3_ref, wpg_ref, wpp_ref, g4_ref, o_ref, acc_ref):
    x = x_ref[...] + _dot(m_ref[...], wo_ref[...])
    hn = _rms(x, g2_ref[...]).astype(BF16)
    _swiglu_into(acc_ref, hn, wg_ref, wu_ref, wd_ref)
    x = x + 0.5 * acc_ref[...]
    h3 = _rms(x, g3_ref[...]).astype(BF16)
    gate = _sigmoid(_dot(h3, wpg_ref[...]))
    proj = _dot(p_ref[...].astype(BF16), wpp_ref[...])
    o_ref[...] = x + _rms(gate * proj, g4_ref[...])


def _stage2(x2d, mixed, p3d, layer, wo, j, g2, ffn, ffn_layer, g3, wpg, wpp, g4):
    n = x2d.shape[0]
    tm = TOKEN_TILE
    row = lambda i: (i, 0)
    return pl.pallas_call(
        _s2_kernel,
        grid=(n // tm,),
        in_specs=[pl.BlockSpec((tm, D_MODEL), row), pl.BlockSpec((tm, D_MODEL), row),
                  pl.BlockSpec((None, tm, D_PLE), lambda i: (layer, i, 0)),
                  _const_spec((D_MODEL, D_MODEL), (j,)), _const_spec((1, D_MODEL))] + _ffn_specs(ffn_layer) + [
                  _const_spec((1, D_MODEL)), _const_spec((D_MODEL, D_MODEL), (layer,)),
                  _const_spec((D_PLE, D_MODEL), (layer,)), _const_spec((1, D_MODEL))],
        out_specs=pl.BlockSpec((tm, D_MODEL), row),
        out_shape=jax.ShapeDtypeStruct((n, D_MODEL), F32),
        scratch_shapes=[pltpu.VMEM((tm, D_MODEL), F32)],
        compiler_params=pltpu.CompilerParams(dimension_semantics=("parallel",),
                                             vmem_limit_bytes=VMEM_LIMIT_BYTES),
        name="stage2",
    )(x2d, mixed, p3d, wo, g2, *ffn, g3, wpg, wpp, g4)


def _level_map(ts):
    t = np.arange(ts)[:, None]
    s = np.arange(ts)[None, :]
    lvl = np.full((ts, ts), len(HGRN_LEVELS) + 1, np.int32)
    lvl[(t // SUBLANES == s // SUBLANES) & (s <= t)] = len(HGRN_LEVELS)
    for li, h in enumerate(HGRN_LEVELS):
        own = (t // (2 * h) == s // (2 * h)) & (t % (2 * h) >= h) & (s % (2 * h) < h)
        lvl[own] = li
    return lvl


def _even_core_kernel(u_ref, v_ref, q_ref, f_ref, i_ref, g_ref,
                      vnorm_ref, ind_ref, wcat_ref, bfull_ref, lb_ref, onorm_ref, ltri_ref, lvl_ref,
                      o_ref, state_ref):
    nb = HGRN_BATCH
    ts = HGRN_TILE
    sub = HGRN_SUB
    n_lvl = len(HGRN_LEVELS)
    rows_all = nb * ts

    def stacked(ref):
        return ref[...].reshape(rows_all, ref.shape[-1])

    u = _gelu_tanh(stacked(u_ref).astype(F32))
    v = _gelu_tanh(stacked(v_ref).astype(F32))
    ss = _dot((v * v).astype(BF16), ind_ref[...])
    vn = (v * lax.rsqrt(ss * (1.0 / A_HEAD_DIM) + EPS) * vnorm_ref[...]).astype(BF16)
    lane_head = lax.broadcasted_iota(jnp.int32, (1, A_WIDTH), 1) // A_HEAD_DIM
    for c in range(rows_all // A_CHUNK):
        rows = slice(c * A_CHUNK, (c + 1) * A_CHUNK)
        vc = vn[rows, :]
        zero = jnp.zeros_like(vc)
        vbd = jnp.concatenate([jnp.where(lane_head == hd, vc, zero) for hd in range(A_HEADS)], axis=0)
        mixed = _dot(wcat_ref[...], vbd) + bfull_ref[...]
        bi, local = divmod(c * A_CHUNK, ts)
        o_ref[bi, local:local + A_CHUNK, 0:A_WIDTH] = (u[rows, :] * mixed).astype(BF16)

    @pl.when(pl.program_id(1) == 0)
    def _():
        state_ref[...] = jnp.zeros_like(state_ref)

    lb = lb_ref[...]
    fg = lb + (1.0 - lb) * _sigmoid(stacked(f_ref))
    lf = jnp.log2(jnp.maximum(fg, F_MIN))
    kk = 1.0 - fg
    qq = stacked(q_ref).astype(F32)
    lf_hi, lf_lo = _split2(lf)
    ltri = ltri_ref[...]
    seqs = [slice(bi * ts, (bi + 1) * ts) for bi in range(nb)]
    b = jnp.concatenate([_dot(ltri, lf_hi[r, :]) + _dot(ltri, lf_lo[r, :]) for r in seqs], axis=0)

    q_lvls, k_lvls = [], []
    for h in HGRN_LEVELS:
        q_parts, k_parts = [], []
        zero = jnp.zeros((h, B_WIDTH), F32)
        for j in range(rows_all // (2 * h)):
            r0 = j * 2 * h
            bm = b[r0 + h - 1:r0 + h, :]
            lo_rows = slice(r0, r0 + h)
            up_rows = slice(r0 + h, r0 + 2 * h)
            k_parts += [kk[lo_rows, :] * jnp.exp2(bm - b[lo_rows, :]), zero]
            q_parts += [zero, qq[up_rows, :] * jnp.exp2(b[up_rows, :] - bm)]
        q_lvls.append(jnp.concatenate(q_parts, axis=0).astype(BF16))
        k_lvls.append(jnp.concatenate(k_parts, axis=0).astype(BF16))
    b3 = b.reshape(rows_all // SUBLANES, SUBLANES, B_WIDTH)
    bref = jnp.broadcast_to(b3[:, 3:4, :], b3.shape).reshape(rows_all, B_WIDTH)
    q_lvls.append((qq * jnp.exp2(b - bref)).astype(BF16))
    k_lvls.append((kk * jnp.exp2(bref - b)).astype(BF16))

    q_top, k_top, k_out, carry = [], [], [], []
    for bi in range(nb):
        r0 = bi * ts
        bm = b[r0 + sub - 1:r0 + sub, :]
        q_top.append((qq[r0 + sub:r0 + ts, :] * jnp.exp2(b[r0 + sub:r0 + ts, :] - bm)).astype(BF16))
        k_top.append((kk[r0:r0 + sub, :] * jnp.exp2(bm - b[r0:r0 + sub, :])).astype(BF16))
        b_last = b[r0 + ts - 1:r0 + ts, :]
        k_out.append((kk[seqs[bi], :] * jnp.exp2(b_last - b[seqs[bi], :])).astype(BF16))
        carry.append(jnp.exp2(b_last))
    q_in = (qq * jnp.exp2(b)).astype(BF16)
    vv = stacked(i_ref)
    gate = _silu(stacked(g_ref).astype(F32))
    onorm = onorm_ref[...]

    lvl = lvl_ref[...]
    masks = [lvl == li for li in range(n_lvl + 1)]
    units = [(bi, hd, slice(hd * B_DIM, (hd + 1) * B_DIM)) for hd in range(B_HEADS) for bi in range(nb)]
    states = {(bi, hd): state_ref[bi, hd] for bi, hd, _ in units}
    inter = {(bi, hd): _dot_nt(q_in[seqs[bi], cols], states[bi, hd].astype(BF16)) for bi, hd, cols in units}
    for bi, hd, cols in units:
        state_ref[bi, hd] = (states[bi, hd] * carry[bi][:, cols]
                             + _dot_tn(vv[seqs[bi], cols], k_out[bi][:, cols]))
    score_blocks = {}
    for bi, hd, cols in units:
        diag = []
        for a in range(ts // sub):
            rows = slice(bi * ts + a * sub, bi * ts + (a + 1) * sub)
            blk = jnp.where(masks[n_lvl], _dot_nt(q_lvls[n_lvl][rows, cols], k_lvls[n_lvl][rows, cols]), 0.0)
            for li in range(n_lvl):
                blk = jnp.where(masks[li], _dot_nt(q_lvls[li][rows, cols], k_lvls[li][rows, cols]), blk)
            diag.append(blk.astype(BF16))
        top = _dot_nt(q_top[bi][:, cols], k_top[bi][:, cols]).astype(BF16)
        score_blocks[bi, hd] = (diag[0], jnp.concatenate([top, diag[1]], axis=1))
    outs = {}
    for bi, hd, cols in units:
        v_h = vv[seqs[bi], cols]
        first, second = score_blocks[bi, hd]
        o_intra = jnp.concatenate([_dot(first, v_h[:sub, :]), _dot(second, v_h)], axis=0)
        outs[bi, hd] = o_intra + inter[bi, hd]
    for bi, hd, cols in units:
        on = _rms(outs[bi, hd], onorm[:, cols]) * gate[seqs[bi], cols]
        o_ref[bi, :, A_WIDTH + hd * B_DIM:A_WIDTH + (hd + 1) * B_DIM] = on.astype(BF16)


def _even_core(parts, vnorm, ind, wcat, bfull, lb, onorm, ltri, lvl):
    bsz, seq, _ = parts[0].shape
    ts = HGRN_TILE
    nb = HGRN_BATCH
    tile = lambda b, i: (b, i, 0)
    part_spec = pl.BlockSpec((nb, ts, A_WIDTH), tile)
    return pl.pallas_call(
        _even_core_kernel,
        grid=(bsz // nb, seq // ts),
        in_specs=[part_spec] * EVEN_PARTS + [
            _const_spec((1, A_WIDTH)), _const_spec((A_WIDTH, A_WIDTH)),
            _const_spec((A_CHUNK, A_HEADS * A_CHUNK)), _const_spec((A_CHUNK, A_WIDTH)),
            _const_spec((1, B_WIDTH)), _const_spec((1, B_WIDTH)),
            _const_spec((ts, ts)), _const_spec((HGRN_SUB, HGRN_SUB))],
        out_specs=pl.BlockSpec((nb, ts, D_MODEL), tile),
        out_shape=jax.ShapeDtypeStruct((bsz, seq, D_MODEL), BF16),
        scratch_shapes=[pltpu.VMEM((nb, B_HEADS, B_DIM, B_DIM), F32)],
        compiler_params=pltpu.CompilerParams(dimension_semantics=("parallel", "arbitrary"),
                                             vmem_limit_bytes=VMEM_LIMIT_BYTES),
        name="even_core",
    )(*parts, vnorm, ind, wcat, bfull, lb, onorm, ltri, lvl)


def _attn_kernel(qt_ref, k_ref, vt_ref, o_ref, m_ref, l_ref, acc_ref, sa_ref, sb_ref):
    tq = ATTN_TILE
    tks = tq // ATTN_KEY_SPLIT
    qi = pl.program_id(2)
    heads = range(ATTN_HEADS_PER_STEP)

    m_ref[...] = jnp.full_like(m_ref, NEG_BIG)
    l_ref[...] = jnp.zeros_like(l_ref)
    acc_ref[...] = jnp.zeros_like(acc_ref)

    def key_subs(j):
        start = pl.multiple_of(j * tq, tq)
        return [pl.ds(pl.multiple_of(start + c * tks, tks), tks) for c in range(ATTN_KEY_SPLIT)]

    def scores(j, s_ref, heads=heads):
        for hd in heads:
            qt = qt_ref[0, hd]
            for c, sub in enumerate(key_subs(j)):
                s_ref[hd, c * tks:(c + 1) * tks, :] = _dot(k_ref[0, hd, sub, :], qt)

    def consume(j, s_ref, masked, heads=heads):
        if masked:
            key = lax.broadcasted_iota(jnp.int32, (tks, tq), 0)
            qry = lax.broadcasted_iota(jnp.int32, (tks, tq), 1)
        s3 = {}
        for hd in heads:
            for c in range(ATTN_KEY_SPLIT):
                s = s_ref[hd, c * tks:(c + 1) * tks, :]
                if masked:
                    s = jnp.where(key + c * tks <= qry, s, NEG_BIG)
                s3[hd, c] = s.reshape(tks // SUBLANES, SUBLANES, tq)
        m_new, alpha = {}, {}
        for hd in heads:
            m_prev = m_ref[hd]
            cmax = functools.reduce(jnp.maximum, [jnp.max(s3[hd, c], axis=0) for c in range(ATTN_KEY_SPLIT)])
            cmax = jnp.max(cmax, axis=0, keepdims=True)
            m_new[hd] = jnp.maximum(m_prev, jnp.broadcast_to(cmax, (SUBLANES, tq)))
            alpha[hd] = jnp.exp2(m_prev - m_new[hd])
            m_ref[hd] = m_new[hd]
        ones = jnp.ones((2 * SUBLANES, tks), BF16)
        for hd in heads:
            acc3 = acc_ref[hd].reshape(C_V // SUBLANES, SUBLANES, tq) * alpha[hd][None]
            acc = acc3.reshape(C_V, tq)
            csum = jnp.zeros((SUBLANES, tq), F32)
            for c, sub in enumerate(key_subs(j)):
                p = jnp.exp2(s3[hd, c] - m_new[hd][None]).reshape(tks, tq).astype(BF16)
                lhs = jnp.concatenate([vt_ref[0, hd * C_V:(hd + 1) * C_V, sub], ones], axis=0)
                pv = _dot(lhs, p)
                acc = acc + pv[:C_V, :]
                csum = csum + pv[C_V:C_V + SUBLANES, :]
            l_ref[hd] = alpha[hd] * l_ref[hd] + csum
            acc_ref[hd] = acc

    scores(0, sa_ref)
    pairs = qi // 2

    def body(t, carry):
        j = 2 * t
        for hd in heads:
            scores(j + 1, sb_ref, [hd])
            consume(j, sa_ref, False, [hd])
        for hd in heads:
            scores(j + 2, sa_ref, [hd])
            consume(j + 1, sb_ref, False, [hd])
        return carry

    lax.fori_loop(0, pairs, body, 0)

    @pl.when(qi % 2 == 0)
    def _():
        consume(qi, sa_ref, True)

    @pl.when(qi % 2 == 1)
    def _():
        for hd in heads:
            scores(qi, sb_ref, [hd])
            consume(qi - 1, sa_ref, False, [hd])
        consume(qi, sb_ref, True)
    for hd in range(ATTN_HEADS_PER_STEP):
        inv_l = 1.0 / l_ref[hd]
        out_t = acc_ref[hd].reshape(C_V // SUBLANES, SUBLANES, tq) * inv_l[None]
        o_ref[0, :, hd * C_V:(hd + 1) * C_V] = out_t.reshape(C_V, tq).T.astype(BF16)


def _attention(qt, k, vt):
    bsz, heads, seq, _ = k.shape
    tq = ATTN_TILE
    g = ATTN_HEADS_PER_STEP
    return pl.pallas_call(
        _attn_kernel,
        grid=(bsz, heads // g, seq // tq),
        in_specs=[pl.BlockSpec((1, g, C_QK, tq), lambda b, h, i: (b, h, 0, i)),
                  pl.BlockSpec((1, g, seq, C_QK), lambda b, h, i: (b, h, 0, 0)),
                  pl.BlockSpec((1, g * C_V, seq), lambda b, h, i: (b, h, 0))],
        out_specs=pl.BlockSpec((1, tq, g * C_V), lambda b, h, i: (b, i, h)),
        out_shape=jax.ShapeDtypeStruct((bsz, seq, heads * C_V), BF16),
        scratch_shapes=[pltpu.VMEM((g, SUBLANES, tq), F32), pltpu.VMEM((g, SUBLANES, tq), F32),
                        pltpu.VMEM((g, C_V, tq), F32), pltpu.VMEM((g, tq, tq), F32),
                        pltpu.VMEM((g, tq, tq), F32)],
        compiler_params=pltpu.CompilerParams(dimension_semantics=("parallel", "parallel", "arbitrary"),
                                             vmem_limit_bytes=VMEM_LIMIT_BYTES),
        name="mla_attention",
    )(qt, k, vt)


def _row(v):
    return v.reshape(1, -1).astype(F32)


def _pad_cols(w, total):
    return jnp.pad(w, ((0, 0), (0, total - w.shape[1])))


def _even_params(v_norm, w_s, b_s, lb, out_norm):
    causal = np.tril(np.ones((A_CHUNK, A_CHUNK), bool))
    w = jnp.where(causal[None], w_s, 0.0)
    wcat = jnp.transpose(w, (1, 0, 2)).reshape(A_CHUNK, A_HEADS * A_CHUNK)
    bfull = jnp.repeat(b_s.T, A_HEAD_DIM, axis=1)
    head_of = np.arange(A_WIDTH) // A_HEAD_DIM
    ind = (head_of[:, None] == head_of[None, :]).astype(np.float32)
    ltri = np.tril(np.ones((HGRN_TILE, HGRN_TILE), np.float32))
    return dict(vnorm=_row(v_norm), ind=jnp.asarray(ind, BF16),
                wcat=wcat.astype(BF16), bfull=bfull.astype(F32), lb=_row(lb),
                onorm=_row(jnp.tile(out_norm, B_HEADS)), ltri=jnp.asarray(ltri, BF16),
                lvl=jnp.asarray(_level_map(HGRN_SUB)))


def _odd_params(w_in, q_a_norm, kv_a_norm, w_q_b, w_kv_b, q_norm, k_norm):
    tm = TOKEN_TILE
    win = _pad_cols(w_in, ODD_IN_PAD)
    wkv = w_kv_b.reshape(KV_LORA, C_HEADS, C_NOPE + C_V)
    wkvk = wkv[:, :, :C_NOPE].reshape(KV_LORA, C_HEADS * C_NOPE)
    wkvvt = wkv[:, :, C_NOPE:].reshape(KV_LORA, C_HEADS * C_V).T
    inv_freq = ROPE_THETA ** (-jnp.arange(0, C_ROPE, 2, dtype=F32) / C_ROPE)
    invfc = jnp.broadcast_to(inv_freq[:, None], (C_ROPE // 2, tm))
    head_of = np.arange(C_HEADS * C_NOPE) // C_NOPE
    splat = (head_of[:, None] == np.arange(LANES)[None, :]).astype(np.float32)
    rope_rows = np.broadcast_to((np.arange(LANES) < C_HEADS).astype(np.float32), (LANES, LANES))
    indk = np.concatenate([splat, rope_rows], axis=0)
    return dict(win=win.astype(BF16), wropet=w_in[:, Q_LORA + KV_LORA:].T.astype(BF16),
                qan=_row(q_a_norm), kvan=_row(kv_a_norm),
                wqbt=w_q_b.T.astype(BF16), wkvk=wkvk.astype(BF16), wkvvt=wkvvt.astype(BF16),
                gq=jnp.broadcast_to(q_norm.astype(F32)[:, None], (C_QK, tm)),
                gkr=jnp.broadcast_to(k_norm.astype(F32)[C_NOPE:, None], (C_ROPE, tm)),
                kn=_row(k_norm[:C_NOPE]), invfc=invfc,
                indk=jnp.asarray(indk, BF16), splat=jnp.asarray(splat.T, BF16))


def kernel(x, p, positions, norm_gains, ffn_w_gate, ffn_w_up, ffn_w_down, ple_w_gate, ple_w_proj,
           even_w_in, gmlp_v_norm, gmlp_w_s, gmlp_b_s, hgrn_lb_raw, hgrn_out_norm, even_w_out,
           mla_w_in, mla_q_a_norm, mla_kv_a_norm, mla_w_q_b, mla_w_kv_b, mla_q_norm, mla_k_norm, mla_w_out):
    bsz, seq, d = x.shape
    depth = norm_gains.shape[0]
    n = bsz * seq
    assert d == D_MODEL and seq % ATTN_TILE == 0 and seq % TOKEN_TILE == 0 and seq % HGRN_TILE == 0
    assert bsz % HGRN_BATCH == 0

    lb_sm = jax.nn.softmax(hgrn_lb_raw.astype(F32), axis=0)
    lower_bounds = jnp.clip(jnp.cumsum(lb_sm, axis=0) - lb_sm[0], 0.0, LB_MAX)
    posr = positions.astype(F32).reshape(bsz, 1, seq)

    ffn = tuple(w.astype(BF16) for w in (ffn_w_gate, ffn_w_up, ffn_w_down))
    ple_wg, ple_wp = ple_w_gate.astype(BF16), ple_w_proj.astype(BF16)
    even_win, even_wout, mla_wout = (w.astype(BF16) for w in (even_w_in, even_w_out, mla_w_out))
    p3d = p.reshape(depth, n, D_PLE)

    xc = x.reshape(n, d)
    for i in range(depth):
        g = norm_gains[i]
        j = i // 2
        if i % 2 == 0:
            ep = _even_params(gmlp_v_norm[j], gmlp_w_s[j], gmlp_b_s[j], lower_bounds[j], hgrn_out_norm[j])
            outs = _stage1_even(xc, _row(g[0]), ffn, (i, 0), _row(g[1]), even_win, j)
            xc = outs[0]
            parts = [o.reshape(bsz, seq, A_WIDTH) for o in outs[1:]]
            mixed = _even_core(parts, ep["vnorm"], ep["ind"], ep["wcat"], ep["bfull"], ep["lb"],
                               ep["onorm"], ep["ltri"], ep["lvl"])
            w_out = even_wout
        else:
            op = _odd_params(mla_w_in[j], mla_q_a_norm[j], mla_kv_a_norm[j], mla_w_q_b[j], mla_w_kv_b[j],
                             mla_q_norm[j], mla_k_norm[j])
            x1, qt, k, vt = _stage1_odd(xc.reshape(bsz, seq, d), posr, op, _row(g[0]), ffn, (i, 0),
                                        _row(g[1]))
            xc = x1.reshape(n, d)
            mixed = _attention(qt, k, vt)
            w_out = mla_wout
        xc = _stage2(xc, mixed.reshape(n, d), p3d, i, w_out, j, _row(g[2]), ffn, (i, 1), _row(g[3]),
                     ple_wg, ple_wp, _row(g[4]))
    return xc.reshape(bsz, seq, d)
```

```python
import functools
import math

import numpy as np
import jax
import jax.numpy as jnp
from jax import lax
from jax.experimental import pallas as pl
from jax.experimental.pallas import tpu as pltpu

F32 = jnp.float32
BF16 = jnp.bfloat16

D_MODEL = 1024
D_FF = 2816
D_PLE = 256
EPS = 1e-6
NEG_BIG = -1e30
F_MIN = 1e-6
LB_MAX = 0.999

A_HEADS = 8
A_HEAD_DIM = 64
A_WIDTH = A_HEADS * A_HEAD_DIM
A_CHUNK = 128
B_HEADS = 4
B_DIM = 128
B_WIDTH = B_HEADS * B_DIM
EVEN_PARTS = 6

C_HEADS = 8
C_NOPE = 128
C_ROPE = 64
C_V = 128
C_QK = C_NOPE + C_ROPE
Q_LORA = 384
KV_LORA = 256
ODD_IN_PAD = Q_LORA + KV_LORA + 128
ROPE_THETA = 10000.0
ATTN_SCALE = C_QK ** -0.5
LOG2E = math.log2(math.e)

LANES = 128
SUBLANES = 8
MXU_DIM = 256
VMEM_LIMIT_BYTES = 60000 * 1024

TOKEN_TILE = 512
FF_CHUNK = 2 * MXU_DIM
TAIL_SPLIT = 2
HGRN_TILE = 256
HGRN_BATCH = 2
HGRN_SUB = HGRN_TILE // 2
HGRN_LEVELS = (64, 32, 16, 8)
ATTN_TILE = 512
ATTN_HEADS_PER_STEP = 4
ATTN_KEY_SPLIT = 2


def _ff_chunks():
    chunks, off = [], 0
    while off < D_FF:
        size = min(FF_CHUNK, D_FF - off)
        chunks.append((off, size))
        off += size
    return tuple(chunks)


def _sigmoid(x):
    return 0.5 + 0.5 * jnp.tanh(0.5 * x)


def _silu(x):
    t = 0.5 * x
    return t + t * jnp.tanh(t)


def _gelu_tanh(x):
    c = math.sqrt(2.0 / math.pi)
    hx = 0.5 * x
    return hx + hx * jnp.tanh(x * (c + (c * 0.044715) * (x * x)))


def _rms(x, g):
    ms = jnp.mean(x * x, axis=-1, keepdims=True)
    return x * lax.rsqrt(ms + EPS) * g


def _dot(a, b):
    return jnp.dot(a, b, preferred_element_type=F32)


def _dot_nt(a, b):
    return lax.dot_general(a, b, (((1,), (1,)), ((), ())), preferred_element_type=F32)


def _dot_tn(a, b):
    return lax.dot_general(a, b, (((0,), (0,)), ((), ())), preferred_element_type=F32)


def _split2(x):
    hi = x.astype(BF16)
    lo = (x - hi.astype(F32)).astype(BF16)
    return hi, lo


def _swiglu_into(acc_ref, hn, wg_ref, wu_ref, wd_ref):
    for ci, (off, size) in enumerate(_ff_chunks()):
        gate = _dot(hn, wg_ref[:, off:off + size])
        up = _dot(hn, wu_ref[:, off:off + size])
        act = (_silu(gate) * up).astype(BF16)
        part = _dot(act, wd_ref[off:off + size, :])
        if ci == 0:
            acc_ref[...] = part
        else:
            acc_ref[...] += part


def _s1_even_kernel(x_ref, g0_ref, wg_ref, wu_ref, wd_ref, g1_ref, win_ref,
                    x1_ref, u_ref, v_ref, q_ref, f_ref, i_ref, gg_ref, acc_ref):
    x = x_ref[...]
    hn = _rms(x, g0_ref[...]).astype(BF16)
    _swiglu_into(acc_ref, hn, wg_ref, wu_ref, wd_ref)
    x1 = x + 0.5 * acc_ref[...]
    x1_ref[...] = x1
    h = _rms(x1, g1_ref[...]).astype(BF16)
    outs = (u_ref, v_ref, q_ref, f_ref, i_ref, gg_ref)
    for k, o_ref in enumerate(outs):
        z = _dot(h, win_ref[:, k * A_WIDTH:(k + 1) * A_WIDTH])
        o_ref[...] = z.astype(o_ref.dtype)


def _const_spec(shape, prefix=()):
    nd = len(shape)
    return pl.BlockSpec((None,) * len(prefix) + tuple(shape), lambda *_: tuple(prefix) + (0,) * nd,
                        pipeline_mode=pl.Buffered(1))


def _ffn_specs(layer):
    return [_const_spec((D_MODEL, D_FF), layer), _const_spec((D_MODEL, D_FF), layer),
            _const_spec((D_FF, D_MODEL), layer)]


def _stage1_even(x2d, g0, ffn, ffn_layer, g1, win, j):
    n = x2d.shape[0]
    tm = TOKEN_TILE
    row = lambda i: (i, 0)
    part_spec = pl.BlockSpec((tm, A_WIDTH), row)
    out_shape = [jax.ShapeDtypeStruct((n, D_MODEL), F32)]
    out_shape += [jax.ShapeDtypeStruct((n, A_WIDTH), F32 if k == 3 else BF16) for k in range(EVEN_PARTS)]
    return pl.pallas_call(
        _s1_even_kernel,
        grid=(n // tm,),
        in_specs=[pl.BlockSpec((tm, D_MODEL), row), _const_spec((1, D_MODEL))] + _ffn_specs(ffn_layer) + [
                  _const_spec((1, D_MODEL)), _const_spec((D_MODEL, EVEN_PARTS * A_WIDTH), (j,))],
        out_specs=[pl.BlockSpec((tm, D_MODEL), row)] + [part_spec] * EVEN_PARTS,
        out_shape=out_shape,
        scratch_shapes=[pltpu.VMEM((tm, D_MODEL), F32)],
        compiler_params=pltpu.CompilerParams(dimension_semantics=("parallel",),
                                             vmem_limit_bytes=VMEM_LIMIT_BYTES),
        name="stage1_even",
    )(x2d, g0, *ffn, g1, win)


def _s1_odd_kernel(x_ref, posr_ref, invfc_ref, g0_ref, wg_ref, wu_ref, wd_ref, g1_ref,
                   win_ref, wropet_ref, qan_ref, kvan_ref, wqbt_ref, wkvk_ref, wkvvt_ref, gq_ref, gkr_ref, kn_ref,
                   indk_ref, splat_ref,
                   x1_ref, qt_ref, k_ref, vt_ref, acc_ref):
    x = x_ref[0]
    hn = _rms(x, g0_ref[...]).astype(BF16)
    _swiglu_into(acc_ref, hn, wg_ref, wu_ref, wd_ref)
    x1 = x + 0.5 * acc_ref[...]
    x1_ref[0] = x1
    h = _rms(x1, g1_ref[...]).astype(BF16)
    z = _dot(h, win_ref[...])
    cqn = _rms(z[:, :Q_LORA], qan_ref[...]).astype(BF16)
    ckvn = _rms(z[:, Q_LORA:Q_LORA + KV_LORA], kvan_ref[...]).astype(BF16)
    k_rope = z[:, Q_LORA + KV_LORA:]
    q_t = _dot_nt(wqbt_ref[...], cqn)
    k_nope = _dot(ckvn, wkvk_ref[...])
    vt_ref[0] = _dot_nt(wkvvt_ref[...], ckvn).astype(BF16)

    ang_t = invfc_ref[...] * posr_ref[0]
    cos_f, sin_f = jnp.cos(ang_t), jnp.sin(ang_t)
    gq = gq_ref[...]
    half = C_ROPE // 2
    for hd in range(C_HEADS):
        blk = q_t[hd * C_QK:(hd + 1) * C_QK, :]
        ss = jnp.sum(blk * blk, axis=0, keepdims=True)
        r = lax.rsqrt(ss * (1.0 / C_QK) + EPS) * (ATTN_SCALE * LOG2E)
        sc = blk * r * gq
        a1 = sc[C_NOPE:C_NOPE + half, :]
        a2 = sc[C_NOPE + half:, :]
        qt_ref[0, hd, 0:C_NOPE, :] = sc[:C_NOPE, :].astype(BF16)
        qt_ref[0, hd, C_NOPE:C_NOPE + half, :] = (a1 * cos_f - a2 * sin_f).astype(BF16)
        qt_ref[0, hd, C_NOPE + half:C_QK, :] = (a2 * cos_f + a1 * sin_f).astype(BF16)

    kr_t = _dot_nt(wropet_ref[...], h) * gkr_ref[...]
    b1, b2 = kr_t[:half, :], kr_t[half:, :]
    rot_t = jnp.concatenate([b1 * cos_f - b2 * sin_f, b2 * cos_f + b1 * sin_f,
                             jnp.zeros((LANES - C_ROPE, kr_t.shape[1]), F32)], axis=0)
    kr_rot = rot_t.T

    lane = lax.broadcasted_iota(jnp.int32, (1, LANES), 1)
    kn_nope = kn_ref[...]
    k2_hi, k2_lo = _split2(jnp.concatenate([k_nope * k_nope, k_rope * k_rope], axis=1))
    ssk = _dot(k2_hi, indk_ref[...]) + _dot(k2_lo, indk_ref[...])
    rk = jnp.where(lane < C_HEADS, lax.rsqrt(ssk * (1.0 / C_QK) + EPS), 0.0)
    rk_hi, rk_lo = _split2(rk)
    rk_b = _dot(rk_hi, splat_ref[...]) + _dot(rk_lo, splat_ref[...])
    for hd in range(C_HEADS):
        cols = slice(hd * C_NOPE, (hd + 1) * C_NOPE)
        k_ref[0, hd, :, 0:C_NOPE] = (k_nope[:, cols] * rk_b[:, cols] * kn_nope).astype(BF16)
        k_ref[0, hd, :, C_NOPE:C_QK] = (kr_rot * rk_b[:, cols])[:, :C_ROPE].astype(BF16)


def _stage1_odd(x3d, posr, op, g0, ffn, ffn_layer, g1):
    bsz, seq, _ = x3d.shape
    tm = TOKEN_TILE
    tile = lambda b, i: (b, i, 0)
    return pl.pallas_call(
        _s1_odd_kernel,
        grid=(bsz, seq // tm),
        in_specs=[pl.BlockSpec((1, tm, D_MODEL), tile), pl.BlockSpec((1, 1, tm), lambda b, i: (b, 0, i)),
                  _const_spec((C_ROPE // 2, tm)), _const_spec((1, D_MODEL))]
                 + _ffn_specs(ffn_layer) + [
                  _const_spec((1, D_MODEL)), _const_spec((D_MODEL, ODD_IN_PAD)), _const_spec((C_ROPE, D_MODEL)),
                  _const_spec((1, Q_LORA)), _const_spec((1, KV_LORA)),
                  _const_spec((C_HEADS * C_QK, Q_LORA)), _const_spec((KV_LORA, C_HEADS * C_NOPE)),
                  _const_spec((C_HEADS * C_V, KV_LORA)), _const_spec((C_QK, tm)), _const_spec((C_ROPE, tm)),
                  _const_spec((1, C_NOPE)), _const_spec((C_HEADS * C_NOPE + LANES, LANES)),
                  _const_spec((LANES, C_HEADS * C_NOPE))],
        out_specs=[pl.BlockSpec((1, tm, D_MODEL), tile),
                   pl.BlockSpec((1, C_HEADS, C_QK, tm), lambda b, i: (b, 0, 0, i)),
                   pl.BlockSpec((1, C_HEADS, tm, C_QK), lambda b, i: (b, 0, i, 0)),
                   pl.BlockSpec((1, C_HEADS * C_V, tm), lambda b, i: (b, 0, i))],
        out_shape=[jax.ShapeDtypeStruct((bsz, seq, D_MODEL), F32),
                   jax.ShapeDtypeStruct((bsz, C_HEADS, C_QK, seq), BF16),
                   jax.ShapeDtypeStruct((bsz, C_HEADS, seq, C_QK), BF16),
                   jax.ShapeDtypeStruct((bsz, C_HEADS * C_V, seq), BF16)],
        scratch_shapes=[pltpu.VMEM((tm, D_MODEL), F32)],
        compiler_params=pltpu.CompilerParams(dimension_semantics=("parallel", "parallel"),
                                             vmem_limit_bytes=VMEM_LIMIT_BYTES),
        name="stage1_odd",
    )(x3d, posr, op["invfc"], g0, *ffn, g1, op["win"], op["wropet"], op["qan"], op["kvan"],
      op["wqbt"], op["wkvk"], op["wkvvt"], op["gq"], op["gkr"], op["kn"], op["indk"], op["splat"])


def _s2_kernel(x_ref, m_ref, p_ref, wo_ref, g2_ref, wg_ref, wu_ref, wd_ref,
               g3_ref, wpg_ref, wpp_ref, g4_ref, o_ref, acc_ref):
    tm = x_ref.shape[0]
    groups = [slice(r, r + tm // TAIL_SPLIT) for r in range(0, tm, tm // TAIL_SPLIT)]
    xs = [x_ref[r, :] + _dot(m_ref[r, :], wo_ref[...]) for r in groups]
    hn = jnp.concatenate([_rms(xg, g2_ref[...]).astype(BF16) for xg in xs], axis=0)
    _swiglu_into(acc_ref, hn, wg_ref, wu_ref, wd_ref)
    xs = [xg + 0.5 * acc_ref[r, :] for xg, r in zip(xs, groups)]
    h3 = [_rms(xg, g3_ref[...]).astype(BF16) for xg in xs]
    proj = [_dot(p_ref[r, :].astype(BF16), wpp_ref[...]) for r in groups]
    gate = [_sigmoid(_dot(hg, wpg_ref[...])) for hg in h3]
    for xg, gg, pg, r in zip(xs, gate, proj, groups):
        o_ref[r, :] = xg + _rms(gg * pg, g4_ref[...])


def _stage2(x2d, mixed, p3d, layer, wo, j, g2, ffn, ffn_layer, g3, wpg, wpp, g4):
    n = x2d.shape[0]
    tm = TOKEN_TILE
    row = lambda i: (i, 0)
    return pl.pallas_call(
        _s2_kernel,
        grid=(n // tm,),
        in_specs=[pl.BlockSpec((tm, D_MODEL), row), pl.BlockSpec((tm, D_MODEL), row),
                  pl.BlockSpec((None, tm, D_PLE), lambda i: (layer, i, 0)),
                  _const_spec((D_MODEL, D_MODEL), (j,)), _const_spec((1, D_MODEL))] + _ffn_specs(ffn_layer) + [
                  _const_spec((1, D_MODEL)), _const_spec((D_MODEL, D_MODEL), (layer,)),
                  _const_spec((D_PLE, D_MODEL), (layer,)), _const_spec((1, D_MODEL))],
        out_specs=pl.BlockSpec((tm, D_MODEL), row),
        out_shape=jax.ShapeDtypeStruct((n, D_MODEL), F32),
        scratch_shapes=[pltpu.VMEM((tm, D_MODEL), F32)],
        compiler_params=pltpu.CompilerParams(dimension_semantics=("parallel",),
                                             vmem_limit_bytes=VMEM_LIMIT_BYTES),
        name="stage2",
    )(x2d, mixed, p3d, wo, g2, *ffn, g3, wpg, wpp, g4)


def _level_map(ts):
    t = np.arange(ts)[:, None]
    s = np.arange(ts)[None, :]
    lvl = np.full((ts, ts), len(HGRN_LEVELS) + 1, np.int32)
    lvl[(t // SUBLANES == s // SUBLANES) & (s <= t)] = len(HGRN_LEVELS)
    for li, h in enumerate(HGRN_LEVELS):
        own = (t // (2 * h) == s // (2 * h)) & (t % (2 * h) >= h) & (s % (2 * h) < h)
        lvl[own] = li
    return lvl


def _even_core_kernel(u_ref, v_ref, q_ref, f_ref, i_ref, g_ref,
                      vnorm_ref, ind_ref, wcat_ref, bfull_ref, lb_ref, onorm_ref, ltri_ref, lvl_ref,
                      o_ref, state_ref):
    nb = HGRN_BATCH
    ts = HGRN_TILE
    sub = HGRN_SUB
    n_lvl = len(HGRN_LEVELS)
    rows_all = nb * ts

    def stacked(ref):
        return ref[...].reshape(rows_all, ref.shape[-1])

    u = _gelu_tanh(stacked(u_ref).astype(F32))
    v = _gelu_tanh(stacked(v_ref).astype(F32))
    ss = _dot((v * v).astype(BF16), ind_ref[...])
    vn = (v * lax.rsqrt(ss * (1.0 / A_HEAD_DIM) + EPS) * vnorm_ref[...]).astype(BF16)
    lane_head = lax.broadcasted_iota(jnp.int32, (1, A_WIDTH), 1) // A_HEAD_DIM
    for c in range(rows_all // A_CHUNK):
        rows = slice(c * A_CHUNK, (c + 1) * A_CHUNK)
        vc = vn[rows, :]
        zero = jnp.zeros_like(vc)
        vbd = jnp.concatenate([jnp.where(lane_head == hd, vc, zero) for hd in range(A_HEADS)], axis=0)
        mixed = _dot(wcat_ref[...], vbd) + bfull_ref[...]
        bi, local = divmod(c * A_CHUNK, ts)
        o_ref[bi, local:local + A_CHUNK, 0:A_WIDTH] = (u[rows, :] * mixed).astype(BF16)

    @pl.when(pl.program_id(1) == 0)
    def _():
        state_ref[...] = jnp.zeros_like(state_ref)

    lb = lb_ref[...]
    fg = lb + (1.0 - lb) * _sigmoid(stacked(f_ref))
    lf = jnp.log2(jnp.maximum(fg, F_MIN))
    kk = 1.0 - fg
    qq = stacked(q_ref).astype(F32)
    lf_hi, lf_lo = _split2(lf)
    ltri = ltri_ref[...]
    seqs = [slice(bi * ts, (bi + 1) * ts) for bi in range(nb)]
    b = jnp.concatenate([_dot(ltri, lf_hi[r, :]) + _dot(ltri, lf_lo[r, :]) for r in seqs], axis=0)

    q_lvls, k_lvls = [], []
    for h in HGRN_LEVELS:
        q_parts, k_parts = [], []
        zero = jnp.zeros((h, B_WIDTH), F32)
        for j in range(rows_all // (2 * h)):
            r0 = j * 2 * h
            bm = b[r0 + h - 1:r0 + h, :]
            lo_rows = slice(r0, r0 + h)
            up_rows = slice(r0 + h, r0 + 2 * h)
            k_parts += [kk[lo_rows, :] * jnp.exp2(bm - b[lo_rows, :]), zero]
            q_parts += [zero, qq[up_rows, :] * jnp.exp2(b[up_rows, :] - bm)]
        q_lvls.append(jnp.concatenate(q_parts, axis=0).astype(BF16))
        k_lvls.append(jnp.concatenate(k_parts, axis=0).astype(BF16))
    b3 = b.reshape(rows_all // SUBLANES, SUBLANES, B_WIDTH)
    bref = jnp.broadcast_to(b3[:, 3:4, :], b3.shape).reshape(rows_all, B_WIDTH)
    q_lvls.append((qq * jnp.exp2(b - bref)).astype(BF16))
    k_lvls.append((kk * jnp.exp2(bref - b)).astype(BF16))

    q_top, k_top, k_out, carry = [], [], [], []
    for bi in range(nb):
        r0 = bi * ts
        bm = b[r0 + sub - 1:r0 + sub, :]
        q_top.append((qq[r0 + sub:r0 + ts, :] * jnp.exp2(b[r0 + sub:r0 + ts, :] - bm)).astype(BF16))
        k_top.append((kk[r0:r0 + sub, :] * jnp.exp2(bm - b[r0:r0 + sub, :])).astype(BF16))
        b_last = b[r0 + ts - 1:r0 + ts, :]
        k_out.append((kk[seqs[bi], :] * jnp.exp2(b_last - b[seqs[bi], :])).astype(BF16))
        carry.append(jnp.exp2(b_last))
    q_in = (qq * jnp.exp2(b)).astype(BF16)
    vv = stacked(i_ref)
    gate = _silu(stacked(g_ref).astype(F32))
    onorm = onorm_ref[...]

    lvl = lvl_ref[...]
    masks = [lvl == li for li in range(n_lvl + 1)]
    units = [(bi, hd, slice(hd * B_DIM, (hd + 1) * B_DIM)) for hd in range(B_HEADS) for bi in range(nb)]
    states = {(bi, hd): state_ref[bi, hd] for bi, hd, _ in units}
    inter = {(bi, hd): _dot_nt(q_in[seqs[bi], cols], states[bi, hd].astype(BF16)) for bi, hd, cols in units}
    for bi, hd, cols in units:
        state_ref[bi, hd] = (states[bi, hd] * carry[bi][:, cols]
                             + _dot_tn(vv[seqs[bi], cols], k_out[bi][:, cols]))
    score_blocks = {}
    for bi, hd, cols in units:
        diag = []
        for a in range(ts // sub):
            rows = slice(bi * ts + a * sub, bi * ts + (a + 1) * sub)
            blk = jnp.where(masks[n_lvl], _dot_nt(q_lvls[n_lvl][rows, cols], k_lvls[n_lvl][rows, cols]), 0.0)
            for li in range(n_lvl):
                blk = jnp.where(masks[li], _dot_nt(q_lvls[li][rows, cols], k_lvls[li][rows, cols]), blk)
            diag.append(blk.astype(BF16))
        top = _dot_nt(q_top[bi][:, cols], k_top[bi][:, cols]).astype(BF16)
        score_blocks[bi, hd] = (diag[0], jnp.concatenate([top, diag[1]], axis=1))
    outs = {}
    for bi, hd, cols in units:
        v_h = vv[seqs[bi], cols]
        first, second = score_blocks[bi, hd]
        o_intra = jnp.concatenate([_dot(first, v_h[:sub, :]), _dot(second, v_h)], axis=0)
        outs[bi, hd] = o_intra + inter[bi, hd]
    for bi, hd, cols in units:
        on = _rms(outs[bi, hd], onorm[:, cols]) * gate[seqs[bi], cols]
        o_ref[bi, :, A_WIDTH + hd * B_DIM:A_WIDTH + (hd + 1) * B_DIM] = on.astype(BF16)


def _even_core(parts, vnorm, ind, wcat, bfull, lb, onorm, ltri, lvl):
    bsz, seq, _ = parts[0].shape
    ts = HGRN_TILE
    nb = HGRN_BATCH
    tile = lambda b, i: (b, i, 0)
    part_spec = pl.BlockSpec((nb, ts, A_WIDTH), tile)
    return pl.pallas_call(
        _even_core_kernel,
        grid=(bsz // nb, seq // ts),
        in_specs=[part_spec] * EVEN_PARTS + [
            _const_spec((1, A_WIDTH)), _const_spec((A_WIDTH, A_WIDTH)),
            _const_spec((A_CHUNK, A_HEADS * A_CHUNK)), _const_spec((A_CHUNK, A_WIDTH)),
            _const_spec((1, B_WIDTH)), _const_spec((1, B_WIDTH)),
            _const_spec((ts, ts)), _const_spec((HGRN_SUB, HGRN_SUB))],
        out_specs=pl.BlockSpec((nb, ts, D_MODEL), tile),
        out_shape=jax.ShapeDtypeStruct((bsz, seq, D_MODEL), BF16),
        scratch_shapes=[pltpu.VMEM((nb, B_HEADS, B_DIM, B_DIM), F32)],
        compiler_params=pltpu.CompilerParams(dimension_semantics=("parallel", "arbitrary"),
                                             vmem_limit_bytes=VMEM_LIMIT_BYTES),
        name="even_core",
    )(*parts, vnorm, ind, wcat, bfull, lb, onorm, ltri, lvl)


def _attn_kernel(qt_ref, k_ref, vt_ref, o_ref, m_ref, l_ref, acc_ref, sa_ref, sb_ref):
    tq = ATTN_TILE
    tks = tq // ATTN_KEY_SPLIT
    qi = pl.program_id(2)
    heads = range(ATTN_HEADS_PER_STEP)

    m_ref[...] = jnp.full_like(m_ref, NEG_BIG)
    l_ref[...] = jnp.zeros_like(l_ref)
    acc_ref[...] = jnp.zeros_like(acc_ref)

    def key_subs(j):
        start = pl.multiple_of(j * tq, tq)
        return [pl.ds(pl.multiple_of(start + c * tks, tks), tks) for c in range(ATTN_KEY_SPLIT)]

    def scores(j, s_ref, heads=heads):
        for hd in heads:
            qt = qt_ref[0, hd]
            for c, sub in enumerate(key_subs(j)):
                s_ref[hd, c * tks:(c + 1) * tks, :] = _dot(k_ref[0, hd, sub, :], qt)

    def consume(j, s_ref, masked, heads=heads):
        if masked:
            key = lax.broadcasted_iota(jnp.int32, (tks, tq), 0)
            qry = lax.broadcasted_iota(jnp.int32, (tks, tq), 1)
        s3 = {}
        for hd in heads:
            for c in range(ATTN_KEY_SPLIT):
                s = s_ref[hd, c * tks:(c + 1) * tks, :]
                if masked:
                    s = jnp.where(key + c * tks <= qry, s, NEG_BIG)
                s3[hd, c] = s.reshape(tks // SUBLANES, SUBLANES, tq)
        m_new, alpha = {}, {}
        for hd in heads:
            m_prev = m_ref[hd]
            cmax = functools.reduce(jnp.maximum, [jnp.max(s3[hd, c], axis=0) for c in range(ATTN_KEY_SPLIT)])
            cmax = jnp.max(cmax, axis=0, keepdims=True)
            m_new[hd] = jnp.maximum(m_prev, jnp.broadcast_to(cmax, (SUBLANES, tq)))
            alpha[hd] = jnp.exp2(m_prev - m_new[hd])
            m_ref[hd] = m_new[hd]
        ones = jnp.ones((2 * SUBLANES, tks), BF16)
        for hd in heads:
            acc3 = acc_ref[hd].reshape(C_V // SUBLANES, SUBLANES, tq) * alpha[hd][None]
            acc = acc3.reshape(C_V, tq)
            csum = jnp.zeros((SUBLANES, tq), F32)
            for c, sub in enumerate(key_subs(j)):
                p = jnp.exp2(s3[hd, c] - m_new[hd][None]).reshape(tks, tq).astype(BF16)
                lhs = jnp.concatenate([vt_ref[0, hd * C_V:(hd + 1) * C_V, sub], ones], axis=0)
                pv = _dot(lhs, p)
                acc = acc + pv[:C_V, :]
                csum = csum + pv[C_V:C_V + SUBLANES, :]
            l_ref[hd] = alpha[hd] * l_ref[hd] + csum
            acc_ref[hd] = acc

    scores(0, sa_ref)
    pairs = qi // 2

    def body(t, carry):
        j = 2 * t
        for hd in heads:
            scores(j + 1, sb_ref, [hd])
            consume(j, sa_ref, False, [hd])
        for hd in heads:
            scores(j + 2, sa_ref, [hd])
            consume(j + 1, sb_ref, False, [hd])
        return carry

    lax.fori_loop(0, pairs, body, 0)

    @pl.when(qi % 2 == 0)
    def _():
        for hd in heads:
            consume(qi, sa_ref, True, [hd])

    @pl.when(qi % 2 == 1)
    def _():
        for hd in heads:
            scores(qi, sb_ref, [hd])
            consume(qi - 1, sa_ref, False, [hd])
        for hd in heads:
            consume(qi, sb_ref, True, [hd])
    for hd in range(ATTN_HEADS_PER_STEP):
        inv_l = 1.0 / l_ref[hd]
        out_t = acc_ref[hd].reshape(C_V // SUBLANES, SUBLANES, tq) * inv_l[None]
        o_ref[0, :, hd * C_V:(hd + 1) * C_V] = out_t.reshape(C_V, tq).T.astype(BF16)


def _attention(qt, k, vt):
    bsz, heads, seq, _ = k.shape
    tq = ATTN_TILE
    g = ATTN_HEADS_PER_STEP
    return pl.pallas_call(
        _attn_kernel,
        grid=(bsz, heads // g, seq // tq),
        in_specs=[pl.BlockSpec((1, g, C_QK, tq), lambda b, h, i: (b, h, 0, i)),
                  pl.BlockSpec((1, g, seq, C_QK), lambda b, h, i: (b, h, 0, 0)),
                  pl.BlockSpec((1, g * C_V, seq), lambda b, h, i: (b, h, 0))],
        out_specs=pl.BlockSpec((1, tq, g * C_V), lambda b, h, i: (b, i, h)),
        out_shape=jax.ShapeDtypeStruct((bsz, seq, heads * C_V), BF16),
        scratch_shapes=[pltpu.VMEM((g, SUBLANES, tq), F32), pltpu.VMEM((g, SUBLANES, tq), F32),
                        pltpu.VMEM((g, C_V, tq), F32), pltpu.VMEM((g, tq, tq), F32),
                        pltpu.VMEM((g, tq, tq), F32)],
        compiler_params=pltpu.CompilerParams(dimension_semantics=("parallel", "parallel", "arbitrary"),
                                             vmem_limit_bytes=VMEM_LIMIT_BYTES),
        name="mla_attention",
    )(qt, k, vt)


def _row(v):
    return v.reshape(1, -1).astype(F32)


def _pad_cols(w, total):
    return jnp.pad(w, ((0, 0), (0, total - w.shape[1])))


def _even_params(v_norm, w_s, b_s, lb, out_norm):
    causal = np.tril(np.ones((A_CHUNK, A_CHUNK), bool))
    w = jnp.where(causal[None], w_s, 0.0)
    wcat = jnp.transpose(w, (1, 0, 2)).reshape(A_CHUNK, A_HEADS * A_CHUNK)
    bfull = jnp.repeat(b_s.T, A_HEAD_DIM, axis=1)
    head_of = np.arange(A_WIDTH) // A_HEAD_DIM
    ind = (head_of[:, None] == head_of[None, :]).astype(np.float32)
    ltri = np.tril(np.ones((HGRN_TILE, HGRN_TILE), np.float32))
    return dict(vnorm=_row(v_norm), ind=jnp.asarray(ind, BF16),
                wcat=wcat.astype(BF16), bfull=bfull.astype(F32), lb=_row(lb),
                onorm=_row(jnp.tile(out_norm, B_HEADS)), ltri=jnp.asarray(ltri, BF16),
                lvl=jnp.asarray(_level_map(HGRN_SUB)))


def _odd_params(w_in, q_a_norm, kv_a_norm, w_q_b, w_kv_b, q_norm, k_norm):
    tm = TOKEN_TILE
    win = _pad_cols(w_in, ODD_IN_PAD)
    wkv = w_kv_b.reshape(KV_LORA, C_HEADS, C_NOPE + C_V)
    wkvk = wkv[:, :, :C_NOPE].reshape(KV_LORA, C_HEADS * C_NOPE)
    wkvvt = wkv[:, :, C_NOPE:].reshape(KV_LORA, C_HEADS * C_V).T
    inv_freq = ROPE_THETA ** (-jnp.arange(0, C_ROPE, 2, dtype=F32) / C_ROPE)
    invfc = jnp.broadcast_to(inv_freq[:, None], (C_ROPE // 2, tm))
    head_of = np.arange(C_HEADS * C_NOPE) // C_NOPE
    splat = (head_of[:, None] == np.arange(LANES)[None, :]).astype(np.float32)
    rope_rows = np.broadcast_to((np.arange(LANES) < C_HEADS).astype(np.float32), (LANES, LANES))
    indk = np.concatenate([splat, rope_rows], axis=0)
    return dict(win=win.astype(BF16), wropet=w_in[:, Q_LORA + KV_LORA:].T.astype(BF16),
                qan=_row(q_a_norm), kvan=_row(kv_a_norm),
                wqbt=w_q_b.T.astype(BF16), wkvk=wkvk.astype(BF16), wkvvt=wkvvt.astype(BF16),
                gq=jnp.broadcast_to(q_norm.astype(F32)[:, None], (C_QK, tm)),
                gkr=jnp.broadcast_to(k_norm.astype(F32)[C_NOPE:, None], (C_ROPE, tm)),
                kn=_row(k_norm[:C_NOPE]), invfc=invfc,
                indk=jnp.asarray(indk, BF16), splat=jnp.asarray(splat.T, BF16))


def kernel(x, p, positions, norm_gains, ffn_w_gate, ffn_w_up, ffn_w_down, ple_w_gate, ple_w_proj,
           even_w_in, gmlp_v_norm, gmlp_w_s, gmlp_b_s, hgrn_lb_raw, hgrn_out_norm, even_w_out,
           mla_w_in, mla_q_a_norm, mla_kv_a_norm, mla_w_q_b, mla_w_kv_b, mla_q_norm, mla_k_norm, mla_w_out):
    bsz, seq, d = x.shape
    depth = norm_gains.shape[0]
    n = bsz * seq
    assert d == D_MODEL and seq % ATTN_TILE == 0 and seq % TOKEN_TILE == 0 and seq % HGRN_TILE == 0
    assert bsz % HGRN_BATCH == 0

    lb_sm = jax.nn.softmax(hgrn_lb_raw.astype(F32), axis=0)
    lower_bounds = jnp.clip(jnp.cumsum(lb_sm, axis=0) - lb_sm[0], 0.0, LB_MAX)
    posr = positions.astype(F32).reshape(bsz, 1, seq)

    ffn = tuple(w.astype(BF16) for w in (ffn_w_gate, ffn_w_up, ffn_w_down))
    ple_wg, ple_wp = ple_w_gate.astype(BF16), ple_w_proj.astype(BF16)
    even_win, even_wout, mla_wout = (w.astype(BF16) for w in (even_w_in, even_w_out, mla_w_out))
    p3d = p.reshape(depth, n, D_PLE)

    xc = x.reshape(n, d)
    for i in range(depth):
        g = norm_gains[i]
        j = i // 2
        if i % 2 == 0:
            ep = _even_params(gmlp_v_norm[j], gmlp_w_s[j], gmlp_b_s[j], lower_bounds[j], hgrn_out_norm[j])
            outs = _stage1_even(xc, _row(g[0]), ffn, (i, 0), _row(g[1]), even_win, j)
            xc = outs[0]
            parts = [o.reshape(bsz, seq, A_WIDTH) for o in outs[1:]]
            mixed = _even_core(parts, ep["vnorm"], ep["ind"], ep["wcat"], ep["bfull"], ep["lb"],
                               ep["onorm"], ep["ltri"], ep["lvl"])
            w_out = even_wout
        else:
            op = _odd_params(mla_w_in[j], mla_q_a_norm[j], mla_kv_a_norm[j], mla_w_q_b[j], mla_w_kv_b[j],
                             mla_q_norm[j], mla_k_norm[j])
            x1, qt, k, vt = _stage1_odd(xc.reshape(bsz, seq, d), posr, op, _row(g[0]), ffn, (i, 0),
                                        _row(g[1]))
            xc = x1.reshape(n, d)
            mixed = _attention(qt, k, vt)
            w_out = mla_wout
        xc = _stage2(xc, mixed.reshape(n, d), p3d, i, w_out, j, _row(g[2]), ffn, (i, 1), _row(g[3]),
                     ple_wg, ple_wp, _row(g[4]))
    return xc.reshape(bsz, seq, d)
```

```python
import functools
import math

import numpy as np
import jax
import jax.numpy as jnp
from jax import lax
from jax.experimental import pallas as pl
from jax.experimental.pallas import tpu as pltpu

F32 = jnp.float32
BF16 = jnp.bfloat16

D_MODEL = 1024
D_FF = 2816
D_PLE = 256
EPS = 1e-6
NEG_BIG = -1e30
F_MIN = 1e-6
LB_MAX = 0.999

A_HEADS = 8
A_HEAD_DIM = 64
A_WIDTH = A_HEADS * A_HEAD_DIM
A_CHUNK = 128
B_HEADS = 4
B_DIM = 128
B_WIDTH = B_HEADS * B_DIM
EVEN_PARTS = 6

C_HEADS = 8
C_NOPE = 128
C_ROPE = 64
C_V = 128
C_QK = C_NOPE + C_ROPE
Q_LORA = 384
KV_LORA = 256
ROPE_THETA = 10000.0
ATTN_SCALE = C_QK ** -0.5
LOG2E = math.log2(math.e)

LANES = 128
SUBLANES = 8
MXU_DIM = 256
ODD_IN_PAD = Q_LORA + KV_LORA + LANES
VMEM_LIMIT_BYTES = 60000 * 1024

TOKEN_TILE = 512
FF_CHUNK = 2 * MXU_DIM
HGRN_TILE = 256
HGRN_BATCH = 2
HGRN_SUB = HGRN_TILE // 2
HGRN_LEVELS = (64, 32, 16, 8)
ATTN_TILE = 512
ATTN_HEADS_PER_STEP = 4


def _ff_chunks():
    chunks, off = [], 0
    while off < D_FF:
        size = min(FF_CHUNK, D_FF - off)
        chunks.append((off, size))
        off += size
    return tuple(chunks)


def _sigmoid(x):
    return 0.5 + 0.5 * jnp.tanh(0.5 * x)


def _silu(x):
    t = 0.5 * x
    return t + t * jnp.tanh(t)


def _gelu_tanh(x):
    c = math.sqrt(2.0 / math.pi)
    hx = 0.5 * x
    return hx + hx * jnp.tanh(x * (c + (c * 0.044715) * (x * x)))


def _rms(x, g):
    ms = jnp.mean(x * x, axis=-1, keepdims=True)
    return x * lax.rsqrt(ms + EPS) * g


def _dot(a, b):
    return jnp.dot(a, b, preferred_element_type=F32)


def _dot_nt(a, b):
    return lax.dot_general(a, b, (((1,), (1,)), ((), ())), preferred_element_type=F32)


def _dot_tn(a, b):
    return lax.dot_general(a, b, (((0,), (0,)), ((), ())), preferred_element_type=F32)


def _split2(x):
    hi = x.astype(BF16)
    lo = (x - hi.astype(F32)).astype(BF16)
    return hi, lo


def _swiglu_into(acc_ref, hn, wg_ref, wu_ref, wd_ref):
    for ci, (off, size) in enumerate(_ff_chunks()):
        gate = _dot(hn, wg_ref[:, off:off + size])
        up = _dot(hn, wu_ref[:, off:off + size])
        act = (_silu(gate) * up).astype(BF16)
        part = _dot(act, wd_ref[off:off + size, :])
        if ci == 0:
            acc_ref[...] = part
        else:
            acc_ref[...] += part


def _s1_even_kernel(x_ref, g0_ref, wg_ref, wu_ref, wd_ref, g1_ref, win_ref,
                    x1_ref, u_ref, v_ref, q_ref, f_ref, i_ref, gg_ref, acc_ref):
    x = x_ref[...]
    hn = _rms(x, g0_ref[...]).astype(BF16)
    _swiglu_into(acc_ref, hn, wg_ref, wu_ref, wd_ref)
    x1 = x + 0.5 * acc_ref[...]
    x1_ref[...] = x1
    h = _rms(x1, g1_ref[...]).astype(BF16)
    outs = (u_ref, v_ref, q_ref, f_ref, i_ref, gg_ref)
    for k, o_ref in enumerate(outs):
        z = _dot(h, win_ref[:, k * A_WIDTH:(k + 1) * A_WIDTH])
        o_ref[...] = z.astype(o_ref.dtype)


def _const_spec(shape, prefix=()):
    nd = len(shape)
    return pl.BlockSpec((None,) * len(prefix) + tuple(shape), lambda *_: tuple(prefix) + (0,) * nd,
                        pipeline_mode=pl.Buffered(1))


def _ffn_specs(layer):
    return [_const_spec((D_MODEL, D_FF), layer), _const_spec((D_MODEL, D_FF), layer),
            _const_spec((D_FF, D_MODEL), layer)]


def _stage1_even(x2d, g0, ffn, ffn_layer, g1, win, j):
    n = x2d.shape[0]
    tm = TOKEN_TILE
    row = lambda i: (i, 0)
    part_spec = pl.BlockSpec((tm, A_WIDTH), row)
    out_shape = [jax.ShapeDtypeStruct((n, D_MODEL), F32)]
    out_shape += [jax.ShapeDtypeStruct((n, A_WIDTH), F32 if k == 3 else BF16) for k in range(EVEN_PARTS)]
    return pl.pallas_call(
        _s1_even_kernel,
        grid=(n // tm,),
        in_specs=[pl.BlockSpec((tm, D_MODEL), row), _const_spec((1, D_MODEL))] + _ffn_specs(ffn_layer) + [
                  _const_spec((1, D_MODEL)), _const_spec((D_MODEL, EVEN_PARTS * A_WIDTH), (j,))],
        out_specs=[pl.BlockSpec((tm, D_MODEL), row)] + [part_spec] * EVEN_PARTS,
        out_shape=out_shape,
        scratch_shapes=[pltpu.VMEM((tm, D_MODEL), F32)],
        compiler_params=pltpu.CompilerParams(dimension_semantics=("parallel",),
                                             vmem_limit_bytes=VMEM_LIMIT_BYTES),
        name="stage1_even",
    )(x2d, g0, *ffn, g1, win)


def _s1_odd_kernel(x_ref, posr_ref, invfc_ref, g0_ref, wg_ref, wu_ref, wd_ref, g1_ref,
                   win_ref, wropet_ref, qan_ref, kvan_ref, wqbt_ref, wkvk_ref, wkvvt_ref, gq_ref, gkr_ref, kn_ref,
                   indk_ref, splat_ref,
                   x1_ref, qt_ref, k_ref, vt_ref, acc_ref):
    x = x_ref[0]
    hn = _rms(x, g0_ref[...]).astype(BF16)
    _swiglu_into(acc_ref, hn, wg_ref, wu_ref, wd_ref)
    x1 = x + 0.5 * acc_ref[...]
    x1_ref[0] = x1
    h = _rms(x1, g1_ref[...]).astype(BF16)
    z = _dot(h, win_ref[...])
    cqn = _rms(z[:, :Q_LORA], qan_ref[...]).astype(BF16)
    ckvn = _rms(z[:, Q_LORA:Q_LORA + KV_LORA], kvan_ref[...]).astype(BF16)
    k_rope = z[:, Q_LORA + KV_LORA:]
    q_t = _dot_nt(wqbt_ref[...], cqn)
    k_nope = _dot(ckvn, wkvk_ref[...])
    vt_ref[0] = _dot_nt(wkvvt_ref[...], ckvn).astype(BF16)

    ang_t = invfc_ref[...] * posr_ref[0]
    cos_f, sin_f = jnp.cos(ang_t), jnp.sin(ang_t)
    gq = gq_ref[...]
    half = C_ROPE // 2
    for hd in range(C_HEADS):
        blk = q_t[hd * C_QK:(hd + 1) * C_QK, :]
        ss = jnp.sum(blk * blk, axis=0, keepdims=True)
        r = lax.rsqrt(ss * (1.0 / C_QK) + EPS) * (ATTN_SCALE * LOG2E)
        sc = blk * r * gq
        a1 = sc[C_NOPE:C_NOPE + half, :]
        a2 = sc[C_NOPE + half:, :]
        qt_ref[0, hd, 0:C_NOPE, :] = sc[:C_NOPE, :].astype(BF16)
        qt_ref[0, hd, C_NOPE:C_NOPE + half, :] = (a1 * cos_f - a2 * sin_f).astype(BF16)
        qt_ref[0, hd, C_NOPE + half:C_QK, :] = (a2 * cos_f + a1 * sin_f).astype(BF16)

    kr_t = _dot_nt(wropet_ref[...], h) * gkr_ref[...]
    b1, b2 = kr_t[:half, :], kr_t[half:, :]
    rot_t = jnp.concatenate([b1 * cos_f - b2 * sin_f, b2 * cos_f + b1 * sin_f,
                             jnp.zeros((LANES - C_ROPE, kr_t.shape[1]), F32)], axis=0)
    kr_rot = rot_t.T

    lane = lax.broadcasted_iota(jnp.int32, (1, LANES), 1)
    kn_nope = kn_ref[...]
    k2_hi, k2_lo = _split2(jnp.concatenate([k_nope * k_nope, k_rope * k_rope], axis=1))
    ssk = _dot(k2_hi, indk_ref[...]) + _dot(k2_lo, indk_ref[...])
    rk = jnp.where(lane < C_HEADS, lax.rsqrt(ssk * (1.0 / C_QK) + EPS), 0.0)
    rk_hi, rk_lo = _split2(rk)
    rk_b = _dot(rk_hi, splat_ref[...]) + _dot(rk_lo, splat_ref[...])
    for hd in range(C_HEADS):
        cols = slice(hd * C_NOPE, (hd + 1) * C_NOPE)
        k_ref[0, hd, :, 0:C_NOPE] = (k_nope[:, cols] * rk_b[:, cols] * kn_nope).astype(BF16)
        k_ref[0, hd, :, C_NOPE:C_QK] = (kr_rot * rk_b[:, cols])[:, :C_ROPE].astype(BF16)


def _stage1_odd(x3d, posr, op, g0, ffn, ffn_layer, g1):
    bsz, seq, _ = x3d.shape
    tm = TOKEN_TILE
    tile = lambda b, i: (b, i, 0)
    return pl.pallas_call(
        _s1_odd_kernel,
        grid=(bsz, seq // tm),
        in_specs=[pl.BlockSpec((1, tm, D_MODEL), tile), pl.BlockSpec((1, 1, tm), lambda b, i: (b, 0, i)),
                  _const_spec((C_ROPE // 2, tm)), _const_spec((1, D_MODEL))]
                 + _ffn_specs(ffn_layer) + [
                  _const_spec((1, D_MODEL)), _const_spec((D_MODEL, ODD_IN_PAD)), _const_spec((C_ROPE, D_MODEL)),
                  _const_spec((1, Q_LORA)), _const_spec((1, KV_LORA)),
                  _const_spec((C_HEADS * C_QK, Q_LORA)), _const_spec((KV_LORA, C_HEADS * C_NOPE)),
                  _const_spec((C_HEADS * C_V, KV_LORA)), _const_spec((C_QK, tm)), _const_spec((C_ROPE, tm)),
                  _const_spec((1, C_NOPE)), _const_spec((C_HEADS * C_NOPE + LANES, LANES)),
                  _const_spec((LANES, C_HEADS * C_NOPE))],
        out_specs=[pl.BlockSpec((1, tm, D_MODEL), tile),
                   pl.BlockSpec((1, C_HEADS, C_QK, tm), lambda b, i: (b, 0, 0, i)),
                   pl.BlockSpec((1, C_HEADS, tm, C_QK), lambda b, i: (b, 0, i, 0)),
                   pl.BlockSpec((1, C_HEADS * C_V, tm), lambda b, i: (b, 0, i))],
        out_shape=[jax.ShapeDtypeStruct((bsz, seq, D_MODEL), F32),
                   jax.ShapeDtypeStruct((bsz, C_HEADS, C_QK, seq), BF16),
                   jax.ShapeDtypeStruct((bsz, C_HEADS, seq, C_QK), BF16),
                   jax.ShapeDtypeStruct((bsz, C_HEADS * C_V, seq), BF16)],
        scratch_shapes=[pltpu.VMEM((tm, D_MODEL), F32)],
        compiler_params=pltpu.CompilerParams(dimension_semantics=("parallel", "parallel"),
                                             vmem_limit_bytes=VMEM_LIMIT_BYTES),
        name="stage1_odd",
    )(x3d, posr, op["invfc"], g0, *ffn, g1, op["win"], op["wropet"], op["qan"], op["kvan"],
      op["wqbt"], op["wkvk"], op["wkvvt"], op["gq"], op["gkr"], op["kn"], op["indk"], op["splat"])


def _s2_kernel(x_ref, m_ref, p_ref, wo_ref, g2_ref, wg_ref, wu_ref, wd_ref,
               g3_ref, wpg_ref, wpp_ref, g4_ref, o_ref, acc_ref):
    x = x_ref[...] + _dot(m_ref[...], wo_ref[...])
    hn = _rms(x, g2_ref[...]).astype(BF16)
    _swiglu_into(acc_ref, hn, wg_ref, wu_ref, wd_ref)
    x = x + 0.5 * acc_ref[...]
    h3 = _rms(x, g3_ref[...]).astype(BF16)
    gate = _sigmoid(_dot(h3, wpg_ref[...]))
    proj = _dot(p_ref[...].astype(BF16), wpp_ref[...])
    o_ref[...] = x + _rms(gate * proj, g4_ref[...])


def _stage2(x2d, mixed, p3d, layer, wo, j, g2, ffn, ffn_layer, g3, wpg, wpp, g4):
    n = x2d.shape[0]
    tm = TOKEN_TILE
    row = lambda i: (i, 0)
    return pl.pallas_call(
        _s2_kernel,
        grid=(n // tm,),
        in_specs=[pl.BlockSpec((tm, D_MODEL), row), pl.BlockSpec((tm, D_MODEL), row),
                  pl.BlockSpec((None, tm, D_PLE), lambda i: (layer, i, 0)),
                  _const_spec((D_MODEL, D_MODEL), (j,)), _const_spec((1, D_MODEL))] + _ffn_specs(ffn_layer) + [
                  _const_spec((1, D_MODEL)), _const_spec((D_MODEL, D_MODEL), (layer,)),
                  _const_spec((D_PLE, D_MODEL), (layer,)), _const_spec((1, D_MODEL))],
        out_specs=pl.BlockSpec((tm, D_MODEL), row),
        out_shape=jax.ShapeDtypeStruct((n, D_MODEL), F32),
        scratch_shapes=[pltpu.VMEM((tm, D_MODEL), F32)],
        compiler_params=pltpu.CompilerParams(dimension_semantics=("parallel",),
                                             vmem_limit_bytes=VMEM_LIMIT_BYTES),
        name="stage2",
    )(x2d, mixed, p3d, wo, g2, *ffn, g3, wpg, wpp, g4)


def _level_map(ts):
    t = np.arange(ts)[:, None]
    s = np.arange(ts)[None, :]
    lvl = np.full((ts, ts), len(HGRN_LEVELS) + 1, np.int32)
    lvl[(t // SUBLANES == s // SUBLANES) & (s <= t)] = len(HGRN_LEVELS)
    for li, h in enumerate(HGRN_LEVELS):
        own = (t // (2 * h) == s // (2 * h)) & (t % (2 * h) >= h) & (s % (2 * h) < h)
        lvl[own] = li
    return lvl


def _even_core_kernel(u_ref, v_ref, q_ref, f_ref, i_ref, g_ref,
                      vnorm_ref, ind_ref, wcat_ref, bfull_ref, lb_ref, onorm_ref, ltri_ref, lvl_ref,
                      o_ref, state_ref):
    nb = HGRN_BATCH
    ts = HGRN_TILE
    sub = HGRN_SUB
    n_lvl = len(HGRN_LEVELS)
    rows_all = nb * ts

    def stacked(ref):
        return ref[...].reshape(rows_all, ref.shape[-1])

    u = _gelu_tanh(stacked(u_ref).astype(F32))
    v = _gelu_tanh(stacked(v_ref).astype(F32))
    ss = _dot((v * v).astype(BF16), ind_ref[...])
    vn = (v * lax.rsqrt(ss * (1.0 / A_HEAD_DIM) + EPS) * vnorm_ref[...]).astype(BF16)
    lane_head = lax.broadcasted_iota(jnp.int32, (1, A_WIDTH), 1) // A_HEAD_DIM
    for c in range(rows_all // A_CHUNK):
        rows = slice(c * A_CHUNK, (c + 1) * A_CHUNK)
        vc = vn[rows, :]
        zero = jnp.zeros_like(vc)
        vbd = jnp.concatenate([jnp.where(lane_head == hd, vc, zero) for hd in range(A_HEADS)], axis=0)
        mixed = _dot(wcat_ref[...], vbd) + bfull_ref[...]
        bi, local = divmod(c * A_CHUNK, ts)
        o_ref[bi, local:local + A_CHUNK, 0:A_WIDTH] = (u[rows, :] * mixed).astype(BF16)

    @pl.when(pl.program_id(1) == 0)
    def _():
        state_ref[...] = jnp.zeros_like(state_ref)

    lb = lb_ref[...]
    fg = lb + (1.0 - lb) * _sigmoid(stacked(f_ref))
    lf = jnp.log2(jnp.maximum(fg, F_MIN))
    kk = 1.0 - fg
    qq = stacked(q_ref).astype(F32)
    lf_hi, lf_lo = _split2(lf)
    ltri = ltri_ref[...]
    seqs = [slice(bi * ts, (bi + 1) * ts) for bi in range(nb)]
    b = jnp.concatenate([_dot(ltri, lf_hi[r, :]) + _dot(ltri, lf_lo[r, :]) for r in seqs], axis=0)

    q_lvls, k_lvls = [], []
    for h in HGRN_LEVELS:
        q_parts, k_parts = [], []
        zero = jnp.zeros((h, B_WIDTH), F32)
        for j in range(rows_all // (2 * h)):
            r0 = j * 2 * h
            bm = b[r0 + h - 1:r0 + h, :]
            lo_rows = slice(r0, r0 + h)
            up_rows = slice(r0 + h, r0 + 2 * h)
            k_parts += [kk[lo_rows, :] * jnp.exp2(bm - b[lo_rows, :]), zero]
            q_parts += [zero, qq[up_rows, :] * jnp.exp2(b[up_rows, :] - bm)]
        q_lvls.append(jnp.concatenate(q_parts, axis=0).astype(BF16))
        k_lvls.append(jnp.concatenate(k_parts, axis=0).astype(BF16))
    b3 = b.reshape(rows_all // SUBLANES, SUBLANES, B_WIDTH)
    bref = jnp.broadcast_to(b3[:, 3:4, :], b3.shape).reshape(rows_all, B_WIDTH)
    q_lvls.append((qq * jnp.exp2(b - bref)).astype(BF16))
    k_lvls.append((kk * jnp.exp2(bref - b)).astype(BF16))

    q_top, k_top, k_out, carry = [], [], [], []
    for bi in range(nb):
        r0 = bi * ts
        bm = b[r0 + sub - 1:r0 + sub, :]
        q_top.append((qq[r0 + sub:r0 + ts, :] * jnp.exp2(b[r0 + sub:r0 + ts, :] - bm)).astype(BF16))
        k_top.append((kk[r0:r0 + sub, :] * jnp.exp2(bm - b[r0:r0 + sub, :])).astype(BF16))
        b_last = b[r0 + ts - 1:r0 + ts, :]
        k_out.append((kk[seqs[bi], :] * jnp.exp2(b_last - b[seqs[bi], :])).astype(BF16))
        carry.append(jnp.exp2(b_last))
    q_in = (qq * jnp.exp2(b)).astype(BF16)
    vv = stacked(i_ref)
    gate = _silu(stacked(g_ref).astype(F32))
    onorm = onorm_ref[...]

    lvl = lvl_ref[...]
    masks = [lvl == li for li in range(n_lvl + 1)]
    units = [(bi, hd, slice(hd * B_DIM, (hd + 1) * B_DIM)) for hd in range(B_HEADS) for bi in range(nb)]
    states = {(bi, hd): state_ref[bi, hd] for bi, hd, _ in units}
    inter = {(bi, hd): _dot_nt(q_in[seqs[bi], cols], states[bi, hd].astype(BF16)) for bi, hd, cols in units}
    for bi, hd, cols in units:
        state_ref[bi, hd] = (states[bi, hd] * carry[bi][:, cols]
                             + _dot_tn(vv[seqs[bi], cols], k_out[bi][:, cols]))
    score_blocks = {}
    for bi, hd, cols in units:
        diag = []
        for a in range(ts // sub):
            rows = slice(bi * ts + a * sub, bi * ts + (a + 1) * sub)
            blk = jnp.where(masks[n_lvl], _dot_nt(q_lvls[n_lvl][rows, cols], k_lvls[n_lvl][rows, cols]), 0.0)
            for li in range(n_lvl):
                blk = jnp.where(masks[li], _dot_nt(q_lvls[li][rows, cols], k_lvls[li][rows, cols]), blk)
            diag.append(blk.astype(BF16))
        top = _dot_nt(q_top[bi][:, cols], k_top[bi][:, cols]).astype(BF16)
        score_blocks[bi, hd] = (diag[0], jnp.concatenate([top, diag[1]], axis=1))
    outs = {}
    for bi, hd, cols in units:
        v_h = vv[seqs[bi], cols]
        first, second = score_blocks[bi, hd]
        o_intra = jnp.concatenate([_dot(first, v_h[:sub, :]), _dot(second, v_h)], axis=0)
        outs[bi, hd] = o_intra + inter[bi, hd]
    for bi, hd, cols in units:
        on = _rms(outs[bi, hd], onorm[:, cols]) * gate[seqs[bi], cols]
        o_ref[bi, :, A_WIDTH + hd * B_DIM:A_WIDTH + (hd + 1) * B_DIM] = on.astype(BF16)


def _even_core(parts, vnorm, ind, wcat, bfull, lb, onorm, ltri, lvl):
    bsz, seq, _ = parts[0].shape
    ts = HGRN_TILE
    nb = HGRN_BATCH
    tile = lambda b, i: (b, i, 0)
    part_spec = pl.BlockSpec((nb, ts, A_WIDTH), tile)
    return pl.pallas_call(
        _even_core_kernel,
        grid=(bsz // nb, seq // ts),
        in_specs=[part_spec] * EVEN_PARTS + [
            _const_spec((1, A_WIDTH)), _const_spec((A_WIDTH, A_WIDTH)),
            _const_spec((A_CHUNK, A_HEADS * A_CHUNK)), _const_spec((A_CHUNK, A_WIDTH)),
            _const_spec((1, B_WIDTH)), _const_spec((1, B_WIDTH)),
            _const_spec((ts, ts)), _const_spec((HGRN_SUB, HGRN_SUB))],
        out_specs=pl.BlockSpec((nb, ts, D_MODEL), tile),
        out_shape=jax.ShapeDtypeStruct((bsz, seq, D_MODEL), BF16),
        scratch_shapes=[pltpu.VMEM((nb, B_HEADS, B_DIM, B_DIM), F32)],
        compiler_params=pltpu.CompilerParams(dimension_semantics=("parallel", "arbitrary"),
                                             vmem_limit_bytes=VMEM_LIMIT_BYTES),
        name="even_core",
    )(*parts, vnorm, ind, wcat, bfull, lb, onorm, ltri, lvl)


def _attn_kernel(qt_ref, k_ref, vt_ref, o_ref, m_ref, l_ref, acc_ref, sa_ref, sb_ref, mxa_ref, mxb_ref):
    tq = ATTN_TILE
    groups = tq // SUBLANES
    qi = pl.program_id(2)
    heads = range(ATTN_HEADS_PER_STEP)
    buf_a, buf_b = (sa_ref, mxa_ref), (sb_ref, mxb_ref)

    m_ref[...] = jnp.full_like(m_ref, NEG_BIG)
    l_ref[...] = jnp.zeros_like(l_ref)
    acc_ref[...] = jnp.zeros_like(acc_ref)

    def keys(j):
        return pl.ds(pl.multiple_of(j * tq, tq), tq)

    def scores(j, buf, heads=heads):
        s_ref, mx_ref = buf
        for hd in heads:
            s = _dot(k_ref[0, hd, keys(j), :], qt_ref[0, hd])
            s_ref[hd] = s
            mx_ref[hd] = jnp.max(s.reshape(groups, SUBLANES, tq), axis=0)

    def consume(j, buf, masked, heads=heads):
        s_ref, mx_ref = buf
        if masked:
            keep = (lax.broadcasted_iota(jnp.int32, (tq, tq), 0) <= lax.broadcasted_iota(jnp.int32, (tq, tq), 1))
        s3, m_new, alpha = {}, {}, {}
        for hd in heads:
            if masked:
                s3[hd] = jnp.where(keep, s_ref[hd], NEG_BIG).reshape(groups, SUBLANES, tq)
                cmax8 = jnp.max(s3[hd], axis=0)
            else:
                cmax8 = mx_ref[hd]
            cmax = jnp.max(cmax8, axis=0, keepdims=True)
            m_prev = m_ref[hd]
            m_new[hd] = jnp.maximum(m_prev, jnp.broadcast_to(cmax, (SUBLANES, tq)))
            alpha[hd] = jnp.exp2(m_prev - m_new[hd])
            m_ref[hd] = m_new[hd]
        ones = jnp.ones((2 * SUBLANES, tq), BF16)
        for hd in heads:
            sv = s3[hd] if masked else s_ref[hd].reshape(groups, SUBLANES, tq)
            p = jnp.exp2(sv - m_new[hd][None]).reshape(tq, tq).astype(BF16)
            lhs = jnp.concatenate([vt_ref[0, hd * C_V:(hd + 1) * C_V, keys(j)], ones], axis=0)
            pv = _dot(lhs, p)
            acc3 = acc_ref[hd].reshape(C_V // SUBLANES, SUBLANES, tq) * alpha[hd][None]
            acc_ref[hd] = acc3.reshape(C_V, tq) + pv[:C_V, :]
            l_ref[hd] = alpha[hd] * l_ref[hd] + pv[C_V:C_V + SUBLANES, :]

    scores(0, buf_a)
    pairs = qi // 2

    def body(t, carry):
        j = 2 * t
        for hd in heads:
            scores(j + 1, buf_b, [hd])
            consume(j, buf_a, False, [hd])
        for hd in heads:
            scores(j + 2, buf_a, [hd])
            consume(j + 1, buf_b, False, [hd])
        return carry

    lax.fori_loop(0, pairs, body, 0)

    @pl.when(qi % 2 == 0)
    def _():
        consume(qi, buf_a, True)

    @pl.when(qi % 2 == 1)
    def _():
        for hd in heads:
            scores(qi, buf_b, [hd])
            consume(qi - 1, buf_a, False, [hd])
        consume(qi, buf_b, True)
    for hd in range(ATTN_HEADS_PER_STEP):
        inv_l = 1.0 / l_ref[hd]
        out_t = acc_ref[hd].reshape(C_V // SUBLANES, SUBLANES, tq) * inv_l[None]
        o_ref[0, :, hd * C_V:(hd + 1) * C_V] = out_t.reshape(C_V, tq).T.astype(BF16)


def _attention(qt, k, vt):
    bsz, heads, seq, _ = k.shape
    tq = ATTN_TILE
    g = ATTN_HEADS_PER_STEP
    return pl.pallas_call(
        _attn_kernel,
        grid=(bsz, heads // g, seq // tq),
        in_specs=[pl.BlockSpec((1, g, C_QK, tq), lambda b, h, i: (b, h, 0, i)),
                  pl.BlockSpec((1, g, seq, C_QK), lambda b, h, i: (b, h, 0, 0)),
                  pl.BlockSpec((1, g * C_V, seq), lambda b, h, i: (b, h, 0))],
        out_specs=pl.BlockSpec((1, tq, g * C_V), lambda b, h, i: (b, i, h)),
        out_shape=jax.ShapeDtypeStruct((bsz, seq, heads * C_V), BF16),
        scratch_shapes=[pltpu.VMEM((g, SUBLANES, tq), F32), pltpu.VMEM((g, SUBLANES, tq), F32),
                        pltpu.VMEM((g, C_V, tq), F32), pltpu.VMEM((g, tq, tq), F32),
                        pltpu.VMEM((g, tq, tq), F32), pltpu.VMEM((g, SUBLANES, tq), F32),
                        pltpu.VMEM((g, SUBLANES, tq), F32)],
        compiler_params=pltpu.CompilerParams(dimension_semantics=("parallel", "parallel", "arbitrary"),
                                             vmem_limit_bytes=VMEM_LIMIT_BYTES),
        name="mla_attention",
    )(qt, k, vt)


def _row(v):
    return v.reshape(1, -1).astype(F32)


def _pad_cols(w, total):
    return jnp.pad(w, ((0, 0), (0, total - w.shape[1])))


def _even_params(v_norm, w_s, b_s, lb, out_norm):
    causal = np.tril(np.ones((A_CHUNK, A_CHUNK), bool))
    w = jnp.where(causal[None], w_s, 0.0)
    wcat = jnp.transpose(w, (1, 0, 2)).reshape(A_CHUNK, A_HEADS * A_CHUNK)
    bfull = jnp.repeat(b_s.T, A_HEAD_DIM, axis=1)
    head_of = np.arange(A_WIDTH) // A_HEAD_DIM
    ind = (head_of[:, None] == head_of[None, :]).astype(np.float32)
    ltri = np.tril(np.ones((HGRN_TILE, HGRN_TILE), np.float32))
    return dict(vnorm=_row(v_norm), ind=jnp.asarray(ind, BF16),
                wcat=wcat.astype(BF16), bfull=bfull.astype(F32), lb=_row(lb),
                onorm=_row(jnp.tile(out_norm, B_HEADS)), ltri=jnp.asarray(ltri, BF16),
                lvl=jnp.asarray(_level_map(HGRN_SUB)))


def _odd_params(w_in, q_a_norm, kv_a_norm, w_q_b, w_kv_b, q_norm, k_norm):
    tm = TOKEN_TILE
    win = _pad_cols(w_in, ODD_IN_PAD)
    wkv = w_kv_b.reshape(KV_LORA, C_HEADS, C_NOPE + C_V)
    wkvk = wkv[:, :, :C_NOPE].reshape(KV_LORA, C_HEADS * C_NOPE)
    wkvvt = wkv[:, :, C_NOPE:].reshape(KV_LORA, C_HEADS * C_V).T
    inv_freq = ROPE_THETA ** (-jnp.arange(0, C_ROPE, 2, dtype=F32) / C_ROPE)
    invfc = jnp.broadcast_to(inv_freq[:, None], (C_ROPE // 2, tm))
    head_of = np.arange(C_HEADS * C_NOPE) // C_NOPE
    splat = (head_of[:, None] == np.arange(LANES)[None, :]).astype(np.float32)
    rope_rows = np.broadcast_to((np.arange(LANES) < C_HEADS).astype(np.float32), (LANES, LANES))
    indk = np.concatenate([splat, rope_rows], axis=0)
    return dict(win=win.astype(BF16), wropet=w_in[:, Q_LORA + KV_LORA:].T.astype(BF16),
                qan=_row(q_a_norm), kvan=_row(kv_a_norm),
                wqbt=w_q_b.T.astype(BF16), wkvk=wkvk.astype(BF16), wkvvt=wkvvt.astype(BF16),
                gq=jnp.broadcast_to(q_norm.astype(F32)[:, None], (C_QK, tm)),
                gkr=jnp.broadcast_to(k_norm.astype(F32)[C_NOPE:, None], (C_ROPE, tm)),
                kn=_row(k_norm[:C_NOPE]), invfc=invfc,
                indk=jnp.asarray(indk, BF16), splat=jnp.asarray(splat.T, BF16))


def kernel(x, p, positions, norm_gains, ffn_w_gate, ffn_w_up, ffn_w_down, ple_w_gate, ple_w_proj,
           even_w_in, gmlp_v_norm, gmlp_w_s, gmlp_b_s, hgrn_lb_raw, hgrn_out_norm, even_w_out,
           mla_w_in, mla_q_a_norm, mla_kv_a_norm, mla_w_q_b, mla_w_kv_b, mla_q_norm, mla_k_norm, mla_w_out):
    bsz, seq, d = x.shape
    depth = norm_gains.shape[0]
    n = bsz * seq
    assert d == D_MODEL and seq % ATTN_TILE == 0 and seq % TOKEN_TILE == 0 and seq % HGRN_TILE == 0
    assert bsz % HGRN_BATCH == 0

    lb_sm = jax.nn.softmax(hgrn_lb_raw.astype(F32), axis=0)
    lower_bounds = jnp.clip(jnp.cumsum(lb_sm, axis=0) - lb_sm[0], 0.0, LB_MAX)
    posr = positions.astype(F32).reshape(bsz, 1, seq)

    ffn = tuple(w.astype(BF16) for w in (ffn_w_gate, ffn_w_up, ffn_w_down))
    ple_wg, ple_wp = ple_w_gate.astype(BF16), ple_w_proj.astype(BF16)
    even_win, even_wout, mla_wout = (w.astype(BF16) for w in (even_w_in, even_w_out, mla_w_out))
    p3d = p.reshape(depth, n, D_PLE)

    xc = x.reshape(n, d)
    for i in range(depth):
        g = norm_gains[i]
        j = i // 2
        if i % 2 == 0:
            ep = _even_params(gmlp_v_norm[j], gmlp_w_s[j], gmlp_b_s[j], lower_bounds[j], hgrn_out_norm[j])
            outs = _stage1_even(xc, _row(g[0]), ffn, (i, 0), _row(g[1]), even_win, j)
            xc = outs[0]
            parts = [o.reshape(bsz, seq, A_WIDTH) for o in outs[1:]]
            mixed = _even_core(parts, ep["vnorm"], ep["ind"], ep["wcat"], ep["bfull"], ep["lb"],
                               ep["onorm"], ep["ltri"], ep["lvl"])
            w_out = even_wout
        else:
            op = _odd_params(mla_w_in[j], mla_q_a_norm[j], mla_kv_a_norm[j], mla_w_q_b[j], mla_w_kv_b[j],
                             mla_q_norm[j], mla_k_norm[j])
            x1, qt, k, vt = _stage1_odd(xc.reshape(bsz, seq, d), posr, op, _row(g[0]), ffn, (i, 0),
                                        _row(g[1]))
            xc = x1.reshape(n, d)
            mixed = _attention(qt, k, vt)
            w_out = mla_wout
        xc = _stage2(xc, mixed.reshape(n, d), p3d, i, w_out, j, _row(g[2]), ffn, (i, 1), _row(g[3]),
                     ple_wg, ple_wp, _row(g[4]))
    return xc.reshape(bsz, seq, d)
```

```python
import functools
import math

import numpy as np
import jax
import jax.numpy as jnp
from jax import lax
from jax.experimental import pallas as pl
from jax.experimental.pallas import tpu as pltpu

F32 = jnp.float32
BF16 = jnp.bfloat16

D_MODEL = 1024
D_FF = 2816
D_PLE = 256
EPS = 1e-6
NEG_BIG = -1e30
F_MIN = 1e-6
LB_MAX = 0.999

A_HEADS = 8
A_HEAD_DIM = 64
A_WIDTH = A_HEADS * A_HEAD_DIM
A_CHUNK = 128
B_HEADS = 4
B_DIM = 128
B_WIDTH = B_HEADS * B_DIM
EVEN_PARTS = 6

C_HEADS = 8
C_NOPE = 128
C_ROPE = 64
C_V = 128
C_QK = C_NOPE + C_ROPE
Q_LORA = 384
KV_LORA = 256
ROPE_THETA = 10000.0
ATTN_SCALE = C_QK ** -0.5
LOG2E = math.log2(math.e)

LANES = 128
SUBLANES = 8
MXU_DIM = 256
ODD_IN_PAD = Q_LORA + KV_LORA + LANES
VMEM_LIMIT_BYTES = 60000 * 1024

TOKEN_TILE = 512
FF_CHUNK = 4 * MXU_DIM
HGRN_TILE = 256
HGRN_BATCH = 2
HGRN_SUB = HGRN_TILE // 2
HGRN_LEVELS = (64, 32, 16, 8)
ATTN_TILE = 512
ATTN_HEADS_PER_STEP = 4


def _ff_chunks():
    chunks, off = [], 0
    while off < D_FF:
        size = min(FF_CHUNK, D_FF - off)
        chunks.append((off, size))
        off += size
    return tuple(chunks)


def _sigmoid(x):
    return 0.5 + 0.5 * jnp.tanh(0.5 * x)


def _silu(x):
    t = 0.5 * x
    return t + t * jnp.tanh(t)


def _gelu_tanh(x):
    c = math.sqrt(2.0 / math.pi)
    hx = 0.5 * x
    return hx + hx * jnp.tanh(x * (c + (c * 0.044715) * (x * x)))


def _rms(x, g):
    ms = jnp.mean(x * x, axis=-1, keepdims=True)
    return x * lax.rsqrt(ms + EPS) * g


def _dot(a, b):
    return jnp.dot(a, b, preferred_element_type=F32)


def _dot_nt(a, b):
    return lax.dot_general(a, b, (((1,), (1,)), ((), ())), preferred_element_type=F32)


def _dot_tn(a, b):
    return lax.dot_general(a, b, (((0,), (0,)), ((), ())), preferred_element_type=F32)


def _split2(x):
    hi = x.astype(BF16)
    lo = (x - hi.astype(F32)).astype(BF16)
    return hi, lo


def _swiglu_into(acc_ref, hn, wg_ref, wu_ref, wd_ref):
    for ci, (off, size) in enumerate(_ff_chunks()):
        gate = _dot(hn, wg_ref[:, off:off + size])
        up = _dot(hn, wu_ref[:, off:off + size])
        act = (_silu(gate) * up).astype(BF16)
        part = _dot(act, wd_ref[off:off + size, :])
        if ci == 0:
            acc_ref[...] = part
        else:
            acc_ref[...] += part


def _s1_even_kernel(x_ref, g0_ref, wg_ref, wu_ref, wd_ref, g1_ref, win_ref,
                    x1_ref, u_ref, v_ref, q_ref, f_ref, i_ref, gg_ref, acc_ref):
    x = x_ref[...]
    hn = _rms(x, g0_ref[...]).astype(BF16)
    _swiglu_into(acc_ref, hn, wg_ref, wu_ref, wd_ref)
    x1 = x + 0.5 * acc_ref[...]
    x1_ref[...] = x1
    h = _rms(x1, g1_ref[...]).astype(BF16)
    outs = (u_ref, v_ref, q_ref, f_ref, i_ref, gg_ref)
    for k, o_ref in enumerate(outs):
        z = _dot(h, win_ref[:, k * A_WIDTH:(k + 1) * A_WIDTH])
        o_ref[...] = z.astype(o_ref.dtype)


def _const_spec(shape, prefix=()):
    nd = len(shape)
    return pl.BlockSpec((None,) * len(prefix) + tuple(shape), lambda *_: tuple(prefix) + (0,) * nd,
                        pipeline_mode=pl.Buffered(1))


def _ffn_specs(layer):
    return [_const_spec((D_MODEL, D_FF), layer), _const_spec((D_MODEL, D_FF), layer),
            _const_spec((D_FF, D_MODEL), layer)]


def _stage1_even(x2d, g0, ffn, ffn_layer, g1, win, j):
    n = x2d.shape[0]
    tm = TOKEN_TILE
    row = lambda i: (i, 0)
    part_spec = pl.BlockSpec((tm, A_WIDTH), row)
    out_shape = [jax.ShapeDtypeStruct((n, D_MODEL), F32)]
    out_shape += [jax.ShapeDtypeStruct((n, A_WIDTH), F32 if k == 3 else BF16) for k in range(EVEN_PARTS)]
    return pl.pallas_call(
        _s1_even_kernel,
        grid=(n // tm,),
        in_specs=[pl.BlockSpec((tm, D_MODEL), row), _const_spec((1, D_MODEL))] + _ffn_specs(ffn_layer) + [
                  _const_spec((1, D_MODEL)), _const_spec((D_MODEL, EVEN_PARTS * A_WIDTH), (j,))],
        out_specs=[pl.BlockSpec((tm, D_MODEL), row)] + [part_spec] * EVEN_PARTS,
        out_shape=out_shape,
        scratch_shapes=[pltpu.VMEM((tm, D_MODEL), F32)],
        compiler_params=pltpu.CompilerParams(dimension_semantics=("parallel",),
                                             vmem_limit_bytes=VMEM_LIMIT_BYTES),
        name="stage1_even",
    )(x2d, g0, *ffn, g1, win)


def _s1_odd_kernel(x_ref, posr_ref, invfc_ref, g0_ref, wg_ref, wu_ref, wd_ref, g1_ref,
                   win_ref, wropet_ref, qan_ref, kvan_ref, wqbt_ref, wkvk_ref, wkvvt_ref, gq_ref, gkr_ref, kn_ref,
                   indk_ref, splat_ref,
                   x1_ref, qt_ref, k_ref, vt_ref, acc_ref):
    x = x_ref[0]
    hn = _rms(x, g0_ref[...]).astype(BF16)
    _swiglu_into(acc_ref, hn, wg_ref, wu_ref, wd_ref)
    x1 = x + 0.5 * acc_ref[...]
    x1_ref[0] = x1
    h = _rms(x1, g1_ref[...]).astype(BF16)
    z = _dot(h, win_ref[...])
    cqn = _rms(z[:, :Q_LORA], qan_ref[...]).astype(BF16)
    ckvn = _rms(z[:, Q_LORA:Q_LORA + KV_LORA], kvan_ref[...]).astype(BF16)
    k_rope = z[:, Q_LORA + KV_LORA:]
    q_t = _dot_nt(wqbt_ref[...], cqn)
    k_nope = _dot(ckvn, wkvk_ref[...])
    vt_ref[0] = _dot_nt(wkvvt_ref[...], ckvn).astype(BF16)

    ang_t = invfc_ref[...] * posr_ref[0]
    cos_f, sin_f = jnp.cos(ang_t), jnp.sin(ang_t)
    gq = gq_ref[...]
    half = C_ROPE // 2
    for hd in range(C_HEADS):
        blk = q_t[hd * C_QK:(hd + 1) * C_QK, :]
        ss = jnp.sum(blk * blk, axis=0, keepdims=True)
        r = lax.rsqrt(ss * (1.0 / C_QK) + EPS) * (ATTN_SCALE * LOG2E)
        sc = blk * r * gq
        a1 = sc[C_NOPE:C_NOPE + half, :]
        a2 = sc[C_NOPE + half:, :]
        qt_ref[0, hd, 0:C_NOPE, :] = sc[:C_NOPE, :].astype(BF16)
        qt_ref[0, hd, C_NOPE:C_NOPE + half, :] = (a1 * cos_f - a2 * sin_f).astype(BF16)
        qt_ref[0, hd, C_NOPE + half:C_QK, :] = (a2 * cos_f + a1 * sin_f).astype(BF16)

    kr_t = _dot_nt(wropet_ref[...], h) * gkr_ref[...]
    b1, b2 = kr_t[:half, :], kr_t[half:, :]
    rot_t = jnp.concatenate([b1 * cos_f - b2 * sin_f, b2 * cos_f + b1 * sin_f,
                             jnp.zeros((LANES - C_ROPE, kr_t.shape[1]), F32)], axis=0)
    kr_rot = rot_t.T

    lane = lax.broadcasted_iota(jnp.int32, (1, LANES), 1)
    kn_nope = kn_ref[...]
    k2_hi, k2_lo = _split2(jnp.concatenate([k_nope * k_nope, k_rope * k_rope], axis=1))
    ssk = _dot(k2_hi, indk_ref[...]) + _dot(k2_lo, indk_ref[...])
    rk = jnp.where(lane < C_HEADS, lax.rsqrt(ssk * (1.0 / C_QK) + EPS), 0.0)
    rk_hi, rk_lo = _split2(rk)
    rk_b = _dot(rk_hi, splat_ref[...]) + _dot(rk_lo, splat_ref[...])
    for hd in range(C_HEADS):
        cols = slice(hd * C_NOPE, (hd + 1) * C_NOPE)
        k_ref[0, hd, :, 0:C_NOPE] = (k_nope[:, cols] * rk_b[:, cols] * kn_nope).astype(BF16)
        k_ref[0, hd, :, C_NOPE:C_QK] = (kr_rot * rk_b[:, cols])[:, :C_ROPE].astype(BF16)


def _stage1_odd(x3d, posr, op, g0, ffn, ffn_layer, g1):
    bsz, seq, _ = x3d.shape
    tm = TOKEN_TILE
    tile = lambda b, i: (b, i, 0)
    return pl.pallas_call(
        _s1_odd_kernel,
        grid=(bsz, seq // tm),
        in_specs=[pl.BlockSpec((1, tm, D_MODEL), tile), pl.BlockSpec((1, 1, tm), lambda b, i: (b, 0, i)),
                  _const_spec((C_ROPE // 2, tm)), _const_spec((1, D_MODEL))]
                 + _ffn_specs(ffn_layer) + [
                  _const_spec((1, D_MODEL)), _const_spec((D_MODEL, ODD_IN_PAD)), _const_spec((C_ROPE, D_MODEL)),
                  _const_spec((1, Q_LORA)), _const_spec((1, KV_LORA)),
                  _const_spec((C_HEADS * C_QK, Q_LORA)), _const_spec((KV_LORA, C_HEADS * C_NOPE)),
                  _const_spec((C_HEADS * C_V, KV_LORA)), _const_spec((C_QK, tm)), _const_spec((C_ROPE, tm)),
                  _const_spec((1, C_NOPE)), _const_spec((C_HEADS * C_NOPE + LANES, LANES)),
                  _const_spec((LANES, C_HEADS * C_NOPE))],
        out_specs=[pl.BlockSpec((1, tm, D_MODEL), tile),
                   pl.BlockSpec((1, C_HEADS, C_QK, tm), lambda b, i: (b, 0, 0, i)),
                   pl.BlockSpec((1, C_HEADS, tm, C_QK), lambda b, i: (b, 0, i, 0)),
                   pl.BlockSpec((1, C_HEADS * C_V, tm), lambda b, i: (b, 0, i))],
        out_shape=[jax.ShapeDtypeStruct((bsz, seq, D_MODEL), F32),
                   jax.ShapeDtypeStruct((bsz, C_HEADS, C_QK, seq), BF16),
                   jax.ShapeDtypeStruct((bsz, C_HEADS, seq, C_QK), BF16),
                   jax.ShapeDtypeStruct((bsz, C_HEADS * C_V, seq), BF16)],
        scratch_shapes=[pltpu.VMEM((tm, D_MODEL), F32)],
        compiler_params=pltpu.CompilerParams(dimension_semantics=("parallel", "parallel"),
                                             vmem_limit_bytes=VMEM_LIMIT_BYTES),
        name="stage1_odd",
    )(x3d, posr, op["invfc"], g0, *ffn, g1, op["win"], op["wropet"], op["qan"], op["kvan"],
      op["wqbt"], op["wkvk"], op["wkvvt"], op["gq"], op["gkr"], op["kn"], op["indk"], op["splat"])


def _s2_kernel(x_ref, m_ref, p_ref, wo_ref, g2_ref, wg_ref, wu_ref, wd_ref,
               g3_ref, wpg_ref, wpp_ref, g4_ref, o_ref, acc_ref):
    x = x_ref[...] + _dot(m_ref[...], wo_ref[...])
    hn = _rms(x, g2_ref[...]).astype(BF16)
    _swiglu_into(acc_ref, hn, wg_ref, wu_ref, wd_ref)
    x = x + 0.5 * acc_ref[...]
    h3 = _rms(x, g3_ref[...]).astype(BF16)
    gate = _sigmoid(_dot(h3, wpg_ref[...]))
    proj = _dot(p_ref[...].astype(BF16), wpp_ref[...])
    o_ref[...] = x + _rms(gate * proj, g4_ref[...])


def _stage2(x2d, mixed, p3d, layer, wo, j, g2, ffn, ffn_layer, g3, wpg, wpp, g4):
    n = x2d.shape[0]
    tm = TOKEN_TILE
    row = lambda i: (i, 0)
    return pl.pallas_call(
        _s2_kernel,
        grid=(n // tm,),
        in_specs=[pl.BlockSpec((tm, D_MODEL), row), pl.BlockSpec((tm, D_MODEL), row),
                  pl.BlockSpec((None, tm, D_PLE), lambda i: (layer, i, 0)),
                  _const_spec((D_MODEL, D_MODEL), (j,)), _const_spec((1, D_MODEL))] + _ffn_specs(ffn_layer) + [
                  _const_spec((1, D_MODEL)), _const_spec((D_MODEL, D_MODEL), (layer,)),
                  _const_spec((D_PLE, D_MODEL), (layer,)), _const_spec((1, D_MODEL))],
        out_specs=pl.BlockSpec((tm, D_MODEL), row),
        out_shape=jax.ShapeDtypeStruct((n, D_MODEL), F32),
        scratch_shapes=[pltpu.VMEM((tm, D_MODEL), F32)],
        compiler_params=pltpu.CompilerParams(dimension_semantics=("parallel",),
                                             vmem_limit_bytes=VMEM_LIMIT_BYTES),
        name="stage2",
    )(x2d, mixed, p3d, wo, g2, *ffn, g3, wpg, wpp, g4)


def _level_map(ts):
    t = np.arange(ts)[:, None]
    s = np.arange(ts)[None, :]
    lvl = np.full((ts, ts), len(HGRN_LEVELS) + 1, np.int32)
    lvl[(t // SUBLANES == s // SUBLANES) & (s <= t)] = len(HGRN_LEVELS)
    for li, h in enumerate(HGRN_LEVELS):
        own = (t // (2 * h) == s // (2 * h)) & (t % (2 * h) >= h) & (s % (2 * h) < h)
        lvl[own] = li
    return lvl


def _even_core_kernel(u_ref, v_ref, q_ref, f_ref, i_ref, g_ref,
                      vnorm_ref, ind_ref, wcat_ref, bfull_ref, lb_ref, onorm_ref, ltri_ref, lvl_ref,
                      o_ref, state_ref):
    nb = HGRN_BATCH
    ts = HGRN_TILE
    sub = HGRN_SUB
    n_lvl = len(HGRN_LEVELS)
    rows_all = nb * ts

    def stacked(ref):
        return ref[...].reshape(rows_all, ref.shape[-1])

    u = _gelu_tanh(stacked(u_ref).astype(F32))
    v = _gelu_tanh(stacked(v_ref).astype(F32))
    ss = _dot((v * v).astype(BF16), ind_ref[...])
    vn = (v * lax.rsqrt(ss * (1.0 / A_HEAD_DIM) + EPS) * vnorm_ref[...]).astype(BF16)
    lane_head = lax.broadcasted_iota(jnp.int32, (1, A_WIDTH), 1) // A_HEAD_DIM
    for c in range(rows_all // A_CHUNK):
        rows = slice(c * A_CHUNK, (c + 1) * A_CHUNK)
        vc = vn[rows, :]
        zero = jnp.zeros_like(vc)
        vbd = jnp.concatenate([jnp.where(lane_head == hd, vc, zero) for hd in range(A_HEADS)], axis=0)
        mixed = _dot(wcat_ref[...], vbd) + bfull_ref[...]
        bi, local = divmod(c * A_CHUNK, ts)
        o_ref[bi, local:local + A_CHUNK, 0:A_WIDTH] = (u[rows, :] * mixed).astype(BF16)

    @pl.when(pl.program_id(1) == 0)
    def _():
        state_ref[...] = jnp.zeros_like(state_ref)

    lb = lb_ref[...]
    fg = lb + (1.0 - lb) * _sigmoid(stacked(f_ref))
    lf = jnp.log2(jnp.maximum(fg, F_MIN))
    kk = 1.0 - fg
    qq = stacked(q_ref).astype(F32)
    lf_hi, lf_lo = _split2(lf)
    ltri = ltri_ref[...]
    seqs = [slice(bi * ts, (bi + 1) * ts) for bi in range(nb)]
    b = jnp.concatenate([_dot(ltri, lf_hi[r, :]) + _dot(ltri, lf_lo[r, :]) for r in seqs], axis=0)

    q_lvls, k_lvls = [], []
    for h in HGRN_LEVELS:
        q_parts, k_parts = [], []
        zero = jnp.zeros((h, B_WIDTH), F32)
        for j in range(rows_all // (2 * h)):
            r0 = j * 2 * h
            bm = b[r0 + h - 1:r0 + h, :]
            lo_rows = slice(r0, r0 + h)
            up_rows = slice(r0 + h, r0 + 2 * h)
            k_parts += [kk[lo_rows, :] * jnp.exp2(bm - b[lo_rows, :]), zero]
            q_parts += [zero, qq[up_rows, :] * jnp.exp2(b[up_rows, :] - bm)]
        q_lvls.append(jnp.concatenate(q_parts, axis=0).astype(BF16))
        k_lvls.append(jnp.concatenate(k_parts, axis=0).astype(BF16))
    b3 = b.reshape(rows_all // SUBLANES, SUBLANES, B_WIDTH)
    bref = jnp.broadcast_to(b3[:, 3:4, :], b3.shape).reshape(rows_all, B_WIDTH)
    q_lvls.append((qq * jnp.exp2(b - bref)).astype(BF16))
    k_lvls.append((kk * jnp.exp2(bref - b)).astype(BF16))

    q_top, k_top, k_out, carry = [], [], [], []
    for bi in range(nb):
        r0 = bi * ts
        bm = b[r0 + sub - 1:r0 + sub, :]
        q_top.append((qq[r0 + sub:r0 + ts, :] * jnp.exp2(b[r0 + sub:r0 + ts, :] - bm)).astype(BF16))
        k_top.append((kk[r0:r0 + sub, :] * jnp.exp2(bm - b[r0:r0 + sub, :])).astype(BF16))
        b_last = b[r0 + ts - 1:r0 + ts, :]
        k_out.append((kk[seqs[bi], :] * jnp.exp2(b_last - b[seqs[bi], :])).astype(BF16))
        carry.append(jnp.exp2(b_last))
    q_in = (qq * jnp.exp2(b)).astype(BF16)
    vv = stacked(i_ref)
    gate = _silu(stacked(g_ref).astype(F32))
    onorm = onorm_ref[...]

    lvl = lvl_ref[...]
    masks = [lvl == li for li in range(n_lvl + 1)]
    units = [(bi, hd, slice(hd * B_DIM, (hd + 1) * B_DIM)) for hd in range(B_HEADS) for bi in range(nb)]
    states = {(bi, hd): state_ref[bi, hd] for bi, hd, _ in units}
    inter = {(bi, hd): _dot_nt(q_in[seqs[bi], cols], states[bi, hd].astype(BF16)) for bi, hd, cols in units}
    for bi, hd, cols in units:
        state_ref[bi, hd] = (states[bi, hd] * carry[bi][:, cols]
                             + _dot_tn(vv[seqs[bi], cols], k_out[bi][:, cols]))
    score_blocks = {}
    for bi, hd, cols in units:
        diag = []
        for a in range(ts // sub):
            rows = slice(bi * ts + a * sub, bi * ts + (a + 1) * sub)
            blk = jnp.where(masks[n_lvl], _dot_nt(q_lvls[n_lvl][rows, cols], k_lvls[n_lvl][rows, cols]), 0.0)
            for li in range(n_lvl):
                blk = jnp.where(masks[li], _dot_nt(q_lvls[li][rows, cols], k_lvls[li][rows, cols]), blk)
            diag.append(blk.astype(BF16))
        top = _dot_nt(q_top[bi][:, cols], k_top[bi][:, cols]).astype(BF16)
        score_blocks[bi, hd] = (diag[0], jnp.concatenate([top, diag[1]], axis=1))
    outs = {}
    for bi, hd, cols in units:
        v_h = vv[seqs[bi], cols]
        first, second = score_blocks[bi, hd]
        o_intra = jnp.concatenate([_dot(first, v_h[:sub, :]), _dot(second, v_h)], axis=0)
        outs[bi, hd] = o_intra + inter[bi, hd]
    for bi, hd, cols in units:
        on = _rms(outs[bi, hd], onorm[:, cols]) * gate[seqs[bi], cols]
        o_ref[bi, :, A_WIDTH + hd * B_DIM:A_WIDTH + (hd + 1) * B_DIM] = on.astype(BF16)


def _even_core(parts, vnorm, ind, wcat, bfull, lb, onorm, ltri, lvl):
    bsz, seq, _ = parts[0].shape
    ts = HGRN_TILE
    nb = HGRN_BATCH
    tile = lambda b, i: (b, i, 0)
    part_spec = pl.BlockSpec((nb, ts, A_WIDTH), tile)
    return pl.pallas_call(
        _even_core_kernel,
        grid=(bsz // nb, seq // ts),
        in_specs=[part_spec] * EVEN_PARTS + [
            _const_spec((1, A_WIDTH)), _const_spec((A_WIDTH, A_WIDTH)),
            _const_spec((A_CHUNK, A_HEADS * A_CHUNK)), _const_spec((A_CHUNK, A_WIDTH)),
            _const_spec((1, B_WIDTH)), _const_spec((1, B_WIDTH)),
            _const_spec((ts, ts)), _const_spec((HGRN_SUB, HGRN_SUB))],
        out_specs=pl.BlockSpec((nb, ts, D_MODEL), tile),
        out_shape=jax.ShapeDtypeStruct((bsz, seq, D_MODEL), BF16),
        scratch_shapes=[pltpu.VMEM((nb, B_HEADS, B_DIM, B_DIM), F32)],
        compiler_params=pltpu.CompilerParams(dimension_semantics=("parallel", "arbitrary"),
                                             vmem_limit_bytes=VMEM_LIMIT_BYTES),
        name="even_core",
    )(*parts, vnorm, ind, wcat, bfull, lb, onorm, ltri, lvl)


def _attn_kernel(qt_ref, k_ref, vt_ref, o_ref, m_ref, l_ref, acc_ref, sa_ref, sb_ref, mxa_ref, mxb_ref):
    tq = ATTN_TILE
    groups = tq // SUBLANES
    qi = pl.program_id(2)
    heads = range(ATTN_HEADS_PER_STEP)
    buf_a, buf_b = (sa_ref, mxa_ref), (sb_ref, mxb_ref)

    m_ref[...] = jnp.full_like(m_ref, NEG_BIG)
    l_ref[...] = jnp.zeros_like(l_ref)
    acc_ref[...] = jnp.zeros_like(acc_ref)

    def keys(j):
        return pl.ds(pl.multiple_of(j * tq, tq), tq)

    def scores(j, buf, heads=heads):
        s_ref, mx_ref = buf
        for hd in heads:
            s = _dot(k_ref[0, hd, keys(j), :], qt_ref[0, hd])
            s_ref[hd] = s
            mx_ref[hd] = jnp.max(s.reshape(groups, SUBLANES, tq), axis=0)

    def consume(j, buf, masked, heads=heads):
        s_ref, mx_ref = buf
        if masked:
            keep = (lax.broadcasted_iota(jnp.int32, (tq, tq), 0) <= lax.broadcasted_iota(jnp.int32, (tq, tq), 1))
        s3, m_new, alpha = {}, {}, {}
        for hd in heads:
            if masked:
                s3[hd] = jnp.where(keep, s_ref[hd], NEG_BIG).reshape(groups, SUBLANES, tq)
                cmax8 = jnp.max(s3[hd], axis=0)
            else:
                cmax8 = mx_ref[hd]
            cmax = jnp.max(cmax8, axis=0, keepdims=True)
            m_prev = m_ref[hd]
            m_new[hd] = jnp.maximum(m_prev, jnp.broadcast_to(cmax, (SUBLANES, tq)))
            alpha[hd] = jnp.exp2(m_prev - m_new[hd])
            m_ref[hd] = m_new[hd]
        ones = jnp.ones((2 * SUBLANES, tq), BF16)
        for hd in heads:
            sv = s3[hd] if masked else s_ref[hd].reshape(groups, SUBLANES, tq)
            p = jnp.exp2(sv - m_new[hd][None]).reshape(tq, tq).astype(BF16)
            lhs = jnp.concatenate([vt_ref[0, hd * C_V:(hd + 1) * C_V, keys(j)], ones], axis=0)
            pv = _dot(lhs, p)
            acc3 = acc_ref[hd].reshape(C_V // SUBLANES, SUBLANES, tq) * alpha[hd][None]
            acc_ref[hd] = acc3.reshape(C_V, tq) + pv[:C_V, :]
            l_ref[hd] = alpha[hd] * l_ref[hd] + pv[C_V:C_V + SUBLANES, :]

    scores(0, buf_a)
    pairs = qi // 2

    def body(t, carry):
        j = 2 * t
        for hd in heads:
            scores(j + 1, buf_b, [hd])
            consume(j, buf_a, False, [hd])
        for hd in heads:
            scores(j + 2, buf_a, [hd])
            consume(j + 1, buf_b, False, [hd])
        return carry

    lax.fori_loop(0, pairs, body, 0)

    @pl.when(qi % 2 == 0)
    def _():
        consume(qi, buf_a, True)

    @pl.when(qi % 2 == 1)
    def _():
        for hd in heads:
            scores(qi, buf_b, [hd])
            consume(qi - 1, buf_a, False, [hd])
        consume(qi, buf_b, True)
    for hd in range(ATTN_HEADS_PER_STEP):
        inv_l = 1.0 / l_ref[hd]
        out_t = acc_ref[hd].reshape(C_V // SUBLANES, SUBLANES, tq) * inv_l[None]
        o_ref[0, :, hd * C_V:(hd + 1) * C_V] = out_t.reshape(C_V, tq).T.astype(BF16)


def _attention(qt, k, vt):
    bsz, heads, seq, _ = k.shape
    tq = ATTN_TILE
    g = ATTN_HEADS_PER_STEP
    return pl.pallas_call(
        _attn_kernel,
        grid=(bsz, heads // g, seq // tq),
        in_specs=[pl.BlockSpec((1, g, C_QK, tq), lambda b, h, i: (b, h, 0, i)),
                  pl.BlockSpec((1, g, seq, C_QK), lambda b, h, i: (b, h, 0, 0)),
                  pl.BlockSpec((1, g * C_V, seq), lambda b, h, i: (b, h, 0))],
        out_specs=pl.BlockSpec((1, tq, g * C_V), lambda b, h, i: (b, i, h)),
        out_shape=jax.ShapeDtypeStruct((bsz, seq, heads * C_V), BF16),
        scratch_shapes=[pltpu.VMEM((g, SUBLANES, tq), F32), pltpu.VMEM((g, SUBLANES, tq), F32),
                        pltpu.VMEM((g, C_V, tq), F32), pltpu.VMEM((g, tq, tq), F32),
                        pltpu.VMEM((g, tq, tq), F32), pltpu.VMEM((g, SUBLANES, tq), F32),
                        pltpu.VMEM((g, SUBLANES, tq), F32)],
        compiler_params=pltpu.CompilerParams(dimension_semantics=("parallel", "parallel", "arbitrary"),
                                             vmem_limit_bytes=VMEM_LIMIT_BYTES),
        name="mla_attention",
    )(qt, k, vt)


def _row(v):
    return v.reshape(1, -1).astype(F32)


def _pad_cols(w, total):
    return jnp.pad(w, ((0, 0), (0, total - w.shape[1])))


def _even_params(v_norm, w_s, b_s, lb, out_norm):
    causal = np.tril(np.ones((A_CHUNK, A_CHUNK), bool))
    w = jnp.where(causal[None], w_s, 0.0)
    wcat = jnp.transpose(w, (1, 0, 2)).reshape(A_CHUNK, A_HEADS * A_CHUNK)
    bfull = jnp.repeat(b_s.T, A_HEAD_DIM, axis=1)
    head_of = np.arange(A_WIDTH) // A_HEAD_DIM
    ind = (head_of[:, None] == head_of[None, :]).astype(np.float32)
    ltri = np.tril(np.ones((HGRN_TILE, HGRN_TILE), np.float32))
    return dict(vnorm=_row(v_norm), ind=jnp.asarray(ind, BF16),
                wcat=wcat.astype(BF16), bfull=bfull.astype(F32), lb=_row(lb),
                onorm=_row(jnp.tile(out_norm, B_HEADS)), ltri=jnp.asarray(ltri, BF16),
                lvl=jnp.asarray(_level_map(HGRN_SUB)))


def _odd_params(w_in, q_a_norm, kv_a_norm, w_q_b, w_kv_b, q_norm, k_norm):
    tm = TOKEN_TILE
    win = _pad_cols(w_in, ODD_IN_PAD)
    wkv = w_kv_b.reshape(KV_LORA, C_HEADS, C_NOPE + C_V)
    wkvk = wkv[:, :, :C_NOPE].reshape(KV_LORA, C_HEADS * C_NOPE)
    wkvvt = wkv[:, :, C_NOPE:].reshape(KV_LORA, C_HEADS * C_V).T
    inv_freq = ROPE_THETA ** (-jnp.arange(0, C_ROPE, 2, dtype=F32) / C_ROPE)
    invfc = jnp.broadcast_to(inv_freq[:, None], (C_ROPE // 2, tm))
    head_of = np.arange(C_HEADS * C_NOPE) // C_NOPE
    splat = (head_of[:, None] == np.arange(LANES)[None, :]).astype(np.float32)
    rope_rows = np.broadcast_to((np.arange(LANES) < C_HEADS).astype(np.float32), (LANES, LANES))
    indk = np.concatenate([splat, rope_rows], axis=0)
    return dict(win=win.astype(BF16), wropet=w_in[:, Q_LORA + KV_LORA:].T.astype(BF16),
                qan=_row(q_a_norm), kvan=_row(kv_a_norm),
                wqbt=w_q_b.T.astype(BF16), wkvk=wkvk.astype(BF16), wkvvt=wkvvt.astype(BF16),
                gq=jnp.broadcast_to(q_norm.astype(F32)[:, None], (C_QK, tm)),
                gkr=jnp.broadcast_to(k_norm.astype(F32)[C_NOPE:, None], (C_ROPE, tm)),
                kn=_row(k_norm[:C_NOPE]), invfc=invfc,
                indk=jnp.asarray(indk, BF16), splat=jnp.asarray(splat.T, BF16))


def kernel(x, p, positions, norm_gains, ffn_w_gate, ffn_w_up, ffn_w_down, ple_w_gate, ple_w_proj,
           even_w_in, gmlp_v_norm, gmlp_w_s, gmlp_b_s, hgrn_lb_raw, hgrn_out_norm, even_w_out,
           mla_w_in, mla_q_a_norm, mla_kv_a_norm, mla_w_q_b, mla_w_kv_b, mla_q_norm, mla_k_norm, mla_w_out):
    bsz, seq, d = x.shape
    depth = norm_gains.shape[0]
    n = bsz * seq
    assert d == D_MODEL and seq % ATTN_TILE == 0 and seq % TOKEN_TILE == 0 and seq % HGRN_TILE == 0
    assert bsz % HGRN_BATCH == 0

    lb_sm = jax.nn.softmax(hgrn_lb_raw.astype(F32), axis=0)
    lower_bounds = jnp.clip(jnp.cumsum(lb_sm, axis=0) - lb_sm[0], 0.0, LB_MAX)
    posr = positions.astype(F32).reshape(bsz, 1, seq)

    ffn = tuple(w.astype(BF16) for w in (ffn_w_gate, ffn_w_up, ffn_w_down))
    ple_wg, ple_wp = ple_w_gate.astype(BF16), ple_w_proj.astype(BF16)
    even_win, even_wout, mla_wout = (w.astype(BF16) for w in (even_w_in, even_w_out, mla_w_out))
    p3d = p.reshape(depth, n, D_PLE)

    xc = x.reshape(n, d)
    for i in range(depth):
        g = norm_gains[i]
        j = i // 2
        if i % 2 == 0:
            ep = _even_params(gmlp_v_norm[j], gmlp_w_s[j], gmlp_b_s[j], lower_bounds[j], hgrn_out_norm[j])
            outs = _stage1_even(xc, _row(g[0]), ffn, (i, 0), _row(g[1]), even_win, j)
            xc = outs[0]
            parts = [o.reshape(bsz, seq, A_WIDTH) for o in outs[1:]]
            mixed = _even_core(parts, ep["vnorm"], ep["ind"], ep["wcat"], ep["bfull"], ep["lb"],
                               ep["onorm"], ep["ltri"], ep["lvl"])
            w_out = even_wout
        else:
            op = _odd_params(mla_w_in[j], mla_q_a_norm[j], mla_kv_a_norm[j], mla_w_q_b[j], mla_w_kv_b[j],
                             mla_q_norm[j], mla_k_norm[j])
            x1, qt, k, vt = _stage1_odd(xc.reshape(bsz, seq, d), posr, op, _row(g[0]), ffn, (i, 0),
                                        _row(g[1]))
            xc = x1.reshape(n, d)
            mixed = _attention(qt, k, vt)
            w_out = mla_wout
        xc = _stage2(xc, mixed.reshape(n, d), p3d, i, w_out, j, _row(g[2]), ffn, (i, 1), _row(g[3]),
                     ple_wg, ple_wp, _row(g[4]))
    return xc.reshape(bsz, seq, d)
```

```python
import functools
import math

import numpy as np
import jax
import jax.numpy as jnp
from jax import lax
from jax.experimental import pallas as pl
from jax.experimental.pallas import tpu as pltpu

F32 = jnp.float32
BF16 = jnp.bfloat16

D_MODEL = 1024
D_FF = 2816
D_PLE = 256
EPS = 1e-6
NEG_BIG = -1e30
F_MIN = 1e-6
LB_MAX = 0.999

A_HEADS = 8
A_HEAD_DIM = 64
A_WIDTH = A_HEADS * A_HEAD_DIM
A_CHUNK = 128
B_HEADS = 4
B_DIM = 128
B_WIDTH = B_HEADS * B_DIM
EVEN_PARTS = 6

C_HEADS = 8
C_NOPE = 128
C_ROPE = 64
C_V = 128
C_QK = C_NOPE + C_ROPE
Q_LORA = 384
KV_LORA = 256
ROPE_THETA = 10000.0
ATTN_SCALE = C_QK ** -0.5
LOG2E = math.log2(math.e)

LANES = 128
SUBLANES = 8
MXU_DIM = 256
ODD_IN_PAD = Q_LORA + KV_LORA + LANES
VMEM_LIMIT_BYTES = 60000 * 1024

TOKEN_TILE = 512
HGRN_TILE = 256
HGRN_BATCH = 2
HGRN_SUB = HGRN_TILE // 2
HGRN_LEVELS = (64, 32, 16, 8)
ATTN_TILE = 512
ATTN_HEADS_PER_STEP = 4


def _sigmoid(x):
    return 0.5 + 0.5 * jnp.tanh(0.5 * x)


def _silu(x):
    t = 0.5 * x
    return t + t * jnp.tanh(t)


def _gelu_tanh(x):
    c = math.sqrt(2.0 / math.pi)
    hx = 0.5 * x
    return hx + hx * jnp.tanh(x * (c + (c * 0.044715) * (x * x)))


def _rms(x, g):
    ms = jnp.mean(x * x, axis=-1, keepdims=True)
    return x * lax.rsqrt(ms + EPS) * g


def _dot(a, b):
    return jnp.dot(a, b, preferred_element_type=F32)


def _dot_nt(a, b):
    return lax.dot_general(a, b, (((1,), (1,)), ((), ())), preferred_element_type=F32)


def _dot_tn(a, b):
    return lax.dot_general(a, b, (((0,), (0,)), ((), ())), preferred_element_type=F32)


def _split2(x):
    hi = x.astype(BF16)
    lo = (x - hi.astype(F32)).astype(BF16)
    return hi, lo


def _swiglu(hn, wg_ref, wu_ref, wd_ref):
    act = (_silu(_dot(hn, wg_ref[...])) * _dot(hn, wu_ref[...])).astype(BF16)
    return _dot(act, wd_ref[...])


def _s1_even_kernel(x_ref, g0_ref, wg_ref, wu_ref, wd_ref, g1_ref, win_ref,
                    x1_ref, u_ref, v_ref, q_ref, f_ref, i_ref, gg_ref):
    x = x_ref[...]
    hn = _rms(x, g0_ref[...]).astype(BF16)
    x1 = x + 0.5 * _swiglu(hn, wg_ref, wu_ref, wd_ref)
    x1_ref[...] = x1
    h = _rms(x1, g1_ref[...]).astype(BF16)
    outs = (u_ref, v_ref, q_ref, f_ref, i_ref, gg_ref)
    for k, o_ref in enumerate(outs):
        z = _dot(h, win_ref[:, k * A_WIDTH:(k + 1) * A_WIDTH])
        o_ref[...] = z.astype(o_ref.dtype)


def _const_spec(shape, prefix=()):
    nd = len(shape)
    return pl.BlockSpec((None,) * len(prefix) + tuple(shape), lambda *_: tuple(prefix) + (0,) * nd,
                        pipeline_mode=pl.Buffered(1))


def _ffn_specs(layer):
    return [_const_spec((D_MODEL, D_FF), layer), _const_spec((D_MODEL, D_FF), layer),
            _const_spec((D_FF, D_MODEL), layer)]


def _stage1_even(x2d, g0, ffn, ffn_layer, g1, win, j):
    n = x2d.shape[0]
    tm = TOKEN_TILE
    row = lambda i: (i, 0)
    part_spec = pl.BlockSpec((tm, A_WIDTH), row)
    out_shape = [jax.ShapeDtypeStruct((n, D_MODEL), F32)]
    out_shape += [jax.ShapeDtypeStruct((n, A_WIDTH), F32 if k == 3 else BF16) for k in range(EVEN_PARTS)]
    return pl.pallas_call(
        _s1_even_kernel,
        grid=(n // tm,),
        in_specs=[pl.BlockSpec((tm, D_MODEL), row), _const_spec((1, D_MODEL))] + _ffn_specs(ffn_layer) + [
                  _const_spec((1, D_MODEL)), _const_spec((D_MODEL, EVEN_PARTS * A_WIDTH), (j,))],
        out_specs=[pl.BlockSpec((tm, D_MODEL), row)] + [part_spec] * EVEN_PARTS,
        out_shape=out_shape,
        compiler_params=pltpu.CompilerParams(dimension_semantics=("parallel",),
                                             vmem_limit_bytes=VMEM_LIMIT_BYTES),
        name="stage1_even",
    )(x2d, g0, *ffn, g1, win)


def _s1_odd_kernel(x_ref, posr_ref, invfc_ref, g0_ref, wg_ref, wu_ref, wd_ref, g1_ref,
                   win_ref, wropet_ref, qan_ref, kvan_ref, wqbt_ref, wkvk_ref, wkvvt_ref, gq_ref, gkr_ref, kn_ref,
                   indk_ref, splat_ref,
                   x1_ref, qt_ref, k_ref, vt_ref):
    x = x_ref[0]
    hn = _rms(x, g0_ref[...]).astype(BF16)
    x1 = x + 0.5 * _swiglu(hn, wg_ref, wu_ref, wd_ref)
    x1_ref[0] = x1
    h = _rms(x1, g1_ref[...]).astype(BF16)
    z = _dot(h, win_ref[...])
    cqn = _rms(z[:, :Q_LORA], qan_ref[...]).astype(BF16)
    ckvn = _rms(z[:, Q_LORA:Q_LORA + KV_LORA], kvan_ref[...]).astype(BF16)
    k_rope = z[:, Q_LORA + KV_LORA:]
    q_t = _dot_nt(wqbt_ref[...], cqn)
    k_nope = _dot(ckvn, wkvk_ref[...])
    vt_ref[0] = _dot_nt(wkvvt_ref[...], ckvn).astype(BF16)

    ang_t = invfc_ref[...] * posr_ref[0]
    cos_f, sin_f = jnp.cos(ang_t), jnp.sin(ang_t)
    gq = gq_ref[...]
    half = C_ROPE // 2
    for hd in range(C_HEADS):
        blk = q_t[hd * C_QK:(hd + 1) * C_QK, :]
        ss = jnp.sum(blk * blk, axis=0, keepdims=True)
        r = lax.rsqrt(ss * (1.0 / C_QK) + EPS) * (ATTN_SCALE * LOG2E)
        sc = blk * r * gq
        a1 = sc[C_NOPE:C_NOPE + half, :]
        a2 = sc[C_NOPE + half:, :]
        qt_ref[0, hd, 0:C_NOPE, :] = sc[:C_NOPE, :].astype(BF16)
        qt_ref[0, hd, C_NOPE:C_NOPE + half, :] = (a1 * cos_f - a2 * sin_f).astype(BF16)
        qt_ref[0, hd, C_NOPE + half:C_QK, :] = (a2 * cos_f + a1 * sin_f).astype(BF16)

    kr_t = _dot_nt(wropet_ref[...], h) * gkr_ref[...]
    b1, b2 = kr_t[:half, :], kr_t[half:, :]
    rot_t = jnp.concatenate([b1 * cos_f - b2 * sin_f, b2 * cos_f + b1 * sin_f,
                             jnp.zeros((LANES - C_ROPE, kr_t.shape[1]), F32)], axis=0)
    kr_rot = rot_t.T

    lane = lax.broadcasted_iota(jnp.int32, (1, LANES), 1)
    kn_nope = kn_ref[...]
    k2_hi, k2_lo = _split2(jnp.concatenate([k_nope * k_nope, k_rope * k_rope], axis=1))
    ssk = _dot(k2_hi, indk_ref[...]) + _dot(k2_lo, indk_ref[...])
    rk = jnp.where(lane < C_HEADS, lax.rsqrt(ssk * (1.0 / C_QK) + EPS), 0.0)
    rk_hi, rk_lo = _split2(rk)
    rk_b = _dot(rk_hi, splat_ref[...]) + _dot(rk_lo, splat_ref[...])
    for hd in range(C_HEADS):
        cols = slice(hd * C_NOPE, (hd + 1) * C_NOPE)
        k_ref[0, hd, :, 0:C_NOPE] = (k_nope[:, cols] * rk_b[:, cols] * kn_nope).astype(BF16)
        k_ref[0, hd, :, C_NOPE:C_QK] = (kr_rot * rk_b[:, cols])[:, :C_ROPE].astype(BF16)


def _stage1_odd(x3d, posr, op, g0, ffn, ffn_layer, g1):
    bsz, seq, _ = x3d.shape
    tm = TOKEN_TILE
    tile = lambda b, i: (b, i, 0)
    return pl.pallas_call(
        _s1_odd_kernel,
        grid=(bsz, seq // tm),
        in_specs=[pl.BlockSpec((1, tm, D_MODEL), tile), pl.BlockSpec((1, 1, tm), lambda b, i: (b, 0, i)),
                  _const_spec((C_ROPE // 2, tm)), _const_spec((1, D_MODEL))]
                 + _ffn_specs(ffn_layer) + [
                  _const_spec((1, D_MODEL)), _const_spec((D_MODEL, ODD_IN_PAD)), _const_spec((C_ROPE, D_MODEL)),
                  _const_spec((1, Q_LORA)), _const_spec((1, KV_LORA)),
                  _const_spec((C_HEADS * C_QK, Q_LORA)), _const_spec((KV_LORA, C_HEADS * C_NOPE)),
                  _const_spec((C_HEADS * C_V, KV_LORA)), _const_spec((C_QK, tm)), _const_spec((C_ROPE, tm)),
                  _const_spec((1, C_NOPE)), _const_spec((C_HEADS * C_NOPE + LANES, LANES)),
                  _const_spec((LANES, C_HEADS * C_NOPE))],
        out_specs=[pl.BlockSpec((1, tm, D_MODEL), tile),
                   pl.BlockSpec((1, C_HEADS, C_QK, tm), lambda b, i: (b, 0, 0, i)),
                   pl.BlockSpec((1, C_HEADS, tm, C_QK), lambda b, i: (b, 0, i, 0)),
                   pl.BlockSpec((1, C_HEADS * C_V, tm), lambda b, i: (b, 0, i))],
        out_shape=[jax.ShapeDtypeStruct((bsz, seq, D_MODEL), F32),
                   jax.ShapeDtypeStruct((bsz, C_HEADS, C_QK, seq), BF16),
                   jax.ShapeDtypeStruct((bsz, C_HEADS, seq, C_QK), BF16),
                   jax.ShapeDtypeStruct((bsz, C_HEADS * C_V, seq), BF16)],
        compiler_params=pltpu.CompilerParams(dimension_semantics=("parallel", "parallel"),
                                             vmem_limit_bytes=VMEM_LIMIT_BYTES),
        name="stage1_odd",
    )(x3d, posr, op["invfc"], g0, *ffn, g1, op["win"], op["wropet"], op["qan"], op["kvan"],
      op["wqbt"], op["wkvk"], op["wkvvt"], op["gq"], op["gkr"], op["kn"], op["indk"], op["splat"])


def _s2_kernel(x_ref, m_ref, p_ref, wo_ref, g2_ref, wg_ref, wu_ref, wd_ref,
               g3_ref, wpg_ref, wpp_ref, g4_ref, o_ref):
    x = x_ref[...] + _dot(m_ref[...], wo_ref[...])
    hn = _rms(x, g2_ref[...]).astype(BF16)
    x = x + 0.5 * _swiglu(hn, wg_ref, wu_ref, wd_ref)
    h3 = _rms(x, g3_ref[...]).astype(BF16)
    gate = _sigmoid(_dot(h3, wpg_ref[...]))
    proj = _dot(p_ref[...].astype(BF16), wpp_ref[...])
    o_ref[...] = x + _rms(gate * proj, g4_ref[...])


def _stage2(x2d, mixed, p3d, layer, wo, j, g2, ffn, ffn_layer, g3, wpg, wpp, g4):
    n = x2d.shape[0]
    tm = TOKEN_TILE
    row = lambda i: (i, 0)
    return pl.pallas_call(
        _s2_kernel,
        grid=(n // tm,),
        in_specs=[pl.BlockSpec((tm, D_MODEL), row), pl.BlockSpec((tm, D_MODEL), row),
                  pl.BlockSpec((None, tm, D_PLE), lambda i: (layer, i, 0)),
                  _const_spec((D_MODEL, D_MODEL), (j,)), _const_spec((1, D_MODEL))] + _ffn_specs(ffn_layer) + [
                  _const_spec((1, D_MODEL)), _const_spec((D_MODEL, D_MODEL), (layer,)),
                  _const_spec((D_PLE, D_MODEL), (layer,)), _const_spec((1, D_MODEL))],
        out_specs=pl.BlockSpec((tm, D_MODEL), row),
        out_shape=jax.ShapeDtypeStruct((n, D_MODEL), F32),
        compiler_params=pltpu.CompilerParams(dimension_semantics=("parallel",),
                                             vmem_limit_bytes=VMEM_LIMIT_BYTES),
        name="stage2",
    )(x2d, mixed, p3d, wo, g2, *ffn, g3, wpg, wpp, g4)


def _level_map(ts):
    t = np.arange(ts)[:, None]
    s = np.arange(ts)[None, :]
    lvl = np.full((ts, ts), len(HGRN_LEVELS) + 1, np.int32)
    lvl[(t // SUBLANES == s // SUBLANES) & (s <= t)] = len(HGRN_LEVELS)
    for li, h in enumerate(HGRN_LEVELS):
        own = (t // (2 * h) == s // (2 * h)) & (t % (2 * h) >= h) & (s % (2 * h) < h)
        lvl[own] = li
    return lvl


def _even_core_kernel(u_ref, v_ref, q_ref, f_ref, i_ref, g_ref,
                      vnorm_ref, ind_ref, wcat_ref, bfull_ref, lb_ref, onorm_ref, ltri_ref, lvl_ref,
                      o_ref, state_ref):
    nb = HGRN_BATCH
    ts = HGRN_TILE
    sub = HGRN_SUB
    n_lvl = len(HGRN_LEVELS)
    rows_all = nb * ts

    def stacked(ref):
        return ref[...].reshape(rows_all, ref.shape[-1])

    u = _gelu_tanh(stacked(u_ref).astype(F32))
    v = _gelu_tanh(stacked(v_ref).astype(F32))
    ss = _dot((v * v).astype(BF16), ind_ref[...])
    vn = (v * lax.rsqrt(ss * (1.0 / A_HEAD_DIM) + EPS) * vnorm_ref[...]).astype(BF16)
    lane_head = lax.broadcasted_iota(jnp.int32, (1, A_WIDTH), 1) // A_HEAD_DIM
    for c in range(rows_all // A_CHUNK):
        rows = slice(c * A_CHUNK, (c + 1) * A_CHUNK)
        vc = vn[rows, :]
        zero = jnp.zeros_like(vc)
        vbd = jnp.concatenate([jnp.where(lane_head == hd, vc, zero) for hd in range(A_HEADS)], axis=0)
        mixed = _dot(wcat_ref[...], vbd) + bfull_ref[...]
        bi, local = divmod(c * A_CHUNK, ts)
        o_ref[bi, local:local + A_CHUNK, 0:A_WIDTH] = (u[rows, :] * mixed).astype(BF16)

    @pl.when(pl.program_id(1) == 0)
    def _():
        state_ref[...] = jnp.zeros_like(state_ref)

    lb = lb_ref[...]
    fg = lb + (1.0 - lb) * _sigmoid(stacked(f_ref))
    lf = jnp.log2(jnp.maximum(fg, F_MIN))
    kk = 1.0 - fg
    qq = stacked(q_ref).astype(F32)
    lf_hi, lf_lo = _split2(lf)
    ltri = ltri_ref[...]
    seqs = [slice(bi * ts, (bi + 1) * ts) for bi in range(nb)]
    b = jnp.concatenate([_dot(ltri, lf_hi[r, :]) + _dot(ltri, lf_lo[r, :]) for r in seqs], axis=0)

    q_lvls, k_lvls = [], []
    for h in HGRN_LEVELS:
        q_parts, k_parts = [], []
        zero = jnp.zeros((h, B_WIDTH), F32)
        for j in range(rows_all // (2 * h)):
            r0 = j * 2 * h
            bm = b[r0 + h - 1:r0 + h, :]
            lo_rows = slice(r0, r0 + h)
            up_rows = slice(r0 + h, r0 + 2 * h)
            k_parts += [kk[lo_rows, :] * jnp.exp2(bm - b[lo_rows, :]), zero]
            q_parts += [zero, qq[up_rows, :] * jnp.exp2(b[up_rows, :] - bm)]
        q_lvls.append(jnp.concatenate(q_parts, axis=0).astype(BF16))
        k_lvls.append(jnp.concatenate(k_parts, axis=0).astype(BF16))
    b3 = b.reshape(rows_all // SUBLANES, SUBLANES, B_WIDTH)
    bref = jnp.broadcast_to(b3[:, 3:4, :], b3.shape).reshape(rows_all, B_WIDTH)
    q_lvls.append((qq * jnp.exp2(b - bref)).astype(BF16))
    k_lvls.append((kk * jnp.exp2(bref - b)).astype(BF16))

    q_top, k_top, k_out, carry = [], [], [], []
    for bi in range(nb):
        r0 = bi * ts
        bm = b[r0 + sub - 1:r0 + sub, :]
        q_top.append((qq[r0 + sub:r0 + ts, :] * jnp.exp2(b[r0 + sub:r0 + ts, :] - bm)).astype(BF16))
        k_top.append((kk[r0:r0 + sub, :] * jnp.exp2(bm - b[r0:r0 + sub, :])).astype(BF16))
        b_last = b[r0 + ts - 1:r0 + ts, :]
        k_out.append((kk[seqs[bi], :] * jnp.exp2(b_last - b[seqs[bi], :])).astype(BF16))
        carry.append(jnp.exp2(b_last))
    q_in = (qq * jnp.exp2(b)).astype(BF16)
    vv = stacked(i_ref)
    gate = _silu(stacked(g_ref).astype(F32))
    onorm = onorm_ref[...]

    lvl = lvl_ref[...]
    masks = [lvl == li for li in range(n_lvl + 1)]
    units = [(bi, hd, slice(hd * B_DIM, (hd + 1) * B_DIM)) for hd in range(B_HEADS) for bi in range(nb)]
    states = {(bi, hd): state_ref[bi, hd] for bi, hd, _ in units}
    inter = {(bi, hd): _dot_nt(q_in[seqs[bi], cols], states[bi, hd].astype(BF16)) for bi, hd, cols in units}
    for bi, hd, cols in units:
        state_ref[bi, hd] = (states[bi, hd] * carry[bi][:, cols]
                             + _dot_tn(vv[seqs[bi], cols], k_out[bi][:, cols]))
    score_blocks = {}
    for bi, hd, cols in units:
        diag = []
        for a in range(ts // sub):
            rows = slice(bi * ts + a * sub, bi * ts + (a + 1) * sub)
            blk = jnp.where(masks[n_lvl], _dot_nt(q_lvls[n_lvl][rows, cols], k_lvls[n_lvl][rows, cols]), 0.0)
            for li in range(n_lvl):
                blk = jnp.where(masks[li], _dot_nt(q_lvls[li][rows, cols], k_lvls[li][rows, cols]), blk)
            diag.append(blk.astype(BF16))
        top = _dot_nt(q_top[bi][:, cols], k_top[bi][:, cols]).astype(BF16)
        score_blocks[bi, hd] = (diag[0], jnp.concatenate([top, diag[1]], axis=1))
    outs = {}
    for bi, hd, cols in units:
        v_h = vv[seqs[bi], cols]
        first, second = score_blocks[bi, hd]
        o_intra = jnp.concatenate([_dot(first, v_h[:sub, :]), _dot(second, v_h)], axis=0)
        outs[bi, hd] = o_intra + inter[bi, hd]
    for bi, hd, cols in units:
        on = _rms(outs[bi, hd], onorm[:, cols]) * gate[seqs[bi], cols]
        o_ref[bi, :, A_WIDTH + hd * B_DIM:A_WIDTH + (hd + 1) * B_DIM] = on.astype(BF16)


def _even_core(parts, vnorm, ind, wcat, bfull, lb, onorm, ltri, lvl):
    bsz, seq, _ = parts[0].shape
    ts = HGRN_TILE
    nb = HGRN_BATCH
    tile = lambda b, i: (b, i, 0)
    part_spec = pl.BlockSpec((nb, ts, A_WIDTH), tile)
    return pl.pallas_call(
        _even_core_kernel,
        grid=(bsz // nb, seq // ts),
        in_specs=[part_spec] * EVEN_PARTS + [
            _const_spec((1, A_WIDTH)), _const_spec((A_WIDTH, A_WIDTH)),
            _const_spec((A_CHUNK, A_HEADS * A_CHUNK)), _const_spec((A_CHUNK, A_WIDTH)),
            _const_spec((1, B_WIDTH)), _const_spec((1, B_WIDTH)),
            _const_spec((ts, ts)), _const_spec((HGRN_SUB, HGRN_SUB))],
        out_specs=pl.BlockSpec((nb, ts, D_MODEL), tile),
        out_shape=jax.ShapeDtypeStruct((bsz, seq, D_MODEL), BF16),
        scratch_shapes=[pltpu.VMEM((nb, B_HEADS, B_DIM, B_DIM), F32)],
        compiler_params=pltpu.CompilerParams(dimension_semantics=("parallel", "arbitrary"),
                                             vmem_limit_bytes=VMEM_LIMIT_BYTES),
        name="even_core",
    )(*parts, vnorm, ind, wcat, bfull, lb, onorm, ltri, lvl)


def _attn_kernel(qt_ref, k_ref, vt_ref, o_ref, m_ref, l_ref, acc_ref, sa_ref, sb_ref, mxa_ref, mxb_ref):
    tq = ATTN_TILE
    groups = tq // SUBLANES
    qi = pl.program_id(2)
    heads = range(ATTN_HEADS_PER_STEP)
    buf_a, buf_b = (sa_ref, mxa_ref), (sb_ref, mxb_ref)

    m_ref[...] = jnp.full_like(m_ref, NEG_BIG)
    l_ref[...] = jnp.zeros_like(l_ref)
    acc_ref[...] = jnp.zeros_like(acc_ref)

    def keys(j):
        return pl.ds(pl.multiple_of(j * tq, tq), tq)

    def scores(j, buf, heads=heads):
        s_ref, mx_ref = buf
        for hd in heads:
            s = _dot(k_ref[0, hd, keys(j), :], qt_ref[0, hd])
            s_ref[hd] = s
            mx_ref[hd] = jnp.max(s.reshape(groups, SUBLANES, tq), axis=0)

    def consume(j, buf, masked, heads=heads):
        s_ref, mx_ref = buf
        if masked:
            keep = (lax.broadcasted_iota(jnp.int32, (tq, tq), 0) <= lax.broadcasted_iota(jnp.int32, (tq, tq), 1))
        s3, m_new, alpha = {}, {}, {}
        for hd in heads:
            if masked:
                s3[hd] = jnp.where(keep, s_ref[hd], NEG_BIG).reshape(groups, SUBLANES, tq)
                cmax8 = jnp.max(s3[hd], axis=0)
            else:
                cmax8 = mx_ref[hd]
            cmax = jnp.max(cmax8, axis=0, keepdims=True)
            m_prev = m_ref[hd]
            m_new[hd] = jnp.maximum(m_prev, jnp.broadcast_to(cmax, (SUBLANES, tq)))
            alpha[hd] = jnp.exp2(m_prev - m_new[hd])
            m_ref[hd] = m_new[hd]
        ones = jnp.ones((2 * SUBLANES, tq), BF16)
        for hd in heads:
            sv = s3[hd] if masked else s_ref[hd].reshape(groups, SUBLANES, tq)
            p = jnp.exp2(sv - m_new[hd][None]).reshape(tq, tq).astype(BF16)
            lhs = jnp.concatenate([vt_ref[0, hd * C_V:(hd + 1) * C_V, keys(j)], ones], axis=0)
            pv = _dot(lhs, p)
            acc3 = acc_ref[hd].reshape(C_V // SUBLANES, SUBLANES, tq) * alpha[hd][None]
            acc_ref[hd] = acc3.reshape(C_V, tq) + pv[:C_V, :]
            l_ref[hd] = alpha[hd] * l_ref[hd] + pv[C_V:C_V + SUBLANES, :]

    scores(0, buf_a)
    pairs = qi // 2

    def body(t, carry):
        j = 2 * t
        for hd in heads:
            scores(j + 1, buf_b, [hd])
            consume(j, buf_a, False, [hd])
        for hd in heads:
            scores(j + 2, buf_a, [hd])
            consume(j + 1, buf_b, False, [hd])
        return carry

    lax.fori_loop(0, pairs, body, 0)

    @pl.when(qi % 2 == 0)
    def _():
        consume(qi, buf_a, True)

    @pl.when(qi % 2 == 1)
    def _():
        for hd in heads:
            scores(qi, buf_b, [hd])
            consume(qi - 1, buf_a, False, [hd])
        consume(qi, buf_b, True)
    for hd in range(ATTN_HEADS_PER_STEP):
        inv_l = 1.0 / l_ref[hd]
        out_t = acc_ref[hd].reshape(C_V // SUBLANES, SUBLANES, tq) * inv_l[None]
        o_ref[0, :, hd * C_V:(hd + 1) * C_V] = out_t.reshape(C_V, tq).T.astype(BF16)


def _attention(qt, k, vt):
    bsz, heads, seq, _ = k.shape
    tq = ATTN_TILE
    g = ATTN_HEADS_PER_STEP
    return pl.pallas_call(
        _attn_kernel,
        grid=(bsz, heads // g, seq // tq),
        in_specs=[pl.BlockSpec((1, g, C_QK, tq), lambda b, h, i: (b, h, 0, i)),
                  pl.BlockSpec((1, g, seq, C_QK), lambda b, h, i: (b, h, 0, 0)),
                  pl.BlockSpec((1, g * C_V, seq), lambda b, h, i: (b, h, 0))],
        out_specs=pl.BlockSpec((1, tq, g * C_V), lambda b, h, i: (b, i, h)),
        out_shape=jax.ShapeDtypeStruct((bsz, seq, heads * C_V), BF16),
        scratch_shapes=[pltpu.VMEM((g, SUBLANES, tq), F32), pltpu.VMEM((g, SUBLANES, tq), F32),
                        pltpu.VMEM((g, C_V, tq), F32), pltpu.VMEM((g, tq, tq), F32),
                        pltpu.VMEM((g, tq, tq), F32), pltpu.VMEM((g, SUBLANES, tq), F32),
                        pltpu.VMEM((g, SUBLANES, tq), F32)],
        compiler_params=pltpu.CompilerParams(dimension_semantics=("parallel", "parallel", "arbitrary"),
                                             vmem_limit_bytes=VMEM_LIMIT_BYTES),
        name="mla_attention",
    )(qt, k, vt)


def _row(v):
    return v.reshape(1, -1).astype(F32)


def _pad_cols(w, total):
    return jnp.pad(w, ((0, 0), (0, total - w.shape[1])))


def _even_params(v_norm, w_s, b_s, lb, out_norm):
    causal = np.tril(np.ones((A_CHUNK, A_CHUNK), bool))
    w = jnp.where(causal[None], w_s, 0.0)
    wcat = jnp.transpose(w, (1, 0, 2)).reshape(A_CHUNK, A_HEADS * A_CHUNK)
    bfull = jnp.repeat(b_s.T, A_HEAD_DIM, axis=1)
    head_of = np.arange(A_WIDTH) // A_HEAD_DIM
    ind = (head_of[:, None] == head_of[None, :]).astype(np.float32)
    ltri = np.tril(np.ones((HGRN_TILE, HGRN_TILE), np.float32))
    return dict(vnorm=_row(v_norm), ind=jnp.asarray(ind, BF16),
                wcat=wcat.astype(BF16), bfull=bfull.astype(F32), lb=_row(lb),
                onorm=_row(jnp.tile(out_norm, B_HEADS)), ltri=jnp.asarray(ltri, BF16),
                lvl=jnp.asarray(_level_map(HGRN_SUB)))


def _odd_params(w_in, q_a_norm, kv_a_norm, w_q_b, w_kv_b, q_norm, k_norm):
    tm = TOKEN_TILE
    win = _pad_cols(w_in, ODD_IN_PAD)
    wkv = w_kv_b.reshape(KV_LORA, C_HEADS, C_NOPE + C_V)
    wkvk = wkv[:, :, :C_NOPE].reshape(KV_LORA, C_HEADS * C_NOPE)
    wkvvt = wkv[:, :, C_NOPE:].reshape(KV_LORA, C_HEADS * C_V).T
    inv_freq = ROPE_THETA ** (-jnp.arange(0, C_ROPE, 2, dtype=F32) / C_ROPE)
    invfc = jnp.broadcast_to(inv_freq[:, None], (C_ROPE // 2, tm))
    head_of = np.arange(C_HEADS * C_NOPE) // C_NOPE
    splat = (head_of[:, None] == np.arange(LANES)[None, :]).astype(np.float32)
    rope_rows = np.broadcast_to((np.arange(LANES) < C_HEADS).astype(np.float32), (LANES, LANES))
    indk = np.concatenate([splat, rope_rows], axis=0)
    return dict(win=win.astype(BF16), wropet=w_in[:, Q_LORA + KV_LORA:].T.astype(BF16),
                qan=_row(q_a_norm), kvan=_row(kv_a_norm),
                wqbt=w_q_b.T.astype(BF16), wkvk=wkvk.astype(BF16), wkvvt=wkvvt.astype(BF16),
                gq=jnp.broadcast_to(q_norm.astype(F32)[:, None], (C_QK, tm)),
                gkr=jnp.broadcast_to(k_norm.astype(F32)[C_NOPE:, None], (C_ROPE, tm)),
                kn=_row(k_norm[:C_NOPE]), invfc=invfc,
                indk=jnp.asarray(indk, BF16), splat=jnp.asarray(splat.T, BF16))


def kernel(x, p, positions, norm_gains, ffn_w_gate, ffn_w_up, ffn_w_down, ple_w_gate, ple_w_proj,
           even_w_in, gmlp_v_norm, gmlp_w_s, gmlp_b_s, hgrn_lb_raw, hgrn_out_norm, even_w_out,
           mla_w_in, mla_q_a_norm, mla_kv_a_norm, mla_w_q_b, mla_w_kv_b, mla_q_norm, mla_k_norm, mla_w_out):
    bsz, seq, d = x.shape
    depth = norm_gains.shape[0]
    n = bsz * seq
    assert d == D_MODEL and seq % ATTN_TILE == 0 and seq % TOKEN_TILE == 0 and seq % HGRN_TILE == 0
    assert bsz % HGRN_BATCH == 0

    lb_sm = jax.nn.softmax(hgrn_lb_raw.astype(F32), axis=0)
    lower_bounds = jnp.clip(jnp.cumsum(lb_sm, axis=0) - lb_sm[0], 0.0, LB_MAX)
    posr = positions.astype(F32).reshape(bsz, 1, seq)

    ffn = tuple(w.astype(BF16) for w in (ffn_w_gate, ffn_w_up, ffn_w_down))
    ple_wg, ple_wp = ple_w_gate.astype(BF16), ple_w_proj.astype(BF16)
    even_win, even_wout, mla_wout = (w.astype(BF16) for w in (even_w_in, even_w_out, mla_w_out))
    p3d = p.reshape(depth, n, D_PLE)

    xc = x.reshape(n, d)
    for i in range(depth):
        g = norm_gains[i]
        j = i // 2
        if i % 2 == 0:
            ep = _even_params(gmlp_v_norm[j], gmlp_w_s[j], gmlp_b_s[j], lower_bounds[j], hgrn_out_norm[j])
            outs = _stage1_even(xc, _row(g[0]), ffn, (i, 0), _row(g[1]), even_win, j)
            xc = outs[0]
            parts = [o.reshape(bsz, seq, A_WIDTH) for o in outs[1:]]
            mixed = _even_core(parts, ep["vnorm"], ep["ind"], ep["wcat"], ep["bfull"], ep["lb"],
                               ep["onorm"], ep["ltri"], ep["lvl"])
            w_out = even_wout
        else:
            op = _odd_params(mla_w_in[j], mla_q_a_norm[j], mla_kv_a_norm[j], mla_w_q_b[j], mla_w_kv_b[j],
                             mla_q_norm[j], mla_k_norm[j])
            x1, qt, k, vt = _stage1_odd(xc.reshape(bsz, seq, d), posr, op, _row(g[0]), ffn, (i, 0),
                                        _row(g[1]))
            xc = x1.reshape(n, d)
            mixed = _attention(qt, k, vt)
            w_out = mla_wout
        xc = _stage2(xc, mixed.reshape(n, d), p3d, i, w_out, j, _row(g[2]), ffn, (i, 1), _row(g[3]),
                     ple_wg, ple_wp, _row(g[4]))
    return xc.reshape(bsz, seq, d)
```

```python
import math

import numpy as np
import jax
import jax.numpy as jnp
from jax import lax
from jax.experimental import pallas as pl
from jax.experimental.pallas import tpu as pltpu

F32 = jnp.float32
BF16 = jnp.bfloat16

D_MODEL = 1024
D_FF = 2816
D_PLE = 256
EPS = 1e-6
NEG_BIG = -1e30
F_MIN = 1e-6
LB_MAX = 0.999

A_HEADS = 8
A_HEAD_DIM = 64
A_WIDTH = A_HEADS * A_HEAD_DIM
A_CHUNK = 128
B_HEADS = 4
B_DIM = 128
B_WIDTH = B_HEADS * B_DIM
EVEN_PARTS = 6

C_HEADS = 8
C_NOPE = 128
C_ROPE = 64
C_V = 128
C_QK = C_NOPE + C_ROPE
Q_LORA = 384
KV_LORA = 256
ROPE_THETA = 10000.0
ATTN_SCALE = C_QK ** -0.5
LOG2E = math.log2(math.e)

LANES = 128
SUBLANES = 8
ODD_IN_PAD = Q_LORA + KV_LORA + LANES
VMEM_LIMIT_BYTES = 60000 * 1024

TOKEN_TILE = 512
HGRN_TILE = 256
HGRN_BATCH = 2
HGRN_SUB = HGRN_TILE // 2
HGRN_LEVELS = (64, 32, 16, 8)
ATTN_TILE = 512
ATTN_HEADS_PER_STEP = 4


def _sigmoid(x):
    return 0.5 + 0.5 * jnp.tanh(0.5 * x)


def _silu(x):
    t = 0.5 * x
    return t + t * jnp.tanh(t)


def _gelu_tanh(x):
    c = math.sqrt(2.0 / math.pi)
    hx = 0.5 * x
    return hx + hx * jnp.tanh(x * (c + (c * 0.044715) * (x * x)))


def _rms(x, g):
    ms = jnp.mean(x * x, axis=-1, keepdims=True)
    return x * lax.rsqrt(ms + EPS) * g


def _dot(a, b):
    return jnp.dot(a, b, preferred_element_type=F32)


def _dot_nt(a, b):
    return lax.dot_general(a, b, (((1,), (1,)), ((), ())), preferred_element_type=F32)


def _dot_tn(a, b):
    return lax.dot_general(a, b, (((0,), (0,)), ((), ())), preferred_element_type=F32)


def _split2(x):
    hi = x.astype(BF16)
    lo = (x - hi.astype(F32)).astype(BF16)
    return hi, lo


def _swiglu(hn, wg_ref, wu_ref, wd_ref):
    act = (_silu(_dot(hn, wg_ref[...])) * _dot(hn, wu_ref[...])).astype(BF16)
    return _dot(act, wd_ref[...])


def _s1_even_kernel(x_ref, g0_ref, wg_ref, wu_ref, wd_ref, g1_ref, win_ref,
                    x1_ref, u_ref, v_ref, q_ref, f_ref, i_ref, gg_ref):
    x = x_ref[...]
    hn = _rms(x, g0_ref[...]).astype(BF16)
    x1 = x + 0.5 * _swiglu(hn, wg_ref, wu_ref, wd_ref)
    x1_ref[...] = x1
    h = _rms(x1, g1_ref[...]).astype(BF16)
    outs = (u_ref, v_ref, q_ref, f_ref, i_ref, gg_ref)
    for k, o_ref in enumerate(outs):
        z = _dot(h, win_ref[:, k * A_WIDTH:(k + 1) * A_WIDTH])
        o_ref[...] = z.astype(o_ref.dtype)


def _const_spec(shape, prefix=()):
    nd = len(shape)
    return pl.BlockSpec((None,) * len(prefix) + tuple(shape), lambda *_: tuple(prefix) + (0,) * nd,
                        pipeline_mode=pl.Buffered(1))


def _ffn_specs(layer):
    return [_const_spec((D_MODEL, D_FF), layer), _const_spec((D_MODEL, D_FF), layer),
            _const_spec((D_FF, D_MODEL), layer)]


def _stage1_even(x2d, g0, ffn, ffn_layer, g1, win, j):
    n = x2d.shape[0]
    tm = TOKEN_TILE
    row = lambda i: (i, 0)
    part_spec = pl.BlockSpec((tm, A_WIDTH), row)
    out_shape = [jax.ShapeDtypeStruct((n, D_MODEL), F32)]
    out_shape += [jax.ShapeDtypeStruct((n, A_WIDTH), F32 if k == 3 else BF16) for k in range(EVEN_PARTS)]
    return pl.pallas_call(
        _s1_even_kernel,
        grid=(n // tm,),
        in_specs=[pl.BlockSpec((tm, D_MODEL), row), _const_spec((1, D_MODEL))] + _ffn_specs(ffn_layer) + [
                  _const_spec((1, D_MODEL)), _const_spec((D_MODEL, EVEN_PARTS * A_WIDTH), (j,))],
        out_specs=[pl.BlockSpec((tm, D_MODEL), row)] + [part_spec] * EVEN_PARTS,
        out_shape=out_shape,
        compiler_params=pltpu.CompilerParams(dimension_semantics=("parallel",),
                                             vmem_limit_bytes=VMEM_LIMIT_BYTES),
        name="stage1_even",
    )(x2d, g0, *ffn, g1, win)


def _s1_odd_kernel(x_ref, posr_ref, invfc_ref, g0_ref, wg_ref, wu_ref, wd_ref, g1_ref,
                   win_ref, wropet_ref, qan_ref, kvan_ref, wqbt_ref, wkvk_ref, wkvvt_ref, gq_ref, gkr_ref, kn_ref,
                   indk_ref, splat_ref,
                   x1_ref, qt_ref, k_ref, vt_ref):
    x = x_ref[0]
    hn = _rms(x, g0_ref[...]).astype(BF16)
    x1 = x + 0.5 * _swiglu(hn, wg_ref, wu_ref, wd_ref)
    x1_ref[0] = x1
    h = _rms(x1, g1_ref[...]).astype(BF16)
    z = _dot(h, win_ref[...])
    cqn = _rms(z[:, :Q_LORA], qan_ref[...]).astype(BF16)
    ckvn = _rms(z[:, Q_LORA:Q_LORA + KV_LORA], kvan_ref[...]).astype(BF16)
    k_rope = z[:, Q_LORA + KV_LORA:]
    q_t = _dot_nt(wqbt_ref[...], cqn)
    k_nope = _dot(ckvn, wkvk_ref[...])
    vt_ref[0] = _dot_nt(wkvvt_ref[...], ckvn).astype(BF16)

    ang_t = invfc_ref[...] * posr_ref[0]
    cos_f, sin_f = jnp.cos(ang_t), jnp.sin(ang_t)
    gq = gq_ref[...]
    half = C_ROPE // 2
    for hd in range(C_HEADS):
        blk = q_t[hd * C_QK:(hd + 1) * C_QK, :]
        ss = jnp.sum(blk * blk, axis=0, keepdims=True)
        r = lax.rsqrt(ss * (1.0 / C_QK) + EPS) * (ATTN_SCALE * LOG2E)
        sc = blk * r * gq
        a1 = sc[C_NOPE:C_NOPE + half, :]
        a2 = sc[C_NOPE + half:, :]
        qt_ref[0, hd, 0:C_NOPE, :] = sc[:C_NOPE, :].astype(BF16)
        qt_ref[0, hd, C_NOPE:C_NOPE + half, :] = (a1 * cos_f - a2 * sin_f).astype(BF16)
        qt_ref[0, hd, C_NOPE + half:C_QK, :] = (a2 * cos_f + a1 * sin_f).astype(BF16)

    kr_t = _dot_nt(wropet_ref[...], h) * gkr_ref[...]
    b1, b2 = kr_t[:half, :], kr_t[half:, :]
    rot_t = jnp.concatenate([b1 * cos_f - b2 * sin_f, b2 * cos_f + b1 * sin_f,
                             jnp.zeros((LANES - C_ROPE, kr_t.shape[1]), F32)], axis=0)
    kr_rot = rot_t.T

    lane = lax.broadcasted_iota(jnp.int32, (1, LANES), 1)
    kn_nope = kn_ref[...]
    k2_hi, k2_lo = _split2(jnp.concatenate([k_nope * k_nope, k_rope * k_rope], axis=1))
    ssk = _dot(k2_hi, indk_ref[...]) + _dot(k2_lo, indk_ref[...])
    rk = jnp.where(lane < C_HEADS, lax.rsqrt(ssk * (1.0 / C_QK) + EPS), 0.0)
    rk_hi, rk_lo = _split2(rk)
    rk_b = _dot(rk_hi, splat_ref[...]) + _dot(rk_lo, splat_ref[...])
    for hd in range(C_HEADS):
        cols = slice(hd * C_NOPE, (hd + 1) * C_NOPE)
        k_ref[0, hd, :, 0:C_NOPE] = (k_nope[:, cols] * rk_b[:, cols] * kn_nope).astype(BF16)
        k_ref[0, hd, :, C_NOPE:C_QK] = (kr_rot * rk_b[:, cols])[:, :C_ROPE].astype(BF16)


def _stage1_odd(x3d, posr, op, g0, ffn, ffn_layer, g1):
    bsz, seq, _ = x3d.shape
    tm = TOKEN_TILE
    tile = lambda b, i: (b, i, 0)
    return pl.pallas_call(
        _s1_odd_kernel,
        grid=(bsz, seq // tm),
        in_specs=[pl.BlockSpec((1, tm, D_MODEL), tile), pl.BlockSpec((1, 1, tm), lambda b, i: (b, 0, i)),
                  _const_spec((C_ROPE // 2, tm)), _const_spec((1, D_MODEL))]
                 + _ffn_specs(ffn_layer) + [
                  _const_spec((1, D_MODEL)), _const_spec((D_MODEL, ODD_IN_PAD)), _const_spec((C_ROPE, D_MODEL)),
                  _const_spec((1, Q_LORA)), _const_spec((1, KV_LORA)),
                  _const_spec((C_HEADS * C_QK, Q_LORA)), _const_spec((KV_LORA, C_HEADS * C_NOPE)),
                  _const_spec((C_HEADS * C_V, KV_LORA)), _const_spec((C_QK, tm)), _const_spec((C_ROPE, tm)),
                  _const_spec((1, C_NOPE)), _const_spec((C_HEADS * C_NOPE + LANES, LANES)),
                  _const_spec((LANES, C_HEADS * C_NOPE))],
        out_specs=[pl.BlockSpec((1, tm, D_MODEL), tile),
                   pl.BlockSpec((1, C_HEADS, C_QK, tm), lambda b, i: (b, 0, 0, i)),
                   pl.BlockSpec((1, C_HEADS, tm, C_QK), lambda b, i: (b, 0, i, 0)),
                   pl.BlockSpec((1, C_HEADS * C_V, tm), lambda b, i: (b, 0, i))],
        out_shape=[jax.ShapeDtypeStruct((bsz, seq, D_MODEL), F32),
                   jax.ShapeDtypeStruct((bsz, C_HEADS, C_QK, seq), BF16),
                   jax.ShapeDtypeStruct((bsz, C_HEADS, seq, C_QK), BF16),
                   jax.ShapeDtypeStruct((bsz, C_HEADS * C_V, seq), BF16)],
        compiler_params=pltpu.CompilerParams(dimension_semantics=("parallel", "parallel"),
                                             vmem_limit_bytes=VMEM_LIMIT_BYTES),
        name="stage1_odd",
    )(x3d, posr, op["invfc"], g0, *ffn, g1, op["win"], op["wropet"], op["qan"], op["kvan"],
      op["wqbt"], op["wkvk"], op["wkvvt"], op["gq"], op["gkr"], op["kn"], op["indk"], op["splat"])


def _s2_kernel(x_ref, m_ref, p_ref, wo_ref, g2_ref, wg_ref, wu_ref, wd_ref,
               g3_ref, wpg_ref, wpp_ref, g4_ref, o_ref):
    x = x_ref[...] + _dot(m_ref[...], wo_ref[...])
    hn = _rms(x, g2_ref[...]).astype(BF16)
    x = x + 0.5 * _swiglu(hn, wg_ref, wu_ref, wd_ref)
    h3 = _rms(x, g3_ref[...]).astype(BF16)
    gate = _sigmoid(_dot(h3, wpg_ref[...]))
    proj = _dot(p_ref[...].astype(BF16), wpp_ref[...])
    o_ref[...] = x + _rms(gate * proj, g4_ref[...])


def _stage2(x2d, mixed, p3d, layer, wo, j, g2, ffn, ffn_layer, g3, wpg, wpp, g4):
    n = x2d.shape[0]
    tm = TOKEN_TILE
    row = lambda i: (i, 0)
    return pl.pallas_call(
        _s2_kernel,
        grid=(n // tm,),
        in_specs=[pl.BlockSpec((tm, D_MODEL), row), pl.BlockSpec((tm, D_MODEL), row),
                  pl.BlockSpec((None, tm, D_PLE), lambda i: (layer, i, 0)),
                  _const_spec((D_MODEL, D_MODEL), (j,)), _const_spec((1, D_MODEL))] + _ffn_specs(ffn_layer) + [
                  _const_spec((1, D_MODEL)), _const_spec((D_MODEL, D_MODEL), (layer,)),
                  _const_spec((D_PLE, D_MODEL), (layer,)), _const_spec((1, D_MODEL))],
        out_specs=pl.BlockSpec((tm, D_MODEL), row),
        out_shape=jax.ShapeDtypeStruct((n, D_MODEL), F32),
        compiler_params=pltpu.CompilerParams(dimension_semantics=("parallel",),
                                             vmem_limit_bytes=VMEM_LIMIT_BYTES),
        name="stage2",
    )(x2d, mixed, p3d, wo, g2, *ffn, g3, wpg, wpp, g4)


def _level_map(ts):
    t = np.arange(ts)[:, None]
    s = np.arange(ts)[None, :]
    lvl = np.full((ts, ts), len(HGRN_LEVELS) + 1, np.int32)
    lvl[(t // SUBLANES == s // SUBLANES) & (s <= t)] = len(HGRN_LEVELS)
    for li, h in enumerate(HGRN_LEVELS):
        own = (t // (2 * h) == s // (2 * h)) & (t % (2 * h) >= h) & (s % (2 * h) < h)
        lvl[own] = li
    return lvl


def _even_core_kernel(u_ref, v_ref, q_ref, f_ref, i_ref, g_ref,
                      vnorm_ref, ind_ref, wcat_ref, bfull_ref, lb_ref, onorm_ref, ltri_ref, lvl_ref,
                      o_ref, state_ref):
    nb = HGRN_BATCH
    ts = HGRN_TILE
    sub = HGRN_SUB
    n_lvl = len(HGRN_LEVELS)
    rows_all = nb * ts

    def stacked(ref):
        return ref[...].reshape(rows_all, ref.shape[-1])

    u = _gelu_tanh(stacked(u_ref).astype(F32))
    v = _gelu_tanh(stacked(v_ref).astype(F32))
    ss = _dot((v * v).astype(BF16), ind_ref[...])
    vn = (v * lax.rsqrt(ss * (1.0 / A_HEAD_DIM) + EPS) * vnorm_ref[...]).astype(BF16)
    lane_head = lax.broadcasted_iota(jnp.int32, (1, A_WIDTH), 1) // A_HEAD_DIM
    for c in range(rows_all // A_CHUNK):
        rows = slice(c * A_CHUNK, (c + 1) * A_CHUNK)
        vc = vn[rows, :]
        zero = jnp.zeros_like(vc)
        vbd = jnp.concatenate([jnp.where(lane_head == hd, vc, zero) for hd in range(A_HEADS)], axis=0)
        mixed = _dot(wcat_ref[...], vbd) + bfull_ref[...]
        bi, local = divmod(c * A_CHUNK, ts)
        o_ref[bi, local:local + A_CHUNK, 0:A_WIDTH] = (u[rows, :] * mixed).astype(BF16)

    @pl.when(pl.program_id(1) == 0)
    def _():
        state_ref[...] = jnp.zeros_like(state_ref)

    lb = lb_ref[...]
    fg = lb + (1.0 - lb) * _sigmoid(stacked(f_ref))
    lf = jnp.log2(jnp.maximum(fg, F_MIN))
    kk = 1.0 - fg
    qq = stacked(q_ref).astype(F32)
    lf_hi, lf_lo = _split2(lf)
    ltri = ltri_ref[...]
    seqs = [slice(bi * ts, (bi + 1) * ts) for bi in range(nb)]
    b = jnp.concatenate([_dot(ltri, lf_hi[r, :]) + _dot(ltri, lf_lo[r, :]) for r in seqs], axis=0)

    q_lvls, k_lvls = [], []
    for h in HGRN_LEVELS:
        q_parts, k_parts = [], []
        zero = jnp.zeros((h, B_WIDTH), F32)
        for j in range(rows_all // (2 * h)):
            r0 = j * 2 * h
            bm = b[r0 + h - 1:r0 + h, :]
            lo_rows = slice(r0, r0 + h)
            up_rows = slice(r0 + h, r0 + 2 * h)
            k_parts += [kk[lo_rows, :] * jnp.exp2(bm - b[lo_rows, :]), zero]
            q_parts += [zero, qq[up_rows, :] * jnp.exp2(b[up_rows, :] - bm)]
        q_lvls.append(jnp.concatenate(q_parts, axis=0).astype(BF16))
        k_lvls.append(jnp.concatenate(k_parts, axis=0).astype(BF16))
    b3 = b.reshape(rows_all // SUBLANES, SUBLANES, B_WIDTH)
    bref = jnp.broadcast_to(b3[:, 3:4, :], b3.shape).reshape(rows_all, B_WIDTH)
    q_lvls.append((qq * jnp.exp2(b - bref)).astype(BF16))
    k_lvls.append((kk * jnp.exp2(bref - b)).astype(BF16))

    q_top, k_top, k_out, carry = [], [], [], []
    for bi in range(nb):
        r0 = bi * ts
        bm = b[r0 + sub - 1:r0 + sub, :]
        q_top.append((qq[r0 + sub:r0 + ts, :] * jnp.exp2(b[r0 + sub:r0 + ts, :] - bm)).astype(BF16))
        k_top.append((kk[r0:r0 + sub, :] * jnp.exp2(bm - b[r0:r0 + sub, :])).astype(BF16))
        b_last = b[r0 + ts - 1:r0 + ts, :]
        k_out.append((kk[seqs[bi], :] * jnp.exp2(b_last - b[seqs[bi], :])).astype(BF16))
        carry.append(jnp.exp2(b_last))
    q_in = (qq * jnp.exp2(b)).astype(BF16)
    vv = stacked(i_ref)
    gate = _silu(stacked(g_ref).astype(F32))
    onorm = onorm_ref[...]

    lvl = lvl_ref[...]
    masks = [lvl == li for li in range(n_lvl + 1)]
    units = [(bi, hd, slice(hd * B_DIM, (hd + 1) * B_DIM)) for hd in range(B_HEADS) for bi in range(nb)]
    states = {(bi, hd): state_ref[bi, hd] for bi, hd, _ in units}
    inter = {(bi, hd): _dot_nt(q_in[seqs[bi], cols], states[bi, hd].astype(BF16)) for bi, hd, cols in units}
    for bi, hd, cols in units:
        state_ref[bi, hd] = (states[bi, hd] * carry[bi][:, cols]
                             + _dot_tn(vv[seqs[bi], cols], k_out[bi][:, cols]))
    score_blocks = {}
    for bi, hd, cols in units:
        diag = []
        for a in range(ts // sub):
            rows = slice(bi * ts + a * sub, bi * ts + (a + 1) * sub)
            blk = jnp.where(masks[n_lvl], _dot_nt(q_lvls[n_lvl][rows, cols], k_lvls[n_lvl][rows, cols]), 0.0)
            for li in range(n_lvl):
                blk = jnp.where(masks[li], _dot_nt(q_lvls[li][rows, cols], k_lvls[li][rows, cols]), blk)
            diag.append(blk.astype(BF16))
        top = _dot_nt(q_top[bi][:, cols], k_top[bi][:, cols]).astype(BF16)
        score_blocks[bi, hd] = (diag[0], jnp.concatenate([top, diag[1]], axis=1))
    outs = {}
    for bi, hd, cols in units:
        v_h = vv[seqs[bi], cols]
        first, second = score_blocks[bi, hd]
        o_intra = jnp.concatenate([_dot(first, v_h[:sub, :]), _dot(second, v_h)], axis=0)
        outs[bi, hd] = o_intra + inter[bi, hd]
    for bi, hd, cols in units:
        on = _rms(outs[bi, hd], onorm[:, cols]) * gate[seqs[bi], cols]
        o_ref[bi, :, A_WIDTH + hd * B_DIM:A_WIDTH + (hd + 1) * B_DIM] = on.astype(BF16)


def _even_core(parts, vnorm, ind, wcat, bfull, lb, onorm, ltri, lvl):
    bsz, seq, _ = parts[0].shape
    ts = HGRN_TILE
    nb = HGRN_BATCH
    tile = lambda b, i: (b, i, 0)
    part_spec = pl.BlockSpec((nb, ts, A_WIDTH), tile)
    return pl.pallas_call(
        _even_core_kernel,
        grid=(bsz // nb, seq // ts),
        in_specs=[part_spec] * EVEN_PARTS + [
            _const_spec((1, A_WIDTH)), _const_spec((A_WIDTH, A_WIDTH)),
            _const_spec((A_CHUNK, A_HEADS * A_CHUNK)), _const_spec((A_CHUNK, A_WIDTH)),
            _const_spec((1, B_WIDTH)), _const_spec((1, B_WIDTH)),
            _const_spec((ts, ts)), _const_spec((HGRN_SUB, HGRN_SUB))],
        out_specs=pl.BlockSpec((nb, ts, D_MODEL), tile),
        out_shape=jax.ShapeDtypeStruct((bsz, seq, D_MODEL), BF16),
        scratch_shapes=[pltpu.VMEM((nb, B_HEADS, B_DIM, B_DIM), F32)],
        compiler_params=pltpu.CompilerParams(dimension_semantics=("parallel", "arbitrary"),
                                             vmem_limit_bytes=VMEM_LIMIT_BYTES),
        name="even_core",
    )(*parts, vnorm, ind, wcat, bfull, lb, onorm, ltri, lvl)


def _attn_kernel(qt_ref, k_ref, vt_ref, o_ref, m_ref, l_ref, acc_ref, sa_ref, sb_ref, mxa_ref, mxb_ref):
    tq = ATTN_TILE
    groups = tq // SUBLANES
    qi = pl.program_id(2)
    heads = range(ATTN_HEADS_PER_STEP)
    buf_a, buf_b = (sa_ref, mxa_ref), (sb_ref, mxb_ref)

    m_ref[...] = jnp.full_like(m_ref, NEG_BIG)
    l_ref[...] = jnp.zeros_like(l_ref)
    acc_ref[...] = jnp.zeros_like(acc_ref)

    def keys(j):
        return pl.ds(pl.multiple_of(j * tq, tq), tq)

    def scores(j, buf, heads=heads):
        s_ref, mx_ref = buf
        for hd in heads:
            s = _dot(k_ref[0, hd, keys(j), :], qt_ref[0, hd])
            s_ref[hd] = s
            mx_ref[hd] = jnp.max(s.reshape(groups, SUBLANES, tq), axis=0)

    def consume(j, buf, masked, heads=heads):
        s_ref, mx_ref = buf
        if masked:
            keep = (lax.broadcasted_iota(jnp.int32, (tq, tq), 0) <= lax.broadcasted_iota(jnp.int32, (tq, tq), 1))
        s3, m_new, alpha = {}, {}, {}
        for hd in heads:
            if masked:
                s3[hd] = jnp.where(keep, s_ref[hd], NEG_BIG).reshape(groups, SUBLANES, tq)
                cmax8 = jnp.max(s3[hd], axis=0)
            else:
                cmax8 = mx_ref[hd]
            cmax = jnp.max(cmax8, axis=0, keepdims=True)
            m_prev = m_ref[hd]
            m_new[hd] = jnp.maximum(m_prev, jnp.broadcast_to(cmax, (SUBLANES, tq)))
            alpha[hd] = jnp.exp2(m_prev - m_new[hd])
            m_ref[hd] = m_new[hd]
        ones = jnp.ones((2 * SUBLANES, tq), BF16)
        for hd in heads:
            sv = s3[hd] if masked else s_ref[hd].reshape(groups, SUBLANES, tq)
            p = jnp.exp2(sv - m_new[hd][None]).reshape(tq, tq).astype(BF16)
            lhs = jnp.concatenate([vt_ref[0, hd * C_V:(hd + 1) * C_V, keys(j)], ones], axis=0)
            pv = _dot(lhs, p)
            acc3 = acc_ref[hd].reshape(C_V // SUBLANES, SUBLANES, tq) * alpha[hd][None]
            acc_ref[hd] = acc3.reshape(C_V, tq) + pv[:C_V, :]
            l_ref[hd] = alpha[hd] * l_ref[hd] + pv[C_V:C_V + SUBLANES, :]

    scores(0, buf_a)
    pairs = qi // 2

    def body(t, carry):
        j = 2 * t
        for hd in heads:
            scores(j + 1, buf_b, [hd])
            consume(j, buf_a, False, [hd])
        for hd in heads:
            scores(j + 2, buf_a, [hd])
            consume(j + 1, buf_b, False, [hd])
        return carry

    lax.fori_loop(0, pairs, body, 0)

    @pl.when(qi % 2 == 0)
    def _():
        consume(qi, buf_a, True)

    @pl.when(qi % 2 == 1)
    def _():
        for hd in heads:
            scores(qi, buf_b, [hd])
            consume(qi - 1, buf_a, False, [hd])
        consume(qi, buf_b, True)
    for hd in range(ATTN_HEADS_PER_STEP):
        inv_l = 1.0 / l_ref[hd]
        out_t = acc_ref[hd].reshape(C_V // SUBLANES, SUBLANES, tq) * inv_l[None]
        o_ref[0, :, hd * C_V:(hd + 1) * C_V] = out_t.reshape(C_V, tq).T.astype(BF16)


def _attention(qt, k, vt):
    bsz, heads, seq, _ = k.shape
    tq = ATTN_TILE
    g = ATTN_HEADS_PER_STEP
    return pl.pallas_call(
        _attn_kernel,
        grid=(bsz, heads // g, seq // tq),
        in_specs=[pl.BlockSpec((1, g, C_QK, tq), lambda b, h, i: (b, h, 0, i)),
                  pl.BlockSpec((1, g, seq, C_QK), lambda b, h, i: (b, h, 0, 0)),
                  pl.BlockSpec((1, g * C_V, seq), lambda b, h, i: (b, h, 0))],
        out_specs=pl.BlockSpec((1, tq, g * C_V), lambda b, h, i: (b, i, h)),
        out_shape=jax.ShapeDtypeStruct((bsz, seq, heads * C_V), BF16),
        scratch_shapes=[pltpu.VMEM((g, SUBLANES, tq), F32), pltpu.VMEM((g, SUBLANES, tq), F32),
                        pltpu.VMEM((g, C_V, tq), F32), pltpu.VMEM((g, tq, tq), F32),
                        pltpu.VMEM((g, tq, tq), F32), pltpu.VMEM((g, SUBLANES, tq), F32),
                        pltpu.VMEM((g, SUBLANES, tq), F32)],
        compiler_params=pltpu.CompilerParams(dimension_semantics=("parallel", "parallel", "arbitrary"),
                                             vmem_limit_bytes=VMEM_LIMIT_BYTES),
        name="mla_attention",
    )(qt, k, vt)


def _row(v):
    return v.reshape(1, -1).astype(F32)


def _pad_cols(w, total):
    return jnp.pad(w, ((0, 0), (0, total - w.shape[1])))


def _even_params(v_norm, w_s, b_s, lb, out_norm):
    causal = np.tril(np.ones((A_CHUNK, A_CHUNK), bool))
    w = jnp.where(causal[None], w_s, 0.0)
    wcat = jnp.transpose(w, (1, 0, 2)).reshape(A_CHUNK, A_HEADS * A_CHUNK)
    bfull = jnp.repeat(b_s.T, A_HEAD_DIM, axis=1)
    head_of = np.arange(A_WIDTH) // A_HEAD_DIM
    ind = (head_of[:, None] == head_of[None, :]).astype(np.float32)
    ltri = np.tril(np.ones((HGRN_TILE, HGRN_TILE), np.float32))
    return dict(vnorm=_row(v_norm), ind=jnp.asarray(ind, BF16),
                wcat=wcat.astype(BF16), bfull=bfull.astype(F32), lb=_row(lb),
                onorm=_row(jnp.tile(out_norm, B_HEADS)), ltri=jnp.asarray(ltri, BF16),
                lvl=jnp.asarray(_level_map(HGRN_SUB)))


def _odd_params(w_in, q_a_norm, kv_a_norm, w_q_b, w_kv_b, q_norm, k_norm):
    tm = TOKEN_TILE
    win = _pad_cols(w_in, ODD_IN_PAD)
    wkv = w_kv_b.reshape(KV_LORA, C_HEADS, C_NOPE + C_V)
    wkvk = wkv[:, :, :C_NOPE].reshape(KV_LORA, C_HEADS * C_NOPE)
    wkvvt = wkv[:, :, C_NOPE:].reshape(KV_LORA, C_HEADS * C_V).T
    inv_freq = ROPE_THETA ** (-jnp.arange(0, C_ROPE, 2, dtype=F32) / C_ROPE)
    invfc = jnp.broadcast_to(inv_freq[:, None], (C_ROPE // 2, tm))
    head_of = np.arange(C_HEADS * C_NOPE) // C_NOPE
    splat = (head_of[:, None] == np.arange(LANES)[None, :]).astype(np.float32)
    rope_rows = np.broadcast_to((np.arange(LANES) < C_HEADS).astype(np.float32), (LANES, LANES))
    indk = np.concatenate([splat, rope_rows], axis=0)
    return dict(win=win.astype(BF16), wropet=w_in[:, Q_LORA + KV_LORA:].T.astype(BF16),
                qan=_row(q_a_norm), kvan=_row(kv_a_norm),
                wqbt=w_q_b.T.astype(BF16), wkvk=wkvk.astype(BF16), wkvvt=wkvvt.astype(BF16),
                gq=jnp.broadcast_to(q_norm.astype(F32)[:, None], (C_QK, tm)),
                gkr=jnp.broadcast_to(k_norm.astype(F32)[C_NOPE:, None], (C_ROPE, tm)),
                kn=_row(k_norm[:C_NOPE]), invfc=invfc,
                indk=jnp.asarray(indk, BF16), splat=jnp.asarray(splat.T, BF16))


def kernel(x, p, positions, norm_gains, ffn_w_gate, ffn_w_up, ffn_w_down, ple_w_gate, ple_w_proj,
           even_w_in, gmlp_v_norm, gmlp_w_s, gmlp_b_s, hgrn_lb_raw, hgrn_out_norm, even_w_out,
           mla_w_in, mla_q_a_norm, mla_kv_a_norm, mla_w_q_b, mla_w_kv_b, mla_q_norm, mla_k_norm, mla_w_out):
    bsz, seq, d = x.shape
    depth = norm_gains.shape[0]
    n = bsz * seq
    assert d == D_MODEL and seq % ATTN_TILE == 0 and seq % TOKEN_TILE == 0 and seq % HGRN_TILE == 0
    assert bsz % HGRN_BATCH == 0

    lb_sm = jax.nn.softmax(hgrn_lb_raw.astype(F32), axis=0)
    lower_bounds = jnp.clip(jnp.cumsum(lb_sm, axis=0) - lb_sm[0], 0.0, LB_MAX)
    posr = positions.astype(F32).reshape(bsz, 1, seq)

    ffn = tuple(w.astype(BF16) for w in (ffn_w_gate, ffn_w_up, ffn_w_down))
    ple_wg, ple_wp = ple_w_gate.astype(BF16), ple_w_proj.astype(BF16)
    even_win, even_wout, mla_wout = (w.astype(BF16) for w in (even_w_in, even_w_out, mla_w_out))
    p3d = p.reshape(depth, n, D_PLE)

    xc = x.reshape(n, d)
    for i in range(depth):
        g = norm_gains[i]
        j = i // 2
        if i % 2 == 0:
            ep = _even_params(gmlp_v_norm[j], gmlp_w_s[j], gmlp_b_s[j], lower_bounds[j], hgrn_out_norm[j])
            outs = _stage1_even(xc, _row(g[0]), ffn, (i, 0), _row(g[1]), even_win, j)
            xc = outs[0]
            parts = [o.reshape(bsz, seq, A_WIDTH) for o in outs[1:]]
            mixed = _even_core(parts, ep["vnorm"], ep["ind"], ep["wcat"], ep["bfull"], ep["lb"],
                               ep["onorm"], ep["ltri"], ep["lvl"])
            w_out = even_wout
        else:
            op = _odd_params(mla_w_in[j], mla_q_a_norm[j], mla_kv_a_norm[j], mla_w_q_b[j], mla_w_kv_b[j],
                             mla_q_norm[j], mla_k_norm[j])
            x1, qt, k, vt = _stage1_odd(xc.reshape(bsz, seq, d), posr, op, _row(g[0]), ffn, (i, 0),
                                        _row(g[1]))
            xc = x1.reshape(n, d)
            mixed = _attention(qt, k, vt)
            w_out = mla_wout
        xc = _stage2(xc, mixed.reshape(n, d), p3d, i, w_out, j, _row(g[2]), ffn, (i, 1), _row(g[3]),
                     ple_wg, ple_wp, _row(g[4]))
    return xc.reshape(bsz, seq, d)
```

```python
import functools
import math

import numpy as np
import jax
import jax.numpy as jnp
from jax import lax
from jax.experimental import pallas as pl
from jax.experimental.pallas import tpu as pltpu

F32 = jnp.float32
BF16 = jnp.bfloat16

D_MODEL = 1024
D_FF = 2816
D_PLE = 256
EPS = 1e-6
NEG_BIG = -1e30
F_MIN = 1e-6
LB_MAX = 0.999

A_HEADS = 8
A_HEAD_DIM = 64
A_WIDTH = A_HEADS * A_HEAD_DIM
A_CHUNK = 128
B_HEADS = 4
B_DIM = 128
B_WIDTH = B_HEADS * B_DIM
EVEN_PARTS = 6

C_HEADS = 8
C_NOPE = 128
C_ROPE = 64
C_V = 128
C_QK = C_NOPE + C_ROPE
Q_LORA = 384
KV_LORA = 256
ROPE_THETA = 10000.0
ATTN_SCALE = C_QK ** -0.5
LOG2E = math.log2(math.e)

LANES = 128
SUBLANES = 8
ODD_IN_PAD = Q_LORA + KV_LORA + LANES
VMEM_LIMIT_BYTES = 60000 * 1024

TOKEN_TILE = 512
FFN_STAGE_ROWS_IN = 128
FFN_STAGE_ROWS_OUT = 256
HGRN_TILE = 256
HGRN_BATCH = 2
HGRN_SUB = HGRN_TILE // 2
HGRN_LEVELS = (64, 32, 16, 8)
ATTN_TILE = 512
ATTN_HEADS_PER_STEP = 4


def _sigmoid(x):
    return 0.5 + 0.5 * jnp.tanh(0.5 * x)


def _silu(x):
    t = 0.5 * x
    return t + t * jnp.tanh(t)


def _gelu_tanh(x):
    c = math.sqrt(2.0 / math.pi)
    hx = 0.5 * x
    return hx + hx * jnp.tanh(x * (c + (c * 0.044715) * (x * x)))


def _rms(x, g):
    ms = jnp.mean(x * x, axis=-1, keepdims=True)
    return x * lax.rsqrt(ms + EPS) * g


def _dot(a, b):
    return jnp.dot(a, b, preferred_element_type=F32)


def _dot_nt(a, b):
    return lax.dot_general(a, b, (((1,), (1,)), ((), ())), preferred_element_type=F32)


def _dot_tn(a, b):
    return lax.dot_general(a, b, (((0,), (0,)), ((), ())), preferred_element_type=F32)


def _split2(x):
    hi = x.astype(BF16)
    lo = (x - hi.astype(F32)).astype(BF16)
    return hi, lo


def _swiglu(hn, wg_ref, wu_ref, wd_ref):
    act = (_silu(_dot(hn, wg_ref[...])) * _dot(hn, wu_ref[...])).astype(BF16)
    return _dot(act, wd_ref[...])


def _s1_even_kernel(ffn_layer, x_ref, g0_ref, wg_hbm, wu_hbm, wd_hbm, g1_ref, win_ref,
                    x1_ref, u_ref, v_ref, q_ref, f_ref, i_ref, gg_ref, wg_ref, wu_ref, wd_ref, *stage):
    @pl.when(pl.program_id(0) == 0)
    def _():
        _load_ffn(ffn_layer, wg_hbm, wu_hbm, wd_hbm, wg_ref, wu_ref, wd_ref, *stage)

    x = x_ref[...]
    hn = _rms(x, g0_ref[...]).astype(BF16)
    x1 = x + 0.5 * _swiglu(hn, wg_ref, wu_ref, wd_ref)
    x1_ref[...] = x1
    h = _rms(x1, g1_ref[...]).astype(BF16)
    outs = (u_ref, v_ref, q_ref, f_ref, i_ref, gg_ref)
    for k, o_ref in enumerate(outs):
        z = _dot(h, win_ref[:, k * A_WIDTH:(k + 1) * A_WIDTH])
        o_ref[...] = z.astype(o_ref.dtype)


def _const_spec(shape, prefix=()):
    nd = len(shape)
    return pl.BlockSpec((None,) * len(prefix) + tuple(shape), lambda *_: tuple(prefix) + (0,) * nd,
                        pipeline_mode=pl.Buffered(1))


def _ffn_specs():
    return [pl.BlockSpec(memory_space=pl.ANY)] * 3


def _ffn_scratch():
    return [pltpu.VMEM((D_MODEL, D_FF), BF16), pltpu.VMEM((D_MODEL, D_FF), BF16), pltpu.VMEM((D_FF, D_MODEL), BF16),
            pltpu.VMEM((2, FFN_STAGE_ROWS_IN, D_FF), F32), pltpu.VMEM((2, FFN_STAGE_ROWS_OUT, D_MODEL), F32),
            pltpu.SemaphoreType.DMA((2,))]


def _load_cast(src_hbm, layer, dst_ref, stage_ref, sem_ref):
    rows = stage_ref.shape[1]
    n_chunks = dst_ref.shape[0] // rows

    def copy(c):
        return pltpu.make_async_copy(src_hbm.at[layer[0], layer[1], pl.ds(c * rows, rows)],
                                     stage_ref.at[c % 2], sem_ref.at[c % 2])

    copy(0).start()
    for c in range(n_chunks):
        if c + 1 < n_chunks:
            copy(c + 1).start()
        copy(c).wait()
        dst_ref[c * rows:(c + 1) * rows, :] = stage_ref[c % 2].astype(BF16)


def _load_ffn(layer, wg_hbm, wu_hbm, wd_hbm, wg_ref, wu_ref, wd_ref, stage_in_ref, stage_out_ref, sem_ref):
    _load_cast(wg_hbm, layer, wg_ref, stage_in_ref, sem_ref)
    _load_cast(wu_hbm, layer, wu_ref, stage_in_ref, sem_ref)
    _load_cast(wd_hbm, layer, wd_ref, stage_out_ref, sem_ref)


def _stage1_even(x2d, g0, ffn, ffn_layer, g1, win, j):
    n = x2d.shape[0]
    tm = TOKEN_TILE
    row = lambda i: (i, 0)
    part_spec = pl.BlockSpec((tm, A_WIDTH), row)
    out_shape = [jax.ShapeDtypeStruct((n, D_MODEL), F32)]
    out_shape += [jax.ShapeDtypeStruct((n, A_WIDTH), F32 if k == 3 else BF16) for k in range(EVEN_PARTS)]
    return pl.pallas_call(
        functools.partial(_s1_even_kernel, ffn_layer),
        grid=(n // tm,),
        in_specs=[pl.BlockSpec((tm, D_MODEL), row), _const_spec((1, D_MODEL))] + _ffn_specs() + [
                  _const_spec((1, D_MODEL)), _const_spec((D_MODEL, EVEN_PARTS * A_WIDTH), (j,))],
        out_specs=[pl.BlockSpec((tm, D_MODEL), row)] + [part_spec] * EVEN_PARTS,
        out_shape=out_shape,
        scratch_shapes=_ffn_scratch(),
        compiler_params=pltpu.CompilerParams(dimension_semantics=("arbitrary",),
                                             vmem_limit_bytes=VMEM_LIMIT_BYTES),
        name="stage1_even",
    )(x2d, g0, *ffn, g1, win)


def _s1_odd_kernel(ffn_layer, x_ref, posr_ref, invfc_ref, g0_ref, wg_hbm, wu_hbm, wd_hbm, g1_ref,
                   win_ref, wropet_ref, qan_ref, kvan_ref, wqbt_ref, wkvk_ref, wkvvt_ref, gq_ref, gkr_ref, kn_ref,
                   indk_ref, splat_ref,
                   x1_ref, qt_ref, k_ref, vt_ref, wg_ref, wu_ref, wd_ref, *stage):
    @pl.when((pl.program_id(0) == 0) & (pl.program_id(1) == 0))
    def _():
        _load_ffn(ffn_layer, wg_hbm, wu_hbm, wd_hbm, wg_ref, wu_ref, wd_ref, *stage)

    x = x_ref[0]
    hn = _rms(x, g0_ref[...]).astype(BF16)
    x1 = x + 0.5 * _swiglu(hn, wg_ref, wu_ref, wd_ref)
    x1_ref[0] = x1
    h = _rms(x1, g1_ref[...]).astype(BF16)
    z = _dot(h, win_ref[...])
    cqn = _rms(z[:, :Q_LORA], qan_ref[...]).astype(BF16)
    ckvn = _rms(z[:, Q_LORA:Q_LORA + KV_LORA], kvan_ref[...]).astype(BF16)
    k_rope = z[:, Q_LORA + KV_LORA:]
    q_t = _dot_nt(wqbt_ref[...], cqn)
    k_nope = _dot(ckvn, wkvk_ref[...])
    vt_ref[0] = _dot_nt(wkvvt_ref[...], ckvn).astype(BF16)

    ang_t = invfc_ref[...] * posr_ref[0]
    cos_f, sin_f = jnp.cos(ang_t), jnp.sin(ang_t)
    gq = gq_ref[...]
    half = C_ROPE // 2
    for hd in range(C_HEADS):
        blk = q_t[hd * C_QK:(hd + 1) * C_QK, :]
        ss = jnp.sum(blk * blk, axis=0, keepdims=True)
        r = lax.rsqrt(ss * (1.0 / C_QK) + EPS) * (ATTN_SCALE * LOG2E)
        sc = blk * r * gq
        a1 = sc[C_NOPE:C_NOPE + half, :]
        a2 = sc[C_NOPE + half:, :]
        qt_ref[0, hd, 0:C_NOPE, :] = sc[:C_NOPE, :].astype(BF16)
        qt_ref[0, hd, C_NOPE:C_NOPE + half, :] = (a1 * cos_f - a2 * sin_f).astype(BF16)
        qt_ref[0, hd, C_NOPE + half:C_QK, :] = (a2 * cos_f + a1 * sin_f).astype(BF16)

    kr_t = _dot_nt(wropet_ref[...], h) * gkr_ref[...]
    b1, b2 = kr_t[:half, :], kr_t[half:, :]
    rot_t = jnp.concatenate([b1 * cos_f - b2 * sin_f, b2 * cos_f + b1 * sin_f,
                             jnp.zeros((LANES - C_ROPE, kr_t.shape[1]), F32)], axis=0)
    kr_rot = rot_t.T

    lane = lax.broadcasted_iota(jnp.int32, (1, LANES), 1)
    kn_nope = kn_ref[...]
    k2_hi, k2_lo = _split2(jnp.concatenate([k_nope * k_nope, k_rope * k_rope], axis=1))
    ssk = _dot(k2_hi, indk_ref[...]) + _dot(k2_lo, indk_ref[...])
    rk = jnp.where(lane < C_HEADS, lax.rsqrt(ssk * (1.0 / C_QK) + EPS), 0.0)
    rk_hi, rk_lo = _split2(rk)
    rk_b = _dot(rk_hi, splat_ref[...]) + _dot(rk_lo, splat_ref[...])
    for hd in range(C_HEADS):
        cols = slice(hd * C_NOPE, (hd + 1) * C_NOPE)
        k_ref[0, hd, :, 0:C_NOPE] = (k_nope[:, cols] * rk_b[:, cols] * kn_nope).astype(BF16)
        k_ref[0, hd, :, C_NOPE:C_QK] = (kr_rot * rk_b[:, cols])[:, :C_ROPE].astype(BF16)


def _stage1_odd(x3d, posr, op, g0, ffn, ffn_layer, g1):
    bsz, seq, _ = x3d.shape
    tm = TOKEN_TILE
    tile = lambda b, i: (b, i, 0)
    return pl.pallas_call(
        functools.partial(_s1_odd_kernel, ffn_layer),
        grid=(bsz, seq // tm),
        in_specs=[pl.BlockSpec((1, tm, D_MODEL), tile), pl.BlockSpec((1, 1, tm), lambda b, i: (b, 0, i)),
                  _const_spec((C_ROPE // 2, tm)), _const_spec((1, D_MODEL))]
                 + _ffn_specs() + [
                  _const_spec((1, D_MODEL)), _const_spec((D_MODEL, ODD_IN_PAD)), _const_spec((C_ROPE, D_MODEL)),
                  _const_spec((1, Q_LORA)), _const_spec((1, KV_LORA)),
                  _const_spec((C_HEADS * C_QK, Q_LORA)), _const_spec((KV_LORA, C_HEADS * C_NOPE)),
                  _const_spec((C_HEADS * C_V, KV_LORA)), _const_spec((C_QK, tm)), _const_spec((C_ROPE, tm)),
                  _const_spec((1, C_NOPE)), _const_spec((C_HEADS * C_NOPE + LANES, LANES)),
                  _const_spec((LANES, C_HEADS * C_NOPE))],
        out_specs=[pl.BlockSpec((1, tm, D_MODEL), tile),
                   pl.BlockSpec((1, C_HEADS, C_QK, tm), lambda b, i: (b, 0, 0, i)),
                   pl.BlockSpec((1, C_HEADS, tm, C_QK), lambda b, i: (b, 0, i, 0)),
                   pl.BlockSpec((1, C_HEADS * C_V, tm), lambda b, i: (b, 0, i))],
        out_shape=[jax.ShapeDtypeStruct((bsz, seq, D_MODEL), F32),
                   jax.ShapeDtypeStruct((bsz, C_HEADS, C_QK, seq), BF16),
                   jax.ShapeDtypeStruct((bsz, C_HEADS, seq, C_QK), BF16),
                   jax.ShapeDtypeStruct((bsz, C_HEADS * C_V, seq), BF16)],
        scratch_shapes=_ffn_scratch(),
        compiler_params=pltpu.CompilerParams(dimension_semantics=("arbitrary", "arbitrary"),
                                             vmem_limit_bytes=VMEM_LIMIT_BYTES),
        name="stage1_odd",
    )(x3d, posr, op["invfc"], g0, *ffn, g1, op["win"], op["wropet"], op["qan"], op["kvan"],
      op["wqbt"], op["wkvk"], op["wkvvt"], op["gq"], op["gkr"], op["kn"], op["indk"], op["splat"])


def _s2_kernel(ffn_layer, x_ref, m_ref, p_ref, wo_ref, g2_ref, wg_hbm, wu_hbm, wd_hbm,
               g3_ref, wpg_ref, wpp_ref, g4_ref, o_ref, wg_ref, wu_ref, wd_ref, *stage):
    @pl.when(pl.program_id(0) == 0)
    def _():
        _load_ffn(ffn_layer, wg_hbm, wu_hbm, wd_hbm, wg_ref, wu_ref, wd_ref, *stage)

    x = x_ref[...] + _dot(m_ref[...], wo_ref[...])
    hn = _rms(x, g2_ref[...]).astype(BF16)
    x = x + 0.5 * _swiglu(hn, wg_ref, wu_ref, wd_ref)
    h3 = _rms(x, g3_ref[...]).astype(BF16)
    gate = _sigmoid(_dot(h3, wpg_ref[...]))
    proj = _dot(p_ref[...].astype(BF16), wpp_ref[...])
    o_ref[...] = x + _rms(gate * proj, g4_ref[...])


def _stage2(x2d, mixed, p3d, layer, wo, j, g2, ffn, ffn_layer, g3, wpg, wpp, g4):
    n = x2d.shape[0]
    tm = TOKEN_TILE
    row = lambda i: (i, 0)
    return pl.pallas_call(
        functools.partial(_s2_kernel, ffn_layer),
        grid=(n // tm,),
        in_specs=[pl.BlockSpec((tm, D_MODEL), row), pl.BlockSpec((tm, D_MODEL), row),
                  pl.BlockSpec((None, tm, D_PLE), lambda i: (layer, i, 0)),
                  _const_spec((D_MODEL, D_MODEL), (j,)), _const_spec((1, D_MODEL))] + _ffn_specs() + [
                  _const_spec((1, D_MODEL)), _const_spec((D_MODEL, D_MODEL), (layer,)),
                  _const_spec((D_PLE, D_MODEL), (layer,)), _const_spec((1, D_MODEL))],
        out_specs=pl.BlockSpec((tm, D_MODEL), row),
        out_shape=jax.ShapeDtypeStruct((n, D_MODEL), F32),
        scratch_shapes=_ffn_scratch(),
        compiler_params=pltpu.CompilerParams(dimension_semantics=("arbitrary",),
                                             vmem_limit_bytes=VMEM_LIMIT_BYTES),
        name="stage2",
    )(x2d, mixed, p3d, wo, g2, *ffn, g3, wpg, wpp, g4)


def _level_map(ts):
    t = np.arange(ts)[:, None]
    s = np.arange(ts)[None, :]
    lvl = np.full((ts, ts), len(HGRN_LEVELS) + 1, np.int32)
    lvl[(t // SUBLANES == s // SUBLANES) & (s <= t)] = len(HGRN_LEVELS)
    for li, h in enumerate(HGRN_LEVELS):
        own = (t // (2 * h) == s // (2 * h)) & (t % (2 * h) >= h) & (s % (2 * h) < h)
        lvl[own] = li
    return lvl


def _even_core_kernel(u_ref, v_ref, q_ref, f_ref, i_ref, g_ref,
                      vnorm_ref, ind_ref, wcat_ref, bfull_ref, lb_ref, onorm_ref, ltri_ref, lvl_ref,
                      o_ref, state_ref):
    nb = HGRN_BATCH
    ts = HGRN_TILE
    sub = HGRN_SUB
    n_lvl = len(HGRN_LEVELS)
    rows_all = nb * ts

    def stacked(ref):
        return ref[...].reshape(rows_all, ref.shape[-1])

    u = _gelu_tanh(stacked(u_ref).astype(F32))
    v = _gelu_tanh(stacked(v_ref).astype(F32))
    ss = _dot((v * v).astype(BF16), ind_ref[...])
    vn = (v * lax.rsqrt(ss * (1.0 / A_HEAD_DIM) + EPS) * vnorm_ref[...]).astype(BF16)
    lane_head = lax.broadcasted_iota(jnp.int32, (1, A_WIDTH), 1) // A_HEAD_DIM
    for c in range(rows_all // A_CHUNK):
        rows = slice(c * A_CHUNK, (c + 1) * A_CHUNK)
        vc = vn[rows, :]
        zero = jnp.zeros_like(vc)
        vbd = jnp.concatenate([jnp.where(lane_head == hd, vc, zero) for hd in range(A_HEADS)], axis=0)
        mixed = _dot(wcat_ref[...], vbd) + bfull_ref[...]
        bi, local = divmod(c * A_CHUNK, ts)
        o_ref[bi, local:local + A_CHUNK, 0:A_WIDTH] = (u[rows, :] * mixed).astype(BF16)

    @pl.when(pl.program_id(1) == 0)
    def _():
        state_ref[...] = jnp.zeros_like(state_ref)

    lb = lb_ref[...]
    fg = lb + (1.0 - lb) * _sigmoid(stacked(f_ref))
    lf = jnp.log2(jnp.maximum(fg, F_MIN))
    kk = 1.0 - fg
    qq = stacked(q_ref).astype(F32)
    lf_hi, lf_lo = _split2(lf)
    ltri = ltri_ref[...]
    seqs = [slice(bi * ts, (bi + 1) * ts) for bi in range(nb)]
    b = jnp.concatenate([_dot(ltri, lf_hi[r, :]) + _dot(ltri, lf_lo[r, :]) for r in seqs], axis=0)

    q_lvls, k_lvls = [], []
    for h in HGRN_LEVELS:
        q_parts, k_parts = [], []
        zero = jnp.zeros((h, B_WIDTH), F32)
        for j in range(rows_all // (2 * h)):
            r0 = j * 2 * h
            bm = b[r0 + h - 1:r0 + h, :]
            lo_rows = slice(r0, r0 + h)
            up_rows = slice(r0 + h, r0 + 2 * h)
            k_parts += [kk[lo_rows, :] * jnp.exp2(bm - b[lo_rows, :]), zero]
            q_parts += [zero, qq[up_rows, :] * jnp.exp2(b[up_rows, :] - bm)]
        q_lvls.append(jnp.concatenate(q_parts, axis=0).astype(BF16))
        k_lvls.append(jnp.concatenate(k_parts, axis=0).astype(BF16))
    b3 = b.reshape(rows_all // SUBLANES, SUBLANES, B_WIDTH)
    bref = jnp.broadcast_to(b3[:, 3:4, :], b3.shape).reshape(rows_all, B_WIDTH)
    q_lvls.append((qq * jnp.exp2(b - bref)).astype(BF16))
    k_lvls.append((kk * jnp.exp2(bref - b)).astype(BF16))

    q_top, k_top, k_out, carry = [], [], [], []
    for bi in range(nb):
        r0 = bi * ts
        bm = b[r0 + sub - 1:r0 + sub, :]
        q_top.append((qq[r0 + sub:r0 + ts, :] * jnp.exp2(b[r0 + sub:r0 + ts, :] - bm)).astype(BF16))
        k_top.append((kk[r0:r0 + sub, :] * jnp.exp2(bm - b[r0:r0 + sub, :])).astype(BF16))
        b_last = b[r0 + ts - 1:r0 + ts, :]
        k_out.append((kk[seqs[bi], :] * jnp.exp2(b_last - b[seqs[bi], :])).astype(BF16))
        carry.append(jnp.exp2(b_last))
    q_in = (qq * jnp.exp2(b)).astype(BF16)
    vv = stacked(i_ref)
    gate = _silu(stacked(g_ref).astype(F32))
    onorm = onorm_ref[...]

    lvl = lvl_ref[...]
    masks = [lvl == li for li in range(n_lvl + 1)]
    units = [(bi, hd, slice(hd * B_DIM, (hd + 1) * B_DIM)) for hd in range(B_HEADS) for bi in range(nb)]
    states = {(bi, hd): state_ref[bi, hd] for bi, hd, _ in units}
    inter = {(bi, hd): _dot_nt(q_in[seqs[bi], cols], states[bi, hd].astype(BF16)) for bi, hd, cols in units}
    for bi, hd, cols in units:
        state_ref[bi, hd] = (states[bi, hd] * carry[bi][:, cols]
                             + _dot_tn(vv[seqs[bi], cols], k_out[bi][:, cols]))
    score_blocks = {}
    for bi, hd, cols in units:
        diag = []
        for a in range(ts // sub):
            rows = slice(bi * ts + a * sub, bi * ts + (a + 1) * sub)
            blk = jnp.where(masks[n_lvl], _dot_nt(q_lvls[n_lvl][rows, cols], k_lvls[n_lvl][rows, cols]), 0.0)
            for li in range(n_lvl):
                blk = jnp.where(masks[li], _dot_nt(q_lvls[li][rows, cols], k_lvls[li][rows, cols]), blk)
            diag.append(blk.astype(BF16))
        top = _dot_nt(q_top[bi][:, cols], k_top[bi][:, cols]).astype(BF16)
        score_blocks[bi, hd] = (diag[0], jnp.concatenate([top, diag[1]], axis=1))
    outs = {}
    for bi, hd, cols in units:
        v_h = vv[seqs[bi], cols]
        first, second = score_blocks[bi, hd]
        o_intra = jnp.concatenate([_dot(first, v_h[:sub, :]), _dot(second, v_h)], axis=0)
        outs[bi, hd] = o_intra + inter[bi, hd]
    for bi, hd, cols in units:
        on = _rms(outs[bi, hd], onorm[:, cols]) * gate[seqs[bi], cols]
        o_ref[bi, :, A_WIDTH + hd * B_DIM:A_WIDTH + (hd + 1) * B_DIM] = on.astype(BF16)


def _even_core(parts, vnorm, ind, wcat, bfull, lb, onorm, ltri, lvl):
    bsz, seq, _ = parts[0].shape
    ts = HGRN_TILE
    nb = HGRN_BATCH
    tile = lambda b, i: (b, i, 0)
    part_spec = pl.BlockSpec((nb, ts, A_WIDTH), tile)
    return pl.pallas_call(
        _even_core_kernel,
        grid=(bsz // nb, seq // ts),
        in_specs=[part_spec] * EVEN_PARTS + [
            _const_spec((1, A_WIDTH)), _const_spec((A_WIDTH, A_WIDTH)),
            _const_spec((A_CHUNK, A_HEADS * A_CHUNK)), _const_spec((A_CHUNK, A_WIDTH)),
            _const_spec((1, B_WIDTH)), _const_spec((1, B_WIDTH)),
            _const_spec((ts, ts)), _const_spec((HGRN_SUB, HGRN_SUB))],
        out_specs=pl.BlockSpec((nb, ts, D_MODEL), tile),
        out_shape=jax.ShapeDtypeStruct((bsz, seq, D_MODEL), BF16),
        scratch_shapes=[pltpu.VMEM((nb, B_HEADS, B_DIM, B_DIM), F32)],
        compiler_params=pltpu.CompilerParams(dimension_semantics=("parallel", "arbitrary"),
                                             vmem_limit_bytes=VMEM_LIMIT_BYTES),
        name="even_core",
    )(*parts, vnorm, ind, wcat, bfull, lb, onorm, ltri, lvl)


def _attn_kernel(qt_ref, k_ref, vt_ref, o_ref, m_ref, l_ref, acc_ref, sa_ref, sb_ref, mxa_ref, mxb_ref):
    tq = ATTN_TILE
    groups = tq // SUBLANES
    qi = pl.program_id(2)
    heads = range(ATTN_HEADS_PER_STEP)
    buf_a, buf_b = (sa_ref, mxa_ref), (sb_ref, mxb_ref)

    m_ref[...] = jnp.full_like(m_ref, NEG_BIG)
    l_ref[...] = jnp.zeros_like(l_ref)
    acc_ref[...] = jnp.zeros_like(acc_ref)

    def keys(j):
        return pl.ds(pl.multiple_of(j * tq, tq), tq)

    def scores(j, buf, heads=heads):
        s_ref, mx_ref = buf
        for hd in heads:
            s = _dot(k_ref[0, hd, keys(j), :], qt_ref[0, hd])
            s_ref[hd] = s
            mx_ref[hd] = jnp.max(s.reshape(groups, SUBLANES, tq), axis=0)

    def consume(j, buf, masked, heads=heads):
        s_ref, mx_ref = buf
        if masked:
            keep = (lax.broadcasted_iota(jnp.int32, (tq, tq), 0) <= lax.broadcasted_iota(jnp.int32, (tq, tq), 1))
        s3, m_new, alpha = {}, {}, {}
        for hd in heads:
            if masked:
                s3[hd] = jnp.where(keep, s_ref[hd], NEG_BIG).reshape(groups, SUBLANES, tq)
                cmax8 = jnp.max(s3[hd], axis=0)
            else:
                cmax8 = mx_ref[hd]
            cmax = jnp.max(cmax8, axis=0, keepdims=True)
            m_prev = m_ref[hd]
            m_new[hd] = jnp.maximum(m_prev, jnp.broadcast_to(cmax, (SUBLANES, tq)))
            alpha[hd] = jnp.exp2(m_prev - m_new[hd])
            m_ref[hd] = m_new[hd]
        ones = jnp.ones((2 * SUBLANES, tq), BF16)
        for hd in heads:
            sv = s3[hd] if masked else s_ref[hd].reshape(groups, SUBLANES, tq)
            p = jnp.exp2(sv - m_new[hd][None]).reshape(tq, tq).astype(BF16)
            lhs = jnp.concatenate([vt_ref[0, hd * C_V:(hd + 1) * C_V, keys(j)], ones], axis=0)
            pv = _dot(lhs, p)
            acc3 = acc_ref[hd].reshape(C_V // SUBLANES, SUBLANES, tq) * alpha[hd][None]
            acc_ref[hd] = acc3.reshape(C_V, tq) + pv[:C_V, :]
            l_ref[hd] = alpha[hd] * l_ref[hd] + pv[C_V:C_V + SUBLANES, :]

    scores(0, buf_a)
    pairs = qi // 2

    def body(t, carry):
        j = 2 * t
        for hd in heads:
            scores(j + 1, buf_b, [hd])
            consume(j, buf_a, False, [hd])
        for hd in heads:
            scores(j + 2, buf_a, [hd])
            consume(j + 1, buf_b, False, [hd])
        return carry

    lax.fori_loop(0, pairs, body, 0)

    @pl.when(qi % 2 == 0)
    def _():
        consume(qi, buf_a, True)

    @pl.when(qi % 2 == 1)
    def _():
        for hd in heads:
            scores(qi, buf_b, [hd])
            consume(qi - 1, buf_a, False, [hd])
        consume(qi, buf_b, True)
    for hd in range(ATTN_HEADS_PER_STEP):
        inv_l = 1.0 / l_ref[hd]
        out_t = acc_ref[hd].reshape(C_V // SUBLANES, SUBLANES, tq) * inv_l[None]
        o_ref[0, :, hd * C_V:(hd + 1) * C_V] = out_t.reshape(C_V, tq).T.astype(BF16)


def _attention(qt, k, vt):
    bsz, heads, seq, _ = k.shape
    tq = ATTN_TILE
    g = ATTN_HEADS_PER_STEP
    return pl.pallas_call(
        _attn_kernel,
        grid=(bsz, heads // g, seq // tq),
        in_specs=[pl.BlockSpec((1, g, C_QK, tq), lambda b, h, i: (b, h, 0, i)),
                  pl.BlockSpec((1, g, seq, C_QK), lambda b, h, i: (b, h, 0, 0)),
                  pl.BlockSpec((1, g * C_V, seq), lambda b, h, i: (b, h, 0))],
        out_specs=pl.BlockSpec((1, tq, g * C_V), lambda b, h, i: (b, i, h)),
        out_shape=jax.ShapeDtypeStruct((bsz, seq, heads * C_V), BF16),
        scratch_shapes=[pltpu.VMEM((g, SUBLANES, tq), F32), pltpu.VMEM((g, SUBLANES, tq), F32),
                        pltpu.VMEM((g, C_V, tq), F32), pltpu.VMEM((g, tq, tq), F32),
                        pltpu.VMEM((g, tq, tq), F32), pltpu.VMEM((g, SUBLANES, tq), F32),
                        pltpu.VMEM((g, SUBLANES, tq), F32)],
        compiler_params=pltpu.CompilerParams(dimension_semantics=("parallel", "parallel", "arbitrary"),
                                             vmem_limit_bytes=VMEM_LIMIT_BYTES),
        name="mla_attention",
    )(qt, k, vt)


def _row(v):
    return v.reshape(1, -1).astype(F32)


def _pad_cols(w, total):
    return jnp.pad(w, ((0, 0), (0, total - w.shape[1])))


def _even_params(v_norm, w_s, b_s, lb, out_norm):
    causal = np.tril(np.ones((A_CHUNK, A_CHUNK), bool))
    w = jnp.where(causal[None], w_s, 0.0)
    wcat = jnp.transpose(w, (1, 0, 2)).reshape(A_CHUNK, A_HEADS * A_CHUNK)
    bfull = jnp.repeat(b_s.T, A_HEAD_DIM, axis=1)
    head_of = np.arange(A_WIDTH) // A_HEAD_DIM
    ind = (head_of[:, None] == head_of[None, :]).astype(np.float32)
    ltri = np.tril(np.ones((HGRN_TILE, HGRN_TILE), np.float32))
    return dict(vnorm=_row(v_norm), ind=jnp.asarray(ind, BF16),
                wcat=wcat.astype(BF16), bfull=bfull.astype(F32), lb=_row(lb),
                onorm=_row(jnp.tile(out_norm, B_HEADS)), ltri=jnp.asarray(ltri, BF16),
                lvl=jnp.asarray(_level_map(HGRN_SUB)))


def _odd_params(w_in, q_a_norm, kv_a_norm, w_q_b, w_kv_b, q_norm, k_norm):
    tm = TOKEN_TILE
    win = _pad_cols(w_in, ODD_IN_PAD)
    wkv = w_kv_b.reshape(KV_LORA, C_HEADS, C_NOPE + C_V)
    wkvk = wkv[:, :, :C_NOPE].reshape(KV_LORA, C_HEADS * C_NOPE)
    wkvvt = wkv[:, :, C_NOPE:].reshape(KV_LORA, C_HEADS * C_V).T
    inv_freq = ROPE_THETA ** (-jnp.arange(0, C_ROPE, 2, dtype=F32) / C_ROPE)
    invfc = jnp.broadcast_to(inv_freq[:, None], (C_ROPE // 2, tm))
    head_of = np.arange(C_HEADS * C_NOPE) // C_NOPE
    splat = (head_of[:, None] == np.arange(LANES)[None, :]).astype(np.float32)
    rope_rows = np.broadcast_to((np.arange(LANES) < C_HEADS).astype(np.float32), (LANES, LANES))
    indk = np.concatenate([splat, rope_rows], axis=0)
    return dict(win=win.astype(BF16), wropet=w_in[:, Q_LORA + KV_LORA:].T.astype(BF16),
                qan=_row(q_a_norm), kvan=_row(kv_a_norm),
                wqbt=w_q_b.T.astype(BF16), wkvk=wkvk.astype(BF16), wkvvt=wkvvt.astype(BF16),
                gq=jnp.broadcast_to(q_norm.astype(F32)[:, None], (C_QK, tm)),
                gkr=jnp.broadcast_to(k_norm.astype(F32)[C_NOPE:, None], (C_ROPE, tm)),
                kn=_row(k_norm[:C_NOPE]), invfc=invfc,
                indk=jnp.asarray(indk, BF16), splat=jnp.asarray(splat.T, BF16))


def kernel(x, p, positions, norm_gains, ffn_w_gate, ffn_w_up, ffn_w_down, ple_w_gate, ple_w_proj,
           even_w_in, gmlp_v_norm, gmlp_w_s, gmlp_b_s, hgrn_lb_raw, hgrn_out_norm, even_w_out,
           mla_w_in, mla_q_a_norm, mla_kv_a_norm, mla_w_q_b, mla_w_kv_b, mla_q_norm, mla_k_norm, mla_w_out):
    bsz, seq, d = x.shape
    depth = norm_gains.shape[0]
    n = bsz * seq
    assert d == D_MODEL and seq % ATTN_TILE == 0 and seq % TOKEN_TILE == 0 and seq % HGRN_TILE == 0
    assert bsz % HGRN_BATCH == 0

    lb_sm = jax.nn.softmax(hgrn_lb_raw.astype(F32), axis=0)
    lower_bounds = jnp.clip(jnp.cumsum(lb_sm, axis=0) - lb_sm[0], 0.0, LB_MAX)
    posr = positions.astype(F32).reshape(bsz, 1, seq)

    ffn = (ffn_w_gate, ffn_w_up, ffn_w_down)
    ple_wg, ple_wp = ple_w_gate.astype(BF16), ple_w_proj.astype(BF16)
    even_win, even_wout, mla_wout = (w.astype(BF16) for w in (even_w_in, even_w_out, mla_w_out))
    p3d = p.reshape(depth, n, D_PLE)

    xc = x.reshape(n, d)
    for i in range(depth):
        g = norm_gains[i]
        j = i // 2
        if i % 2 == 0:
            ep = _even_params(gmlp_v_norm[j], gmlp_w_s[j], gmlp_b_s[j], lower_bounds[j], hgrn_out_norm[j])
            outs = _stage1_even(xc, _row(g[0]), ffn, (i, 0), _row(g[1]), even_win, j)
            xc = outs[0]
            parts = [o.reshape(bsz, seq, A_WIDTH) for o in outs[1:]]
            mixed = _even_core(parts, ep["vnorm"], ep["ind"], ep["wcat"], ep["bfull"], ep["lb"],
                               ep["onorm"], ep["ltri"], ep["lvl"])
            w_out = even_wout
        else:
            op = _odd_params(mla_w_in[j], mla_q_a_norm[j], mla_kv_a_norm[j], mla_w_q_b[j], mla_w_kv_b[j],
                             mla_q_norm[j], mla_k_norm[j])
            x1, qt, k, vt = _stage1_odd(xc.reshape(bsz, seq, d), posr, op, _row(g[0]), ffn, (i, 0),
                                        _row(g[1]))
            xc = x1.reshape(n, d)
            mixed = _attention(qt, k, vt)
            w_out = mla_wout
        xc = _stage2(xc, mixed.reshape(n, d), p3d, i, w_out, j, _row(g[2]), ffn, (i, 1), _row(g[3]),
                     ple_wg, ple_wp, _row(g[4]))
    return xc.reshape(bsz, seq, d)
```

```python
import functools
import math

import numpy as np
import jax
import jax.numpy as jnp
from jax import lax
from jax.experimental import pallas as pl
from jax.experimental.pallas import tpu as pltpu

F32 = jnp.float32
BF16 = jnp.bfloat16

D_MODEL = 1024
D_FF = 2816
D_PLE = 256
EPS = 1e-6
NEG_BIG = -1e30
F_MIN = 1e-6
LB_MAX = 0.999

A_HEADS = 8
A_HEAD_DIM = 64
A_WIDTH = A_HEADS * A_HEAD_DIM
A_CHUNK = 128
B_HEADS = 4
B_DIM = 128
B_WIDTH = B_HEADS * B_DIM
EVEN_PARTS = 6

C_HEADS = 8
C_NOPE = 128
C_ROPE = 64
C_V = 128
C_QK = C_NOPE + C_ROPE
Q_LORA = 384
KV_LORA = 256
ROPE_THETA = 10000.0
ATTN_SCALE = C_QK ** -0.5
LOG2E = math.log2(math.e)

LANES = 128
SUBLANES = 8
ODD_IN_PAD = Q_LORA + KV_LORA + LANES
VMEM_LIMIT_BYTES = 60000 * 1024

TOKEN_TILE = 512
FFN_STAGE_ROWS_IN = 128
FFN_STAGE_ROWS_OUT = 256
FFN_STAGE_SLOTS = 4
HGRN_TILE = 256
HGRN_BATCH = 2
HGRN_SUB = HGRN_TILE // 2
HGRN_LEVELS = (64, 32, 16, 8)
ATTN_TILE = 512
ATTN_HEADS_PER_STEP = 4


def _sigmoid(x):
    return 0.5 + 0.5 * jnp.tanh(0.5 * x)


def _silu(x):
    t = 0.5 * x
    return t + t * jnp.tanh(t)


def _gelu_tanh(x):
    c = math.sqrt(2.0 / math.pi)
    hx = 0.5 * x
    return hx + hx * jnp.tanh(x * (c + (c * 0.044715) * (x * x)))


def _rms(x, g):
    ms = jnp.mean(x * x, axis=-1, keepdims=True)
    return x * lax.rsqrt(ms + EPS) * g


def _dot(a, b):
    return jnp.dot(a, b, preferred_element_type=F32)


def _dot_nt(a, b):
    return lax.dot_general(a, b, (((1,), (1,)), ((), ())), preferred_element_type=F32)


def _dot_tn(a, b):
    return lax.dot_general(a, b, (((0,), (0,)), ((), ())), preferred_element_type=F32)


def _split2(x):
    hi = x.astype(BF16)
    lo = (x - hi.astype(F32)).astype(BF16)
    return hi, lo


def _swiglu(hn, wg_ref, wu_ref, wd_ref):
    act = (_silu(_dot(hn, wg_ref[...])) * _dot(hn, wu_ref[...])).astype(BF16)
    return _dot(act, wd_ref[...])


def _s1_even_kernel(ffn_layer, x_ref, g0_ref, wg_hbm, wu_hbm, wd_hbm, g1_ref, win_ref,
                    x1_ref, u_ref, v_ref, q_ref, f_ref, i_ref, gg_ref, wg_ref, wu_ref, wd_ref, *stage):
    @pl.when(pl.program_id(0) == 0)
    def _():
        _load_ffn(ffn_layer, wg_hbm, wu_hbm, wd_hbm, wg_ref, wu_ref, wd_ref, *stage)

    x = x_ref[...]
    hn = _rms(x, g0_ref[...]).astype(BF16)
    x1 = x + 0.5 * _swiglu(hn, wg_ref, wu_ref, wd_ref)
    x1_ref[...] = x1
    h = _rms(x1, g1_ref[...]).astype(BF16)
    outs = (u_ref, v_ref, q_ref, f_ref, i_ref, gg_ref)
    for k, o_ref in enumerate(outs):
        z = _dot(h, win_ref[:, k * A_WIDTH:(k + 1) * A_WIDTH])
        o_ref[...] = z.astype(o_ref.dtype)


def _const_spec(shape, prefix=()):
    nd = len(shape)
    return pl.BlockSpec((None,) * len(prefix) + tuple(shape), lambda *_: tuple(prefix) + (0,) * nd,
                        pipeline_mode=pl.Buffered(1))


def _ffn_specs():
    return [pl.BlockSpec(memory_space=pl.ANY)] * 3


def _ffn_scratch():
    return [pltpu.VMEM((D_MODEL, D_FF), BF16), pltpu.VMEM((D_MODEL, D_FF), BF16), pltpu.VMEM((D_FF, D_MODEL), BF16),
            pltpu.VMEM((FFN_STAGE_SLOTS, FFN_STAGE_ROWS_IN, D_FF), F32),
            pltpu.VMEM((FFN_STAGE_SLOTS, FFN_STAGE_ROWS_OUT, D_MODEL), F32),
            pltpu.SemaphoreType.DMA((FFN_STAGE_SLOTS,))]


def _load_cast(src_hbm, layer, dst_ref, stage_ref, sem_ref):
    slots, rows = stage_ref.shape[0], stage_ref.shape[1]
    n_chunks = dst_ref.shape[0] // rows
    ahead = slots - 1

    def copy(c):
        return pltpu.make_async_copy(src_hbm.at[layer[0], layer[1], pl.ds(c * rows, rows)],
                                     stage_ref.at[c % slots], sem_ref.at[c % slots])

    for c in range(min(ahead, n_chunks)):
        copy(c).start()
    for c in range(n_chunks):
        if c + ahead < n_chunks:
            copy(c + ahead).start()
        copy(c).wait()
        dst_ref[c * rows:(c + 1) * rows, :] = stage_ref[c % slots].astype(BF16)


def _load_ffn(layer, wg_hbm, wu_hbm, wd_hbm, wg_ref, wu_ref, wd_ref, stage_in_ref, stage_out_ref, sem_ref):
    _load_cast(wg_hbm, layer, wg_ref, stage_in_ref, sem_ref)
    _load_cast(wu_hbm, layer, wu_ref, stage_in_ref, sem_ref)
    _load_cast(wd_hbm, layer, wd_ref, stage_out_ref, sem_ref)


def _stage1_even(x2d, g0, ffn, ffn_layer, g1, win, j):
    n = x2d.shape[0]
    tm = TOKEN_TILE
    row = lambda i: (i, 0)
    part_spec = pl.BlockSpec((tm, A_WIDTH), row)
    out_shape = [jax.ShapeDtypeStruct((n, D_MODEL), F32)]
    out_shape += [jax.ShapeDtypeStruct((n, A_WIDTH), F32 if k == 3 else BF16) for k in range(EVEN_PARTS)]
    return pl.pallas_call(
        functools.partial(_s1_even_kernel, ffn_layer),
        grid=(n // tm,),
        in_specs=[pl.BlockSpec((tm, D_MODEL), row), _const_spec((1, D_MODEL))] + _ffn_specs() + [
                  _const_spec((1, D_MODEL)), _const_spec((D_MODEL, EVEN_PARTS * A_WIDTH), (j,))],
        out_specs=[pl.BlockSpec((tm, D_MODEL), row)] + [part_spec] * EVEN_PARTS,
        out_shape=out_shape,
        scratch_shapes=_ffn_scratch(),
        compiler_params=pltpu.CompilerParams(dimension_semantics=("arbitrary",),
                                             vmem_limit_bytes=VMEM_LIMIT_BYTES),
        name="stage1_even",
    )(x2d, g0, *ffn, g1, win)


def _s1_odd_kernel(ffn_layer, x_ref, posr_ref, invfc_ref, g0_ref, wg_hbm, wu_hbm, wd_hbm, g1_ref,
                   win_ref, wropet_ref, qan_ref, kvan_ref, wqbt_ref, wkvk_ref, wkvvt_ref, gq_ref, gkr_ref, kn_ref,
                   indk_ref, splat_ref,
                   x1_ref, qt_ref, k_ref, vt_ref, wg_ref, wu_ref, wd_ref, *stage):
    @pl.when((pl.program_id(0) == 0) & (pl.program_id(1) == 0))
    def _():
        _load_ffn(ffn_layer, wg_hbm, wu_hbm, wd_hbm, wg_ref, wu_ref, wd_ref, *stage)

    x = x_ref[0]
    hn = _rms(x, g0_ref[...]).astype(BF16)
    x1 = x + 0.5 * _swiglu(hn, wg_ref, wu_ref, wd_ref)
    x1_ref[0] = x1
    h = _rms(x1, g1_ref[...]).astype(BF16)
    z = _dot(h, win_ref[...])
    cqn = _rms(z[:, :Q_LORA], qan_ref[...]).astype(BF16)
    ckvn = _rms(z[:, Q_LORA:Q_LORA + KV_LORA], kvan_ref[...]).astype(BF16)
    k_rope = z[:, Q_LORA + KV_LORA:]
    q_t = _dot_nt(wqbt_ref[...], cqn)
    k_nope = _dot(ckvn, wkvk_ref[...])
    vt_ref[0] = _dot_nt(wkvvt_ref[...], ckvn).astype(BF16)

    ang_t = invfc_ref[...] * posr_ref[0]
    cos_f, sin_f = jnp.cos(ang_t), jnp.sin(ang_t)
    gq = gq_ref[...]
    half = C_ROPE // 2
    for hd in range(C_HEADS):
        blk = q_t[hd * C_QK:(hd + 1) * C_QK, :]
        ss = jnp.sum(blk * blk, axis=0, keepdims=True)
        r = lax.rsqrt(ss * (1.0 / C_QK) + EPS) * (ATTN_SCALE * LOG2E)
        sc = blk * r * gq
        a1 = sc[C_NOPE:C_NOPE + half, :]
        a2 = sc[C_NOPE + half:, :]
        qt_ref[0, hd, 0:C_NOPE, :] = sc[:C_NOPE, :].astype(BF16)
        qt_ref[0, hd, C_NOPE:C_NOPE + half, :] = (a1 * cos_f - a2 * sin_f).astype(BF16)
        qt_ref[0, hd, C_NOPE + half:C_QK, :] = (a2 * cos_f + a1 * sin_f).astype(BF16)

    kr_t = _dot_nt(wropet_ref[...], h) * gkr_ref[...]
    b1, b2 = kr_t[:half, :], kr_t[half:, :]
    rot_t = jnp.concatenate([b1 * cos_f - b2 * sin_f, b2 * cos_f + b1 * sin_f,
                             jnp.zeros((LANES - C_ROPE, kr_t.shape[1]), F32)], axis=0)
    kr_rot = rot_t.T

    lane = lax.broadcasted_iota(jnp.int32, (1, LANES), 1)
    kn_nope = kn_ref[...]
    k2_hi, k2_lo = _split2(jnp.concatenate([k_nope * k_nope, k_rope * k_rope], axis=1))
    ssk = _dot(k2_hi, indk_ref[...]) + _dot(k2_lo, indk_ref[...])
    rk = jnp.where(lane < C_HEADS, lax.rsqrt(ssk * (1.0 / C_QK) + EPS), 0.0)
    rk_hi, rk_lo = _split2(rk)
    rk_b = _dot(rk_hi, splat_ref[...]) + _dot(rk_lo, splat_ref[...])
    for hd in range(C_HEADS):
        cols = slice(hd * C_NOPE, (hd + 1) * C_NOPE)
        k_ref[0, hd, :, 0:C_NOPE] = (k_nope[:, cols] * rk_b[:, cols] * kn_nope).astype(BF16)
        k_ref[0, hd, :, C_NOPE:C_QK] = (kr_rot * rk_b[:, cols])[:, :C_ROPE].astype(BF16)


def _stage1_odd(x3d, posr, op, g0, ffn, ffn_layer, g1):
    bsz, seq, _ = x3d.shape
    tm = TOKEN_TILE
    tile = lambda b, i: (b, i, 0)
    return pl.pallas_call(
        functools.partial(_s1_odd_kernel, ffn_layer),
        grid=(bsz, seq // tm),
        in_specs=[pl.BlockSpec((1, tm, D_MODEL), tile), pl.BlockSpec((1, 1, tm), lambda b, i: (b, 0, i)),
                  _const_spec((C_ROPE // 2, tm)), _const_spec((1, D_MODEL))]
                 + _ffn_specs() + [
                  _const_spec((1, D_MODEL)), _const_spec((D_MODEL, ODD_IN_PAD)), _const_spec((C_ROPE, D_MODEL)),
                  _const_spec((1, Q_LORA)), _const_spec((1, KV_LORA)),
                  _const_spec((C_HEADS * C_QK, Q_LORA)), _const_spec((KV_LORA, C_HEADS * C_NOPE)),
                  _const_spec((C_HEADS * C_V, KV_LORA)), _const_spec((C_QK, tm)), _const_spec((C_ROPE, tm)),
                  _const_spec((1, C_NOPE)), _const_spec((C_HEADS * C_NOPE + LANES, LANES)),
                  _const_spec((LANES, C_HEADS * C_NOPE))],
        out_specs=[pl.BlockSpec((1, tm, D_MODEL), tile),
                   pl.BlockSpec((1, C_HEADS, C_QK, tm), lambda b, i: (b, 0, 0, i)),
                   pl.BlockSpec((1, C_HEADS, tm, C_QK), lambda b, i: (b, 0, i, 0)),
                   pl.BlockSpec((1, C_HEADS * C_V, tm), lambda b, i: (b, 0, i))],
        out_shape=[jax.ShapeDtypeStruct((bsz, seq, D_MODEL), F32),
                   jax.ShapeDtypeStruct((bsz, C_HEADS, C_QK, seq), BF16),
                   jax.ShapeDtypeStruct((bsz, C_HEADS, seq, C_QK), BF16),
                   jax.ShapeDtypeStruct((bsz, C_HEADS * C_V, seq), BF16)],
        scratch_shapes=_ffn_scratch(),
        compiler_params=pltpu.CompilerParams(dimension_semantics=("arbitrary", "arbitrary"),
                                             vmem_limit_bytes=VMEM_LIMIT_BYTES),
        name="stage1_odd",
    )(x3d, posr, op["invfc"], g0, *ffn, g1, op["win"], op["wropet"], op["qan"], op["kvan"],
      op["wqbt"], op["wkvk"], op["wkvvt"], op["gq"], op["gkr"], op["kn"], op["indk"], op["splat"])


def _s2_kernel(ffn_layer, x_ref, m_ref, p_ref, wo_ref, g2_ref, wg_hbm, wu_hbm, wd_hbm,
               g3_ref, wpg_ref, wpp_ref, g4_ref, o_ref, wg_ref, wu_ref, wd_ref, *stage):
    @pl.when(pl.program_id(0) == 0)
    def _():
        _load_ffn(ffn_layer, wg_hbm, wu_hbm, wd_hbm, wg_ref, wu_ref, wd_ref, *stage)

    x = x_ref[...] + _dot(m_ref[...], wo_ref[...])
    hn = _rms(x, g2_ref[...]).astype(BF16)
    x = x + 0.5 * _swiglu(hn, wg_ref, wu_ref, wd_ref)
    h3 = _rms(x, g3_ref[...]).astype(BF16)
    gate = _sigmoid(_dot(h3, wpg_ref[...]))
    proj = _dot(p_ref[...].astype(BF16), wpp_ref[...])
    o_ref[...] = x + _rms(gate * proj, g4_ref[...])


def _stage2(x2d, mixed, p3d, layer, wo, j, g2, ffn, ffn_layer, g3, wpg, wpp, g4):
    n = x2d.shape[0]
    tm = TOKEN_TILE
    row = lambda i: (i, 0)
    return pl.pallas_call(
        functools.partial(_s2_kernel, ffn_layer),
        grid=(n // tm,),
        in_specs=[pl.BlockSpec((tm, D_MODEL), row), pl.BlockSpec((tm, D_MODEL), row),
                  pl.BlockSpec((None, tm, D_PLE), lambda i: (layer, i, 0)),
                  _const_spec((D_MODEL, D_MODEL), (j,)), _const_spec((1, D_MODEL))] + _ffn_specs() + [
                  _const_spec((1, D_MODEL)), _const_spec((D_MODEL, D_MODEL), (layer,)),
                  _const_spec((D_PLE, D_MODEL), (layer,)), _const_spec((1, D_MODEL))],
        out_specs=pl.BlockSpec((tm, D_MODEL), row),
        out_shape=jax.ShapeDtypeStruct((n, D_MODEL), F32),
        scratch_shapes=_ffn_scratch(),
        compiler_params=pltpu.CompilerParams(dimension_semantics=("arbitrary",),
                                             vmem_limit_bytes=VMEM_LIMIT_BYTES),
        name="stage2",
    )(x2d, mixed, p3d, wo, g2, *ffn, g3, wpg, wpp, g4)


def _level_map(ts):
    t = np.arange(ts)[:, None]
    s = np.arange(ts)[None, :]
    lvl = np.full((ts, ts), len(HGRN_LEVELS) + 1, np.int32)
    lvl[(t // SUBLANES == s // SUBLANES) & (s <= t)] = len(HGRN_LEVELS)
    for li, h in enumerate(HGRN_LEVELS):
        own = (t // (2 * h) == s // (2 * h)) & (t % (2 * h) >= h) & (s % (2 * h) < h)
        lvl[own] = li
    return lvl


def _even_core_kernel(u_ref, v_ref, q_ref, f_ref, i_ref, g_ref,
                      vnorm_ref, ind_ref, wcat_ref, bfull_ref, lb_ref, onorm_ref, ltri_ref, lvl_ref,
                      o_ref, state_ref):
    nb = HGRN_BATCH
    ts = HGRN_TILE
    sub = HGRN_SUB
    n_lvl = len(HGRN_LEVELS)
    rows_all = nb * ts

    def stacked(ref):
        return ref[...].reshape(rows_all, ref.shape[-1])

    u = _gelu_tanh(stacked(u_ref).astype(F32))
    v = _gelu_tanh(stacked(v_ref).astype(F32))
    ss = _dot((v * v).astype(BF16), ind_ref[...])
    vn = (v * lax.rsqrt(ss * (1.0 / A_HEAD_DIM) + EPS) * vnorm_ref[...]).astype(BF16)
    lane_head = lax.broadcasted_iota(jnp.int32, (1, A_WIDTH), 1) // A_HEAD_DIM
    for c in range(rows_all // A_CHUNK):
        rows = slice(c * A_CHUNK, (c + 1) * A_CHUNK)
        vc = vn[rows, :]
        zero = jnp.zeros_like(vc)
        vbd = jnp.concatenate([jnp.where(lane_head == hd, vc, zero) for hd in range(A_HEADS)], axis=0)
        mixed = _dot(wcat_ref[...], vbd) + bfull_ref[...]
        bi, local = divmod(c * A_CHUNK, ts)
        o_ref[bi, local:local + A_CHUNK, 0:A_WIDTH] = (u[rows, :] * mixed).astype(BF16)

    @pl.when(pl.program_id(1) == 0)
    def _():
        state_ref[...] = jnp.zeros_like(state_ref)

    lb = lb_ref[...]
    fg = lb + (1.0 - lb) * _sigmoid(stacked(f_ref))
    lf = jnp.log2(jnp.maximum(fg, F_MIN))
    kk = 1.0 - fg
    qq = stacked(q_ref).astype(F32)
    lf_hi, lf_lo = _split2(lf)
    ltri = ltri_ref[...]
    seqs = [slice(bi * ts, (bi + 1) * ts) for bi in range(nb)]
    b = jnp.concatenate([_dot(ltri, lf_hi[r, :]) + _dot(ltri, lf_lo[r, :]) for r in seqs], axis=0)

    q_lvls, k_lvls = [], []
    for h in HGRN_LEVELS:
        q_parts, k_parts = [], []
        zero = jnp.zeros((h, B_WIDTH), F32)
        for j in range(rows_all // (2 * h)):
            r0 = j * 2 * h
            bm = b[r0 + h - 1:r0 + h, :]
            lo_rows = slice(r0, r0 + h)
            up_rows = slice(r0 + h, r0 + 2 * h)
            k_parts += [kk[lo_rows, :] * jnp.exp2(bm - b[lo_rows, :]), zero]
            q_parts += [zero, qq[up_rows, :] * jnp.exp2(b[up_rows, :] - bm)]
        q_lvls.append(jnp.concatenate(q_parts, axis=0).astype(BF16))
        k_lvls.append(jnp.concatenate(k_parts, axis=0).astype(BF16))
    b3 = b.reshape(rows_all // SUBLANES, SUBLANES, B_WIDTH)
    bref = jnp.broadcast_to(b3[:, 3:4, :], b3.shape).reshape(rows_all, B_WIDTH)
    q_lvls.append((qq * jnp.exp2(b - bref)).astype(BF16))
    k_lvls.append((kk * jnp.exp2(bref - b)).astype(BF16))

    q_top, k_top, k_out, carry = [], [], [], []
    for bi in range(nb):
        r0 = bi * ts
        bm = b[r0 + sub - 1:r0 + sub, :]
        q_top.append((qq[r0 + sub:r0 + ts, :] * jnp.exp2(b[r0 + sub:r0 + ts, :] - bm)).astype(BF16))
        k_top.append((kk[r0:r0 + sub, :] * jnp.exp2(bm - b[r0:r0 + sub, :])).astype(BF16))
        b_last = b[r0 + ts - 1:r0 + ts, :]
        k_out.append((kk[seqs[bi], :] * jnp.exp2(b_last - b[seqs[bi], :])).astype(BF16))
        carry.append(jnp.exp2(b_last))
    q_in = (qq * jnp.exp2(b)).astype(BF16)
    vv = stacked(i_ref)
    gate = _silu(stacked(g_ref).astype(F32))
    onorm = onorm_ref[...]

    lvl = lvl_ref[...]
    masks = [lvl == li for li in range(n_lvl + 1)]
    units = [(bi, hd, slice(hd * B_DIM, (hd + 1) * B_DIM)) for hd in range(B_HEADS) for bi in range(nb)]
    states = {(bi, hd): state_ref[bi, hd] for bi, hd, _ in units}
    inter = {(bi, hd): _dot_nt(q_in[seqs[bi], cols], states[bi, hd].astype(BF16)) for bi, hd, cols in units}
    for bi, hd, cols in units:
        state_ref[bi, hd] = (states[bi, hd] * carry[bi][:, cols]
                             + _dot_tn(vv[seqs[bi], cols], k_out[bi][:, cols]))
    score_blocks = {}
    for bi, hd, cols in units:
        diag = []
        for a in range(ts // sub):
            rows = slice(bi * ts + a * sub, bi * ts + (a + 1) * sub)
            blk = jnp.where(masks[n_lvl], _dot_nt(q_lvls[n_lvl][rows, cols], k_lvls[n_lvl][rows, cols]), 0.0)
            for li in range(n_lvl):
                blk = jnp.where(masks[li], _dot_nt(q_lvls[li][rows, cols], k_lvls[li][rows, cols]), blk)
            diag.append(blk.astype(BF16))
        top = _dot_nt(q_top[bi][:, cols], k_top[bi][:, cols]).astype(BF16)
        score_blocks[bi, hd] = (diag[0], jnp.concatenate([top, diag[1]], axis=1))
    outs = {}
    for bi, hd, cols in units:
        v_h = vv[seqs[bi], cols]
        first, second = score_blocks[bi, hd]
        o_intra = jnp.concatenate([_dot(first, v_h[:sub, :]), _dot(second, v_h)], axis=0)
        outs[bi, hd] = o_intra + inter[bi, hd]
    for bi, hd, cols in units:
        on = _rms(outs[bi, hd], onorm[:, cols]) * gate[seqs[bi], cols]
        o_ref[bi, :, A_WIDTH + hd * B_DIM:A_WIDTH + (hd + 1) * B_DIM] = on.astype(BF16)


def _even_core(parts, vnorm, ind, wcat, bfull, lb, onorm, ltri, lvl):
    bsz, seq, _ = parts[0].shape
    ts = HGRN_TILE
    nb = HGRN_BATCH
    tile = lambda b, i: (b, i, 0)
    part_spec = pl.BlockSpec((nb, ts, A_WIDTH), tile)
    return pl.pallas_call(
        _even_core_kernel,
        grid=(bsz // nb, seq // ts),
        in_specs=[part_spec] * EVEN_PARTS + [
            _const_spec((1, A_WIDTH)), _const_spec((A_WIDTH, A_WIDTH)),
            _const_spec((A_CHUNK, A_HEADS * A_CHUNK)), _const_spec((A_CHUNK, A_WIDTH)),
            _const_spec((1, B_WIDTH)), _const_spec((1, B_WIDTH)),
            _const_spec((ts, ts)), _const_spec((HGRN_SUB, HGRN_SUB))],
        out_specs=pl.BlockSpec((nb, ts, D_MODEL), tile),
        out_shape=jax.ShapeDtypeStruct((bsz, seq, D_MODEL), BF16),
        scratch_shapes=[pltpu.VMEM((nb, B_HEADS, B_DIM, B_DIM), F32)],
        compiler_params=pltpu.CompilerParams(dimension_semantics=("parallel", "arbitrary"),
                                             vmem_limit_bytes=VMEM_LIMIT_BYTES),
        name="even_core",
    )(*parts, vnorm, ind, wcat, bfull, lb, onorm, ltri, lvl)


def _attn_kernel(qt_ref, k_ref, vt_ref, o_ref, m_ref, l_ref, acc_ref, sa_ref, sb_ref, mxa_ref, mxb_ref):
    tq = ATTN_TILE
    groups = tq // SUBLANES
    qi = pl.program_id(2)
    heads = range(ATTN_HEADS_PER_STEP)
    buf_a, buf_b = (sa_ref, mxa_ref), (sb_ref, mxb_ref)

    m_ref[...] = jnp.full_like(m_ref, NEG_BIG)
    l_ref[...] = jnp.zeros_like(l_ref)
    acc_ref[...] = jnp.zeros_like(acc_ref)

    def keys(j):
        return pl.ds(pl.multiple_of(j * tq, tq), tq)

    def scores(j, buf, heads=heads):
        s_ref, mx_ref = buf
        for hd in heads:
            s = _dot(k_ref[0, hd, keys(j), :], qt_ref[0, hd])
            s_ref[hd] = s
            mx_ref[hd] = jnp.max(s.reshape(groups, SUBLANES, tq), axis=0)

    def consume(j, buf, masked, heads=heads):
        s_ref, mx_ref = buf
        if masked:
            keep = (lax.broadcasted_iota(jnp.int32, (tq, tq), 0) <= lax.broadcasted_iota(jnp.int32, (tq, tq), 1))
        s3, m_new, alpha = {}, {}, {}
        for hd in heads:
            if masked:
                s3[hd] = jnp.where(keep, s_ref[hd], NEG_BIG).reshape(groups, SUBLANES, tq)
                cmax8 = jnp.max(s3[hd], axis=0)
            else:
                cmax8 = mx_ref[hd]
            cmax = jnp.max(cmax8, axis=0, keepdims=True)
            m_prev = m_ref[hd]
            m_new[hd] = jnp.maximum(m_prev, jnp.broadcast_to(cmax, (SUBLANES, tq)))
            alpha[hd] = jnp.exp2(m_prev - m_new[hd])
            m_ref[hd] = m_new[hd]
        ones = jnp.ones((2 * SUBLANES, tq), BF16)
        for hd in heads:
            sv = s3[hd] if masked else s_ref[hd].reshape(groups, SUBLANES, tq)
            p = jnp.exp2(sv - m_new[hd][None]).reshape(tq, tq).astype(BF16)
            lhs = jnp.concatenate([vt_ref[0, hd * C_V:(hd + 1) * C_V, keys(j)], ones], axis=0)
            pv = _dot(lhs, p)
            acc3 = acc_ref[hd].reshape(C_V // SUBLANES, SUBLANES, tq) * alpha[hd][None]
            acc_ref[hd] = acc3.reshape(C_V, tq) + pv[:C_V, :]
            l_ref[hd] = alpha[hd] * l_ref[hd] + pv[C_V:C_V + SUBLANES, :]

    scores(0, buf_a)
    pairs = qi // 2

    def body(t, carry):
        j = 2 * t
        for hd in heads:
            scores(j + 1, buf_b, [hd])
            consume(j, buf_a, False, [hd])
        for hd in heads:
            scores(j + 2, buf_a, [hd])
            consume(j + 1, buf_b, False, [hd])
        return carry

    lax.fori_loop(0, pairs, body, 0)

    @pl.when(qi % 2 == 0)
    def _():
        consume(qi, buf_a, True)

    @pl.when(qi % 2 == 1)
    def _():
        for hd in heads:
            scores(qi, buf_b, [hd])
            consume(qi - 1, buf_a, False, [hd])
        consume(qi, buf_b, True)
    for hd in range(ATTN_HEADS_PER_STEP):
        inv_l = 1.0 / l_ref[hd]
        out_t = acc_ref[hd].reshape(C_V // SUBLANES, SUBLANES, tq) * inv_l[None]
        o_ref[0, :, hd * C_V:(hd + 1) * C_V] = out_t.reshape(C_V, tq).T.astype(BF16)


def _attention(qt, k, vt):
    bsz, heads, seq, _ = k.shape
    tq = ATTN_TILE
    g = ATTN_HEADS_PER_STEP
    return pl.pallas_call(
        _attn_kernel,
        grid=(bsz, heads // g, seq // tq),
        in_specs=[pl.BlockSpec((1, g, C_QK, tq), lambda b, h, i: (b, h, 0, i)),
                  pl.BlockSpec((1, g, seq, C_QK), lambda b, h, i: (b, h, 0, 0)),
                  pl.BlockSpec((1, g * C_V, seq), lambda b, h, i: (b, h, 0))],
        out_specs=pl.BlockSpec((1, tq, g * C_V), lambda b, h, i: (b, i, h)),
        out_shape=jax.ShapeDtypeStruct((bsz, seq, heads * C_V), BF16),
        scratch_shapes=[pltpu.VMEM((g, SUBLANES, tq), F32), pltpu.VMEM((g, SUBLANES, tq), F32),
                        pltpu.VMEM((g, C_V, tq), F32), pltpu.VMEM((g, tq, tq), F32),
                        pltpu.VMEM((g, tq, tq), F32), pltpu.VMEM((g, SUBLANES, tq), F32),
                        pltpu.VMEM((g, SUBLANES, tq), F32)],
        compiler_params=pltpu.CompilerParams(dimension_semantics=("parallel", "parallel", "arbitrary"),
                                             vmem_limit_bytes=VMEM_LIMIT_BYTES),
        name="mla_attention",
    )(qt, k, vt)


def _row(v):
    return v.reshape(1, -1).astype(F32)


def _pad_cols(w, total):
    return jnp.pad(w, ((0, 0), (0, total - w.shape[1])))


def _even_params(v_norm, w_s, b_s, lb, out_norm):
    causal = np.tril(np.ones((A_CHUNK, A_CHUNK), bool))
    w = jnp.where(causal[None], w_s, 0.0)
    wcat = jnp.transpose(w, (1, 0, 2)).reshape(A_CHUNK, A_HEADS * A_CHUNK)
    bfull = jnp.repeat(b_s.T, A_HEAD_DIM, axis=1)
    head_of = np.arange(A_WIDTH) // A_HEAD_DIM
    ind = (head_of[:, None] == head_of[None, :]).astype(np.float32)
    ltri = np.tril(np.ones((HGRN_TILE, HGRN_TILE), np.float32))
    return dict(vnorm=_row(v_norm), ind=jnp.asarray(ind, BF16),
                wcat=wcat.astype(BF16), bfull=bfull.astype(F32), lb=_row(lb),
                onorm=_row(jnp.tile(out_norm, B_HEADS)), ltri=jnp.asarray(ltri, BF16),
                lvl=jnp.asarray(_level_map(HGRN_SUB)))


def _odd_params(w_in, q_a_norm, kv_a_norm, w_q_b, w_kv_b, q_norm, k_norm):
    tm = TOKEN_TILE
    win = _pad_cols(w_in, ODD_IN_PAD)
    wkv = w_kv_b.reshape(KV_LORA, C_HEADS, C_NOPE + C_V)
    wkvk = wkv[:, :, :C_NOPE].reshape(KV_LORA, C_HEADS * C_NOPE)
    wkvvt = wkv[:, :, C_NOPE:].reshape(KV_LORA, C_HEADS * C_V).T
    inv_freq = ROPE_THETA ** (-jnp.arange(0, C_ROPE, 2, dtype=F32) / C_ROPE)
    invfc = jnp.broadcast_to(inv_freq[:, None], (C_ROPE // 2, tm))
    head_of = np.arange(C_HEADS * C_NOPE) // C_NOPE
    splat = (head_of[:, None] == np.arange(LANES)[None, :]).astype(np.float32)
    rope_rows = np.broadcast_to((np.arange(LANES) < C_HEADS).astype(np.float32), (LANES, LANES))
    indk = np.concatenate([splat, rope_rows], axis=0)
    return dict(win=win.astype(BF16), wropet=w_in[:, Q_LORA + KV_LORA:].T.astype(BF16),
                qan=_row(q_a_norm), kvan=_row(kv_a_norm),
                wqbt=w_q_b.T.astype(BF16), wkvk=wkvk.astype(BF16), wkvvt=wkvvt.astype(BF16),
                gq=jnp.broadcast_to(q_norm.astype(F32)[:, None], (C_QK, tm)),
                gkr=jnp.broadcast_to(k_norm.astype(F32)[C_NOPE:, None], (C_ROPE, tm)),
                kn=_row(k_norm[:C_NOPE]), invfc=invfc,
                indk=jnp.asarray(indk, BF16), splat=jnp.asarray(splat.T, BF16))


def kernel(x, p, positions, norm_gains, ffn_w_gate, ffn_w_up, ffn_w_down, ple_w_gate, ple_w_proj,
           even_w_in, gmlp_v_norm, gmlp_w_s, gmlp_b_s, hgrn_lb_raw, hgrn_out_norm, even_w_out,
           mla_w_in, mla_q_a_norm, mla_kv_a_norm, mla_w_q_b, mla_w_kv_b, mla_q_norm, mla_k_norm, mla_w_out):
    bsz, seq, d = x.shape
    depth = norm_gains.shape[0]
    n = bsz * seq
    assert d == D_MODEL and seq % ATTN_TILE == 0 and seq % TOKEN_TILE == 0 and seq % HGRN_TILE == 0
    assert bsz % HGRN_BATCH == 0

    lb_sm = jax.nn.softmax(hgrn_lb_raw.astype(F32), axis=0)
    lower_bounds = jnp.clip(jnp.cumsum(lb_sm, axis=0) - lb_sm[0], 0.0, LB_MAX)
    posr = positions.astype(F32).reshape(bsz, 1, seq)

    ffn = (ffn_w_gate, ffn_w_up, ffn_w_down)
    ple_wg, ple_wp = ple_w_gate.astype(BF16), ple_w_proj.astype(BF16)
    even_win, even_wout, mla_wout = (w.astype(BF16) for w in (even_w_in, even_w_out, mla_w_out))
    p3d = p.reshape(depth, n, D_PLE)

    xc = x.reshape(n, d)
    for i in range(depth):
        g = norm_gains[i]
        j = i // 2
        if i % 2 == 0:
            ep = _even_params(gmlp_v_norm[j], gmlp_w_s[j], gmlp_b_s[j], lower_bounds[j], hgrn_out_norm[j])
            outs = _stage1_even(xc, _row(g[0]), ffn, (i, 0), _row(g[1]), even_win, j)
            xc = outs[0]
            parts = [o.reshape(bsz, seq, A_WIDTH) for o in outs[1:]]
            mixed = _even_core(parts, ep["vnorm"], ep["ind"], ep["wcat"], ep["bfull"], ep["lb"],
                               ep["onorm"], ep["ltri"], ep["lvl"])
            w_out = even_wout
        else:
            op = _odd_params(mla_w_in[j], mla_q_a_norm[j], mla_kv_a_norm[j], mla_w_q_b[j], mla_w_kv_b[j],
                             mla_q_norm[j], mla_k_norm[j])
            x1, qt, k, vt = _stage1_odd(xc.reshape(bsz, seq, d), posr, op, _row(g[0]), ffn, (i, 0),
                                        _row(g[1]))
            xc = x1.reshape(n, d)
            mixed = _attention(qt, k, vt)
            w_out = mla_wout
        xc = _stage2(xc, mixed.reshape(n, d), p3d, i, w_out, j, _row(g[2]), ffn, (i, 1), _row(g[3]),
                     ple_wg, ple_wp, _row(g[4]))
    return xc.reshape(bsz, seq, d)
```

```python
import functools
import math

import numpy as np
import jax
import jax.numpy as jnp
from jax import lax
from jax.experimental import pallas as pl
from jax.experimental.pallas import tpu as pltpu

F32 = jnp.float32
BF16 = jnp.bfloat16

D_MODEL = 1024
D_FF = 2816
D_PLE = 256
EPS = 1e-6
NEG_BIG = -1e30
F_MIN = 1e-6
LB_MAX = 0.999

A_HEADS = 8
A_HEAD_DIM = 64
A_WIDTH = A_HEADS * A_HEAD_DIM
A_CHUNK = 128
B_HEADS = 4
B_DIM = 128
B_WIDTH = B_HEADS * B_DIM
EVEN_PARTS = 6

C_HEADS = 8
C_NOPE = 128
C_ROPE = 64
C_V = 128
C_QK = C_NOPE + C_ROPE
Q_LORA = 384
KV_LORA = 256
ROPE_THETA = 10000.0
ATTN_SCALE = C_QK ** -0.5
LOG2E = math.log2(math.e)

LANES = 128
SUBLANES = 8
ODD_IN_PAD = Q_LORA + KV_LORA + LANES
VMEM_LIMIT_BYTES = 60000 * 1024

TOKEN_TILE = 512
FFN_STAGE_ROWS_IN = 128
FFN_STAGE_ROWS_OUT = 256
FFN_STAGE_SLOTS = 4
HGRN_TILE = 256
HGRN_BATCH = 2
HGRN_SUB = HGRN_TILE // 2
HGRN_LEVELS = (64, 32, 16, 8)
ATTN_TILE = 512
ATTN_HEADS_PER_STEP = 4


def _sigmoid(x):
    return 0.5 + 0.5 * jnp.tanh(0.5 * x)


def _silu(x):
    t = 0.5 * x
    return t + t * jnp.tanh(t)


def _gelu_tanh(x):
    c = math.sqrt(2.0 / math.pi)
    hx = 0.5 * x
    return hx + hx * jnp.tanh(x * (c + (c * 0.044715) * (x * x)))


def _rms(x, g):
    ms = jnp.mean(x * x, axis=-1, keepdims=True)
    return x * lax.rsqrt(ms + EPS) * g


def _dot(a, b):
    return jnp.dot(a, b, preferred_element_type=F32)


def _dot_nt(a, b):
    return lax.dot_general(a, b, (((1,), (1,)), ((), ())), preferred_element_type=F32)


def _dot_tn(a, b):
    return lax.dot_general(a, b, (((0,), (0,)), ((), ())), preferred_element_type=F32)


def _split2(x):
    hi = x.astype(BF16)
    lo = (x - hi.astype(F32)).astype(BF16)
    return hi, lo


def _swiglu(hn, wg_ref, wu_ref, wd_ref):
    act = (_silu(_dot(hn, wg_ref[...])) * _dot(hn, wu_ref[...])).astype(BF16)
    return _dot(act, wd_ref[...])


def _s1_even_kernel(ffn_layer, x_ref, g0_ref, wg_hbm, wu_hbm, wd_hbm, g1_ref, win_ref,
                    x1_ref, u_ref, v_ref, q_ref, f_ref, i_ref, gg_ref, wg_ref, wu_ref, wd_ref, *stage):
    @pl.when(pl.program_id(0) == 0)
    def _():
        _load_ffn(ffn_layer, wg_hbm, wu_hbm, wd_hbm, wg_ref, wu_ref, wd_ref, *stage)

    x = x_ref[...]
    hn = _rms(x, g0_ref[...]).astype(BF16)
    x1 = x + 0.5 * _swiglu(hn, wg_ref, wu_ref, wd_ref)
    x1_ref[...] = x1
    h = _rms(x1, g1_ref[...]).astype(BF16)
    outs = (u_ref, v_ref, q_ref, f_ref, i_ref, gg_ref)
    for k, o_ref in enumerate(outs):
        z = _dot(h, win_ref[:, k * A_WIDTH:(k + 1) * A_WIDTH])
        o_ref[...] = z.astype(o_ref.dtype)


def _const_spec(shape, prefix=()):
    nd = len(shape)
    return pl.BlockSpec((None,) * len(prefix) + tuple(shape), lambda *_: tuple(prefix) + (0,) * nd,
                        pipeline_mode=pl.Buffered(1))


def _ffn_specs():
    return [pl.BlockSpec(memory_space=pl.ANY)] * 3


def _ffn_scratch():
    return [pltpu.VMEM((D_MODEL, D_FF), BF16), pltpu.VMEM((D_MODEL, D_FF), BF16), pltpu.VMEM((D_FF, D_MODEL), BF16),
            pltpu.VMEM((FFN_STAGE_SLOTS, FFN_STAGE_ROWS_IN, D_FF), F32),
            pltpu.VMEM((FFN_STAGE_SLOTS, FFN_STAGE_ROWS_OUT, D_MODEL), F32),
            pltpu.SemaphoreType.DMA((FFN_STAGE_SLOTS,)), pltpu.SemaphoreType.DMA((FFN_STAGE_SLOTS,))]


class _CastRing:
    def __init__(self, layer, pairs, stage_ref, sem_ref):
        self.layer, self.stage_ref, self.sem_ref = layer, stage_ref, sem_ref
        self.slots, self.rows = stage_ref.shape[0], stage_ref.shape[1]
        self.chunks = [(src, dst, r0) for src, dst in pairs for r0 in range(0, dst.shape[0], self.rows)]

    def copy(self, c):
        src, _, r0 = self.chunks[c]
        return pltpu.make_async_copy(src.at[self.layer[0], self.layer[1], pl.ds(r0, self.rows)],
                                     self.stage_ref.at[c % self.slots], self.sem_ref.at[c % self.slots])

    def prime(self):
        for c in range(min(self.slots - 1, len(self.chunks))):
            self.copy(c).start()

    def drain(self):
        ahead = self.slots - 1
        for c, (_, dst, r0) in enumerate(self.chunks):
            if c + ahead < len(self.chunks):
                self.copy(c + ahead).start()
            self.copy(c).wait()
            dst[r0:r0 + self.rows, :] = self.stage_ref[c % self.slots].astype(BF16)


def _load_ffn(layer, wg_hbm, wu_hbm, wd_hbm, wg_ref, wu_ref, wd_ref, stage_in_ref, stage_out_ref,
              sem_in_ref, sem_out_ref):
    ring_in = _CastRing(layer, [(wg_hbm, wg_ref), (wu_hbm, wu_ref)], stage_in_ref, sem_in_ref)
    ring_out = _CastRing(layer, [(wd_hbm, wd_ref)], stage_out_ref, sem_out_ref)
    ring_in.prime()
    ring_out.prime()
    ring_in.drain()
    ring_out.drain()


def _stage1_even(x2d, g0, ffn, ffn_layer, g1, win, j):
    n = x2d.shape[0]
    tm = TOKEN_TILE
    row = lambda i: (i, 0)
    part_spec = pl.BlockSpec((tm, A_WIDTH), row)
    out_shape = [jax.ShapeDtypeStruct((n, D_MODEL), F32)]
    out_shape += [jax.ShapeDtypeStruct((n, A_WIDTH), F32 if k == 3 else BF16) for k in range(EVEN_PARTS)]
    return pl.pallas_call(
        functools.partial(_s1_even_kernel, ffn_layer),
        grid=(n // tm,),
        in_specs=[pl.BlockSpec((tm, D_MODEL), row), _const_spec((1, D_MODEL))] + _ffn_specs() + [
                  _const_spec((1, D_MODEL)), _const_spec((D_MODEL, EVEN_PARTS * A_WIDTH), (j,))],
        out_specs=[pl.BlockSpec((tm, D_MODEL), row)] + [part_spec] * EVEN_PARTS,
        out_shape=out_shape,
        scratch_shapes=_ffn_scratch(),
        compiler_params=pltpu.CompilerParams(dimension_semantics=("arbitrary",),
                                             vmem_limit_bytes=VMEM_LIMIT_BYTES),
        name="stage1_even",
    )(x2d, g0, *ffn, g1, win)


def _s1_odd_kernel(ffn_layer, x_ref, posr_ref, invfc_ref, g0_ref, wg_hbm, wu_hbm, wd_hbm, g1_ref,
                   win_ref, wropet_ref, qan_ref, kvan_ref, wqbt_ref, wkvk_ref, wkvvt_ref, gq_ref, gkr_ref, kn_ref,
                   indk_ref, splat_ref,
                   x1_ref, qt_ref, k_ref, vt_ref, wg_ref, wu_ref, wd_ref, *stage):
    @pl.when((pl.program_id(0) == 0) & (pl.program_id(1) == 0))
    def _():
        _load_ffn(ffn_layer, wg_hbm, wu_hbm, wd_hbm, wg_ref, wu_ref, wd_ref, *stage)

    x = x_ref[0]
    hn = _rms(x, g0_ref[...]).astype(BF16)
    x1 = x + 0.5 * _swiglu(hn, wg_ref, wu_ref, wd_ref)
    x1_ref[0] = x1
    h = _rms(x1, g1_ref[...]).astype(BF16)
    z = _dot(h, win_ref[...])
    cqn = _rms(z[:, :Q_LORA], qan_ref[...]).astype(BF16)
    ckvn = _rms(z[:, Q_LORA:Q_LORA + KV_LORA], kvan_ref[...]).astype(BF16)
    k_rope = z[:, Q_LORA + KV_LORA:]
    q_t = _dot_nt(wqbt_ref[...], cqn)
    k_nope = _dot(ckvn, wkvk_ref[...])
    vt_ref[0] = _dot_nt(wkvvt_ref[...], ckvn).astype(BF16)

    ang_t = invfc_ref[...] * posr_ref[0]
    cos_f, sin_f = jnp.cos(ang_t), jnp.sin(ang_t)
    gq = gq_ref[...]
    half = C_ROPE // 2
    for hd in range(C_HEADS):
        blk = q_t[hd * C_QK:(hd + 1) * C_QK, :]
        ss = jnp.sum(blk * blk, axis=0, keepdims=True)
        r = lax.rsqrt(ss * (1.0 / C_QK) + EPS) * (ATTN_SCALE * LOG2E)
        sc = blk * r * gq
        a1 = sc[C_NOPE:C_NOPE + half, :]
        a2 = sc[C_NOPE + half:, :]
        qt_ref[0, hd, 0:C_NOPE, :] = sc[:C_NOPE, :].astype(BF16)
        qt_ref[0, hd, C_NOPE:C_NOPE + half, :] = (a1 * cos_f - a2 * sin_f).astype(BF16)
        qt_ref[0, hd, C_NOPE + half:C_QK, :] = (a2 * cos_f + a1 * sin_f).astype(BF16)

    kr_t = _dot_nt(wropet_ref[...], h) * gkr_ref[...]
    b1, b2 = kr_t[:half, :], kr_t[half:, :]
    rot_t = jnp.concatenate([b1 * cos_f - b2 * sin_f, b2 * cos_f + b1 * sin_f,
                             jnp.zeros((LANES - C_ROPE, kr_t.shape[1]), F32)], axis=0)
    kr_rot = rot_t.T

    lane = lax.broadcasted_iota(jnp.int32, (1, LANES), 1)
    kn_nope = kn_ref[...]
    k2_hi, k2_lo = _split2(jnp.concatenate([k_nope * k_nope, k_rope * k_rope], axis=1))
    ssk = _dot(k2_hi, indk_ref[...]) + _dot(k2_lo, indk_ref[...])
    rk = jnp.where(lane < C_HEADS, lax.rsqrt(ssk * (1.0 / C_QK) + EPS), 0.0)
    rk_hi, rk_lo = _split2(rk)
    rk_b = _dot(rk_hi, splat_ref[...]) + _dot(rk_lo, splat_ref[...])
    for hd in range(C_HEADS):
        cols = slice(hd * C_NOPE, (hd + 1) * C_NOPE)
        k_ref[0, hd, :, 0:C_NOPE] = (k_nope[:, cols] * rk_b[:, cols] * kn_nope).astype(BF16)
        k_ref[0, hd, :, C_NOPE:C_QK] = (kr_rot * rk_b[:, cols])[:, :C_ROPE].astype(BF16)


def _stage1_odd(x3d, posr, op, g0, ffn, ffn_layer, g1):
    bsz, seq, _ = x3d.shape
    tm = TOKEN_TILE
    tile = lambda b, i: (b, i, 0)
    return pl.pallas_call(
        functools.partial(_s1_odd_kernel, ffn_layer),
        grid=(bsz, seq // tm),
        in_specs=[pl.BlockSpec((1, tm, D_MODEL), tile), pl.BlockSpec((1, 1, tm), lambda b, i: (b, 0, i)),
                  _const_spec((C_ROPE // 2, tm)), _const_spec((1, D_MODEL))]
                 + _ffn_specs() + [
                  _const_spec((1, D_MODEL)), _const_spec((D_MODEL, ODD_IN_PAD)), _const_spec((C_ROPE, D_MODEL)),
                  _const_spec((1, Q_LORA)), _const_spec((1, KV_LORA)),
                  _const_spec((C_HEADS * C_QK, Q_LORA)), _const_spec((KV_LORA, C_HEADS * C_NOPE)),
                  _const_spec((C_HEADS * C_V, KV_LORA)), _const_spec((C_QK, tm)), _const_spec((C_ROPE, tm)),
                  _const_spec((1, C_NOPE)), _const_spec((C_HEADS * C_NOPE + LANES, LANES)),
                  _const_spec((LANES, C_HEADS * C_NOPE))],
        out_specs=[pl.BlockSpec((1, tm, D_MODEL), tile),
                   pl.BlockSpec((1, C_HEADS, C_QK, tm), lambda b, i: (b, 0, 0, i)),
                   pl.BlockSpec((1, C_HEADS, tm, C_QK), lambda b, i: (b, 0, i, 0)),
                   pl.BlockSpec((1, C_HEADS * C_V, tm), lambda b, i: (b, 0, i))],
        out_shape=[jax.ShapeDtypeStruct((bsz, seq, D_MODEL), F32),
                   jax.ShapeDtypeStruct((bsz, C_HEADS, C_QK, seq), BF16),
                   jax.ShapeDtypeStruct((bsz, C_HEADS, seq, C_QK), BF16),
                   jax.ShapeDtypeStruct((bsz, C_HEADS * C_V, seq), BF16)],
        scratch_shapes=_ffn_scratch(),
        compiler_params=pltpu.CompilerParams(dimension_semantics=("arbitrary", "arbitrary"),
                                             vmem_limit_bytes=VMEM_LIMIT_BYTES),
        name="stage1_odd",
    )(x3d, posr, op["invfc"], g0, *ffn, g1, op["win"], op["wropet"], op["qan"], op["kvan"],
      op["wqbt"], op["wkvk"], op["wkvvt"], op["gq"], op["gkr"], op["kn"], op["indk"], op["splat"])


def _s2_kernel(ffn_layer, x_ref, m_ref, p_ref, wo_ref, g2_ref, wg_hbm, wu_hbm, wd_hbm,
               g3_ref, wpg_ref, wpp_ref, g4_ref, o_ref, wg_ref, wu_ref, wd_ref, *stage):
    @pl.when(pl.program_id(0) == 0)
    def _():
        _load_ffn(ffn_layer, wg_hbm, wu_hbm, wd_hbm, wg_ref, wu_ref, wd_ref, *stage)

    x = x_ref[...] + _dot(m_ref[...], wo_ref[...])
    hn = _rms(x, g2_ref[...]).astype(BF16)
    x = x + 0.5 * _swiglu(hn, wg_ref, wu_ref, wd_ref)
    h3 = _rms(x, g3_ref[...]).astype(BF16)
    gate = _sigmoid(_dot(h3, wpg_ref[...]))
    proj = _dot(p_ref[...].astype(BF16), wpp_ref[...])
    o_ref[...] = x + _rms(gate * proj, g4_ref[...])


def _stage2(x2d, mixed, p3d, layer, wo, j, g2, ffn, ffn_layer, g3, wpg, wpp, g4):
    n = x2d.shape[0]
    tm = TOKEN_TILE
    row = lambda i: (i, 0)
    return pl.pallas_call(
        functools.partial(_s2_kernel, ffn_layer),
        grid=(n // tm,),
        in_specs=[pl.BlockSpec((tm, D_MODEL), row), pl.BlockSpec((tm, D_MODEL), row),
                  pl.BlockSpec((None, tm, D_PLE), lambda i: (layer, i, 0)),
                  _const_spec((D_MODEL, D_MODEL), (j,)), _const_spec((1, D_MODEL))] + _ffn_specs() + [
                  _const_spec((1, D_MODEL)), _const_spec((D_MODEL, D_MODEL), (layer,)),
                  _const_spec((D_PLE, D_MODEL), (layer,)), _const_spec((1, D_MODEL))],
        out_specs=pl.BlockSpec((tm, D_MODEL), row),
        out_shape=jax.ShapeDtypeStruct((n, D_MODEL), F32),
        scratch_shapes=_ffn_scratch(),
        compiler_params=pltpu.CompilerParams(dimension_semantics=("arbitrary",),
                                             vmem_limit_bytes=VMEM_LIMIT_BYTES),
        name="stage2",
    )(x2d, mixed, p3d, wo, g2, *ffn, g3, wpg, wpp, g4)


def _level_map(ts):
    t = np.arange(ts)[:, None]
    s = np.arange(ts)[None, :]
    lvl = np.full((ts, ts), len(HGRN_LEVELS) + 1, np.int32)
    lvl[(t // SUBLANES == s // SUBLANES) & (s <= t)] = len(HGRN_LEVELS)
    for li, h in enumerate(HGRN_LEVELS):
        own = (t // (2 * h) == s // (2 * h)) & (t % (2 * h) >= h) & (s % (2 * h) < h)
        lvl[own] = li
    return lvl


def _even_core_kernel(u_ref, v_ref, q_ref, f_ref, i_ref, g_ref,
                      vnorm_ref, ind_ref, wcat_ref, bfull_ref, lb_ref, onorm_ref, ltri_ref, lvl_ref,
                      o_ref, state_ref):
    nb = HGRN_BATCH
    ts = HGRN_TILE
    sub = HGRN_SUB
    n_lvl = len(HGRN_LEVELS)
    rows_all = nb * ts

    def stacked(ref):
        return ref[...].reshape(rows_all, ref.shape[-1])

    u = _gelu_tanh(stacked(u_ref).astype(F32))
    v = _gelu_tanh(stacked(v_ref).astype(F32))
    ss = _dot((v * v).astype(BF16), ind_ref[...])
    vn = (v * lax.rsqrt(ss * (1.0 / A_HEAD_DIM) + EPS) * vnorm_ref[...]).astype(BF16)
    lane_head = lax.broadcasted_iota(jnp.int32, (1, A_WIDTH), 1) // A_HEAD_DIM
    for c in range(rows_all // A_CHUNK):
        rows = slice(c * A_CHUNK, (c + 1) * A_CHUNK)
        vc = vn[rows, :]
        zero = jnp.zeros_like(vc)
        vbd = jnp.concatenate([jnp.where(lane_head == hd, vc, zero) for hd in range(A_HEADS)], axis=0)
        mixed = _dot(wcat_ref[...], vbd) + bfull_ref[...]
        bi, local = divmod(c * A_CHUNK, ts)
        o_ref[bi, local:local + A_CHUNK, 0:A_WIDTH] = (u[rows, :] * mixed).astype(BF16)

    @pl.when(pl.program_id(1) == 0)
    def _():
        state_ref[...] = jnp.zeros_like(state_ref)

    lb = lb_ref[...]
    fg = lb + (1.0 - lb) * _sigmoid(stacked(f_ref))
    lf = jnp.log2(jnp.maximum(fg, F_MIN))
    kk = 1.0 - fg
    qq = stacked(q_ref).astype(F32)
    lf_hi, lf_lo = _split2(lf)
    ltri = ltri_ref[...]
    seqs = [slice(bi * ts, (bi + 1) * ts) for bi in range(nb)]
    b = jnp.concatenate([_dot(ltri, lf_hi[r, :]) + _dot(ltri, lf_lo[r, :]) for r in seqs], axis=0)

    q_lvls, k_lvls = [], []
    for h in HGRN_LEVELS:
        q_parts, k_parts = [], []
        zero = jnp.zeros((h, B_WIDTH), F32)
        for j in range(rows_all // (2 * h)):
            r0 = j * 2 * h
            bm = b[r0 + h - 1:r0 + h, :]
            lo_rows = slice(r0, r0 + h)
            up_rows = slice(r0 + h, r0 + 2 * h)
            k_parts += [kk[lo_rows, :] * jnp.exp2(bm - b[lo_rows, :]), zero]
            q_parts += [zero, qq[up_rows, :] * jnp.exp2(b[up_rows, :] - bm)]
        q_lvls.append(jnp.concatenate(q_parts, axis=0).astype(BF16))
        k_lvls.append(jnp.concatenate(k_parts, axis=0).astype(BF16))
    b3 = b.reshape(rows_all // SUBLANES, SUBLANES, B_WIDTH)
    bref = jnp.broadcast_to(b3[:, 3:4, :], b3.shape).reshape(rows_all, B_WIDTH)
    q_lvls.append((qq * jnp.exp2(b - bref)).astype(BF16))
    k_lvls.append((kk * jnp.exp2(bref - b)).astype(BF16))

    q_top, k_top, k_out, carry = [], [], [], []
    for bi in range(nb):
        r0 = bi * ts
        bm = b[r0 + sub - 1:r0 + sub, :]
        q_top.append((qq[r0 + sub:r0 + ts, :] * jnp.exp2(b[r0 + sub:r0 + ts, :] - bm)).astype(BF16))
        k_top.append((kk[r0:r0 + sub, :] * jnp.exp2(bm - b[r0:r0 + sub, :])).astype(BF16))
        b_last = b[r0 + ts - 1:r0 + ts, :]
        k_out.append((kk[seqs[bi], :] * jnp.exp2(b_last - b[seqs[bi], :])).astype(BF16))
        carry.append(jnp.exp2(b_last))
    q_in = (qq * jnp.exp2(b)).astype(BF16)
    vv = stacked(i_ref)
    gate = _silu(stacked(g_ref).astype(F32))
    onorm = onorm_ref[...]

    lvl = lvl_ref[...]
    masks = [lvl == li for li in range(n_lvl + 1)]
    units = [(bi, hd, slice(hd * B_DIM, (hd + 1) * B_DIM)) for hd in range(B_HEADS) for bi in range(nb)]
    states = {(bi, hd): state_ref[bi, hd] for bi, hd, _ in units}
    inter = {(bi, hd): _dot_nt(q_in[seqs[bi], cols], states[bi, hd].astype(BF16)) for bi, hd, cols in units}
    for bi, hd, cols in units:
        state_ref[bi, hd] = (states[bi, hd] * carry[bi][:, cols]
                             + _dot_tn(vv[seqs[bi], cols], k_out[bi][:, cols]))
    score_blocks = {}
    for bi, hd, cols in units:
        diag = []
        for a in range(ts // sub):
            rows = slice(bi * ts + a * sub, bi * ts + (a + 1) * sub)
            blk = jnp.where(masks[n_lvl], _dot_nt(q_lvls[n_lvl][rows, cols], k_lvls[n_lvl][rows, cols]), 0.0)
            for li in range(n_lvl):
                blk = jnp.where(masks[li], _dot_nt(q_lvls[li][rows, cols], k_lvls[li][rows, cols]), blk)
            diag.append(blk.astype(BF16))
        top = _dot_nt(q_top[bi][:, cols], k_top[bi][:, cols]).astype(BF16)
        score_blocks[bi, hd] = (diag[0], jnp.concatenate([top, diag[1]], axis=1))
    outs = {}
    for bi, hd, cols in units:
        v_h = vv[seqs[bi], cols]
        first, second = score_blocks[bi, hd]
        o_intra = jnp.concatenate([_dot(first, v_h[:sub, :]), _dot(second, v_h)], axis=0)
        outs[bi, hd] = o_intra + inter[bi, hd]
    for bi, hd, cols in units:
        on = _rms(outs[bi, hd], onorm[:, cols]) * gate[seqs[bi], cols]
        o_ref[bi, :, A_WIDTH + hd * B_DIM:A_WIDTH + (hd + 1) * B_DIM] = on.astype(BF16)


def _even_core(parts, vnorm, ind, wcat, bfull, lb, onorm, ltri, lvl):
    bsz, seq, _ = parts[0].shape
    ts = HGRN_TILE
    nb = HGRN_BATCH
    tile = lambda b, i: (b, i, 0)
    part_spec = pl.BlockSpec((nb, ts, A_WIDTH), tile)
    return pl.pallas_call(
        _even_core_kernel,
        grid=(bsz // nb, seq // ts),
        in_specs=[part_spec] * EVEN_PARTS + [
            _const_spec((1, A_WIDTH)), _const_spec((A_WIDTH, A_WIDTH)),
            _const_spec((A_CHUNK, A_HEADS * A_CHUNK)), _const_spec((A_CHUNK, A_WIDTH)),
            _const_spec((1, B_WIDTH)), _const_spec((1, B_WIDTH)),
            _const_spec((ts, ts)), _const_spec((HGRN_SUB, HGRN_SUB))],
        out_specs=pl.BlockSpec((nb, ts, D_MODEL), tile),
        out_shape=jax.ShapeDtypeStruct((bsz, seq, D_MODEL), BF16),
        scratch_shapes=[pltpu.VMEM((nb, B_HEADS, B_DIM, B_DIM), F32)],
        compiler_params=pltpu.CompilerParams(dimension_semantics=("parallel", "arbitrary"),
                                             vmem_limit_bytes=VMEM_LIMIT_BYTES),
        name="even_core",
    )(*parts, vnorm, ind, wcat, bfull, lb, onorm, ltri, lvl)


def _attn_kernel(qt_ref, k_ref, vt_ref, o_ref, m_ref, l_ref, acc_ref, sa_ref, sb_ref, mxa_ref, mxb_ref):
    tq = ATTN_TILE
    groups = tq // SUBLANES
    qi = pl.program_id(2)
    heads = range(ATTN_HEADS_PER_STEP)
    buf_a, buf_b = (sa_ref, mxa_ref), (sb_ref, mxb_ref)

    m_ref[...] = jnp.full_like(m_ref, NEG_BIG)
    l_ref[...] = jnp.zeros_like(l_ref)
    acc_ref[...] = jnp.zeros_like(acc_ref)

    def keys(j):
        return pl.ds(pl.multiple_of(j * tq, tq), tq)

    def scores(j, buf, heads=heads):
        s_ref, mx_ref = buf
        for hd in heads:
            s = _dot(k_ref[0, hd, keys(j), :], qt_ref[0, hd])
            s_ref[hd] = s
            mx_ref[hd] = jnp.max(s.reshape(groups, SUBLANES, tq), axis=0)

    def consume(j, buf, masked, heads=heads):
        s_ref, mx_ref = buf
        if masked:
            keep = (lax.broadcasted_iota(jnp.int32, (tq, tq), 0) <= lax.broadcasted_iota(jnp.int32, (tq, tq), 1))
        s3, m_new, alpha = {}, {}, {}
        for hd in heads:
            if masked:
                s3[hd] = jnp.where(keep, s_ref[hd], NEG_BIG).reshape(groups, SUBLANES, tq)
                cmax8 = jnp.max(s3[hd], axis=0)
            else:
                cmax8 = mx_ref[hd]
            cmax = jnp.max(cmax8, axis=0, keepdims=True)
            m_prev = m_ref[hd]
            m_new[hd] = jnp.maximum(m_prev, jnp.broadcast_to(cmax, (SUBLANES, tq)))
            alpha[hd] = jnp.exp2(m_prev - m_new[hd])
            m_ref[hd] = m_new[hd]
        ones = jnp.ones((2 * SUBLANES, tq), BF16)
        for hd in heads:
            sv = s3[hd] if masked else s_ref[hd].reshape(groups, SUBLANES, tq)
            p = jnp.exp2(sv - m_new[hd][None]).reshape(tq, tq).astype(BF16)
            lhs = jnp.concatenate([vt_ref[0, hd * C_V:(hd + 1) * C_V, keys(j)], ones], axis=0)
            pv = _dot(lhs, p)
            acc3 = acc_ref[hd].reshape(C_V // SUBLANES, SUBLANES, tq) * alpha[hd][None]
            acc_ref[hd] = acc3.reshape(C_V, tq) + pv[:C_V, :]
            l_ref[hd] = alpha[hd] * l_ref[hd] + pv[C_V:C_V + SUBLANES, :]

    scores(0, buf_a)
    pairs = qi // 2

    def body(t, carry):
        j = 2 * t
        for hd in heads:
            scores(j + 1, buf_b, [hd])
            consume(j, buf_a, False, [hd])
        for hd in heads:
            scores(j + 2, buf_a, [hd])
            consume(j + 1, buf_b, False, [hd])
        return carry

    lax.fori_loop(0, pairs, body, 0)

    @pl.when(qi % 2 == 0)
    def _():
        consume(qi, buf_a, True)

    @pl.when(qi % 2 == 1)
    def _():
        for hd in heads:
            scores(qi, buf_b, [hd])
            consume(qi - 1, buf_a, False, [hd])
        consume(qi, buf_b, True)
    for hd in range(ATTN_HEADS_PER_STEP):
        inv_l = 1.0 / l_ref[hd]
        out_t = acc_ref[hd].reshape(C_V // SUBLANES, SUBLANES, tq) * inv_l[None]
        o_ref[0, :, hd * C_V:(hd + 1) * C_V] = out_t.reshape(C_V, tq).T.astype(BF16)


def _attention(qt, k, vt):
    bsz, heads, seq, _ = k.shape
    tq = ATTN_TILE
    g = ATTN_HEADS_PER_STEP
    return pl.pallas_call(
        _attn_kernel,
        grid=(bsz, heads // g, seq // tq),
        in_specs=[pl.BlockSpec((1, g, C_QK, tq), lambda b, h, i: (b, h, 0, i)),
                  pl.BlockSpec((1, g, seq, C_QK), lambda b, h, i: (b, h, 0, 0)),
                  pl.BlockSpec((1, g * C_V, seq), lambda b, h, i: (b, h, 0))],
        out_specs=pl.BlockSpec((1, tq, g * C_V), lambda b, h, i: (b, i, h)),
        out_shape=jax.ShapeDtypeStruct((bsz, seq, heads * C_V), BF16),
        scratch_shapes=[pltpu.VMEM((g, SUBLANES, tq), F32), pltpu.VMEM((g, SUBLANES, tq), F32),
                        pltpu.VMEM((g, C_V, tq), F32), pltpu.VMEM((g, tq, tq), F32),
                        pltpu.VMEM((g, tq, tq), F32), pltpu.VMEM((g, SUBLANES, tq), F32),
                        pltpu.VMEM((g, SUBLANES, tq), F32)],
        compiler_params=pltpu.CompilerParams(dimension_semantics=("parallel", "parallel", "arbitrary"),
                                             vmem_limit_bytes=VMEM_LIMIT_BYTES),
        name="mla_attention",
    )(qt, k, vt)


def _row(v):
    return v.reshape(1, -1).astype(F32)


def _pad_cols(w, total):
    return jnp.pad(w, ((0, 0), (0, total - w.shape[1])))


def _even_params(v_norm, w_s, b_s, lb, out_norm):
    causal = np.tril(np.ones((A_CHUNK, A_CHUNK), bool))
    w = jnp.where(causal[None], w_s, 0.0)
    wcat = jnp.transpose(w, (1, 0, 2)).reshape(A_CHUNK, A_HEADS * A_CHUNK)
    bfull = jnp.repeat(b_s.T, A_HEAD_DIM, axis=1)
    head_of = np.arange(A_WIDTH) // A_HEAD_DIM
    ind = (head_of[:, None] == head_of[None, :]).astype(np.float32)
    ltri = np.tril(np.ones((HGRN_TILE, HGRN_TILE), np.float32))
    return dict(vnorm=_row(v_norm), ind=jnp.asarray(ind, BF16),
                wcat=wcat.astype(BF16), bfull=bfull.astype(F32), lb=_row(lb),
                onorm=_row(jnp.tile(out_norm, B_HEADS)), ltri=jnp.asarray(ltri, BF16),
                lvl=jnp.asarray(_level_map(HGRN_SUB)))


def _odd_params(w_in, q_a_norm, kv_a_norm, w_q_b, w_kv_b, q_norm, k_norm):
    tm = TOKEN_TILE
    win = _pad_cols(w_in, ODD_IN_PAD)
    wkv = w_kv_b.reshape(KV_LORA, C_HEADS, C_NOPE + C_V)
    wkvk = wkv[:, :, :C_NOPE].reshape(KV_LORA, C_HEADS * C_NOPE)
    wkvvt = wkv[:, :, C_NOPE:].reshape(KV_LORA, C_HEADS * C_V).T
    inv_freq = ROPE_THETA ** (-jnp.arange(0, C_ROPE, 2, dtype=F32) / C_ROPE)
    invfc = jnp.broadcast_to(inv_freq[:, None], (C_ROPE // 2, tm))
    head_of = np.arange(C_HEADS * C_NOPE) // C_NOPE
    splat = (head_of[:, None] == np.arange(LANES)[None, :]).astype(np.float32)
    rope_rows = np.broadcast_to((np.arange(LANES) < C_HEADS).astype(np.float32), (LANES, LANES))
    indk = np.concatenate([splat, rope_rows], axis=0)
    return dict(win=win.astype(BF16), wropet=w_in[:, Q_LORA + KV_LORA:].T.astype(BF16),
                qan=_row(q_a_norm), kvan=_row(kv_a_norm),
                wqbt=w_q_b.T.astype(BF16), wkvk=wkvk.astype(BF16), wkvvt=wkvvt.astype(BF16),
                gq=jnp.broadcast_to(q_norm.astype(F32)[:, None], (C_QK, tm)),
                gkr=jnp.broadcast_to(k_norm.astype(F32)[C_NOPE:, None], (C_ROPE, tm)),
                kn=_row(k_norm[:C_NOPE]), invfc=invfc,
                indk=jnp.asarray(indk, BF16), splat=jnp.asarray(splat.T, BF16))


def kernel(x, p, positions, norm_gains, ffn_w_gate, ffn_w_up, ffn_w_down, ple_w_gate, ple_w_proj,
           even_w_in, gmlp_v_norm, gmlp_w_s, gmlp_b_s, hgrn_lb_raw, hgrn_out_norm, even_w_out,
           mla_w_in, mla_q_a_norm, mla_kv_a_norm, mla_w_q_b, mla_w_kv_b, mla_q_norm, mla_k_norm, mla_w_out):
    bsz, seq, d = x.shape
    depth = norm_gains.shape[0]
    n = bsz * seq
    assert d == D_MODEL and seq % ATTN_TILE == 0 and seq % TOKEN_TILE == 0 and seq % HGRN_TILE == 0
    assert bsz % HGRN_BATCH == 0

    lb_sm = jax.nn.softmax(hgrn_lb_raw.astype(F32), axis=0)
    lower_bounds = jnp.clip(jnp.cumsum(lb_sm, axis=0) - lb_sm[0], 0.0, LB_MAX)
    posr = positions.astype(F32).reshape(bsz, 1, seq)

    ffn = (ffn_w_gate, ffn_w_up, ffn_w_down)
    ple_wg, ple_wp = ple_w_gate.astype(BF16), ple_w_proj.astype(BF16)
    even_win, even_wout, mla_wout = (w.astype(BF16) for w in (even_w_in, even_w_out, mla_w_out))
    p3d = p.reshape(depth, n, D_PLE)

    xc = x.reshape(n, d)
    for i in range(depth):
        g = norm_gains[i]
        j = i // 2
        if i % 2 == 0:
            ep = _even_params(gmlp_v_norm[j], gmlp_w_s[j], gmlp_b_s[j], lower_bounds[j], hgrn_out_norm[j])
            outs = _stage1_even(xc, _row(g[0]), ffn, (i, 0), _row(g[1]), even_win, j)
            xc = outs[0]
            parts = [o.reshape(bsz, seq, A_WIDTH) for o in outs[1:]]
            mixed = _even_core(parts, ep["vnorm"], ep["ind"], ep["wcat"], ep["bfull"], ep["lb"],
                               ep["onorm"], ep["ltri"], ep["lvl"])
            w_out = even_wout
        else:
            op = _odd_params(mla_w_in[j], mla_q_a_norm[j], mla_kv_a_norm[j], mla_w_q_b[j], mla_w_kv_b[j],
                             mla_q_norm[j], mla_k_norm[j])
            x1, qt, k, vt = _stage1_odd(xc.reshape(bsz, seq, d), posr, op, _row(g[0]), ffn, (i, 0),
                                        _row(g[1]))
            xc = x1.reshape(n, d)
            mixed = _attention(qt, k, vt)
            w_out = mla_wout
        xc = _stage2(xc, mixed.reshape(n, d), p3d, i, w_out, j, _row(g[2]), ffn, (i, 1), _row(g[3]),
                     ple_wg, ple_wp, _row(g[4]))
    return xc.reshape(bsz, seq, d)
```

```python
import functools
import math

import numpy as np
import jax
import jax.numpy as jnp
from jax import lax
from jax.experimental import pallas as pl
from jax.experimental.pallas import tpu as pltpu

F32 = jnp.float32
BF16 = jnp.bfloat16

D_MODEL = 1024
D_FF = 2816
D_PLE = 256
EPS = 1e-6
NEG_BIG = -1e30
F_MIN = 1e-6
LB_MAX = 0.999

A_HEADS = 8
A_HEAD_DIM = 64
A_WIDTH = A_HEADS * A_HEAD_DIM
A_CHUNK = 128
B_HEADS = 4
B_DIM = 128
B_WIDTH = B_HEADS * B_DIM
EVEN_PARTS = 6

C_HEADS = 8
C_NOPE = 128
C_ROPE = 64
C_V = 128
C_QK = C_NOPE + C_ROPE
Q_LORA = 384
KV_LORA = 256
ROPE_THETA = 10000.0
ATTN_SCALE = C_QK ** -0.5
LOG2E = math.log2(math.e)

LANES = 128
SUBLANES = 8
ODD_IN_PAD = Q_LORA + KV_LORA + LANES
VMEM_LIMIT_BYTES = 60000 * 1024

TOKEN_TILE = 512
FFN_STAGE_ROWS_IN = 128
FFN_STAGE_ROWS_OUT = 256
FFN_STAGE_SLOTS = 3
HGRN_TILE = 256
HGRN_BATCH = 2
HGRN_SUB = HGRN_TILE // 2
HGRN_LEVELS = (64, 32, 16, 8)
ATTN_TILE = 512
ATTN_HEADS_PER_STEP = 4


def _sigmoid(x):
    return 0.5 + 0.5 * jnp.tanh(0.5 * x)


def _silu(x):
    t = 0.5 * x
    return t + t * jnp.tanh(t)


def _gelu_tanh(x):
    c = math.sqrt(2.0 / math.pi)
    hx = 0.5 * x
    return hx + hx * jnp.tanh(x * (c + (c * 0.044715) * (x * x)))


def _rms(x, g):
    ms = jnp.mean(x * x, axis=-1, keepdims=True)
    return x * lax.rsqrt(ms + EPS) * g


def _dot(a, b):
    return jnp.dot(a, b, preferred_element_type=F32)


def _dot_nt(a, b):
    return lax.dot_general(a, b, (((1,), (1,)), ((), ())), preferred_element_type=F32)


def _dot_tn(a, b):
    return lax.dot_general(a, b, (((0,), (0,)), ((), ())), preferred_element_type=F32)


def _split2(x):
    hi = x.astype(BF16)
    lo = (x - hi.astype(F32)).astype(BF16)
    return hi, lo


def _swiglu(hn, wg_ref, wu_ref, wd_ref):
    act = (_silu(_dot(hn, wg_ref[...])) * _dot(hn, wu_ref[...])).astype(BF16)
    return _dot(act, wd_ref[...])


def _s1_even_kernel(ffn_layer, x_ref, g0_ref, wg_hbm, wu_hbm, wd_hbm, g1_ref, win_ref,
                    x1_ref, u_ref, v_ref, q_ref, f_ref, i_ref, gg_ref, wg_ref, wu_ref, wd_ref, *stage):
    @pl.when(pl.program_id(0) == 0)
    def _():
        _load_ffn(ffn_layer, wg_hbm, wu_hbm, wd_hbm, wg_ref, wu_ref, wd_ref, *stage)

    x = x_ref[...]
    hn = _rms(x, g0_ref[...]).astype(BF16)
    x1 = x + 0.5 * _swiglu(hn, wg_ref, wu_ref, wd_ref)
    x1_ref[...] = x1
    h = _rms(x1, g1_ref[...]).astype(BF16)
    outs = (u_ref, v_ref, q_ref, f_ref, i_ref, gg_ref)
    for k, o_ref in enumerate(outs):
        z = _dot(h, win_ref[:, k * A_WIDTH:(k + 1) * A_WIDTH])
        o_ref[...] = z.astype(o_ref.dtype)


def _const_spec(shape, prefix=()):
    nd = len(shape)
    return pl.BlockSpec((None,) * len(prefix) + tuple(shape), lambda *_: tuple(prefix) + (0,) * nd,
                        pipeline_mode=pl.Buffered(1))


def _ffn_specs():
    return [pl.BlockSpec(memory_space=pl.ANY)] * 3


def _ffn_scratch():
    return [pltpu.VMEM((D_MODEL, D_FF), BF16), pltpu.VMEM((D_MODEL, D_FF), BF16), pltpu.VMEM((D_FF, D_MODEL), BF16),
            pltpu.VMEM((FFN_STAGE_SLOTS, FFN_STAGE_ROWS_IN, D_FF), F32),
            pltpu.VMEM((FFN_STAGE_SLOTS, FFN_STAGE_ROWS_OUT, D_MODEL), F32),
            pltpu.SemaphoreType.DMA((FFN_STAGE_SLOTS,))]


def _load_cast(src_hbm, layer, dst_ref, stage_ref, sem_ref):
    slots, rows = stage_ref.shape[0], stage_ref.shape[1]
    n_chunks = dst_ref.shape[0] // rows
    ahead = slots - 1

    def copy(c):
        return pltpu.make_async_copy(src_hbm.at[layer[0], layer[1], pl.ds(c * rows, rows)],
                                     stage_ref.at[c % slots], sem_ref.at[c % slots])

    for c in range(min(ahead, n_chunks)):
        copy(c).start()
    for c in range(n_chunks):
        if c + ahead < n_chunks:
            copy(c + ahead).start()
        copy(c).wait()
        dst_ref[c * rows:(c + 1) * rows, :] = stage_ref[c % slots].astype(BF16)


def _load_ffn(layer, wg_hbm, wu_hbm, wd_hbm, wg_ref, wu_ref, wd_ref, stage_in_ref, stage_out_ref, sem_ref):
    _load_cast(wg_hbm, layer, wg_ref, stage_in_ref, sem_ref)
    _load_cast(wu_hbm, layer, wu_ref, stage_in_ref, sem_ref)
    _load_cast(wd_hbm, layer, wd_ref, stage_out_ref, sem_ref)


def _stage1_even(x2d, g0, ffn, ffn_layer, g1, win, j):
    n = x2d.shape[0]
    tm = TOKEN_TILE
    row = lambda i: (i, 0)
    part_spec = pl.BlockSpec((tm, A_WIDTH), row)
    out_shape = [jax.ShapeDtypeStruct((n, D_MODEL), F32)]
    out_shape += [jax.ShapeDtypeStruct((n, A_WIDTH), F32 if k == 3 else BF16) for k in range(EVEN_PARTS)]
    return pl.pallas_call(
        functools.partial(_s1_even_kernel, ffn_layer),
        grid=(n // tm,),
        in_specs=[pl.BlockSpec((tm, D_MODEL), row), _const_spec((1, D_MODEL))] + _ffn_specs() + [
                  _const_spec((1, D_MODEL)), _const_spec((D_MODEL, EVEN_PARTS * A_WIDTH), (j,))],
        out_specs=[pl.BlockSpec((tm, D_MODEL), row)] + [part_spec] * EVEN_PARTS,
        out_shape=out_shape,
        scratch_shapes=_ffn_scratch(),
        compiler_params=pltpu.CompilerParams(dimension_semantics=("arbitrary",),
                                             vmem_limit_bytes=VMEM_LIMIT_BYTES),
        name="stage1_even",
    )(x2d, g0, *ffn, g1, win)


def _s1_odd_kernel(ffn_layer, x_ref, posr_ref, invfc_ref, g0_ref, wg_hbm, wu_hbm, wd_hbm, g1_ref,
                   win_ref, wropet_ref, qan_ref, kvan_ref, wqbt_ref, wkvk_ref, wkvvt_ref, gq_ref, gkr_ref, kn_ref,
                   indk_ref, splat_ref,
                   x1_ref, qt_ref, k_ref, vt_ref, wg_ref, wu_ref, wd_ref, *stage):
    @pl.when((pl.program_id(0) == 0) & (pl.program_id(1) == 0))
    def _():
        _load_ffn(ffn_layer, wg_hbm, wu_hbm, wd_hbm, wg_ref, wu_ref, wd_ref, *stage)

    x = x_ref[0]
    hn = _rms(x, g0_ref[...]).astype(BF16)
    x1 = x + 0.5 * _swiglu(hn, wg_ref, wu_ref, wd_ref)
    x1_ref[0] = x1
    h = _rms(x1, g1_ref[...]).astype(BF16)
    z = _dot(h, win_ref[...])
    cqn = _rms(z[:, :Q_LORA], qan_ref[...]).astype(BF16)
    ckvn = _rms(z[:, Q_LORA:Q_LORA + KV_LORA], kvan_ref[...]).astype(BF16)
    k_rope = z[:, Q_LORA + KV_LORA:]
    q_t = _dot_nt(wqbt_ref[...], cqn)
    k_nope = _dot(ckvn, wkvk_ref[...])
    vt_ref[0] = _dot_nt(wkvvt_ref[...], ckvn).astype(BF16)

    ang_t = invfc_ref[...] * posr_ref[0]
    cos_f, sin_f = jnp.cos(ang_t), jnp.sin(ang_t)
    gq = gq_ref[...]
    half = C_ROPE // 2
    for hd in range(C_HEADS):
        blk = q_t[hd * C_QK:(hd + 1) * C_QK, :]
        ss = jnp.sum(blk * blk, axis=0, keepdims=True)
        r = lax.rsqrt(ss * (1.0 / C_QK) + EPS) * (ATTN_SCALE * LOG2E)
        sc = blk * r * gq
        a1 = sc[C_NOPE:C_NOPE + half, :]
        a2 = sc[C_NOPE + half:, :]
        qt_ref[0, hd, 0:C_NOPE, :] = sc[:C_NOPE, :].astype(BF16)
        qt_ref[0, hd, C_NOPE:C_NOPE + half, :] = (a1 * cos_f - a2 * sin_f).astype(BF16)
        qt_ref[0, hd, C_NOPE + half:C_QK, :] = (a2 * cos_f + a1 * sin_f).astype(BF16)

    kr_t = _dot_nt(wropet_ref[...], h) * gkr_ref[...]
    b1, b2 = kr_t[:half, :], kr_t[half:, :]
    rot_t = jnp.concatenate([b1 * cos_f - b2 * sin_f, b2 * cos_f + b1 * sin_f,
                             jnp.zeros((LANES - C_ROPE, kr_t.shape[1]), F32)], axis=0)
    kr_rot = rot_t.T

    lane = lax.broadcasted_iota(jnp.int32, (1, LANES), 1)
    kn_nope = kn_ref[...]
    k2_hi, k2_lo = _split2(jnp.concatenate([k_nope * k_nope, k_rope * k_rope], axis=1))
    ssk = _dot(k2_hi, indk_ref[...]) + _dot(k2_lo, indk_ref[...])
    rk = jnp.where(lane < C_HEADS, lax.rsqrt(ssk * (1.0 / C_QK) + EPS), 0.0)
    rk_hi, rk_lo = _split2(rk)
    rk_b = _dot(rk_hi, splat_ref[...]) + _dot(rk_lo, splat_ref[...])
    for hd in range(C_HEADS):
        cols = slice(hd * C_NOPE, (hd + 1) * C_NOPE)
        k_ref[0, hd, :, 0:C_NOPE] = (k_nope[:, cols] * rk_b[:, cols] * kn_nope).astype(BF16)
        k_ref[0, hd, :, C_NOPE:C_QK] = (kr_rot * rk_b[:, cols])[:, :C_ROPE].astype(BF16)


def _stage1_odd(x3d, posr, op, g0, ffn, ffn_layer, g1):
    bsz, seq, _ = x3d.shape
    tm = TOKEN_TILE
    tile = lambda b, i: (b, i, 0)
    return pl.pallas_call(
        functools.partial(_s1_odd_kernel, ffn_layer),
        grid=(bsz, seq // tm),
        in_specs=[pl.BlockSpec((1, tm, D_MODEL), tile), pl.BlockSpec((1, 1, tm), lambda b, i: (b, 0, i)),
                  _const_spec((C_ROPE // 2, tm)), _const_spec((1, D_MODEL))]
                 + _ffn_specs() + [
                  _const_spec((1, D_MODEL)), _const_spec((D_MODEL, ODD_IN_PAD)), _const_spec((C_ROPE, D_MODEL)),
                  _const_spec((1, Q_LORA)), _const_spec((1, KV_LORA)),
                  _const_spec((C_HEADS * C_QK, Q_LORA)), _const_spec((KV_LORA, C_HEADS * C_NOPE)),
                  _const_spec((C_HEADS * C_V, KV_LORA)), _const_spec((C_QK, tm)), _const_spec((C_ROPE, tm)),
                  _const_spec((1, C_NOPE)), _const_spec((C_HEADS * C_NOPE + LANES, LANES)),
                  _const_spec((LANES, C_HEADS * C_NOPE))],
        out_specs=[pl.BlockSpec((1, tm, D_MODEL), tile),
                   pl.BlockSpec((1, C_HEADS, C_QK, tm), lambda b, i: (b, 0, 0, i)),
                   pl.BlockSpec((1, C_HEADS, tm, C_QK), lambda b, i: (b, 0, i, 0)),
                   pl.BlockSpec((1, C_HEADS * C_V, tm), lambda b, i: (b, 0, i))],
        out_shape=[jax.ShapeDtypeStruct((bsz, seq, D_MODEL), F32),
                   jax.ShapeDtypeStruct((bsz, C_HEADS, C_QK, seq), BF16),
                   jax.ShapeDtypeStruct((bsz, C_HEADS, seq, C_QK), BF16),
                   jax.ShapeDtypeStruct((bsz, C_HEADS * C_V, seq), BF16)],
        scratch_shapes=_ffn_scratch(),
        compiler_params=pltpu.CompilerParams(dimension_semantics=("arbitrary", "arbitrary"),
                                             vmem_limit_bytes=VMEM_LIMIT_BYTES),
        name="stage1_odd",
    )(x3d, posr, op["invfc"], g0, *ffn, g1, op["win"], op["wropet"], op["qan"], op["kvan"],
      op["wqbt"], op["wkvk"], op["wkvvt"], op["gq"], op["gkr"], op["kn"], op["indk"], op["splat"])


def _s2_kernel(ffn_layer, x_ref, m_ref, p_ref, wo_ref, g2_ref, wg_hbm, wu_hbm, wd_hbm,
               g3_ref, wpg_ref, wpp_ref, g4_ref, o_ref, wg_ref, wu_ref, wd_ref, *stage):
    @pl.when(pl.program_id(0) == 0)
    def _():
        _load_ffn(ffn_layer, wg_hbm, wu_hbm, wd_hbm, wg_ref, wu_ref, wd_ref, *stage)

    x = x_ref[...] + _dot(m_ref[...], wo_ref[...])
    hn = _rms(x, g2_ref[...]).astype(BF16)
    x = x + 0.5 * _swiglu(hn, wg_ref, wu_ref, wd_ref)
    h3 = _rms(x, g3_ref[...]).astype(BF16)
    gate = _sigmoid(_dot(h3, wpg_ref[...]))
    proj = _dot(p_ref[...].astype(BF16), wpp_ref[...])
    o_ref[...] = x + _rms(gate * proj, g4_ref[...])


def _stage2(x2d, mixed, p3d, layer, wo, j, g2, ffn, ffn_layer, g3, wpg, wpp, g4):
    n = x2d.shape[0]
    tm = TOKEN_TILE
    row = lambda i: (i, 0)
    return pl.pallas_call(
        functools.partial(_s2_kernel, ffn_layer),
        grid=(n // tm,),
        in_specs=[pl.BlockSpec((tm, D_MODEL), row), pl.BlockSpec((tm, D_MODEL), row),
                  pl.BlockSpec((None, tm, D_PLE), lambda i: (layer, i, 0)),
                  _const_spec((D_MODEL, D_MODEL), (j,)), _const_spec((1, D_MODEL))] + _ffn_specs() + [
                  _const_spec((1, D_MODEL)), _const_spec((D_MODEL, D_MODEL), (layer,)),
                  _const_spec((D_PLE, D_MODEL), (layer,)), _const_spec((1, D_MODEL))],
        out_specs=pl.BlockSpec((tm, D_MODEL), row),
        out_shape=jax.ShapeDtypeStruct((n, D_MODEL), F32),
        scratch_shapes=_ffn_scratch(),
        compiler_params=pltpu.CompilerParams(dimension_semantics=("arbitrary",),
                                             vmem_limit_bytes=VMEM_LIMIT_BYTES),
        name="stage2",
    )(x2d, mixed, p3d, wo, g2, *ffn, g3, wpg, wpp, g4)


def _level_map(ts):
    t = np.arange(ts)[:, None]
    s = np.arange(ts)[None, :]
    lvl = np.full((ts, ts), len(HGRN_LEVELS) + 1, np.int32)
    lvl[(t // SUBLANES == s // SUBLANES) & (s <= t)] = len(HGRN_LEVELS)
    for li, h in enumerate(HGRN_LEVELS):
        own = (t // (2 * h) == s // (2 * h)) & (t % (2 * h) >= h) & (s % (2 * h) < h)
        lvl[own] = li
    return lvl


def _even_core_kernel(u_ref, v_ref, q_ref, f_ref, i_ref, g_ref,
                      vnorm_ref, ind_ref, wcat_ref, bfull_ref, lb_ref, onorm_ref, ltri_ref, lvl_ref,
                      o_ref, state_ref):
    nb = HGRN_BATCH
    ts = HGRN_TILE
    sub = HGRN_SUB
    n_lvl = len(HGRN_LEVELS)
    rows_all = nb * ts

    def stacked(ref):
        return ref[...].reshape(rows_all, ref.shape[-1])

    u = _gelu_tanh(stacked(u_ref).astype(F32))
    v = _gelu_tanh(stacked(v_ref).astype(F32))
    ss = _dot((v * v).astype(BF16), ind_ref[...])
    vn = (v * lax.rsqrt(ss * (1.0 / A_HEAD_DIM) + EPS) * vnorm_ref[...]).astype(BF16)
    lane_head = lax.broadcasted_iota(jnp.int32, (1, A_WIDTH), 1) // A_HEAD_DIM
    for c in range(rows_all // A_CHUNK):
        rows = slice(c * A_CHUNK, (c + 1) * A_CHUNK)
        vc = vn[rows, :]
        zero = jnp.zeros_like(vc)
        vbd = jnp.concatenate([jnp.where(lane_head == hd, vc, zero) for hd in range(A_HEADS)], axis=0)
        mixed = _dot(wcat_ref[...], vbd) + bfull_ref[...]
        bi, local = divmod(c * A_CHUNK, ts)
        o_ref[bi, local:local + A_CHUNK, 0:A_WIDTH] = (u[rows, :] * mixed).astype(BF16)

    @pl.when(pl.program_id(1) == 0)
    def _():
        state_ref[...] = jnp.zeros_like(state_ref)

    lb = lb_ref[...]
    fg = lb + (1.0 - lb) * _sigmoid(stacked(f_ref))
    lf = jnp.log2(jnp.maximum(fg, F_MIN))
    kk = 1.0 - fg
    qq = stacked(q_ref).astype(F32)
    lf_hi, lf_lo = _split2(lf)
    ltri = ltri_ref[...]
    seqs = [slice(bi * ts, (bi + 1) * ts) for bi in range(nb)]
    b = jnp.concatenate([_dot(ltri, lf_hi[r, :]) + _dot(ltri, lf_lo[r, :]) for r in seqs], axis=0)

    q_lvls, k_lvls = [], []
    for h in HGRN_LEVELS:
        q_parts, k_parts = [], []
        zero = jnp.zeros((h, B_WIDTH), F32)
        for j in range(rows_all // (2 * h)):
            r0 = j * 2 * h
            bm = b[r0 + h - 1:r0 + h, :]
            lo_rows = slice(r0, r0 + h)
            up_rows = slice(r0 + h, r0 + 2 * h)
            k_parts += [kk[lo_rows, :] * jnp.exp2(bm - b[lo_rows, :]), zero]
            q_parts += [zero, qq[up_rows, :] * jnp.exp2(b[up_rows, :] - bm)]
        q_lvls.append(jnp.concatenate(q_parts, axis=0).astype(BF16))
        k_lvls.append(jnp.concatenate(k_parts, axis=0).astype(BF16))
    b3 = b.reshape(rows_all // SUBLANES, SUBLANES, B_WIDTH)
    bref = jnp.broadcast_to(b3[:, 3:4, :], b3.shape).reshape(rows_all, B_WIDTH)
    q_lvls.append((qq * jnp.exp2(b - bref)).astype(BF16))
    k_lvls.append((kk * jnp.exp2(bref - b)).astype(BF16))

    q_top, k_top, k_out, carry = [], [], [], []
    for bi in range(nb):
        r0 = bi * ts
        bm = b[r0 + sub - 1:r0 + sub, :]
        q_top.append((qq[r0 + sub:r0 + ts, :] * jnp.exp2(b[r0 + sub:r0 + ts, :] - bm)).astype(BF16))
        k_top.append((kk[r0:r0 + sub, :] * jnp.exp2(bm - b[r0:r0 + sub, :])).astype(BF16))
        b_last = b[r0 + ts - 1:r0 + ts, :]
        k_out.append((kk[seqs[bi], :] * jnp.exp2(b_last - b[seqs[bi], :])).astype(BF16))
        carry.append(jnp.exp2(b_last))
    q_in = (qq * jnp.exp2(b)).astype(BF16)
    vv = stacked(i_ref)
    gate = _silu(stacked(g_ref).astype(F32))
    onorm = onorm_ref[...]

    lvl = lvl_ref[...]
    masks = [lvl == li for li in range(n_lvl + 1)]
    units = [(bi, hd, slice(hd * B_DIM, (hd + 1) * B_DIM)) for hd in range(B_HEADS) for bi in range(nb)]
    states = {(bi, hd): state_ref[bi, hd] for bi, hd, _ in units}
    inter = {(bi, hd): _dot_nt(q_in[seqs[bi], cols], states[bi, hd].astype(BF16)) for bi, hd, cols in units}
    for bi, hd, cols in units:
        state_ref[bi, hd] = (states[bi, hd] * carry[bi][:, cols]
                             + _dot_tn(vv[seqs[bi], cols], k_out[bi][:, cols]))
    score_blocks = {}
    for bi, hd, cols in units:
        diag = []
        for a in range(ts // sub):
            rows = slice(bi * ts + a * sub, bi * ts + (a + 1) * sub)
            blk = jnp.where(masks[n_lvl], _dot_nt(q_lvls[n_lvl][rows, cols], k_lvls[n_lvl][rows, cols]), 0.0)
            for li in range(n_lvl):
                blk = jnp.where(masks[li], _dot_nt(q_lvls[li][rows, cols], k_lvls[li][rows, cols]), blk)
            diag.append(blk.astype(BF16))
        top = _dot_nt(q_top[bi][:, cols], k_top[bi][:, cols]).astype(BF16)
        score_blocks[bi, hd] = (diag[0], jnp.concatenate([top, diag[1]], axis=1))
    outs = {}
    for bi, hd, cols in units:
        v_h = vv[seqs[bi], cols]
        first, second = score_blocks[bi, hd]
        o_intra = jnp.concatenate([_dot(first, v_h[:sub, :]), _dot(second, v_h)], axis=0)
        outs[bi, hd] = o_intra + inter[bi, hd]
    for bi, hd, cols in units:
        on = _rms(outs[bi, hd], onorm[:, cols]) * gate[seqs[bi], cols]
        o_ref[bi, :, A_WIDTH + hd * B_DIM:A_WIDTH + (hd + 1) * B_DIM] = on.astype(BF16)


def _even_core(parts, vnorm, ind, wcat, bfull, lb, onorm, ltri, lvl):
    bsz, seq, _ = parts[0].shape
    ts = HGRN_TILE
    nb = HGRN_BATCH
    tile = lambda b, i: (b, i, 0)
    part_spec = pl.BlockSpec((nb, ts, A_WIDTH), tile)
    return pl.pallas_call(
        _even_core_kernel,
        grid=(bsz // nb, seq // ts),
        in_specs=[part_spec] * EVEN_PARTS + [
            _const_spec((1, A_WIDTH)), _const_spec((A_WIDTH, A_WIDTH)),
            _const_spec((A_CHUNK, A_HEADS * A_CHUNK)), _const_spec((A_CHUNK, A_WIDTH)),
            _const_spec((1, B_WIDTH)), _const_spec((1, B_WIDTH)),
            _const_spec((ts, ts)), _const_spec((HGRN_SUB, HGRN_SUB))],
        out_specs=pl.BlockSpec((nb, ts, D_MODEL), tile),
        out_shape=jax.ShapeDtypeStruct((bsz, seq, D_MODEL), BF16),
        scratch_shapes=[pltpu.VMEM((nb, B_HEADS, B_DIM, B_DIM), F32)],
        compiler_params=pltpu.CompilerParams(dimension_semantics=("parallel", "arbitrary"),
                                             vmem_limit_bytes=VMEM_LIMIT_BYTES),
        name="even_core",
    )(*parts, vnorm, ind, wcat, bfull, lb, onorm, ltri, lvl)


def _attn_kernel(qt_ref, k_ref, vt_ref, o_ref, m_ref, l_ref, acc_ref, sa_ref, sb_ref, mxa_ref, mxb_ref):
    tq = ATTN_TILE
    groups = tq // SUBLANES
    qi = pl.program_id(2)
    heads = range(ATTN_HEADS_PER_STEP)
    buf_a, buf_b = (sa_ref, mxa_ref), (sb_ref, mxb_ref)

    m_ref[...] = jnp.full_like(m_ref, NEG_BIG)
    l_ref[...] = jnp.zeros_like(l_ref)
    acc_ref[...] = jnp.zeros_like(acc_ref)

    def keys(j):
        return pl.ds(pl.multiple_of(j * tq, tq), tq)

    def scores(j, buf, heads=heads):
        s_ref, mx_ref = buf
        for hd in heads:
            s = _dot(k_ref[0, hd, keys(j), :], qt_ref[0, hd])
            s_ref[hd] = s
            mx_ref[hd] = jnp.max(s.reshape(groups, SUBLANES, tq), axis=0)

    def consume(j, buf, masked, heads=heads):
        s_ref, mx_ref = buf
        if masked:
            keep = (lax.broadcasted_iota(jnp.int32, (tq, tq), 0) <= lax.broadcasted_iota(jnp.int32, (tq, tq), 1))
        s3, m_new, alpha = {}, {}, {}
        for hd in heads:
            if masked:
                s3[hd] = jnp.where(keep, s_ref[hd], NEG_BIG).reshape(groups, SUBLANES, tq)
                cmax8 = jnp.max(s3[hd], axis=0)
            else:
                cmax8 = mx_ref[hd]
            cmax = jnp.max(cmax8, axis=0, keepdims=True)
            m_prev = m_ref[hd]
            m_new[hd] = jnp.maximum(m_prev, jnp.broadcast_to(cmax, (SUBLANES, tq)))
            alpha[hd] = jnp.exp2(m_prev - m_new[hd])
            m_ref[hd] = m_new[hd]
        ones = jnp.ones((2 * SUBLANES, tq), BF16)
        for hd in heads:
            sv = s3[hd] if masked else s_ref[hd].reshape(groups, SUBLANES, tq)
            p = jnp.exp2(sv - m_new[hd][None]).reshape(tq, tq).astype(BF16)
            lhs = jnp.concatenate([vt_ref[0, hd * C_V:(hd + 1) * C_V, keys(j)], ones], axis=0)
            pv = _dot(lhs, p)
            acc3 = acc_ref[hd].reshape(C_V // SUBLANES, SUBLANES, tq) * alpha[hd][None]
            acc_ref[hd] = acc3.reshape(C_V, tq) + pv[:C_V, :]
            l_ref[hd] = alpha[hd] * l_ref[hd] + pv[C_V:C_V + SUBLANES, :]

    scores(0, buf_a)
    pairs = qi // 2

    def body(t, carry):
        j = 2 * t
        for hd in heads:
            scores(j + 1, buf_b, [hd])
            consume(j, buf_a, False, [hd])
        for hd in heads:
            scores(j + 2, buf_a, [hd])
            consume(j + 1, buf_b, False, [hd])
        return carry

    lax.fori_loop(0, pairs, body, 0)

    @pl.when(qi % 2 == 0)
    def _():
        consume(qi, buf_a, True)

    @pl.when(qi % 2 == 1)
    def _():
        for hd in heads:
            scores(qi, buf_b, [hd])
            consume(qi - 1, buf_a, False, [hd])
        consume(qi, buf_b, True)
    for hd in range(ATTN_HEADS_PER_STEP):
        inv_l = 1.0 / l_ref[hd]
        out_t = acc_ref[hd].reshape(C_V // SUBLANES, SUBLANES, tq) * inv_l[None]
        o_ref[0, :, hd * C_V:(hd + 1) * C_V] = out_t.reshape(C_V, tq).T.astype(BF16)


def _attention(qt, k, vt):
    bsz, heads, seq, _ = k.shape
    tq = ATTN_TILE
    g = ATTN_HEADS_PER_STEP
    return pl.pallas_call(
        _attn_kernel,
        grid=(bsz, heads // g, seq // tq),
        in_specs=[pl.BlockSpec((1, g, C_QK, tq), lambda b, h, i: (b, h, 0, i)),
                  pl.BlockSpec((1, g, seq, C_QK), lambda b, h, i: (b, h, 0, 0)),
                  pl.BlockSpec((1, g * C_V, seq), lambda b, h, i: (b, h, 0))],
        out_specs=pl.BlockSpec((1, tq, g * C_V), lambda b, h, i: (b, i, h)),
        out_shape=jax.ShapeDtypeStruct((bsz, seq, heads * C_V), BF16),
        scratch_shapes=[pltpu.VMEM((g, SUBLANES, tq), F32), pltpu.VMEM((g, SUBLANES, tq), F32),
                        pltpu.VMEM((g, C_V, tq), F32), pltpu.VMEM((g, tq, tq), F32),
                        pltpu.VMEM((g, tq, tq), F32), pltpu.VMEM((g, SUBLANES, tq), F32),
                        pltpu.VMEM((g, SUBLANES, tq), F32)],
        compiler_params=pltpu.CompilerParams(dimension_semantics=("parallel", "parallel", "arbitrary"),
                                             vmem_limit_bytes=VMEM_LIMIT_BYTES),
        name="mla_attention",
    )(qt, k, vt)


def _row(v):
    return v.reshape(1, -1).astype(F32)


def _pad_cols(w, total):
    return jnp.pad(w, ((0, 0), (0, total - w.shape[1])))


def _even_params(v_norm, w_s, b_s, lb, out_norm):
    causal = np.tril(np.ones((A_CHUNK, A_CHUNK), bool))
    w = jnp.where(causal[None], w_s, 0.0)
    wcat = jnp.transpose(w, (1, 0, 2)).reshape(A_CHUNK, A_HEADS * A_CHUNK)
    bfull = jnp.repeat(b_s.T, A_HEAD_DIM, axis=1)
    head_of = np.arange(A_WIDTH) // A_HEAD_DIM
    ind = (head_of[:, None] == head_of[None, :]).astype(np.float32)
    ltri = np.tril(np.ones((HGRN_TILE, HGRN_TILE), np.float32))
    return dict(vnorm=_row(v_norm), ind=jnp.asarray(ind, BF16),
                wcat=wcat.astype(BF16), bfull=bfull.astype(F32), lb=_row(lb),
                onorm=_row(jnp.tile(out_norm, B_HEADS)), ltri=jnp.asarray(ltri, BF16),
                lvl=jnp.asarray(_level_map(HGRN_SUB)))


def _odd_params(w_in, q_a_norm, kv_a_norm, w_q_b, w_kv_b, q_norm, k_norm):
    tm = TOKEN_TILE
    win = _pad_cols(w_in, ODD_IN_PAD)
    wkv = w_kv_b.reshape(KV_LORA, C_HEADS, C_NOPE + C_V)
    wkvk = wkv[:, :, :C_NOPE].reshape(KV_LORA, C_HEADS * C_NOPE)
    wkvvt = wkv[:, :, C_NOPE:].reshape(KV_LORA, C_HEADS * C_V).T
    inv_freq = ROPE_THETA ** (-jnp.arange(0, C_ROPE, 2, dtype=F32) / C_ROPE)
    invfc = jnp.broadcast_to(inv_freq[:, None], (C_ROPE // 2, tm))
    head_of = np.arange(C_HEADS * C_NOPE) // C_NOPE
    splat = (head_of[:, None] == np.arange(LANES)[None, :]).astype(np.float32)
    rope_rows = np.broadcast_to((np.arange(LANES) < C_HEADS).astype(np.float32), (LANES, LANES))
    indk = np.concatenate([splat, rope_rows], axis=0)
    return dict(win=win.astype(BF16), wropet=w_in[:, Q_LORA + KV_LORA:].T.astype(BF16),
                qan=_row(q_a_norm), kvan=_row(kv_a_norm),
                wqbt=w_q_b.T.astype(BF16), wkvk=wkvk.astype(BF16), wkvvt=wkvvt.astype(BF16),
                gq=jnp.broadcast_to(q_norm.astype(F32)[:, None], (C_QK, tm)),
                gkr=jnp.broadcast_to(k_norm.astype(F32)[C_NOPE:, None], (C_ROPE, tm)),
                kn=_row(k_norm[:C_NOPE]), invfc=invfc,
                indk=jnp.asarray(indk, BF16), splat=jnp.asarray(splat.T, BF16))


def kernel(x, p, positions, norm_gains, ffn_w_gate, ffn_w_up, ffn_w_down, ple_w_gate, ple_w_proj,
           even_w_in, gmlp_v_norm, gmlp_w_s, gmlp_b_s, hgrn_lb_raw, hgrn_out_norm, even_w_out,
           mla_w_in, mla_q_a_norm, mla_kv_a_norm, mla_w_q_b, mla_w_kv_b, mla_q_norm, mla_k_norm, mla_w_out):
    bsz, seq, d = x.shape
    depth = norm_gains.shape[0]
    n = bsz * seq
    assert d == D_MODEL and seq % ATTN_TILE == 0 and seq % TOKEN_TILE == 0 and seq % HGRN_TILE == 0
    assert bsz % HGRN_BATCH == 0

    lb_sm = jax.nn.softmax(hgrn_lb_raw.astype(F32), axis=0)
    lower_bounds = jnp.clip(jnp.cumsum(lb_sm, axis=0) - lb_sm[0], 0.0, LB_MAX)
    posr = positions.astype(F32).reshape(bsz, 1, seq)

    ffn = (ffn_w_gate, ffn_w_up, ffn_w_down)
    ple_wg, ple_wp = ple_w_gate.astype(BF16), ple_w_proj.astype(BF16)
    even_win, even_wout, mla_wout = (w.astype(BF16) for w in (even_w_in, even_w_out, mla_w_out))
    p3d = p.reshape(depth, n, D_PLE)

    xc = x.reshape(n, d)
    for i in range(depth):
        g = norm_gains[i]
        j = i // 2
        if i % 2 == 0:
            ep = _even_params(gmlp_v_norm[j], gmlp_w_s[j], gmlp_b_s[j], lower_bounds[j], hgrn_out_norm[j])
            outs = _stage1_even(xc, _row(g[0]), ffn, (i, 0), _row(g[1]), even_win, j)
            xc = outs[0]
            parts = [o.reshape(bsz, seq, A_WIDTH) for o in outs[1:]]
            mixed = _even_core(parts, ep["vnorm"], ep["ind"], ep["wcat"], ep["bfull"], ep["lb"],
                               ep["onorm"], ep["ltri"], ep["lvl"])
            w_out = even_wout
        else:
            op = _odd_params(mla_w_in[j], mla_q_a_norm[j], mla_kv_a_norm[j], mla_w_q_b[j], mla_w_kv_b[j],
                             mla_q_norm[j], mla_k_norm[j])
            x1, qt, k, vt = _stage1_odd(xc.reshape(bsz, seq, d), posr, op, _row(g[0]), ffn, (i, 0),
                                        _row(g[1]))
            xc = x1.reshape(n, d)
            mixed = _attention(qt, k, vt)
            w_out = mla_wout
        xc = _stage2(xc, mixed.reshape(n, d), p3d, i, w_out, j, _row(g[2]), ffn, (i, 1), _row(g[3]),
                     ple_wg, ple_wp, _row(g[4]))
    return xc.reshape(bsz, seq, d)
```

```python
import functools
import math

import numpy as np
import jax
import jax.numpy as jnp
from jax import lax
from jax.experimental import pallas as pl
from jax.experimental.pallas import tpu as pltpu

F32 = jnp.float32
BF16 = jnp.bfloat16

D_MODEL = 1024
D_FF = 2816
D_PLE = 256
EPS = 1e-6
NEG_BIG = -1e30
F_MIN = 1e-6
LB_MAX = 0.999

A_HEADS = 8
A_HEAD_DIM = 64
A_WIDTH = A_HEADS * A_HEAD_DIM
A_CHUNK = 128
B_HEADS = 4
B_DIM = 128
B_WIDTH = B_HEADS * B_DIM
EVEN_PARTS = 6

C_HEADS = 8
C_NOPE = 128
C_ROPE = 64
C_V = 128
C_QK = C_NOPE + C_ROPE
Q_LORA = 384
KV_LORA = 256
ROPE_THETA = 10000.0
ATTN_SCALE = C_QK ** -0.5
LOG2E = math.log2(math.e)

LANES = 128
SUBLANES = 8
ODD_IN_PAD = Q_LORA + KV_LORA + LANES
VMEM_LIMIT_BYTES = 60000 * 1024

TOKEN_TILE = 512
FFN_STAGE_ROWS_IN = 128
FFN_STAGE_ROWS_OUT = 256
FFN_STAGE_SLOTS = 4
HGRN_TILE = 256
HGRN_BATCH = 2
HGRN_SUB = HGRN_TILE // 2
HGRN_LEVELS = (64, 32, 16, 8)
ATTN_TILE = 512
ATTN_HEADS_PER_STEP = 4


def _sigmoid(x):
    return 0.5 + 0.5 * jnp.tanh(0.5 * x)


def _silu(x):
    t = 0.5 * x
    return t + t * jnp.tanh(t)


def _gelu_tanh(x):
    c = math.sqrt(2.0 / math.pi)
    hx = 0.5 * x
    return hx + hx * jnp.tanh(x * (c + (c * 0.044715) * (x * x)))


def _rms(x, g):
    ms = jnp.mean(x * x, axis=-1, keepdims=True)
    return x * lax.rsqrt(ms + EPS) * g


def _dot(a, b):
    return jnp.dot(a, b, preferred_element_type=F32)


def _dot_nt(a, b):
    return lax.dot_general(a, b, (((1,), (1,)), ((), ())), preferred_element_type=F32)


def _dot_tn(a, b):
    return lax.dot_general(a, b, (((0,), (0,)), ((), ())), preferred_element_type=F32)


def _split2(x):
    hi = x.astype(BF16)
    lo = (x - hi.astype(F32)).astype(BF16)
    return hi, lo


def _swiglu(hn, wg_ref, wu_ref, wd_ref):
    act = (_silu(_dot(hn, wg_ref[...])) * _dot(hn, wu_ref[...])).astype(BF16)
    return _dot(act, wd_ref[...])


def _s1_even_kernel(ffn_layer, x_ref, g0_ref, wg_hbm, wu_hbm, wd_hbm, g1_ref, win_ref,
                    x1_ref, u_ref, v_ref, q_ref, f_ref, i_ref, gg_ref, wg_ref, wu_ref, wd_ref, *stage):
    @pl.when(pl.program_id(0) == 0)
    def _():
        _load_ffn(ffn_layer, wg_hbm, wu_hbm, wd_hbm, wg_ref, wu_ref, wd_ref, *stage)

    x = x_ref[...]
    hn = _rms(x, g0_ref[...]).astype(BF16)
    x1 = x + 0.5 * _swiglu(hn, wg_ref, wu_ref, wd_ref)
    x1_ref[...] = x1
    h = _rms(x1, g1_ref[...]).astype(BF16)
    outs = (u_ref, v_ref, q_ref, f_ref, i_ref, gg_ref)
    for k, o_ref in enumerate(outs):
        z = _dot(h, win_ref[:, k * A_WIDTH:(k + 1) * A_WIDTH])
        o_ref[...] = z.astype(o_ref.dtype)


def _const_spec(shape, prefix=()):
    nd = len(shape)
    return pl.BlockSpec((None,) * len(prefix) + tuple(shape), lambda *_: tuple(prefix) + (0,) * nd,
                        pipeline_mode=pl.Buffered(1))


def _ffn_specs():
    return [pl.BlockSpec(memory_space=pl.ANY)] * 3


def _ffn_scratch():
    return [pltpu.VMEM((D_MODEL, D_FF), BF16), pltpu.VMEM((D_MODEL, D_FF), BF16), pltpu.VMEM((D_FF, D_MODEL), BF16),
            pltpu.VMEM((FFN_STAGE_SLOTS, FFN_STAGE_ROWS_IN, D_FF), F32),
            pltpu.VMEM((FFN_STAGE_SLOTS, FFN_STAGE_ROWS_OUT, D_MODEL), F32),
            pltpu.SemaphoreType.DMA((FFN_STAGE_SLOTS,))]


def _load_cast(src_hbm, layer, dst_ref, stage_ref, sem_ref):
    slots, rows = stage_ref.shape[0], stage_ref.shape[1]
    n_chunks = dst_ref.shape[0] // rows
    ahead = slots - 1

    def copy(c):
        return pltpu.make_async_copy(src_hbm.at[layer[0], layer[1], pl.ds(c * rows, rows)],
                                     stage_ref.at[c % slots], sem_ref.at[c % slots])

    for c in range(min(ahead, n_chunks)):
        copy(c).start()
    for c in range(n_chunks):
        if c + ahead < n_chunks:
            copy(c + ahead).start()
        copy(c).wait()
        dst_ref[c * rows:(c + 1) * rows, :] = stage_ref[c % slots].astype(BF16)


def _load_ffn(layer, wg_hbm, wu_hbm, wd_hbm, wg_ref, wu_ref, wd_ref, stage_in_ref, stage_out_ref, sem_ref):
    _load_cast(wg_hbm, layer, wg_ref, stage_in_ref, sem_ref)
    _load_cast(wu_hbm, layer, wu_ref, stage_in_ref, sem_ref)
    _load_cast(wd_hbm, layer, wd_ref, stage_out_ref, sem_ref)


def _stage1_even(x2d, g0, ffn, ffn_layer, g1, win, j):
    n = x2d.shape[0]
    tm = TOKEN_TILE
    row = lambda i: (i, 0)
    part_spec = pl.BlockSpec((tm, A_WIDTH), row)
    out_shape = [jax.ShapeDtypeStruct((n, D_MODEL), F32)]
    out_shape += [jax.ShapeDtypeStruct((n, A_WIDTH), F32 if k == 3 else BF16) for k in range(EVEN_PARTS)]
    return pl.pallas_call(
        functools.partial(_s1_even_kernel, ffn_layer),
        grid=(n // tm,),
        in_specs=[pl.BlockSpec((tm, D_MODEL), row), _const_spec((1, D_MODEL))] + _ffn_specs() + [
                  _const_spec((1, D_MODEL)), _const_spec((D_MODEL, EVEN_PARTS * A_WIDTH), (j,))],
        out_specs=[pl.BlockSpec((tm, D_MODEL), row)] + [part_spec] * EVEN_PARTS,
        out_shape=out_shape,
        scratch_shapes=_ffn_scratch(),
        compiler_params=pltpu.CompilerParams(
            dimension_semantics=("arbitrary",), vmem_limit_bytes=VMEM_LIMIT_BYTES,
            allow_input_fusion=[k == 6 for k in range(7)]),
        name="stage1_even",
    )(x2d, g0, *ffn, g1, win)


def _s1_odd_kernel(ffn_layer, x_ref, posr_ref, invfc_ref, g0_ref, wg_hbm, wu_hbm, wd_hbm, g1_ref,
                   win_ref, wropet_ref, qan_ref, kvan_ref, wqbt_ref, wkvk_ref, wkvvt_ref, gq_ref, gkr_ref, kn_ref,
                   indk_ref, splat_ref,
                   x1_ref, qt_ref, k_ref, vt_ref, wg_ref, wu_ref, wd_ref, *stage):
    @pl.when((pl.program_id(0) == 0) & (pl.program_id(1) == 0))
    def _():
        _load_ffn(ffn_layer, wg_hbm, wu_hbm, wd_hbm, wg_ref, wu_ref, wd_ref, *stage)

    x = x_ref[0]
    hn = _rms(x, g0_ref[...]).astype(BF16)
    x1 = x + 0.5 * _swiglu(hn, wg_ref, wu_ref, wd_ref)
    x1_ref[0] = x1
    h = _rms(x1, g1_ref[...]).astype(BF16)
    z = _dot(h, win_ref[...])
    cqn = _rms(z[:, :Q_LORA], qan_ref[...]).astype(BF16)
    ckvn = _rms(z[:, Q_LORA:Q_LORA + KV_LORA], kvan_ref[...]).astype(BF16)
    k_rope = z[:, Q_LORA + KV_LORA:]
    q_t = _dot_nt(wqbt_ref[...], cqn)
    k_nope = _dot(ckvn, wkvk_ref[...])
    vt_ref[0] = _dot_nt(wkvvt_ref[...], ckvn).astype(BF16)

    ang_t = invfc_ref[...] * posr_ref[0]
    cos_f, sin_f = jnp.cos(ang_t), jnp.sin(ang_t)
    gq = gq_ref[...]
    half = C_ROPE // 2
    for hd in range(C_HEADS):
        blk = q_t[hd * C_QK:(hd + 1) * C_QK, :]
        ss = jnp.sum(blk * blk, axis=0, keepdims=True)
        r = lax.rsqrt(ss * (1.0 / C_QK) + EPS) * (ATTN_SCALE * LOG2E)
        sc = blk * r * gq
        a1 = sc[C_NOPE:C_NOPE + half, :]
        a2 = sc[C_NOPE + half:, :]
        qt_ref[0, hd, 0:C_NOPE, :] = sc[:C_NOPE, :].astype(BF16)
        qt_ref[0, hd, C_NOPE:C_NOPE + half, :] = (a1 * cos_f - a2 * sin_f).astype(BF16)
        qt_ref[0, hd, C_NOPE + half:C_QK, :] = (a2 * cos_f + a1 * sin_f).astype(BF16)

    kr_t = _dot_nt(wropet_ref[...], h) * gkr_ref[...]
    b1, b2 = kr_t[:half, :], kr_t[half:, :]
    rot_t = jnp.concatenate([b1 * cos_f - b2 * sin_f, b2 * cos_f + b1 * sin_f,
                             jnp.zeros((LANES - C_ROPE, kr_t.shape[1]), F32)], axis=0)
    kr_rot = rot_t.T

    lane = lax.broadcasted_iota(jnp.int32, (1, LANES), 1)
    kn_nope = kn_ref[...]
    k2_hi, k2_lo = _split2(jnp.concatenate([k_nope * k_nope, k_rope * k_rope], axis=1))
    ssk = _dot(k2_hi, indk_ref[...]) + _dot(k2_lo, indk_ref[...])
    rk = jnp.where(lane < C_HEADS, lax.rsqrt(ssk * (1.0 / C_QK) + EPS), 0.0)
    rk_hi, rk_lo = _split2(rk)
    rk_b = _dot(rk_hi, splat_ref[...]) + _dot(rk_lo, splat_ref[...])
    for hd in range(C_HEADS):
        cols = slice(hd * C_NOPE, (hd + 1) * C_NOPE)
        k_ref[0, hd, :, 0:C_NOPE] = (k_nope[:, cols] * rk_b[:, cols] * kn_nope).astype(BF16)
        k_ref[0, hd, :, C_NOPE:C_QK] = (kr_rot * rk_b[:, cols])[:, :C_ROPE].astype(BF16)


def _stage1_odd(x3d, posr, op, g0, ffn, ffn_layer, g1):
    bsz, seq, _ = x3d.shape
    tm = TOKEN_TILE
    tile = lambda b, i: (b, i, 0)
    return pl.pallas_call(
        functools.partial(_s1_odd_kernel, ffn_layer),
        grid=(bsz, seq // tm),
        in_specs=[pl.BlockSpec((1, tm, D_MODEL), tile), pl.BlockSpec((1, 1, tm), lambda b, i: (b, 0, i)),
                  _const_spec((C_ROPE // 2, tm)), _const_spec((1, D_MODEL))]
                 + _ffn_specs() + [
                  _const_spec((1, D_MODEL)), _const_spec((D_MODEL, ODD_IN_PAD)), _const_spec((C_ROPE, D_MODEL)),
                  _const_spec((1, Q_LORA)), _const_spec((1, KV_LORA)),
                  _const_spec((C_HEADS * C_QK, Q_LORA)), _const_spec((KV_LORA, C_HEADS * C_NOPE)),
                  _const_spec((C_HEADS * C_V, KV_LORA)), _const_spec((C_QK, tm)), _const_spec((C_ROPE, tm)),
                  _const_spec((1, C_NOPE)), _const_spec((C_HEADS * C_NOPE + LANES, LANES)),
                  _const_spec((LANES, C_HEADS * C_NOPE))],
        out_specs=[pl.BlockSpec((1, tm, D_MODEL), tile),
                   pl.BlockSpec((1, C_HEADS, C_QK, tm), lambda b, i: (b, 0, 0, i)),
                   pl.BlockSpec((1, C_HEADS, tm, C_QK), lambda b, i: (b, 0, i, 0)),
                   pl.BlockSpec((1, C_HEADS * C_V, tm), lambda b, i: (b, 0, i))],
        out_shape=[jax.ShapeDtypeStruct((bsz, seq, D_MODEL), F32),
                   jax.ShapeDtypeStruct((bsz, C_HEADS, C_QK, seq), BF16),
                   jax.ShapeDtypeStruct((bsz, C_HEADS, seq, C_QK), BF16),
                   jax.ShapeDtypeStruct((bsz, C_HEADS * C_V, seq), BF16)],
        scratch_shapes=_ffn_scratch(),
        compiler_params=pltpu.CompilerParams(dimension_semantics=("arbitrary", "arbitrary"),
                                             vmem_limit_bytes=VMEM_LIMIT_BYTES),
        name="stage1_odd",
    )(x3d, posr, op["invfc"], g0, *ffn, g1, op["win"], op["wropet"], op["qan"], op["kvan"],
      op["wqbt"], op["wkvk"], op["wkvvt"], op["gq"], op["gkr"], op["kn"], op["indk"], op["splat"])


def _s2_kernel(ffn_layer, x_ref, m_ref, p_ref, wo_ref, g2_ref, wg_hbm, wu_hbm, wd_hbm,
               g3_ref, wpg_ref, wpp_ref, g4_ref, o_ref, wg_ref, wu_ref, wd_ref, *stage):
    @pl.when(pl.program_id(0) == 0)
    def _():
        _load_ffn(ffn_layer, wg_hbm, wu_hbm, wd_hbm, wg_ref, wu_ref, wd_ref, *stage)

    x = x_ref[...] + _dot(m_ref[...], wo_ref[...])
    hn = _rms(x, g2_ref[...]).astype(BF16)
    x = x + 0.5 * _swiglu(hn, wg_ref, wu_ref, wd_ref)
    h3 = _rms(x, g3_ref[...]).astype(BF16)
    gate = _sigmoid(_dot(h3, wpg_ref[...]))
    proj = _dot(p_ref[...].astype(BF16), wpp_ref[...])
    o_ref[...] = x + _rms(gate * proj, g4_ref[...])


def _stage2(x2d, mixed, p3d, layer, wo, j, g2, ffn, ffn_layer, g3, wpg, wpp, g4):
    n = x2d.shape[0]
    tm = TOKEN_TILE
    row = lambda i: (i, 0)
    return pl.pallas_call(
        functools.partial(_s2_kernel, ffn_layer),
        grid=(n // tm,),
        in_specs=[pl.BlockSpec((tm, D_MODEL), row), pl.BlockSpec((tm, D_MODEL), row),
                  pl.BlockSpec((None, tm, D_PLE), lambda i: (layer, i, 0)),
                  _const_spec((D_MODEL, D_MODEL), (j,)), _const_spec((1, D_MODEL))] + _ffn_specs() + [
                  _const_spec((1, D_MODEL)), _const_spec((D_MODEL, D_MODEL), (layer,)),
                  _const_spec((D_PLE, D_MODEL), (layer,)), _const_spec((1, D_MODEL))],
        out_specs=pl.BlockSpec((tm, D_MODEL), row),
        out_shape=jax.ShapeDtypeStruct((n, D_MODEL), F32),
        scratch_shapes=_ffn_scratch(),
        compiler_params=pltpu.CompilerParams(
            dimension_semantics=("arbitrary",), vmem_limit_bytes=VMEM_LIMIT_BYTES,
            allow_input_fusion=[k in (3, 9, 10) for k in range(12)]),
        name="stage2",
    )(x2d, mixed, p3d, wo, g2, *ffn, g3, wpg, wpp, g4)


def _level_map(ts):
    t = np.arange(ts)[:, None]
    s = np.arange(ts)[None, :]
    lvl = np.full((ts, ts), len(HGRN_LEVELS) + 1, np.int32)
    lvl[(t // SUBLANES == s // SUBLANES) & (s <= t)] = len(HGRN_LEVELS)
    for li, h in enumerate(HGRN_LEVELS):
        own = (t // (2 * h) == s // (2 * h)) & (t % (2 * h) >= h) & (s % (2 * h) < h)
        lvl[own] = li
    return lvl


def _even_core_kernel(u_ref, v_ref, q_ref, f_ref, i_ref, g_ref,
                      vnorm_ref, ind_ref, wcat_ref, bfull_ref, lb_ref, onorm_ref, ltri_ref, lvl_ref,
                      o_ref, state_ref):
    nb = HGRN_BATCH
    ts = HGRN_TILE
    sub = HGRN_SUB
    n_lvl = len(HGRN_LEVELS)
    rows_all = nb * ts

    def stacked(ref):
        return ref[...].reshape(rows_all, ref.shape[-1])

    u = _gelu_tanh(stacked(u_ref).astype(F32))
    v = _gelu_tanh(stacked(v_ref).astype(F32))
    ss = _dot((v * v).astype(BF16), ind_ref[...])
    vn = (v * lax.rsqrt(ss * (1.0 / A_HEAD_DIM) + EPS) * vnorm_ref[...]).astype(BF16)
    lane_head = lax.broadcasted_iota(jnp.int32, (1, A_WIDTH), 1) // A_HEAD_DIM
    for c in range(rows_all // A_CHUNK):
        rows = slice(c * A_CHUNK, (c + 1) * A_CHUNK)
        vc = vn[rows, :]
        zero = jnp.zeros_like(vc)
        vbd = jnp.concatenate([jnp.where(lane_head == hd, vc, zero) for hd in range(A_HEADS)], axis=0)
        mixed = _dot(wcat_ref[...], vbd) + bfull_ref[...]
        bi, local = divmod(c * A_CHUNK, ts)
        o_ref[bi, local:local + A_CHUNK, 0:A_WIDTH] = (u[rows, :] * mixed).astype(BF16)

    @pl.when(pl.program_id(1) == 0)
    def _():
        state_ref[...] = jnp.zeros_like(state_ref)

    lb = lb_ref[...]
    fg = lb + (1.0 - lb) * _sigmoid(stacked(f_ref))
    lf = jnp.log2(jnp.maximum(fg, F_MIN))
    kk = 1.0 - fg
    qq = stacked(q_ref).astype(F32)
    lf_hi, lf_lo = _split2(lf)
    ltri = ltri_ref[...]
    seqs = [slice(bi * ts, (bi + 1) * ts) for bi in range(nb)]
    b = jnp.concatenate([_dot(ltri, lf_hi[r, :]) + _dot(ltri, lf_lo[r, :]) for r in seqs], axis=0)

    q_lvls, k_lvls = [], []
    for h in HGRN_LEVELS:
        q_parts, k_parts = [], []
        zero = jnp.zeros((h, B_WIDTH), F32)
        for j in range(rows_all // (2 * h)):
            r0 = j * 2 * h
            bm = b[r0 + h - 1:r0 + h, :]
            lo_rows = slice(r0, r0 + h)
            up_rows = slice(r0 + h, r0 + 2 * h)
            k_parts += [kk[lo_rows, :] * jnp.exp2(bm - b[lo_rows, :]), zero]
            q_parts += [zero, qq[up_rows, :] * jnp.exp2(b[up_rows, :] - bm)]
        q_lvls.append(jnp.concatenate(q_parts, axis=0).astype(BF16))
        k_lvls.append(jnp.concatenate(k_parts, axis=0).astype(BF16))
    b3 = b.reshape(rows_all // SUBLANES, SUBLANES, B_WIDTH)
    bref = jnp.broadcast_to(b3[:, 3:4, :], b3.shape).reshape(rows_all, B_WIDTH)
    q_lvls.append((qq * jnp.exp2(b - bref)).astype(BF16))
    k_lvls.append((kk * jnp.exp2(bref - b)).astype(BF16))

    q_top, k_top, k_out, carry = [], [], [], []
    for bi in range(nb):
        r0 = bi * ts
        bm = b[r0 + sub - 1:r0 + sub, :]
        q_top.append((qq[r0 + sub:r0 + ts, :] * jnp.exp2(b[r0 + sub:r0 + ts, :] - bm)).astype(BF16))
        k_top.append((kk[r0:r0 + sub, :] * jnp.exp2(bm - b[r0:r0 + sub, :])).astype(BF16))
        b_last = b[r0 + ts - 1:r0 + ts, :]
        k_out.append((kk[seqs[bi], :] * jnp.exp2(b_last - b[seqs[bi], :])).astype(BF16))
        carry.append(jnp.exp2(b_last))
    q_in = (qq * jnp.exp2(b)).astype(BF16)
    vv = stacked(i_ref)
    gate = _silu(stacked(g_ref).astype(F32))
    onorm = onorm_ref[...]

    lvl = lvl_ref[...]
    masks = [lvl == li for li in range(n_lvl + 1)]
    units = [(bi, hd, slice(hd * B_DIM, (hd + 1) * B_DIM)) for hd in range(B_HEADS) for bi in range(nb)]
    states = {(bi, hd): state_ref[bi, hd] for bi, hd, _ in units}
    inter = {(bi, hd): _dot_nt(q_in[seqs[bi], cols], states[bi, hd].astype(BF16)) for bi, hd, cols in units}
    for bi, hd, cols in units:
        state_ref[bi, hd] = (states[bi, hd] * carry[bi][:, cols]
                             + _dot_tn(vv[seqs[bi], cols], k_out[bi][:, cols]))
    score_blocks = {}
    for bi, hd, cols in units:
        diag = []
        for a in range(ts // sub):
            rows = slice(bi * ts + a * sub, bi * ts + (a + 1) * sub)
            blk = jnp.where(masks[n_lvl], _dot_nt(q_lvls[n_lvl][rows, cols], k_lvls[n_lvl][rows, cols]), 0.0)
            for li in range(n_lvl):
                blk = jnp.where(masks[li], _dot_nt(q_lvls[li][rows, cols], k_lvls[li][rows, cols]), blk)
            diag.append(blk.astype(BF16))
        top = _dot_nt(q_top[bi][:, cols], k_top[bi][:, cols]).astype(BF16)
        score_blocks[bi, hd] = (diag[0], jnp.concatenate([top, diag[1]], axis=1))
    outs = {}
    for bi, hd, cols in units:
        v_h = vv[seqs[bi], cols]
        first, second = score_blocks[bi, hd]
        o_intra = jnp.concatenate([_dot(first, v_h[:sub, :]), _dot(second, v_h)], axis=0)
        outs[bi, hd] = o_intra + inter[bi, hd]
    for bi, hd, cols in units:
        on = _rms(outs[bi, hd], onorm[:, cols]) * gate[seqs[bi], cols]
        o_ref[bi, :, A_WIDTH + hd * B_DIM:A_WIDTH + (hd + 1) * B_DIM] = on.astype(BF16)


def _even_core(parts, vnorm, ind, wcat, bfull, lb, onorm, ltri, lvl):
    bsz, seq, _ = parts[0].shape
    ts = HGRN_TILE
    nb = HGRN_BATCH
    tile = lambda b, i: (b, i, 0)
    part_spec = pl.BlockSpec((nb, ts, A_WIDTH), tile)
    return pl.pallas_call(
        _even_core_kernel,
        grid=(bsz // nb, seq // ts),
        in_specs=[part_spec] * EVEN_PARTS + [
            _const_spec((1, A_WIDTH)), _const_spec((A_WIDTH, A_WIDTH)),
            _const_spec((A_CHUNK, A_HEADS * A_CHUNK)), _const_spec((A_CHUNK, A_WIDTH)),
            _const_spec((1, B_WIDTH)), _const_spec((1, B_WIDTH)),
            _const_spec((ts, ts)), _const_spec((HGRN_SUB, HGRN_SUB))],
        out_specs=pl.BlockSpec((nb, ts, D_MODEL), tile),
        out_shape=jax.ShapeDtypeStruct((bsz, seq, D_MODEL), BF16),
        scratch_shapes=[pltpu.VMEM((nb, B_HEADS, B_DIM, B_DIM), F32)],
        compiler_params=pltpu.CompilerParams(dimension_semantics=("parallel", "arbitrary"),
                                             vmem_limit_bytes=VMEM_LIMIT_BYTES),
        name="even_core",
    )(*parts, vnorm, ind, wcat, bfull, lb, onorm, ltri, lvl)


def _attn_kernel(qt_ref, k_ref, vt_ref, o_ref, m_ref, l_ref, acc_ref, sa_ref, sb_ref, mxa_ref, mxb_ref):
    tq = ATTN_TILE
    groups = tq // SUBLANES
    qi = pl.program_id(2)
    heads = range(ATTN_HEADS_PER_STEP)
    buf_a, buf_b = (sa_ref, mxa_ref), (sb_ref, mxb_ref)

    m_ref[...] = jnp.full_like(m_ref, NEG_BIG)
    l_ref[...] = jnp.zeros_like(l_ref)
    acc_ref[...] = jnp.zeros_like(acc_ref)

    def keys(j):
        return pl.ds(pl.multiple_of(j * tq, tq), tq)

    def scores(j, buf, heads=heads):
        s_ref, mx_ref = buf
        for hd in heads:
            s = _dot(k_ref[0, hd, keys(j), :], qt_ref[0, hd])
            s_ref[hd] = s
            mx_ref[hd] = jnp.max(s.reshape(groups, SUBLANES, tq), axis=0)

    def consume(j, buf, masked, heads=heads):
        s_ref, mx_ref = buf
        if masked:
            keep = (lax.broadcasted_iota(jnp.int32, (tq, tq), 0) <= lax.broadcasted_iota(jnp.int32, (tq, tq), 1))
        s3, m_new, alpha = {}, {}, {}
        for hd in heads:
            if masked:
                s3[hd] = jnp.where(keep, s_ref[hd], NEG_BIG).reshape(groups, SUBLANES, tq)
                cmax8 = jnp.max(s3[hd], axis=0)
            else:
                cmax8 = mx_ref[hd]
            cmax = jnp.max(cmax8, axis=0, keepdims=True)
            m_prev = m_ref[hd]
            m_new[hd] = jnp.maximum(m_prev, jnp.broadcast_to(cmax, (SUBLANES, tq)))
            alpha[hd] = jnp.exp2(m_prev - m_new[hd])
            m_ref[hd] = m_new[hd]
        ones = jnp.ones((2 * SUBLANES, tq), BF16)
        for hd in heads:
            sv = s3[hd] if masked else s_ref[hd].reshape(groups, SUBLANES, tq)
            p = jnp.exp2(sv - m_new[hd][None]).reshape(tq, tq).astype(BF16)
            lhs = jnp.concatenate([vt_ref[0, hd * C_V:(hd + 1) * C_V, keys(j)], ones], axis=0)
            pv = _dot(lhs, p)
            acc3 = acc_ref[hd].reshape(C_V // SUBLANES, SUBLANES, tq) * alpha[hd][None]
            acc_ref[hd] = acc3.reshape(C_V, tq) + pv[:C_V, :]
            l_ref[hd] = alpha[hd] * l_ref[hd] + pv[C_V:C_V + SUBLANES, :]

    scores(0, buf_a)
    pairs = qi // 2

    def body(t, carry):
        j = 2 * t
        for hd in heads:
            scores(j + 1, buf_b, [hd])
            consume(j, buf_a, False, [hd])
        for hd in heads:
            scores(j + 2, buf_a, [hd])
            consume(j + 1, buf_b, False, [hd])
        return carry

    lax.fori_loop(0, pairs, body, 0)

    @pl.when(qi % 2 == 0)
    def _():
        consume(qi, buf_a, True)

    @pl.when(qi % 2 == 1)
    def _():
        for hd in heads:
            scores(qi, buf_b, [hd])
            consume(qi - 1, buf_a, False, [hd])
        consume(qi, buf_b, True)
    for hd in range(ATTN_HEADS_PER_STEP):
        inv_l = 1.0 / l_ref[hd]
        out_t = acc_ref[hd].reshape(C_V // SUBLANES, SUBLANES, tq) * inv_l[None]
        o_ref[0, :, hd * C_V:(hd + 1) * C_V] = out_t.reshape(C_V, tq).T.astype(BF16)


def _attention(qt, k, vt):
    bsz, heads, seq, _ = k.shape
    tq = ATTN_TILE
    g = ATTN_HEADS_PER_STEP
    return pl.pallas_call(
        _attn_kernel,
        grid=(bsz, heads // g, seq // tq),
        in_specs=[pl.BlockSpec((1, g, C_QK, tq), lambda b, h, i: (b, h, 0, i)),
                  pl.BlockSpec((1, g, seq, C_QK), lambda b, h, i: (b, h, 0, 0)),
                  pl.BlockSpec((1, g * C_V, seq), lambda b, h, i: (b, h, 0))],
        out_specs=pl.BlockSpec((1, tq, g * C_V), lambda b, h, i: (b, i, h)),
        out_shape=jax.ShapeDtypeStruct((bsz, seq, heads * C_V), BF16),
        scratch_shapes=[pltpu.VMEM((g, SUBLANES, tq), F32), pltpu.VMEM((g, SUBLANES, tq), F32),
                        pltpu.VMEM((g, C_V, tq), F32), pltpu.VMEM((g, tq, tq), F32),
                        pltpu.VMEM((g, tq, tq), F32), pltpu.VMEM((g, SUBLANES, tq), F32),
                        pltpu.VMEM((g, SUBLANES, tq), F32)],
        compiler_params=pltpu.CompilerParams(dimension_semantics=("parallel", "parallel", "arbitrary"),
                                             vmem_limit_bytes=VMEM_LIMIT_BYTES),
        name="mla_attention",
    )(qt, k, vt)


def _row(v):
    return v.reshape(1, -1).astype(F32)


def _pad_cols(w, total):
    return jnp.pad(w, ((0, 0), (0, total - w.shape[1])))


def _even_params(v_norm, w_s, b_s, lb, out_norm):
    causal = np.tril(np.ones((A_CHUNK, A_CHUNK), bool))
    w = jnp.where(causal[None], w_s, 0.0)
    wcat = jnp.transpose(w, (1, 0, 2)).reshape(A_CHUNK, A_HEADS * A_CHUNK)
    bfull = jnp.repeat(b_s.T, A_HEAD_DIM, axis=1)
    head_of = np.arange(A_WIDTH) // A_HEAD_DIM
    ind = (head_of[:, None] == head_of[None, :]).astype(np.float32)
    ltri = np.tril(np.ones((HGRN_TILE, HGRN_TILE), np.float32))
    return dict(vnorm=_row(v_norm), ind=jnp.asarray(ind, BF16),
                wcat=wcat.astype(BF16), bfull=bfull.astype(F32), lb=_row(lb),
                onorm=_row(jnp.tile(out_norm, B_HEADS)), ltri=jnp.asarray(ltri, BF16),
                lvl=jnp.asarray(_level_map(HGRN_SUB)))


def _odd_params(w_in, q_a_norm, kv_a_norm, w_q_b, w_kv_b, q_norm, k_norm):
    tm = TOKEN_TILE
    win = _pad_cols(w_in, ODD_IN_PAD)
    wkv = w_kv_b.reshape(KV_LORA, C_HEADS, C_NOPE + C_V)
    wkvk = wkv[:, :, :C_NOPE].reshape(KV_LORA, C_HEADS * C_NOPE)
    wkvvt = wkv[:, :, C_NOPE:].reshape(KV_LORA, C_HEADS * C_V).T
    inv_freq = ROPE_THETA ** (-jnp.arange(0, C_ROPE, 2, dtype=F32) / C_ROPE)
    invfc = jnp.broadcast_to(inv_freq[:, None], (C_ROPE // 2, tm))
    head_of = np.arange(C_HEADS * C_NOPE) // C_NOPE
    splat = (head_of[:, None] == np.arange(LANES)[None, :]).astype(np.float32)
    rope_rows = np.broadcast_to((np.arange(LANES) < C_HEADS).astype(np.float32), (LANES, LANES))
    indk = np.concatenate([splat, rope_rows], axis=0)
    return dict(win=win.astype(BF16), wropet=w_in[:, Q_LORA + KV_LORA:].T.astype(BF16),
                qan=_row(q_a_norm), kvan=_row(kv_a_norm),
                wqbt=w_q_b.T.astype(BF16), wkvk=wkvk.astype(BF16), wkvvt=wkvvt.astype(BF16),
                gq=jnp.broadcast_to(q_norm.astype(F32)[:, None], (C_QK, tm)),
                gkr=jnp.broadcast_to(k_norm.astype(F32)[C_NOPE:, None], (C_ROPE, tm)),
                kn=_row(k_norm[:C_NOPE]), invfc=invfc,
                indk=jnp.asarray(indk, BF16), splat=jnp.asarray(splat.T, BF16))


def kernel(x, p, positions, norm_gains, ffn_w_gate, ffn_w_up, ffn_w_down, ple_w_gate, ple_w_proj,
           even_w_in, gmlp_v_norm, gmlp_w_s, gmlp_b_s, hgrn_lb_raw, hgrn_out_norm, even_w_out,
           mla_w_in, mla_q_a_norm, mla_kv_a_norm, mla_w_q_b, mla_w_kv_b, mla_q_norm, mla_k_norm, mla_w_out):
    bsz, seq, d = x.shape
    depth = norm_gains.shape[0]
    n = bsz * seq
    assert d == D_MODEL and seq % ATTN_TILE == 0 and seq % TOKEN_TILE == 0 and seq % HGRN_TILE == 0
    assert bsz % HGRN_BATCH == 0

    lb_sm = jax.nn.softmax(hgrn_lb_raw.astype(F32), axis=0)
    lower_bounds = jnp.clip(jnp.cumsum(lb_sm, axis=0) - lb_sm[0], 0.0, LB_MAX)
    posr = positions.astype(F32).reshape(bsz, 1, seq)

    ffn = (ffn_w_gate, ffn_w_up, ffn_w_down)
    ple_wg, ple_wp = ple_w_gate.astype(BF16), ple_w_proj.astype(BF16)
    even_win, even_wout, mla_wout = (w.astype(BF16) for w in (even_w_in, even_w_out, mla_w_out))
    p3d = p.reshape(depth, n, D_PLE)

    xc = x.reshape(n, d)
    for i in range(depth):
        g = norm_gains[i]
        j = i // 2
        if i % 2 == 0:
            ep = _even_params(gmlp_v_norm[j], gmlp_w_s[j], gmlp_b_s[j], lower_bounds[j], hgrn_out_norm[j])
            outs = _stage1_even(xc, _row(g[0]), ffn, (i, 0), _row(g[1]), even_win, j)
            xc = outs[0]
            parts = [o.reshape(bsz, seq, A_WIDTH) for o in outs[1:]]
            mixed = _even_core(parts, ep["vnorm"], ep["ind"], ep["wcat"], ep["bfull"], ep["lb"],
                               ep["onorm"], ep["ltri"], ep["lvl"])
            w_out = even_wout
        else:
            op = _odd_params(mla_w_in[j], mla_q_a_norm[j], mla_kv_a_norm[j], mla_w_q_b[j], mla_w_kv_b[j],
                             mla_q_norm[j], mla_k_norm[j])
            x1, qt, k, vt = _stage1_odd(xc.reshape(bsz, seq, d), posr, op, _row(g[0]), ffn, (i, 0),
                                        _row(g[1]))
            xc = x1.reshape(n, d)
            mixed = _attention(qt, k, vt)
            w_out = mla_wout
        xc = _stage2(xc, mixed.reshape(n, d), p3d, i, w_out, j, _row(g[2]), ffn, (i, 1), _row(g[3]),
                     ple_wg, ple_wp, _row(g[4]))
    return xc.reshape(bsz, seq, d)
```
